```python
import math
import jax, jax.numpy as jnp
from jax import lax
import numpy as np

D_MODEL = 2048
BATCH = 8
SEQ = 8192
DEPTH = 4

CHUNK = 64
N_MIXERS = 3
N_A = (DEPTH + 2) // 3
N_B = (DEPTH + 1) // 3
N_C = DEPTH // 3
CONV_WIDTH = 31
POOL_WINDOWS = (2, 4, 8, 16)
N_POOL_GROUPS = len(POOL_WINDOWS)
POOL_GC = D_MODEL // N_POOL_GROUPS
HEAD_DIM = 64
N_HEADS = D_MODEL // HEAD_DIM
N_KV = 8
GROUP = N_HEADS // N_KV
WINDOW = 128
WINDOW_CHUNKS = WINDOW // CHUNK
QBLOCK = 128
NUM_BUCKETS = 32
REL_MAX_DIST = 128
D_FF = ((8 * D_MODEL // 3 + 255) // 256) * 256
PLE_DIM = 256
EPS = 1e-6
NEG_INF = -1e30

kernel_name = "hybrid_conv_pool_swa_trunk"


def rms_norm(x, g):
    xf = x.astype(jnp.float32)
    y = xf * lax.rsqrt(jnp.mean(xf * xf, axis=-1, keepdims=True) + EPS)
    return (y * g.astype(jnp.float32)).astype(x.dtype)


def layer_norm(x, g, b):
    xf = x.astype(jnp.float32)
    mu = jnp.mean(xf, axis=-1, keepdims=True)
    xc = xf - mu
    y = xc * lax.rsqrt(jnp.mean(xc * xc, axis=-1, keepdims=True) + EPS)
    return (y * g.astype(jnp.float32) + b.astype(jnp.float32)).astype(x.dtype)


def conformer_conv(h, w_in, b_in, w_dw, b_dw, ln_g, ln_b, w_out, b_out):
    u = h @ w_in + b_in
    a, gate = jnp.split(u, 2, axis=-1)
    u = a * jax.nn.sigmoid(gate)
    u = lax.conv_general_dilated(
        u, w_dw[:, None, :], window_strides=(1,), padding=[(CONV_WIDTH - 1, 0)],
        dimension_numbers=("NWC", "WIO", "NWC"), feature_group_count=D_MODEL) + b_dw
    u = jax.nn.silu(layer_norm(u, ln_g, ln_b))
    return u @ w_out + b_out


def multiscale_pool(h, w_grp, scale):
    B, S, D = h.shape
    hf = h.astype(jnp.float32)
    cs = jnp.concatenate([jnp.zeros((B, 1, D), jnp.float32), jnp.cumsum(hf, axis=1)], axis=1)
    t = jnp.arange(S)
    pooled = []
    for g, w in enumerate(POOL_WINDOWS):
        sl = slice(g * POOL_GC, (g + 1) * POOL_GC)
        start = jnp.maximum(t + 1 - w, 0)
        s = cs[:, 1:, sl] - cs[:, start, sl]
        cnt = (t + 1 - start).astype(jnp.float32)[None, :, None]
        pooled.append(s / cnt)
    pooled = jnp.stack(pooled, axis=2)
    mix = (pooled - hf.reshape(B, S, N_POOL_GROUPS, POOL_GC)).astype(h.dtype)
    y = jnp.einsum('bsgc,gcd->bsgd', mix, w_grp).reshape(B, S, D)
    return y * scale


def t5_bucket(rel):
    nb = NUM_BUCKETS // 2
    n = -rel
    ret = jnp.where(n < 0, nb, 0)
    n = jnp.abs(n)
    max_exact = nb // 2
    nf = jnp.maximum(n, 1).astype(jnp.float32)
    large = max_exact + (jnp.log(nf / max_exact) / math.log(REL_MAX_DIST / max_exact)
                         * (nb - max_exact)).astype(jnp.int32)
    large = jnp.minimum(large, nb - 1)
    return ret + jnp.where(n < max_exact, n, large)


def band_bias_and_mask(rel_bias, n_blocks):
    i = jnp.arange(QBLOCK)[:, None]
    j = jnp.arange(2 * QBLOCK)[None, :]
    rel = j - QBLOCK - i
    bias = rel_bias[t5_bucket(rel)]
    bias = jnp.transpose(bias, (2, 0, 1)).reshape(N_KV, GROUP, QBLOCK, 2 * QBLOCK)
    qc = i // CHUNK
    kc = jnp.floor_divide(j - QBLOCK, CHUNK)
    chunk_ok = (kc <= qc) & (kc >= qc - WINDOW_CHUNKS)
    blk = jnp.arange(n_blocks)[:, None, None]
    mask = chunk_ok[None] & ((blk > 0) | (j[None] >= QBLOCK))
    return bias, mask


def swa_sink_attention(h, w_qkv, q_g, k_g, sinks, w_o, rel_bias):
    B, S, _ = h.shape
    NB = S // QBLOCK
    qkv = h @ w_qkv
    q, k, v = jnp.split(qkv, [N_HEADS * HEAD_DIM, (N_HEADS + N_KV) * HEAD_DIM], axis=-1)
    q = rms_norm(q.reshape(B, S, N_KV, GROUP, HEAD_DIM), q_g)
    k = rms_norm(k.reshape(B, S, N_KV, HEAD_DIM), k_g)
    v = v.reshape(B, S, N_KV, HEAD_DIM)
    q = q.reshape(B, NB, QBLOCK, N_KV, GROUP, HEAD_DIM)

    def band(t):
        tb = t.reshape(B, NB, QBLOCK, N_KV, HEAD_DIM)
        prev = jnp.concatenate([jnp.zeros_like(tb[:, :1]), tb[:, :-1]], axis=1)
        return jnp.concatenate([prev, tb], axis=2)

    kb, vb = band(k), band(v)
    bias, mask = band_bias_and_mask(rel_bias, NB)
    logits = jnp.einsum('bnqhgd,bnkhd->bnhgqk', q, kb,
                        preferred_element_type=jnp.float32) * (HEAD_DIM ** -0.5)
    logits = logits + bias.astype(jnp.float32)
    logits = jnp.where(mask[None, :, None, None], logits, NEG_INF)
    sink = sinks.astype(jnp.float32).reshape(1, 1, N_KV, GROUP, 1, 1)
    m = jnp.maximum(jnp.max(logits, axis=-1, keepdims=True), sink)
    e = jnp.exp(logits - m)
    denom = jnp.sum(e, axis=-1, keepdims=True) + jnp.exp(sink - m)
    probs = (e / denom).astype(v.dtype)
    o = jnp.einsum('bnhgqk,bnkhd->bnqhgd', probs, vb).reshape(B, S, N_HEADS * HEAD_DIM)
    return o @ w_o


def _fwd_setup_inputs(seed: int = 0) -> dict:
    key = jax.random.key(seed)
    ks = jax.random.split(key, 32)
    f32 = jnp.float32
    nrm = lambda k, shape, scale: jax.random.normal(k, shape, f32) * scale
    gain = lambda k, shape: 1.0 + 0.02 * jax.random.normal(k, shape, f32)
    D = D_MODEL
    return {
        "x": nrm(ks[0], (BATCH, SEQ, D), 1.0),
        "p": nrm(ks[1], (DEPTH, BATCH, SEQ, PLE_DIM), 1.0),
        "norm_mix": gain(ks[2], (DEPTH, D)),
        "norm_ffn": gain(ks[3], (DEPTH, D)),
        "norm_ple": gain(ks[4], (DEPTH, D)),
        "conv_w_in": nrm(ks[5], (N_A, D, 2 * D), D ** -0.5),
        "conv_b_in": nrm(ks[6], (N_A, 2 * D), 0.02),
        "conv_w_dw": nrm(ks[7], (N_A, CONV_WIDTH, D), CONV_WIDTH ** -0.5),
        "conv_b_dw": nrm(ks[8], (N_A, D), 0.02),
        "conv_ln_g": gain(ks[9], (N_A, D)),
        "conv_ln_b": nrm(ks[10], (N_A, D), 0.02),
        "conv_w_out": nrm(ks[11], (N_A, D, D), D ** -0.5),
        "conv_b_out": nrm(ks[12], (N_A, D), 0.02),
        "pool_w": nrm(ks[13], (N_B, N_POOL_GROUPS, POOL_GC, POOL_GC), POOL_GC ** -0.5),
        "pool_scale": 0.5 + 0.05 * jax.random.normal(ks[14], (N_B, D), f32),
        "attn_w_qkv": nrm(ks[15], (N_C, D, (N_HEADS + 2 * N_KV) * HEAD_DIM), D ** -0.5),
        "attn_q_norm": gain(ks[16], (N_C, HEAD_DIM)),
        "attn_k_norm": gain(ks[17], (N_C, HEAD_DIM)),
        "attn_sinks": nrm(ks[18], (N_C, N_HEADS), 0.5),
        "attn_w_o": nrm(ks[19], (N_C, N_HEADS * HEAD_DIM, D), (N_HEADS * HEAD_DIM) ** -0.5),
        "rel_bias": nrm(ks[20], (NUM_BUCKETS, N_HEADS), 0.5),
        "ffn_w_gate": nrm(ks[21], (DEPTH, D, D_FF), D ** -0.5),
        "ffn_w_up": nrm(ks[22], (DEPTH, D, D_FF), D ** -0.5),
        "ffn_w_down": nrm(ks[23], (DEPTH, D_FF, D), D_FF ** -0.5),
        "ple_w_proj": nrm(ks[24], (DEPTH, PLE_DIM, D), PLE_DIM ** -0.5),
        "ple_w_gate": nrm(ks[25], (DEPTH, D, D), D ** -0.5),
        "ple_b_gate": nrm(ks[26], (DEPTH, D), 0.02),
    }


def _fwd_reference(x, p, norm_mix, norm_ffn, norm_ple,
              conv_w_in, conv_b_in, conv_w_dw, conv_b_dw, conv_ln_g, conv_ln_b, conv_w_out, conv_b_out,
              pool_w, pool_scale,
              attn_w_qkv, attn_q_norm, attn_k_norm, attn_sinks, attn_w_o, rel_bias,
              ffn_w_gate, ffn_w_up, ffn_w_down,
              ple_w_proj, ple_w_gate, ple_b_gate):
    for i in range(DEPTH):
        kind, j = i % N_MIXERS, i // N_MIXERS
        h = rms_norm(x, norm_mix[i])
        if kind == 0:
            y = conformer_conv(h, conv_w_in[j], conv_b_in[j], conv_w_dw[j], conv_b_dw[j],
                               conv_ln_g[j], conv_ln_b[j], conv_w_out[j], conv_b_out[j])
        elif kind == 1:
            y = multiscale_pool(h, pool_w[j], pool_scale[j])
        else:
            y = swa_sink_attention(h, attn_w_qkv[j], attn_q_norm[j], attn_k_norm[j],
                                   attn_sinks[j], attn_w_o[j], rel_bias)
        x = x + y
        h = rms_norm(x, norm_ffn[i])
        x = x + (jax.nn.silu(h @ ffn_w_gate[i]) * (h @ ffn_w_up[i])) @ ffn_w_down[i]
        g = jax.nn.sigmoid(rms_norm(x, norm_ple[i]) @ ple_w_gate[i] + ple_b_gate[i])
        x = x + g * (p[i] @ ple_w_proj[i])
    return x


import jax as _jax
import jax.numpy as _jnp

TWIN_FORMAT = 'train_step'
FWD_PARAMS = ['x', 'p', 'norm_mix', 'norm_ffn', 'norm_ple', 'conv_w_in', 'conv_b_in', 'conv_w_dw', 'conv_b_dw', 'conv_ln_g', 'conv_ln_b', 'conv_w_out', 'conv_b_out', 'pool_w', 'pool_scale', 'attn_w_qkv', 'attn_q_norm', 'attn_k_norm', 'attn_sinks', 'attn_w_o', 'rel_bias', 'ffn_w_gate', 'ffn_w_up', 'ffn_w_down', 'ple_w_proj', 'ple_w_gate', 'ple_b_gate']
TWIN_WEIGHTS = ['norm_mix', 'norm_ffn', 'norm_ple', 'conv_w_in', 'conv_b_in', 'conv_w_dw', 'conv_b_dw', 'conv_ln_g', 'conv_ln_b', 'conv_w_out', 'conv_b_out', 'pool_w', 'pool_scale', 'attn_w_qkv', 'attn_q_norm', 'attn_k_norm', 'attn_sinks', 'attn_w_o', 'rel_bias', 'ffn_w_gate', 'ffn_w_up', 'ffn_w_down', 'ple_w_proj', 'ple_w_gate', 'ple_b_gate']
TWIN_DIFF_INPUT = 'x'
TWIN_INPUTS = ['x', 'p', 'norm_mix', 'norm_ffn', 'norm_ple', 'conv_w_in', 'conv_b_in', 'conv_w_dw', 'conv_b_dw', 'conv_ln_g', 'conv_ln_b', 'conv_w_out', 'conv_b_out', 'pool_w', 'pool_scale', 'attn_w_qkv', 'attn_q_norm', 'attn_k_norm', 'attn_sinks', 'attn_w_o', 'rel_bias', 'ffn_w_gate', 'ffn_w_up', 'ffn_w_down', 'ple_w_proj', 'ple_w_gate', 'ple_b_gate', 'loss_target', 'm_norm_mix', 'm_norm_ffn', 'm_norm_ple', 'm_conv_w_in', 'm_conv_b_in', 'm_conv_w_dw', 'm_conv_b_dw', 'm_conv_ln_g', 'm_conv_ln_b', 'm_conv_w_out', 'm_conv_b_out', 'm_pool_w', 'm_pool_scale', 'm_attn_w_qkv', 'm_attn_q_norm', 'm_attn_k_norm', 'm_attn_sinks', 'm_attn_w_o', 'm_rel_bias', 'm_ffn_w_gate', 'm_ffn_w_up', 'm_ffn_w_down', 'm_ple_w_proj', 'm_ple_w_gate', 'm_ple_b_gate', 'v_norm_mix', 'v_norm_ffn', 'v_norm_ple', 'v_conv_w_in', 'v_conv_b_in', 'v_conv_w_dw', 'v_conv_b_dw', 'v_conv_ln_g', 'v_conv_ln_b', 'v_conv_w_out', 'v_conv_b_out', 'v_pool_w', 'v_pool_scale', 'v_attn_w_qkv', 'v_attn_q_norm', 'v_attn_k_norm', 'v_attn_sinks', 'v_attn_w_o', 'v_rel_bias', 'v_ffn_w_gate', 'v_ffn_w_up', 'v_ffn_w_down', 'v_ple_w_proj', 'v_ple_w_gate', 'v_ple_b_gate']
TWIN_OUTPUTS = ['loss', 'grad_x', 'grad_norm_mix', 'grad_norm_ffn', 'grad_norm_ple', 'grad_conv_w_in', 'grad_conv_b_in', 'grad_conv_w_dw', 'grad_conv_b_dw', 'grad_conv_ln_g', 'grad_conv_ln_b', 'grad_conv_w_out', 'grad_conv_b_out', 'grad_pool_w', 'grad_pool_scale', 'grad_attn_w_qkv', 'grad_attn_q_norm', 'grad_attn_k_norm', 'grad_attn_sinks', 'grad_attn_w_o', 'grad_rel_bias', 'grad_ffn_w_gate', 'grad_ffn_w_up', 'grad_ffn_w_down', 'grad_ple_w_proj', 'grad_ple_w_gate', 'grad_ple_b_gate', 'delta_norm_mix', 'delta_norm_ffn', 'delta_norm_ple', 'delta_conv_w_in', 'delta_conv_b_in', 'delta_conv_w_dw', 'delta_conv_b_dw', 'delta_conv_ln_g', 'delta_conv_ln_b', 'delta_conv_w_out', 'delta_conv_b_out', 'delta_pool_w', 'delta_pool_scale', 'delta_attn_w_qkv', 'delta_attn_q_norm', 'delta_attn_k_norm', 'delta_attn_sinks', 'delta_attn_w_o', 'delta_rel_bias', 'delta_ffn_w_gate', 'delta_ffn_w_up', 'delta_ffn_w_down', 'delta_ple_w_proj', 'delta_ple_w_gate', 'delta_ple_b_gate', 'new_m_norm_mix', 'new_m_norm_ffn', 'new_m_norm_ple', 'new_m_conv_w_in', 'new_m_conv_b_in', 'new_m_conv_w_dw', 'new_m_conv_b_dw', 'new_m_conv_ln_g', 'new_m_conv_ln_b', 'new_m_conv_w_out', 'new_m_conv_b_out', 'new_m_pool_w', 'new_m_pool_scale', 'new_m_attn_w_qkv', 'new_m_attn_q_norm', 'new_m_attn_k_norm', 'new_m_attn_sinks', 'new_m_attn_w_o', 'new_m_rel_bias', 'new_m_ffn_w_gate', 'new_m_ffn_w_up', 'new_m_ffn_w_down', 'new_m_ple_w_proj', 'new_m_ple_w_gate', 'new_m_ple_b_gate', 'new_v_norm_mix', 'new_v_norm_ffn', 'new_v_norm_ple', 'new_v_conv_w_in', 'new_v_conv_b_in', 'new_v_conv_w_dw', 'new_v_conv_b_dw', 'new_v_conv_ln_g', 'new_v_conv_ln_b', 'new_v_conv_w_out', 'new_v_conv_b_out', 'new_v_pool_w', 'new_v_pool_scale', 'new_v_attn_w_qkv', 'new_v_attn_q_norm', 'new_v_attn_k_norm', 'new_v_attn_sinks', 'new_v_attn_w_o', 'new_v_rel_bias', 'new_v_ffn_w_gate', 'new_v_ffn_w_up', 'new_v_ffn_w_down', 'new_v_ple_w_proj', 'new_v_ple_w_gate', 'new_v_ple_b_gate']
TWIN_LEAF_KINDS = {'loss': 'loss', 'grad_x': 'grad_x', 'grad_norm_mix': 'grad_w', 'grad_norm_ffn': 'grad_w', 'grad_norm_ple': 'grad_w', 'grad_conv_w_in': 'grad_w', 'grad_conv_b_in': 'grad_w', 'grad_conv_w_dw': 'grad_w', 'grad_conv_b_dw': 'grad_w', 'grad_conv_ln_g': 'grad_w', 'grad_conv_ln_b': 'grad_w', 'grad_conv_w_out': 'grad_w', 'grad_conv_b_out': 'grad_w', 'grad_pool_w': 'grad_w', 'grad_pool_scale': 'grad_w', 'grad_attn_w_qkv': 'grad_w', 'grad_attn_q_norm': 'grad_w', 'grad_attn_k_norm': 'grad_w', 'grad_attn_sinks': 'grad_w', 'grad_attn_w_o': 'grad_w', 'grad_rel_bias': 'grad_w', 'grad_ffn_w_gate': 'grad_w', 'grad_ffn_w_up': 'grad_w', 'grad_ffn_w_down': 'grad_w', 'grad_ple_w_proj': 'grad_w', 'grad_ple_w_gate': 'grad_w', 'grad_ple_b_gate': 'grad_w', 'delta_norm_mix': 'delta_w', 'delta_norm_ffn': 'delta_w', 'delta_norm_ple': 'delta_w', 'delta_conv_w_in': 'delta_w', 'delta_conv_b_in': 'delta_w', 'delta_conv_w_dw': 'delta_w', 'delta_conv_b_dw': 'delta_w', 'delta_conv_ln_g': 'delta_w', 'delta_conv_ln_b': 'delta_w', 'delta_conv_w_out': 'delta_w', 'delta_conv_b_out': 'delta_w', 'delta_pool_w': 'delta_w', 'delta_pool_scale': 'delta_w', 'delta_attn_w_qkv': 'delta_w', 'delta_attn_q_norm': 'delta_w', 'delta_attn_k_norm': 'delta_w', 'delta_attn_sinks': 'delta_w', 'delta_attn_w_o': 'delta_w', 'delta_rel_bias': 'delta_w', 'delta_ffn_w_gate': 'delta_w', 'delta_ffn_w_up': 'delta_w', 'delta_ffn_w_down': 'delta_w', 'delta_ple_w_proj': 'delta_w', 'delta_ple_w_gate': 'delta_w', 'delta_ple_b_gate': 'delta_w', 'new_m_norm_mix': 'new_m', 'new_m_norm_ffn': 'new_m', 'new_m_norm_ple': 'new_m', 'new_m_conv_w_in': 'new_m', 'new_m_conv_b_in': 'new_m', 'new_m_conv_w_dw': 'new_m', 'new_m_conv_b_dw': 'new_m', 'new_m_conv_ln_g': 'new_m', 'new_m_conv_ln_b': 'new_m', 'new_m_conv_w_out': 'new_m', 'new_m_conv_b_out': 'new_m', 'new_m_pool_w': 'new_m', 'new_m_pool_scale': 'new_m', 'new_m_attn_w_qkv': 'new_m', 'new_m_attn_q_norm': 'new_m', 'new_m_attn_k_norm': 'new_m', 'new_m_attn_sinks': 'new_m', 'new_m_attn_w_o': 'new_m', 'new_m_rel_bias': 'new_m', 'new_m_ffn_w_gate': 'new_m', 'new_m_ffn_w_up': 'new_m', 'new_m_ffn_w_down': 'new_m', 'new_m_ple_w_proj': 'new_m', 'new_m_ple_w_gate': 'new_m', 'new_m_ple_b_gate': 'new_m', 'new_v_norm_mix': 'new_v', 'new_v_norm_ffn': 'new_v', 'new_v_norm_ple': 'new_v', 'new_v_conv_w_in': 'new_v', 'new_v_conv_b_in': 'new_v', 'new_v_conv_w_dw': 'new_v', 'new_v_conv_b_dw': 'new_v', 'new_v_conv_ln_g': 'new_v', 'new_v_conv_ln_b': 'new_v', 'new_v_conv_w_out': 'new_v', 'new_v_conv_b_out': 'new_v', 'new_v_pool_w': 'new_v', 'new_v_pool_scale': 'new_v', 'new_v_attn_w_qkv': 'new_v', 'new_v_attn_q_norm': 'new_v', 'new_v_attn_k_norm': 'new_v', 'new_v_attn_sinks': 'new_v', 'new_v_attn_w_o': 'new_v', 'new_v_rel_bias': 'new_v', 'new_v_ffn_w_gate': 'new_v', 'new_v_ffn_w_up': 'new_v', 'new_v_ffn_w_down': 'new_v', 'new_v_ple_w_proj': 'new_v', 'new_v_ple_w_gate': 'new_v', 'new_v_ple_b_gate': 'new_v'}


def _forward(args):
    return _fwd_reference(*[args[k] for k in FWD_PARAMS])


def _output_shape():
    def fwd():
        inp = _fwd_setup_inputs(0)
        return _fwd_reference(*[inp[k] for k in FWD_PARAMS])
    out = _jax.eval_shape(fwd)
    return out.shape, out.dtype

N_MICROBATCH = 1
ADAM_LR = 0.001
ADAM_B1 = 0.9
ADAM_B2 = 0.999
ADAM_EPS = 1e-08
ADAM_WD = 0.01
ADAM_STEP = 10
PER_EXAMPLE_BATCH_AXIS = {'x': 0, 'p': 1, 'loss_target': 0}
SHARED_INPUTS = []
_WEIGHT_DTYPES = {'norm_mix': _jnp.float32, 'norm_ffn': _jnp.float32, 'norm_ple': _jnp.float32, 'conv_w_in': _jnp.float32, 'conv_b_in': _jnp.float32, 'conv_w_dw': _jnp.float32, 'conv_b_dw': _jnp.float32, 'conv_ln_g': _jnp.float32, 'conv_ln_b': _jnp.float32, 'conv_w_out': _jnp.float32, 'conv_b_out': _jnp.float32, 'pool_w': _jnp.float32, 'pool_scale': _jnp.float32, 'attn_w_qkv': _jnp.float32, 'attn_q_norm': _jnp.float32, 'attn_k_norm': _jnp.float32, 'attn_sinks': _jnp.float32, 'attn_w_o': _jnp.float32, 'rel_bias': _jnp.float32, 'ffn_w_gate': _jnp.float32, 'ffn_w_up': _jnp.float32, 'ffn_w_down': _jnp.float32, 'ple_w_proj': _jnp.float32, 'ple_w_gate': _jnp.float32, 'ple_b_gate': _jnp.float32}
MOMENT_SCALE = {'norm_mix': 3.015026e+00, 'norm_ffn': 2.461130e+01, 'norm_ple': 1.041157e+00, 'conv_w_in': 3.375967e-01, 'conv_b_in': 5.377078e+00, 'conv_w_dw': 8.240369e-01, 'conv_b_dw': 1.308270e+01, 'conv_ln_g': 1.553100e+01, 'conv_ln_b': 1.206545e+01, 'conv_w_out': 2.941546e+00, 'conv_b_out': 1.497280e+01, 'pool_w': 6.315918e-01, 'pool_scale': 1.213134e+01, 'attn_w_qkv': 1.521285e+00, 'attn_q_norm': 9.678684e+00, 'attn_k_norm': 9.690657e+00, 'attn_sinks': 1.624697e-01, 'attn_w_o': 1.514775e+00, 'rel_bias': 5.388858e-01, 'ffn_w_gate': 4.258315e-01, 'ffn_w_up': 2.979282e-01, 'ffn_w_down': 4.670629e-01, 'ple_w_proj': 4.383600e-01, 'ple_w_gate': 4.062002e-01, 'ple_b_gate': 3.294523e+00}


def _to_microbatches(a, axis):
    t = _jnp.moveaxis(a, axis, 0)
    t = t.reshape((N_MICROBATCH, t.shape[0] // N_MICROBATCH) + t.shape[1:])
    return _jnp.moveaxis(t, 1, axis + 1)


def setup_inputs(seed: int = 0) -> dict:
    inp = _fwd_setup_inputs(seed)
    key = _jax.random.fold_in(_jax.random.key(seed), 7919)
    shape, _ = _output_shape()
    out = dict(inp)
    out["loss_target"] = _jax.random.normal(_jax.random.fold_in(key, 0), shape, _jnp.float32)
    for i, name in enumerate(TWIN_WEIGHTS):
        w = inp[name].astype(_jnp.float32)
        if MOMENT_SCALE is None:
            s = _jnp.sqrt(_jnp.mean(_jnp.square(w)) + 1e-30)
        else:
            s = MOMENT_SCALE[name]
        km, kv = _jax.random.split(_jax.random.fold_in(key, i + 1))
        out[name] = w
        out["m_" + name] = s * _jax.random.normal(km, w.shape, _jnp.float32)
        out["v_" + name] = (s * s) * _jax.random.uniform(kv, w.shape, _jnp.float32, 0.5, 1.5)
    if N_MICROBATCH > 1:
        for name, axis in PER_EXAMPLE_BATCH_AXIS.items():
            out[name] = _to_microbatches(out[name], axis)
    return {'x': out['x'], 'p': out['p'], 'norm_mix': out['norm_mix'], 'norm_ffn': out['norm_ffn'], 'norm_ple': out['norm_ple'], 'conv_w_in': out['conv_w_in'], 'conv_b_in': out['conv_b_in'], 'conv_w_dw': out['conv_w_dw'], 'conv_b_dw': out['conv_b_dw'], 'conv_ln_g': out['conv_ln_g'], 'conv_ln_b': out['conv_ln_b'], 'conv_w_out': out['conv_w_out'], 'conv_b_out': out['conv_b_out'], 'pool_w': out['pool_w'], 'pool_scale': out['pool_scale'], 'attn_w_qkv': out['attn_w_qkv'], 'attn_q_norm': out['attn_q_norm'], 'attn_k_norm': out['attn_k_norm'], 'attn_sinks': out['attn_sinks'], 'attn_w_o': out['attn_w_o'], 'rel_bias': out['rel_bias'], 'ffn_w_gate': out['ffn_w_gate'], 'ffn_w_up': out['ffn_w_up'], 'ffn_w_down': out['ffn_w_down'], 'ple_w_proj': out['ple_w_proj'], 'ple_w_gate': out['ple_w_gate'], 'ple_b_gate': out['ple_b_gate'], 'loss_target': out['loss_target'], 'm_norm_mix': out['m_norm_mix'], 'm_norm_ffn': out['m_norm_ffn'], 'm_norm_ple': out['m_norm_ple'], 'm_conv_w_in': out['m_conv_w_in'], 'm_conv_b_in': out['m_conv_b_in'], 'm_conv_w_dw': out['m_conv_w_dw'], 'm_conv_b_dw': out['m_conv_b_dw'], 'm_conv_ln_g': out['m_conv_ln_g'], 'm_conv_ln_b': out['m_conv_ln_b'], 'm_conv_w_out': out['m_conv_w_out'], 'm_conv_b_out': out['m_conv_b_out'], 'm_pool_w': out['m_pool_w'], 'm_pool_scale': out['m_pool_scale'], 'm_attn_w_qkv': out['m_attn_w_qkv'], 'm_attn_q_norm': out['m_attn_q_norm'], 'm_attn_k_norm': out['m_attn_k_norm'], 'm_attn_sinks': out['m_attn_sinks'], 'm_attn_w_o': out['m_attn_w_o'], 'm_rel_bias': out['m_rel_bias'], 'm_ffn_w_gate': out['m_ffn_w_gate'], 'm_ffn_w_up': out['m_ffn_w_up'], 'm_ffn_w_down': out['m_ffn_w_down'], 'm_ple_w_proj': out['m_ple_w_proj'], 'm_ple_w_gate': out['m_ple_w_gate'], 'm_ple_b_gate': out['m_ple_b_gate'], 'v_norm_mix': out['v_norm_mix'], 'v_norm_ffn': out['v_norm_ffn'], 'v_norm_ple': out['v_norm_ple'], 'v_conv_w_in': out['v_conv_w_in'], 'v_conv_b_in': out['v_conv_b_in'], 'v_conv_w_dw': out['v_conv_w_dw'], 'v_conv_b_dw': out['v_conv_b_dw'], 'v_conv_ln_g': out['v_conv_ln_g'], 'v_conv_ln_b': out['v_conv_ln_b'], 'v_conv_w_out': out['v_conv_w_out'], 'v_conv_b_out': out['v_conv_b_out'], 'v_pool_w': out['v_pool_w'], 'v_pool_scale': out['v_pool_scale'], 'v_attn_w_qkv': out['v_attn_w_qkv'], 'v_attn_q_norm': out['v_attn_q_norm'], 'v_attn_k_norm': out['v_attn_k_norm'], 'v_attn_sinks': out['v_attn_sinks'], 'v_attn_w_o': out['v_attn_w_o'], 'v_rel_bias': out['v_rel_bias'], 'v_ffn_w_gate': out['v_ffn_w_gate'], 'v_ffn_w_up': out['v_ffn_w_up'], 'v_ffn_w_down': out['v_ffn_w_down'], 'v_ple_w_proj': out['v_ple_w_proj'], 'v_ple_w_gate': out['v_ple_w_gate'], 'v_ple_b_gate': out['v_ple_b_gate']}


def _loss(weights, diff, rest, loss_target):
    with _jax.named_scope("forward"):
        args = {**rest, TWIN_DIFF_INPUT: diff, **{k: w.astype(_WEIGHT_DTYPES[k]) for k, w in weights.items()}}
        y = _forward(args)
    with _jax.named_scope("loss_head"):
        err = _jnp.square(y.astype(_jnp.float32) - loss_target)
        return 0.5 * _jnp.sum(_jnp.mean(err, axis=-1)) if err.ndim else 0.5 * err


def _adamw(w, g, m, v):
    m = ADAM_B1 * m + (1.0 - ADAM_B1) * g
    v = ADAM_B2 * v + (1.0 - ADAM_B2) * _jnp.square(g)
    m_hat = m / (1.0 - ADAM_B1 ** ADAM_STEP)
    v_hat = v / (1.0 - ADAM_B2 ** ADAM_STEP)
    delta = -ADAM_LR * (m_hat / (_jnp.sqrt(v_hat) + ADAM_EPS) + ADAM_WD * w)
    return delta, m, v


def reference(x, p, norm_mix, norm_ffn, norm_ple, conv_w_in, conv_b_in, conv_w_dw, conv_b_dw, conv_ln_g, conv_ln_b, conv_w_out, conv_b_out, pool_w, pool_scale, attn_w_qkv, attn_q_norm, attn_k_norm, attn_sinks, attn_w_o, rel_bias, ffn_w_gate, ffn_w_up, ffn_w_down, ple_w_proj, ple_w_gate, ple_b_gate, loss_target, m_norm_mix, m_norm_ffn, m_norm_ple, m_conv_w_in, m_conv_b_in, m_conv_w_dw, m_conv_b_dw, m_conv_ln_g, m_conv_ln_b, m_conv_w_out, m_conv_b_out, m_pool_w, m_pool_scale, m_attn_w_qkv, m_attn_q_norm, m_attn_k_norm, m_attn_sinks, m_attn_w_o, m_rel_bias, m_ffn_w_gate, m_ffn_w_up, m_ffn_w_down, m_ple_w_proj, m_ple_w_gate, m_ple_b_gate, v_norm_mix, v_norm_ffn, v_norm_ple, v_conv_w_in, v_conv_b_in, v_conv_w_dw, v_conv_b_dw, v_conv_ln_g, v_conv_ln_b, v_conv_w_out, v_conv_b_out, v_pool_w, v_pool_scale, v_attn_w_qkv, v_attn_q_norm, v_attn_k_norm, v_attn_sinks, v_attn_w_o, v_rel_bias, v_ffn_w_gate, v_ffn_w_up, v_ffn_w_down, v_ple_w_proj, v_ple_w_gate, v_ple_b_gate):
    given = dict(x=x, p=p, norm_mix=norm_mix, norm_ffn=norm_ffn, norm_ple=norm_ple, conv_w_in=conv_w_in, conv_b_in=conv_b_in, conv_w_dw=conv_w_dw, conv_b_dw=conv_b_dw, conv_ln_g=conv_ln_g, conv_ln_b=conv_ln_b, conv_w_out=conv_w_out, conv_b_out=conv_b_out, pool_w=pool_w, pool_scale=pool_scale, attn_w_qkv=attn_w_qkv, attn_q_norm=attn_q_norm, attn_k_norm=attn_k_norm, attn_sinks=attn_sinks, attn_w_o=attn_w_o, rel_bias=rel_bias, ffn_w_gate=ffn_w_gate, ffn_w_up=ffn_w_up, ffn_w_down=ffn_w_down, ple_w_proj=ple_w_proj, ple_w_gate=ple_w_gate, ple_b_gate=ple_b_gate, loss_target=loss_target, m_norm_mix=m_norm_mix, m_norm_ffn=m_norm_ffn, m_norm_ple=m_norm_ple, m_conv_w_in=m_conv_w_in, m_conv_b_in=m_conv_b_in, m_conv_w_dw=m_conv_w_dw, m_conv_b_dw=m_conv_b_dw, m_conv_ln_g=m_conv_ln_g, m_conv_ln_b=m_conv_ln_b, m_conv_w_out=m_conv_w_out, m_conv_b_out=m_conv_b_out, m_pool_w=m_pool_w, m_pool_scale=m_pool_scale, m_attn_w_qkv=m_attn_w_qkv, m_attn_q_norm=m_attn_q_norm, m_attn_k_norm=m_attn_k_norm, m_attn_sinks=m_attn_sinks, m_attn_w_o=m_attn_w_o, m_rel_bias=m_rel_bias, m_ffn_w_gate=m_ffn_w_gate, m_ffn_w_up=m_ffn_w_up, m_ffn_w_down=m_ffn_w_down, m_ple_w_proj=m_ple_w_proj, m_ple_w_gate=m_ple_w_gate, m_ple_b_gate=m_ple_b_gate, v_norm_mix=v_norm_mix, v_norm_ffn=v_norm_ffn, v_norm_ple=v_norm_ple, v_conv_w_in=v_conv_w_in, v_conv_b_in=v_conv_b_in, v_conv_w_dw=v_conv_w_dw, v_conv_b_dw=v_conv_b_dw, v_conv_ln_g=v_conv_ln_g, v_conv_ln_b=v_conv_ln_b, v_conv_w_out=v_conv_w_out, v_conv_b_out=v_conv_b_out, v_pool_w=v_pool_w, v_pool_scale=v_pool_scale, v_attn_w_qkv=v_attn_w_qkv, v_attn_q_norm=v_attn_q_norm, v_attn_k_norm=v_attn_k_norm, v_attn_sinks=v_attn_sinks, v_attn_w_o=v_attn_w_o, v_rel_bias=v_rel_bias, v_ffn_w_gate=v_ffn_w_gate, v_ffn_w_up=v_ffn_w_up, v_ffn_w_down=v_ffn_w_down, v_ple_w_proj=v_ple_w_proj, v_ple_w_gate=v_ple_w_gate, v_ple_b_gate=v_ple_b_gate)
    weights = {n: given[n] for n in TWIN_WEIGHTS}
    shared = {n: given[n] for n in SHARED_INPUTS}
    per_example = {n: given[n] for n in ['x', 'p']}
    grad_fn = _jax.value_and_grad(_loss, argnums=(0, 1))

    def one_microbatch(ex, loss_target):
        ex = dict(ex)
        diff = ex.pop(TWIN_DIFF_INPUT)
        return grad_fn(weights, diff, {**shared, **ex}, loss_target)

    if N_MICROBATCH == 1:
        loss, (grad_w, grad_x) = one_microbatch(per_example, given["loss_target"])
    else:
        def body(carry, xs):
            loss_sum, grad_sum = carry
            l_k, (gw_k, gx_k) = one_microbatch(xs[0], xs[1])
            with _jax.named_scope("update"):
                return (loss_sum + l_k, _jax.tree.map(_jnp.add, grad_sum, gw_k)), gx_k

        init = (_jnp.zeros((), _jnp.float32), _jax.tree.map(_jnp.zeros_like, weights))
        (loss, grad_w), grad_x = _jax.lax.scan(body, init, (per_example, given["loss_target"]))
    with _jax.named_scope("update"):
        delta_w, new_m, new_v = {}, {}, {}
        for n in TWIN_WEIGHTS:
            delta_w[n], new_m[n], new_v[n] = _adamw(weights[n], grad_w[n], given["m_" + n], given["v_" + n])
    return (loss, grad_x, *[grad_w[n] for n in TWIN_WEIGHTS], *[delta_w[n] for n in TWIN_WEIGHTS],
            *[new_m[n] for n in TWIN_WEIGHTS], *[new_v[n] for n in TWIN_WEIGHTS])
```

```python
import functools
import math

import jax
import jax.numpy as jnp
from jax import lax
from jax.experimental import pallas as pl
from jax.experimental.pallas import tpu as pltpu

F32, BF16 = jnp.float32, jnp.bfloat16
N_DEV = 8
EPS = 1e-6
NEG_INF = -1e30
HEAD_DIM = 64
Q_BLOCK = 128
CHUNK = 64
WINDOW_CHUNKS = 2
CONV_WIDTH = 31
CONV_HALO = 32
POOL_WINDOWS = (2, 4, 8, 16)
POOL_HALO = 16
NUM_BUCKETS = 32
REL_MAX_DIST = 128
ADAM_LR, ADAM_B1, ADAM_B2, ADAM_EPS, ADAM_WD, ADAM_STEP = 0.001, 0.9, 0.999, 1e-08, 0.01, 10
VMEM_LIMIT_BYTES = 44 * 1024 * 1024
ROW_TILE = 512
FULL_ROW_TILE = 256
ELT_TILE = 256
CONV_TILE = 128
CONV_SUB = 32
CONV_LANES = 512
K_TILE = 512
ADAM_BLOCK_BYTES = 1 << 20

_DIMS = {
    "nn": (((1,), (0,)), ((), ())),
    "nt": (((1,), (1,)), ((), ())),
    "tn": (((0,), (0,)), ((), ())),
}


def _dot(a, b, mode):
    return lax.dot_general(a, b, _DIMS[mode], preferred_element_type=F32)


def _tile(n, t):
    t = min(n, t)
    assert n % t == 0, (n, t)
    return t


def _div_tile(n, t):
    if n <= t:
        return n
    for cand in range(t // 16 * 16, 15, -16):
        if n % cand == 0:
            return cand
    return n


def _sds(shape, dtype):
    return jax.ShapeDtypeStruct(tuple(shape), dtype)


def _call(name, body, grid, in_specs, out_specs, out_shape, ins, scratch=()):
    return pl.pallas_call(
        body,
        name=name,
        grid=grid,
        in_specs=list(in_specs),
        out_specs=out_specs,
        out_shape=out_shape,
        scratch_shapes=list(scratch),
        compiler_params=pltpu.CompilerParams(
            dimension_semantics=("arbitrary",) * len(grid), vmem_limit_bytes=VMEM_LIMIT_BYTES
        ),
    )(*ins)


def _sig(x):
    return 1.0 / (1.0 + jnp.exp(-x))


def _rms(x, g):
    r = lax.rsqrt(jnp.mean(x * x, axis=-1, keepdims=True) + EPS)
    return x * r * g


def _rms_bwd(dy, x, g):
    r = lax.rsqrt(jnp.mean(x * x, axis=-1, keepdims=True) + EPS)
    xh = x * r
    dg = jnp.sum(dy * xh, axis=0, keepdims=True)
    dxh = dy * g
    dx = r * (dxh - xh * jnp.mean(dxh * xh, axis=-1, keepdims=True))
    return dx, dg


def _accumulate(ref, val, first):
    @pl.when(first)
    def _():
        ref[...] = val

    @pl.when(jnp.logical_not(first))
    def _():
        ref[...] += val


def _kloop(k, nk, acc_refs, contribs, finish):
    @pl.when(k == 0)
    def _():
        for r, c in zip(acc_refs, contribs):
            r[...] = c

    @pl.when(k > 0)
    def _():
        for r, c in zip(acc_refs, contribs):
            r[...] += c

    @pl.when(k == nk - 1)
    def _():
        finish()


def _rms_call(name, x, g):
    S, D = x.shape
    ts = _tile(S, ELT_TILE)

    def body(x_ref, g_ref, o_ref):
        o_ref[...] = _rms(x_ref[...], g_ref[...]).astype(BF16)

    return _call(
        name, body, (S // ts,),
        [pl.BlockSpec((ts, D), lambda i: (i, 0)), pl.BlockSpec((1, D), lambda i: (0, 0))],
        pl.BlockSpec((ts, D), lambda i: (i, 0)), _sds((S, D), BF16), (x, g),
    )


def _conv_in_call(name, hb, w, layer, b_in):
    S, D = hb.shape
    nsh = w.shape[-1]
    half = N_DEV // 2
    assert nsh * half == D
    tm = _tile(S, ROW_TILE)

    def body(h_ref, wa_ref, wg_ref, ba_ref, bg_ref, a_ref, g_ref, u_ref):
        h = h_ref[...]
        a = _dot(h, wa_ref[...], "nn") + ba_ref[...]
        g = _dot(h, wg_ref[...], "nn") + bg_ref[...]
        a_ref[...] = a.astype(BF16)
        g_ref[...] = g.astype(BF16)
        u_ref[...] = a * _sig(g)

    out_spec = pl.BlockSpec((tm, nsh), lambda d, i: (i, d))
    return _call(
        name, body, (half, S // tm),
        [
            pl.BlockSpec((tm, D), lambda d, i: (i, 0)),
            pl.BlockSpec((None, None, D, nsh), lambda d, i: (d, layer, 0, 0)),
            pl.BlockSpec((None, None, D, nsh), lambda d, i: (d + half, layer, 0, 0)),
            pl.BlockSpec((1, nsh), lambda d, i: (0, d)),
            pl.BlockSpec((1, nsh), lambda d, i: (0, d + half)),
        ],
        [out_spec, out_spec, out_spec],
        [_sds((S, D), BF16), _sds((S, D), BF16), _sds((S, D), F32)],
        (hb, w, w, b_in, b_in),
    )


def _conv_taps(ext_ref, w_ref, row0, lane0, offset, reverse):
    acc = None
    for k in range(CONV_WIDTH):
        off = offset - k if reverse else offset + k
        term = w_ref[k:k + 1, lane0:lane0 + CONV_LANES] * ext_ref[pl.ds(row0 + off, CONV_SUB), pl.ds(lane0, CONV_LANES)]
        acc = term if acc is None else acc + term
    return acc


def _dwconv_call(name, u1, w_dw, b_dw, ln_g, ln_b):
    S, D = u1.shape
    tc = _tile(S, CONV_TILE)
    hb = tc // CONV_HALO
    lanes = min(D, CONV_LANES)
    assert lanes == CONV_LANES and D % CONV_LANES == 0 and tc % CONV_SUB == 0

    def body(cur_ref, halo_ref, w_ref, b_ref, g_ref, bb_ref, u2_ref, u4_ref, ext_ref):
        i = pl.program_id(0)
        ext_ref[0:CONV_HALO, :] = jnp.where(i > 0, halo_ref[...], 0.0)
        ext_ref[CONV_HALO:, :] = cur_ref[...]
        first_tap = CONV_HALO - (CONV_WIDTH - 1)
        for r in range(0, tc, CONV_SUB):
            for c in range(0, D, CONV_LANES):
                u2_ref[r:r + CONV_SUB, c:c + CONV_LANES] = (
                    _conv_taps(ext_ref, w_ref, r, c, first_tap, False) + b_ref[:, c:c + CONV_LANES]
                )
        u2 = u2_ref[...]
        mu = jnp.mean(u2, axis=-1, keepdims=True)
        xc = u2 - mu
        u3 = xc * lax.rsqrt(jnp.mean(xc * xc, axis=-1, keepdims=True) + EPS) * g_ref[...] + bb_ref[...]
        u4_ref[...] = (u3 * _sig(u3)).astype(BF16)

    vec = pl.BlockSpec((1, D), lambda i: (0, 0))
    row = pl.BlockSpec((tc, D), lambda i: (i, 0))
    return _call(
        name, body, (S // tc,),
        [
            row,
            pl.BlockSpec((CONV_HALO, D), lambda i: (jnp.maximum(i * hb - 1, 0), 0)),
            pl.BlockSpec((CONV_WIDTH, D), lambda i: (0, 0)),
            vec, vec, vec,
        ],
        [row, row],
        [_sds((S, D), F32), _sds((S, D), BF16)],
        (u1, u1, w_dw, b_dw, ln_g, ln_b),
        scratch=[pltpu.VMEM((tc + CONV_HALO, D), F32)],
    )


def _rowmm_res_call(name, a, a_chunked, w, layer, x_res, bias, g_next):
    S, D = x_res.shape
    ksh = w.shape[2]
    tm = _tile(S, FULL_ROW_TILE)
    has_b, has_g = bias is not None, g_next is not None

    def body(*refs):
        a_ref, w_ref, x_ref = refs[:3]
        pos = 3
        b_ref = g_ref = None
        if has_b:
            b_ref = refs[pos]
            pos += 1
        if has_g:
            g_ref = refs[pos]
            pos += 1
        xo_ref = refs[pos]
        ho_ref = refs[pos + 1] if has_g else None
        acc_ref = refs[-1]
        k = pl.program_id(1)

        def finish():
            xn = x_ref[...] + acc_ref[...]
            if has_b:
                xn = xn + b_ref[...]
            xo_ref[...] = xn
            if has_g:
                ho_ref[...] = _rms(xn, g_ref[...]).astype(BF16)

        _kloop(k, N_DEV, [acc_ref], [_dot(a_ref[...], w_ref[...], "nn")], finish)

    if a_chunked:
        a_spec = pl.BlockSpec((None, tm, ksh), lambda i, k: (k, i, 0))
    else:
        a_spec = pl.BlockSpec((tm, ksh), lambda i, k: (i, k))
    row = pl.BlockSpec((tm, D), lambda i, k: (i, 0))
    vec = pl.BlockSpec((1, D), lambda i, k: (0, 0))
    in_specs = [a_spec, pl.BlockSpec((None, None, ksh, D), lambda i, k: (k, layer, 0, 0)), row]
    ins = [a, w, x_res]
    if has_b:
        in_specs.append(vec)
        ins.append(bias)
    if has_g:
        in_specs.append(vec)
        ins.append(g_next)
    out_specs, out_shape = [row], [_sds((S, D), F32)]
    if has_g:
        out_specs.append(row)
        out_shape.append(_sds((S, D), BF16))
    res = _call(name, body, (S // tm, N_DEV), in_specs, out_specs, out_shape, ins, scratch=[pltpu.VMEM((tm, D), F32)])
    return res if has_g else (res[0], None)


def _ffn_up_call(name, hb, wg, wu, layer):
    S, D = hb.shape
    fsh = wg.shape[-1]
    tm = _tile(S, ROW_TILE)

    def body(h_ref, wg_ref, wu_ref, g_ref, u_ref, a_ref):
        h = h_ref[...]
        g = _dot(h, wg_ref[...], "nn")
        u = _dot(h, wu_ref[...], "nn")
        g_ref[...] = g.astype(BF16)
        u_ref[...] = u.astype(BF16)
        a_ref[...] = (g * _sig(g) * u).astype(BF16)

    wspec = pl.BlockSpec((None, None, D, fsh), lambda d, i: (d, layer, 0, 0))
    ospec = pl.BlockSpec((None, tm, fsh), lambda d, i: (d, i, 0))
    return _call(
        name, body, (N_DEV, S // tm),
        [pl.BlockSpec((tm, D), lambda d, i: (i, 0)), wspec, wspec],
        [ospec] * 3, [_sds((N_DEV, S, fsh), BF16)] * 3, (hb, wg, wu),
    )


def _colmm_call(name, a, w, layer, out_dtype):
    S, K = a.shape
    nsh = w.shape[-1]
    tm = _tile(S, ROW_TILE)

    def body(a_ref, w_ref, o_ref):
        o_ref[...] = _dot(a_ref[...], w_ref[...], "nn").astype(out_dtype)

    return _call(
        name, body, (N_DEV, S // tm),
        [pl.BlockSpec((tm, K), lambda d, i: (i, 0)), pl.BlockSpec((None, None, K, nsh), lambda d, i: (d, layer, 0, 0))],
        pl.BlockSpec((tm, nsh), lambda d, i: (i, d)), _sds((S, N_DEV * nsh), out_dtype), (a, w),
    )


def _ple_gate_call(name, hb, w, layer, x_res, bias, pp, g_next):
    S, D = x_res.shape
    ksh = w.shape[2]
    tm = _tile(S, FULL_ROW_TILE)
    has_g = g_next is not None

    def body(*refs):
        a_ref, w_ref, x_ref, b_ref, p_ref = refs[:5]
        g_ref = refs[5] if has_g else None
        outs = refs[5 + int(has_g):-1]
        acc_ref = refs[-1]
        k = pl.program_id(1)

        def finish():
            gate = _sig(acc_ref[...] + b_ref[...])
            xn = x_ref[...] + gate * p_ref[...].astype(F32)
            outs[0][...] = xn
            outs[1][...] = gate.astype(BF16)
            if has_g:
                outs[2][...] = _rms(xn, g_ref[...]).astype(BF16)

        _kloop(k, N_DEV, [acc_ref], [_dot(a_ref[...], w_ref[...], "nn")], finish)

    row = pl.BlockSpec((tm, D), lambda i, k: (i, 0))
    vec = pl.BlockSpec((1, D), lambda i, k: (0, 0))
    in_specs = [
        pl.BlockSpec((tm, ksh), lambda i, k: (i, k)),
        pl.BlockSpec((None, None, ksh, D), lambda i, k: (k, layer, 0, 0)),
        row, vec, row,
    ]
    ins = [hb, w, x_res, bias, pp]
    out_specs, out_shape = [row, row], [_sds((S, D), F32), _sds((S, D), BF16)]
    if has_g:
        in_specs.append(vec)
        ins.append(g_next)
        out_specs.append(row)
        out_shape.append(_sds((S, D), BF16))
    res = _call(name, body, (S // tm, N_DEV), in_specs, out_specs, out_shape, ins, scratch=[pltpu.VMEM((tm, D), F32)])
    return res if has_g else (res[0], res[1], None)


def _pool_mix_call(name, x, g):
    S, D = x.shape
    ts = _tile(S, ELT_TILE)
    hb = ts // POOL_HALO
    gc = D // len(POOL_WINDOWS)

    def body(cur_ref, halo_ref, g_ref, o_ref, ext_ref):
        i = pl.program_id(0)
        gain = g_ref[...]
        ext_ref[0:POOL_HALO, :] = jnp.where(i > 0, _rms(halo_ref[...], gain), 0.0)
        ext_ref[POOL_HALO:, :] = _rms(cur_ref[...], gain)
        t = i * ts + lax.broadcasted_iota(jnp.int32, (ts, 1), 0)
        for gi, win in enumerate(POOL_WINDOWS):
            lanes = pl.ds(gi * gc, gc)
            h = ext_ref[pl.ds(POOL_HALO, ts), lanes]
            acc = h
            for j in range(1, win):
                acc = acc + ext_ref[pl.ds(POOL_HALO - j, ts), lanes]
            cnt = jnp.minimum(t + 1, win).astype(F32)
            o_ref[:, gi * gc:(gi + 1) * gc] = (acc / cnt - h).astype(BF16)

    row = pl.BlockSpec((ts, D), lambda i: (i, 0))
    return _call(
        name, body, (S // ts,),
        [row, pl.BlockSpec((POOL_HALO, D), lambda i: (jnp.maximum(i * hb - 1, 0), 0)), pl.BlockSpec((1, D), lambda i: (0, 0))],
        row, _sds((S, D), BF16), (x, x, g), scratch=[pltpu.VMEM((ts + POOL_HALO, D), F32)],
    )


def _pool_out_call(name, mix, wp, scale, x_res, g_next):
    S, D = x_res.shape
    ng, gc, _ = wp.shape
    tm = _tile(S, FULL_ROW_TILE)

    def body(m_ref, w_ref, s_ref, x_ref, g_ref, xo_ref, y_ref, h_ref):
        parts = [_dot(m_ref[:, gi * gc:(gi + 1) * gc], w_ref[gi], "nn") for gi in range(ng)]
        ypre = jnp.concatenate(parts, axis=-1)
        y_ref[...] = ypre.astype(BF16)
        xn = x_ref[...] + ypre * s_ref[...]
        xo_ref[...] = xn
        h_ref[...] = _rms(xn, g_ref[...]).astype(BF16)

    row = pl.BlockSpec((tm, D), lambda i: (i, 0))
    vec = pl.BlockSpec((1, D), lambda i: (0, 0))
    return _call(
        name, body, (S // tm,),
        [row, pl.BlockSpec((ng, gc, gc), lambda i: (0, 0, 0)), vec, row, vec],
        [row, row, row], [_sds((S, D), F32), _sds((S, D), BF16), _sds((S, D), BF16)],
        (mix, wp, scale, x_res, g_next),
    )


def _attn_probs(q_ref, kp_ref, kc_ref, qg_ref, kg_ref, bias_ref, sink_ref, n):
    grp = q_ref.shape[0]
    q = q_ref[...]
    qn = _rms(q, qg_ref[...])
    k = jnp.concatenate([kp_ref[...], kc_ref[...]], axis=0)
    kn = _rms(k, kg_ref[...])
    s = _dot(qn.reshape(grp * Q_BLOCK, HEAD_DIM).astype(BF16), kn.astype(BF16), "nt") * (HEAD_DIM ** -0.5)
    s = s.reshape(grp, Q_BLOCK, 2 * Q_BLOCK) + bias_ref[...]
    qi = lax.broadcasted_iota(jnp.int32, (Q_BLOCK, 2 * Q_BLOCK), 0)
    kj = lax.broadcasted_iota(jnp.int32, (Q_BLOCK, 2 * Q_BLOCK), 1)
    qc = qi // CHUNK
    kc = kj // CHUNK - Q_BLOCK // CHUNK
    ok = (kc <= qc) & (kc >= qc - WINDOW_CHUNKS) & ((n > 0) | (kj >= Q_BLOCK))
    s = jnp.where(ok[None], s, NEG_INF)
    sink = sink_ref[...]
    m = jnp.maximum(jnp.max(s, axis=-1, keepdims=True), sink)
    e = jnp.exp(s - m)
    es = jnp.exp(sink - m)
    inv = 1.0 / (jnp.sum(e, axis=-1, keepdims=True) + es)
    return q, qn, k, kn, e * inv, es * inv


def _attn_specs(grp, nb):
    qspec = pl.BlockSpec((grp, Q_BLOCK, HEAD_DIM), lambda j, n: (j, jnp.minimum(n, nb - 1), 0))
    prev = pl.BlockSpec((None, Q_BLOCK, HEAD_DIM), lambda j, n: (j, jnp.maximum(n - 1, 0), 0))
    cur = pl.BlockSpec((None, Q_BLOCK, HEAD_DIM), lambda j, n: (j, jnp.minimum(n, nb - 1), 0))
    gain = pl.BlockSpec((1, HEAD_DIM), lambda j, n: (0, 0))
    bias = pl.BlockSpec((grp, Q_BLOCK, 2 * Q_BLOCK), lambda j, n: (j, 0, 0))
    sink = pl.BlockSpec((grp, 1, 1), lambda j, n: (j, 0, 0))
    return qspec, prev, cur, gain, bias, sink


def _attn_fwd_call(name, q, k, v, qg, kg, bias, sinks):
    H, S, _ = q.shape
    n_kv = k.shape[0]
    grp = H // n_kv
    nb = S // Q_BLOCK

    def body(q_ref, kp_ref, kc_ref, vp_ref, vc_ref, qg_ref, kg_ref, bias_ref, sink_ref, o_ref):
        n = pl.program_id(1)
        _, _, _, _, p, _ = _attn_probs(q_ref, kp_ref, kc_ref, qg_ref, kg_ref, bias_ref, sink_ref, n)
        vv = jnp.concatenate([vp_ref[...], vc_ref[...]], axis=0).astype(BF16)
        o = _dot(p.reshape(grp * Q_BLOCK, 2 * Q_BLOCK).astype(BF16), vv, "nn")
        o_ref[...] = o.reshape(grp, Q_BLOCK, HEAD_DIM).astype(BF16)

    qspec, prev, cur, gain, bspec, sspec = _attn_specs(grp, nb)
    return _call(
        name, body, (n_kv, nb),
        [qspec, prev, cur, prev, cur, gain, gain, bspec, sspec],
        qspec, _sds((H, S, HEAD_DIM), BF16), (q, k, k, v, v, qg, kg, bias, sinks),
    )


def _loss_call(name, y, target):
    S, D = y.shape
    ts = _tile(S, ELT_TILE)

    def body(y_ref, t_ref, d_ref, db_ref, l_ref):
        err = y_ref[...] - t_ref[...]
        dy = err * (1.0 / D)
        d_ref[...] = dy
        db_ref[...] = dy.astype(BF16)
        part = 0.5 * jnp.sum(jnp.sum(err * err, axis=-1, keepdims=True), axis=0, keepdims=True) * (1.0 / D)
        _accumulate(l_ref, jnp.broadcast_to(part, l_ref.shape), pl.program_id(0) == 0)

    row = pl.BlockSpec((ts, D), lambda i: (i, 0))
    return _call(
        name, body, (S // ts,), [row, row],
        [row, row, pl.BlockSpec((8, 128), lambda i: (0, 0))],
        [_sds((S, D), F32), _sds((S, D), BF16), _sds((8, 128), F32)], (y, target),
    )


def _ple_bwd_elt_call(name, dx, gate, pp):
    S, D = dx.shape
    ts = _tile(S, ELT_TILE)

    def body(dx_ref, gt_ref, p_ref, dz_ref, dp_ref, db_ref):
        d = dx_ref[...]
        gt = gt_ref[...].astype(F32)
        dz = d * p_ref[...].astype(F32) * gt * (1.0 - gt)
        dz_ref[...] = dz.astype(BF16)
        dp_ref[...] = (d * gt).astype(BF16)
        _accumulate(db_ref, jnp.sum(dz, axis=0, keepdims=True), pl.program_id(0) == 0)

    row = pl.BlockSpec((ts, D), lambda i: (i, 0))
    return _call(
        name, body, (S // ts,), [row, row, row],
        [row, row, pl.BlockSpec((1, D), lambda i: (0, 0))],
        [_sds((S, D), BF16), _sds((S, D), BF16), _sds((1, D), F32)], (dx, gate, pp),
    )


def _grad_w_call(name, a, a_mode, b, b_mode, n_out=1):
    bs = b if isinstance(b, (list, tuple)) else [b]
    S = a.shape[-2]
    tk = _tile(S, K_TILE)
    nk = S // tk

    def spec(arr, mode):
        if mode == "full":
            c = arr.shape[-1]
            return pl.BlockSpec((tk, c), lambda d, k: (k, 0)), c
        if mode == "nat":
            c = arr.shape[-1] // N_DEV
            return pl.BlockSpec((tk, c), lambda d, k: (k, d)), c
        c = arr.shape[-1]
        return pl.BlockSpec((None, tk, c), lambda d, k: (d, k, 0)), c

    a_spec, ca = spec(a, a_mode)
    b_specs, cbs = zip(*[spec(x, b_mode) for x in bs])
    nb = len(bs)

    def body(*refs):
        a_ref = refs[0]
        b_refs = refs[1:1 + nb]
        o_refs = refs[1 + nb:1 + 2 * nb]
        acc_refs = refs[1 + 2 * nb:]
        k = pl.program_id(1)
        av = a_ref[...]

        def finish():
            for o, acc in zip(o_refs, acc_refs):
                o[...] = acc[...].astype(BF16)

        _kloop(k, nk, acc_refs, [_dot(av, br[...], "tn") for br in b_refs], finish)

    res = _call(
        name, body, (N_DEV, nk), [a_spec, *b_specs],
        [pl.BlockSpec((None, ca, cb), lambda d, k: (d, 0, 0)) for cb in cbs],
        [_sds((N_DEV, ca, cb), BF16) for cb in cbs], (a, *bs),
        scratch=[pltpu.VMEM((ca, cb), F32) for cb in cbs],
    )
    return res if isinstance(b, (list, tuple)) else res[0]


def _dx_rowsharded_call(name, dy, w, layer, out_dtype):
    S, D = dy.shape
    ksh = w.shape[2]
    tm = _tile(S, ROW_TILE)

    def body(dy_ref, w_ref, o_ref):
        o_ref[...] = _dot(dy_ref[...], w_ref[...], "nt").astype(out_dtype)

    return _call(
        name, body, (N_DEV, S // tm),
        [pl.BlockSpec((tm, D), lambda d, i: (i, 0)), pl.BlockSpec((None, None, ksh, D), lambda d, i: (d, layer, 0, 0))],
        pl.BlockSpec((tm, ksh), lambda d, i: (i, d)), _sds((S, N_DEV * ksh), out_dtype), (dy, w),
    )


def _dx_colsharded_rms_call(name, dys, dy_chunked, ws, layer, x, g, dres, want_colsum):
    S, D = x.shape
    nsh = ws[0].shape[-1]
    tm = _tile(S, FULL_ROW_TILE)
    nt = len(dys)

    def body(*refs):
        dy_refs = refs[:nt]
        w_refs = refs[nt:2 * nt]
        x_ref, g_ref, r_ref = refs[2 * nt:2 * nt + 3]
        outs = refs[2 * nt + 3:-1]
        acc_ref = refs[-1]
        i, k = pl.program_id(0), pl.program_id(1)
        contrib = None
        for dr, wr in zip(dy_refs, w_refs):
            c = _dot(dr[...], wr[...], "nt")
            contrib = c if contrib is None else contrib + c

        def finish():
            dh, dg = _rms_bwd(acc_ref[...], x_ref[...], g_ref[...])
            dx = r_ref[...] + dh
            outs[0][...] = dx
            outs[1][...] = dx.astype(BF16)
            _accumulate(outs[2], dg, i == 0)
            if want_colsum:
                _accumulate(outs[3], jnp.sum(dx, axis=0, keepdims=True), i == 0)

        _kloop(k, N_DEV, [acc_ref], [contrib], finish)

    if dy_chunked:
        dspec = pl.BlockSpec((None, tm, nsh), lambda i, k: (k, i, 0))
    else:
        dspec = pl.BlockSpec((tm, nsh), lambda i, k: (i, k))
    wspec = pl.BlockSpec((None, None, D, nsh), lambda i, k: (k, layer, 0, 0))
    row = pl.BlockSpec((tm, D), lambda i, k: (i, 0))
    vec = pl.BlockSpec((1, D), lambda i, k: (0, 0))
    out_specs = [row, row, vec] + ([vec] if want_colsum else [])
    out_shape = [_sds((S, D), F32), _sds((S, D), BF16), _sds((1, D), F32)] + ([_sds((1, D), F32)] if want_colsum else [])
    res = _call(
        name, body, (S // tm, N_DEV), [dspec] * nt + [wspec] * nt + [row, vec, row],
        out_specs, out_shape, (*dys, *ws, x, g, dres), scratch=[pltpu.VMEM((tm, D), F32)],
    )
    return res if want_colsum else (*res, None)


def _rms_bwd_res_call(name, dh, x, g, dres, want_colsum):
    S, D = x.shape
    ts = _tile(S, ELT_TILE)

    def body(dh_ref, x_ref, g_ref, r_ref, *outs):
        i = pl.program_id(0)
        d, dg = _rms_bwd(dh_ref[...].astype(F32), x_ref[...], g_ref[...])
        dx = r_ref[...] + d
        outs[0][...] = dx
        outs[1][...] = dx.astype(BF16)
        _accumulate(outs[2], dg, i == 0)
        if want_colsum:
            _accumulate(outs[3], jnp.sum(dx, axis=0, keepdims=True), i == 0)

    row = pl.BlockSpec((ts, D), lambda i: (i, 0))
    vec = pl.BlockSpec((1, D), lambda i: (0, 0))
    out_specs = [row, row, vec] + ([vec] if want_colsum else [])
    out_shape = [_sds((S, D), F32), _sds((S, D), BF16), _sds((1, D), F32)] + ([_sds((1, D), F32)] if want_colsum else [])
    res = _call(name, body, (S // ts,), [row, row, vec, row], out_specs, out_shape, (dh, x, g, dres))
    return res if want_colsum else (*res, None)


def _ffn_bwd_hidden_call(name, dyb, w, layer, gpre, upre):
    S, D = dyb.shape
    fsh = w.shape[2]
    tm = _tile(S, ROW_TILE)

    def body(dy_ref, w_ref, g_ref, u_ref, dg_ref, du_ref):
        da = _dot(dy_ref[...], w_ref[...], "nt")
        g = g_ref[...].astype(F32)
        u = u_ref[...].astype(F32)
        s = _sig(g)
        dg_ref[...] = (da * u * s * (1.0 + g * (1.0 - s))).astype(BF16)
        du_ref[...] = (da * g * s).astype(BF16)

    cspec = pl.BlockSpec((None, tm, fsh), lambda d, i: (d, i, 0))
    return _call(
        name, body, (N_DEV, S // tm),
        [pl.BlockSpec((tm, D), lambda d, i: (i, 0)), pl.BlockSpec((None, None, fsh, D), lambda d, i: (d, layer, 0, 0)), cspec, cspec],
        [cspec, cspec], [_sds((N_DEV, S, fsh), BF16)] * 2, (dyb, w, gpre, upre),
    )


def _conv_out_bwd_call(name, dyb, w, layer, u2, ln_g, ln_b):
    S, D = u2.shape
    ksh = w.shape[2]
    tm = _tile(S, ELT_TILE)

    def body(dy_ref, w_ref, u2_ref, g_ref, b_ref, du2_ref, dg_ref, db_ref, dbdw_ref):
        i = pl.program_id(0)
        dy = dy_ref[...]
        du4 = jnp.concatenate([_dot(dy, w_ref[d], "nt") for d in range(N_DEV)], axis=-1)
        u2 = u2_ref[...]
        mu = jnp.mean(u2, axis=-1, keepdims=True)
        xc = u2 - mu
        r = lax.rsqrt(jnp.mean(xc * xc, axis=-1, keepdims=True) + EPS)
        xh = xc * r
        gain = g_ref[...]
        u3 = xh * gain + b_ref[...]
        s = _sig(u3)
        du3 = du4 * s * (1.0 + u3 * (1.0 - s))
        dxh = du3 * gain
        du2 = r * (dxh - jnp.mean(dxh, axis=-1, keepdims=True) - xh * jnp.mean(dxh * xh, axis=-1, keepdims=True))
        du2_ref[...] = du2
        _accumulate(dg_ref, jnp.sum(du3 * xh, axis=0, keepdims=True), i == 0)
        _accumulate(db_ref, jnp.sum(du3, axis=0, keepdims=True), i == 0)
        _accumulate(dbdw_ref, jnp.sum(du2, axis=0, keepdims=True), i == 0)

    row = pl.BlockSpec((tm, D), lambda i: (i, 0))
    vec = pl.BlockSpec((1, D), lambda i: (0, 0))
    return _call(
        name, body, (S // tm,),
        [row, pl.BlockSpec((N_DEV, None, ksh, D), lambda i: (0, layer, 0, 0)), row, vec, vec],
        [row, vec, vec, vec], [_sds((S, D), F32)] + [_sds((1, D), F32)] * 3, (dyb, w, u2, ln_g, ln_b),
    )


def _dwconv_bwd_call(name, du2, u1, w_dw, a_pre, g_pre):
    S, D = u1.shape
    tc = _tile(S, CONV_TILE)
    hb = tc // CONV_HALO
    n_halo = S // CONV_HALO
    assert D % CONV_LANES == 0 and tc % CONV_SUB == 0
    wrows = CONV_HALO

    def body(d_cur, d_next, u_cur, u_prev, w_ref, a_ref, g_ref, du_ref, dbin_ref, dw_ref, dext_ref, uext_ref, du1_ref, dwacc_ref):
        i = pl.program_id(0)
        last = S // tc - 1
        dext_ref[0:tc, :] = d_cur[...]
        dext_ref[tc:, :] = jnp.where(i < last, d_next[...], 0.0)
        uext_ref[0:CONV_HALO, :] = jnp.where(i > 0, u_prev[...], 0.0)
        uext_ref[CONV_HALO:, :] = u_cur[...]

        @pl.when(i == 0)
        def _():
            dwacc_ref[...] = jnp.zeros_like(dwacc_ref)

        first_tap = CONV_HALO - (CONV_WIDTH - 1)
        for r in range(0, tc, CONV_SUB):
            for c in range(0, D, CONV_LANES):
                du1_ref[r:r + CONV_SUB, c:c + CONV_LANES] = _conv_taps(dext_ref, w_ref, r, c, CONV_WIDTH - 1, True)
                dcur = dext_ref[r:r + CONV_SUB, c:c + CONV_LANES]
                for k in range(CONV_WIDTH):
                    prod = dcur * uext_ref[pl.ds(r + first_tap + k, CONV_SUB), pl.ds(c, CONV_LANES)]
                    part = prod[0:8]
                    for q in range(8, CONV_SUB, 8):
                        part = part + prod[q:q + 8]
                    dwacc_ref[k, :, c:c + CONV_LANES] += part

        du1 = du1_ref[...]
        a = a_ref[...].astype(F32)
        sg = _sig(g_ref[...].astype(F32))
        da = du1 * sg
        dgate = du1 * a * sg * (1.0 - sg)
        du_ref[:, 0:D] = da.astype(BF16)
        du_ref[:, D:2 * D] = dgate.astype(BF16)
        _accumulate(dbin_ref, jnp.concatenate([jnp.sum(da, axis=0, keepdims=True), jnp.sum(dgate, axis=0, keepdims=True)], axis=-1), i == 0)

        @pl.when(i == last)
        def _():
            for k in range(CONV_WIDTH):
                dw_ref[k:k + 1, :] = jnp.sum(dwacc_ref[k], axis=0, keepdims=True)
            dw_ref[CONV_WIDTH:, :] = jnp.zeros((wrows - CONV_WIDTH, D), F32)

    row = pl.BlockSpec((tc, D), lambda i: (i, 0))
    nxt = pl.BlockSpec((CONV_HALO, D), lambda i: (jnp.minimum((i + 1) * hb, n_halo - 1), 0))
    prv = pl.BlockSpec((CONV_HALO, D), lambda i: (jnp.maximum(i * hb - 1, 0), 0))
    return _call(
        name, body, (S // tc,),
        [row, nxt, row, prv, pl.BlockSpec((CONV_WIDTH, D), lambda i: (0, 0)), row, row],
        [pl.BlockSpec((tc, 2 * D), lambda i: (i, 0)), pl.BlockSpec((1, 2 * D), lambda i: (0, 0)), pl.BlockSpec((wrows, D), lambda i: (0, 0))],
        [_sds((S, 2 * D), BF16), _sds((1, 2 * D), F32), _sds((wrows, D), F32)],
        (du2, du2, u1, u1, w_dw, a_pre, g_pre),
        scratch=[
            pltpu.VMEM((tc + CONV_HALO, D), F32), pltpu.VMEM((tc + CONV_HALO, D), F32),
            pltpu.VMEM((tc, D), F32), pltpu.VMEM((CONV_WIDTH, 8, D), F32),
        ],
    )


def _pool_out_bwd_call(name, dy, ypre, scale, wp):
    S, D = dy.shape
    ng, gc, _ = wp.shape
    tm = _tile(S, FULL_ROW_TILE)

    def body(dy_ref, y_ref, s_ref, w_ref, dm_ref, dyp_ref, ds_ref):
        dy = dy_ref[...]
        _accumulate(ds_ref, jnp.sum(dy * y_ref[...].astype(F32), axis=0, keepdims=True), pl.program_id(0) == 0)
        dyp = (dy * s_ref[...]).astype(BF16)
        dyp_ref[...] = dyp
        dm_ref[...] = jnp.concatenate([_dot(dyp[:, gi * gc:(gi + 1) * gc], w_ref[gi], "nt") for gi in range(ng)], axis=-1)

    row = pl.BlockSpec((tm, D), lambda i: (i, 0))
    vec = pl.BlockSpec((1, D), lambda i: (0, 0))
    return _call(
        name, body, (S // tm,), [row, row, vec, pl.BlockSpec((ng, gc, gc), lambda i: (0, 0, 0))],
        [row, row, vec], [_sds((S, D), F32), _sds((S, D), BF16), _sds((1, D), F32)], (dy, ypre, scale, wp),
    )


def _pool_w_grad_call(name, mix, dyp, ng):
    S, D = mix.shape
    gc = D // ng
    tk = _tile(S, K_TILE)
    nk = S // tk

    def body(m_ref, d_ref, o_ref, acc_ref):
        def finish():
            o_ref[...] = acc_ref[...]

        _kloop(pl.program_id(1), nk, [acc_ref], [_dot(m_ref[...], d_ref[...], "tn")], finish)

    blk = pl.BlockSpec((tk, gc), lambda g, k: (k, g))
    return _call(
        name, body, (ng, nk), [blk, blk], pl.BlockSpec((None, gc, gc), lambda g, k: (g, 0, 0)),
        _sds((ng, gc, gc), F32), (mix, dyp), scratch=[pltpu.VMEM((gc, gc), F32)],
    )


def _pool_mix_bwd_call(name, dmix, x, g, dres):
    S, D = x.shape
    ts = _tile(S, ELT_TILE)
    hb = ts // POOL_HALO
    n_halo = S // POOL_HALO
    gc = D // len(POOL_WINDOWS)

    def body(cur_ref, nxt_ref, x_ref, g_ref, r_ref, dx_ref, dxb_ref, dg_ref, ext_ref, dh_ref):
        i = pl.program_id(0)
        last = S // ts - 1
        t = i * ts + lax.broadcasted_iota(jnp.int32, (ts + POOL_HALO, 1), 0)
        for gi, win in enumerate(POOL_WINDOWS):
            lanes = slice(gi * gc, (gi + 1) * gc)
            cnt = jnp.minimum(t + 1, win).astype(F32)
            ext_ref[0:ts, lanes] = cur_ref[:, lanes] / cnt[0:ts]
            ext_ref[ts:, lanes] = jnp.where(i < last, nxt_ref[:, lanes] / cnt[ts:], 0.0)
        for gi, win in enumerate(POOL_WINDOWS):
            lanes = pl.ds(gi * gc, gc)
            acc = ext_ref[pl.ds(0, ts), lanes]
            for j in range(1, win):
                acc = acc + ext_ref[pl.ds(j, ts), lanes]
            dh_ref[:, gi * gc:(gi + 1) * gc] = acc - cur_ref[:, gi * gc:(gi + 1) * gc]
        d, dg = _rms_bwd(dh_ref[...], x_ref[...], g_ref[...])
        dx = r_ref[...] + d
        dx_ref[...] = dx
        dxb_ref[...] = dx.astype(BF16)
        _accumulate(dg_ref, dg, i == 0)

    row = pl.BlockSpec((ts, D), lambda i: (i, 0))
    vec = pl.BlockSpec((1, D), lambda i: (0, 0))
    nxt = pl.BlockSpec((POOL_HALO, D), lambda i: (jnp.minimum((i + 1) * hb, n_halo - 1), 0))
    return _call(
        name, body, (S // ts,), [row, nxt, row, vec, row], [row, row, vec],
        [_sds((S, D), F32), _sds((S, D), BF16), _sds((1, D), F32)], (dmix, dmix, x, g, dres),
        scratch=[pltpu.VMEM((ts + POOL_HALO, D), F32), pltpu.VMEM((ts, D), F32)],
    )


def _attn_bwd_call(name, q, k, v, do, qg, kg, bias, sinks):
    H, S, _ = q.shape
    n_kv = k.shape[0]
    grp = H // n_kv
    nb = S // Q_BLOCK
    scale = HEAD_DIM ** -0.5

    def body(q_ref, kp_ref, kc_ref, vp_ref, vc_ref, do_ref, qg_ref, kg_ref, bias_ref, sink_ref,
             dq_ref, dk_ref, dv_ref, dqg_ref, dkg_ref, dbias_ref, dsink_ref, ck_ref, cv_ref):
        n = pl.program_id(1)

        @pl.when(n == 0)
        def _():
            dqg_ref[...] = jnp.zeros_like(dqg_ref)
            dkg_ref[...] = jnp.zeros_like(dkg_ref)
            dbias_ref[...] = jnp.zeros_like(dbias_ref)
            dsink_ref[...] = jnp.zeros_like(dsink_ref)
            ck_ref[...] = jnp.zeros_like(ck_ref)
            cv_ref[...] = jnp.zeros_like(cv_ref)

        def block_grads():
            q, qn, _, kn, p, ps = _attn_probs(q_ref, kp_ref, kc_ref, qg_ref, kg_ref, bias_ref, sink_ref, n)
            rows = grp * Q_BLOCK
            dob = do_ref[...].reshape(rows, HEAD_DIM).astype(BF16)
            vv = jnp.concatenate([vp_ref[...], vc_ref[...]], axis=0).astype(BF16)
            dp = _dot(dob, vv, "nt").reshape(grp, Q_BLOCK, 2 * Q_BLOCK)
            delta = jnp.sum(p * dp, axis=-1, keepdims=True)
            dl = p * (dp - delta)
            dbias_ref[...] += dl
            dsink_ref[...] += jnp.sum(-ps * delta, axis=1, keepdims=True)
            dlb = dl.reshape(rows, 2 * Q_BLOCK).astype(BF16)
            dqn = (_dot(dlb, kn.astype(BF16), "nn") * scale).reshape(grp, Q_BLOCK, HEAD_DIM)
            dkn = _dot(dlb, qn.reshape(rows, HEAD_DIM).astype(BF16), "tn") * scale
            dvv = _dot(p.reshape(rows, 2 * Q_BLOCK).astype(BF16), dob, "tn")
            qgain = qg_ref[...]
            r = lax.rsqrt(jnp.mean(q * q, axis=-1, keepdims=True) + EPS)
            qh = q * r
            dqg_ref[...] += jnp.sum(jnp.sum(dqn * qh, axis=1), axis=0, keepdims=True)
            dqh = dqn * qgain
            dq_ref[...] = r * (dqh - qh * jnp.mean(dqh * qh, axis=-1, keepdims=True))
            return dkn, dvv

        def finish_prev(dkn_prev, dv_prev):
            kraw = kp_ref[...]
            dk, dkg = _rms_bwd(dkn_prev, kraw, kg_ref[...])
            dk_ref[...] = dk
            dv_ref[...] = dv_prev
            dkg_ref[...] += dkg

        @pl.when(n < nb)
        def _():
            dkn, dvv = block_grads()

            @pl.when(n > 0)
            def _():
                finish_prev(ck_ref[...] + dkn[0:Q_BLOCK], cv_ref[...] + dvv[0:Q_BLOCK])

            ck_ref[...] = dkn[Q_BLOCK:]
            cv_ref[...] = dvv[Q_BLOCK:]

        @pl.when(n == nb)
        def _():
            finish_prev(ck_ref[...], cv_ref[...])

    qspec, prev, cur, gain, bspec, sspec = _attn_specs(grp, nb)
    kout = pl.BlockSpec((None, Q_BLOCK, HEAD_DIM), lambda j, n: (j, jnp.maximum(n - 1, 0), 0))
    gout = pl.BlockSpec((None, 1, HEAD_DIM), lambda j, n: (j, 0, 0))
    return _call(
        name, body, (n_kv, nb + 1),
        [qspec, prev, cur, prev, cur, qspec, gain, gain, bspec, sspec],
        [qspec, kout, kout, gout, gout, bspec, sspec],
        [
            _sds((H, S, HEAD_DIM), F32), _sds((n_kv, S, HEAD_DIM), F32), _sds((n_kv, S, HEAD_DIM), F32),
            _sds((n_kv, 1, HEAD_DIM), F32), _sds((n_kv, 1, HEAD_DIM), F32),
            _sds((H, Q_BLOCK, 2 * Q_BLOCK), F32), _sds((H, 1, 1), F32),
        ],
        (q, k, k, v, v, do, qg, kg, bias, sinks),
        scratch=[pltpu.VMEM((Q_BLOCK, HEAD_DIM), F32), pltpu.VMEM((Q_BLOCK, HEAD_DIM), F32)],
    )


def _bucket_sum_call(name, onehot, dbias):
    nbk, n = onehot.shape
    H = dbias.shape[0]

    def body(o_ref, d_ref, out_ref):
        out_ref[...] = lax.dot_general(o_ref[...], d_ref[...], _DIMS["nt"], precision=lax.Precision.HIGHEST, preferred_element_type=F32)

    return _call(
        name, body, (1,), [pl.BlockSpec((nbk, n), lambda i: (0, 0)), pl.BlockSpec((H, n), lambda i: (0, 0))],
        pl.BlockSpec((nbk, H), lambda i: (0, 0)), _sds((nbk, H), F32), (onehot, dbias),
    )


def _mesh_pos():
    return lax.axis_index("x"), lax.axis_index("y"), lax.axis_index("c")


def _slot(px, py, pc):
    return 4 * px + 2 * py + pc


def _peers(x, y, c):
    flips = [(fx, fy, fc) for fx in (0, 1) for fy in (0, 1) for fc in (0, 1)][1:]
    return [(1 - x if fx else x, 1 - y if fy else y, 1 - c if fc else c) for fx, fy, fc in flips]


def _all_gather_call(name, shards):
    n = len(shards)

    def body(*refs):
        in_refs, out_refs = refs[:n], refs[n:2 * n]
        send_sems, recv_sems, local_sems = refs[2 * n:]
        x, y, c = _mesh_pos()
        me, sibling = (x, y, c), (x, y, 1 - c)
        chips = [(1 - x, y), (x, 1 - y), (1 - x, 1 - y)]

        def copy(t, k, block, to, src=None):
            dst = out_refs[t].at[_slot(*block)]
            return pltpu.make_async_remote_copy(
                src_ref=dst if src is None else src, dst_ref=dst,
                send_sem=send_sems.at[7 * t + k], recv_sem=recv_sems.at[7 * t + k],
                device_id=to, device_id_type=pl.DeviceIdType.MESH,
            )

        mine = [pltpu.make_async_copy(in_refs[t], out_refs[t].at[_slot(*me)], local_sems.at[t]) for t in range(n)]
        for cp in mine:
            cp.start()
        first = []
        for t in range(n):
            first.append(copy(t, 0, me, sibling, src=in_refs[t]))
            first += [copy(t, 1 + j, me, (*chip, c), src=in_refs[t]) for j, chip in enumerate(chips)]
        for cp in first:
            cp.start()
        passed = []
        for j, chip in enumerate(chips):
            for t in range(n):
                copy(t, 1 + j, (*chip, c), me).wait_recv()
                fwd = copy(t, 4 + j, (*chip, c), sibling)
                fwd.start()
                passed.append(fwd)
        for t in range(n):
            copy(t, 0, sibling, me).wait_recv()
            for j, chip in enumerate(chips):
                copy(t, 4 + j, (*chip, 1 - c), me).wait_recv()
        for cp in first + passed:
            cp.wait_send()
        for cp in mine:
            cp.wait()

    hbm = pl.BlockSpec(memory_space=pltpu.HBM)
    return pl.pallas_call(
        body, name=name,
        in_specs=[hbm] * n, out_specs=[hbm] * n,
        out_shape=[_sds((N_DEV, *s.shape), s.dtype) for s in shards],
        scratch_shapes=[pltpu.SemaphoreType.DMA((7 * n,)), pltpu.SemaphoreType.DMA((7 * n,)), pltpu.SemaphoreType.DMA((n,))],
        compiler_params=pltpu.CompilerParams(has_side_effects=True),
    )(*shards)


def _scatter_call(name, grads):
    n = len(grads)

    def body(*refs):
        in_refs, out_refs = refs[:n], refs[n:2 * n]
        send_sems, recv_sems, local_sems = refs[2 * n:]
        x, y, c = _mesh_pos()
        me = _slot(x, y, c)
        peers = _peers(x, y, c)

        def copy(t, k, to):
            return pltpu.make_async_remote_copy(
                src_ref=in_refs[t].at[_slot(*to)], dst_ref=out_refs[t].at[me],
                send_sem=send_sems.at[7 * t + k], recv_sem=recv_sems.at[7 * t + k],
                device_id=to, device_id_type=pl.DeviceIdType.MESH,
            )

        mine = [pltpu.make_async_copy(in_refs[t].at[me], out_refs[t].at[me], local_sems.at[t]) for t in range(n)]
        for cp in mine:
            cp.start()
        sends = [copy(t, k, to) for t in range(n) for k, to in enumerate(peers)]
        for cp in sends:
            cp.start()
        for t in range(n):
            for k, frm in enumerate(peers):
                pltpu.make_async_remote_copy(
                    src_ref=in_refs[t].at[me], dst_ref=out_refs[t].at[_slot(*frm)],
                    send_sem=send_sems.at[7 * t + k], recv_sem=recv_sems.at[7 * t + k],
                    device_id=frm, device_id_type=pl.DeviceIdType.MESH,
                ).wait_recv()
        for cp in sends:
            cp.wait_send()
        for cp in mine:
            cp.wait()

    hbm = pl.BlockSpec(memory_space=pltpu.HBM)
    return pl.pallas_call(
        body, name=name,
        in_specs=[hbm] * n, out_specs=[hbm] * n,
        out_shape=[_sds(g.shape, g.dtype) for g in grads],
        scratch_shapes=[pltpu.SemaphoreType.DMA((7 * n,)), pltpu.SemaphoreType.DMA((7 * n,)), pltpu.SemaphoreType.DMA((n,))],
        compiler_params=pltpu.CompilerParams(has_side_effects=True),
    )(*grads)


def _all_reduce_small_call(name, pack):
    R, C = pack.shape

    def body(in_ref, out_ref, land_ref, send_sems, recv_sems):
        x, y, c = _mesh_pos()
        me = _slot(x, y, c)
        peers = _peers(x, y, c)
        land_ref[me] = in_ref[...]
        sends = [
            pltpu.make_async_remote_copy(
                src_ref=in_ref, dst_ref=land_ref.at[me], send_sem=send_sems.at[k], recv_sem=recv_sems.at[k],
                device_id=to, device_id_type=pl.DeviceIdType.MESH,
            )
            for k, to in enumerate(peers)
        ]
        for cp in sends:
            cp.start()
        for k, frm in enumerate(peers):
            pltpu.make_async_remote_copy(
                src_ref=in_ref, dst_ref=land_ref.at[_slot(*frm)], send_sem=send_sems.at[k], recv_sem=recv_sems.at[k],
                device_id=frm, device_id_type=pl.DeviceIdType.MESH,
            ).wait_recv()
        for cp in sends:
            cp.wait_send()
        total = land_ref[0]
        for d in range(1, N_DEV):
            total = total + land_ref[d]
        out_ref[...] = total

    vmem = pl.BlockSpec(memory_space=pltpu.VMEM)
    return pl.pallas_call(
        body, name=name, in_specs=[vmem], out_specs=vmem, out_shape=_sds((R, C), F32),
        scratch_shapes=[pltpu.VMEM((N_DEV, R, C), F32), pltpu.SemaphoreType.DMA((7,)), pltpu.SemaphoreType.DMA((7,))],
        compiler_params=pltpu.CompilerParams(has_side_effects=True, vmem_limit_bytes=VMEM_LIMIT_BYTES),
    )(pack)


def _adamw_call(name, grad, landed, w, m, v):
    R, C = w.shape
    tr = _div_tile(R, ADAM_BLOCK_BYTES // (C * 4))
    c1 = 1.0 / (1.0 - ADAM_B1 ** ADAM_STEP)
    c2 = 1.0 / (1.0 - ADAM_B2 ** ADAM_STEP)

    def body(g_ref, w_ref, m_ref, v_ref, go_ref, d_ref, mo_ref, vo_ref):
        if landed:
            g = g_ref[0].astype(F32)
            for d in range(1, N_DEV):
                g = g + g_ref[d].astype(F32)
        else:
            g = g_ref[...]
        go_ref[...] = g
        mn = ADAM_B1 * m_ref[...] + (1.0 - ADAM_B1) * g
        vn = ADAM_B2 * v_ref[...] + (1.0 - ADAM_B2) * (g * g)
        mo_ref[...] = mn
        vo_ref[...] = vn
        d_ref[...] = -ADAM_LR * ((mn * c1) / (jnp.sqrt(vn * c2) + ADAM_EPS) + ADAM_WD * w_ref[...])

    blk = pl.BlockSpec((tr, C), lambda i: (i, 0))
    gspec = pl.BlockSpec((N_DEV, tr, C), lambda i: (0, i, 0)) if landed else blk
    return _call(name, body, (R // tr,), [gspec, blk, blk, blk], [blk] * 4, [_sds((R, C), F32)] * 4, (grad, w, m, v))


def _t5_bucket(rel):
    nb = NUM_BUCKETS // 2
    n = -rel
    ret = jnp.where(n < 0, nb, 0)
    n = jnp.abs(n)
    max_exact = nb // 2
    nf = jnp.maximum(n, 1).astype(jnp.float32)
    large = max_exact + (jnp.log(nf / max_exact) / math.log(REL_MAX_DIST / max_exact) * (nb - max_exact)).astype(jnp.int32)
    large = jnp.minimum(large, nb - 1)
    return ret + jnp.where(n < max_exact, n, large)


def _band_buckets():
    i = jnp.arange(Q_BLOCK)[:, None]
    j = jnp.arange(2 * Q_BLOCK)[None, :]
    return _t5_bucket(j - Q_BLOCK - i)


def _to_heads(t, n_heads):
    S = t.shape[0]
    return t.reshape(S, n_heads, HEAD_DIM).transpose(1, 0, 2)


def _from_heads(t):
    H, S, _ = t.shape
    return t.transpose(1, 0, 2).reshape(S, H * HEAD_DIM)


def _gathered_vec(t):
    nd, L, n = t.shape
    return t.transpose(1, 0, 2).reshape(L, nd * n)


def _local_step(x, p, target, W, V):
    S, D = x.shape
    depth = V["norm_mix"].shape[0]
    n_heads = D // HEAD_DIM
    n_kv = (W["attn_w_qkv"].shape[-1] * N_DEV - D) // (2 * HEAD_DIM)
    ng = len(POOL_WINDOWS)
    gc = D // ng
    vec = lambda t, i: t[i][None, :]

    buckets = _band_buckets()
    bias_tab = jnp.transpose(V["rel_bias"][buckets], (2, 0, 1))
    sinks3 = V["attn_sinks"].reshape(n_heads, 1, 1)
    wp = W["pool_w"]
    pb = p.astype(BF16)

    saved = []
    hb = _rms_call("rms_in", x, vec(V["norm_mix"], 0))
    for i in range(depth):
        kind, j = i % 3, i // 3
        sv = {"x0": x, "h0": hb}
        g_ffn = vec(V["norm_ffn"], i)
        if kind == 0:
            a_pre, g_pre, u1 = _conv_in_call(f"conv_in{i}", hb, W["conv_w_in"], j, vec(V["conv_b_in"], j))
            u2, u4 = _dwconv_call(f"dwconv{i}", u1, V["conv_w_dw"][j], vec(V["conv_b_dw"], j), vec(V["conv_ln_g"], j), vec(V["conv_ln_b"], j))
            x1, h2 = _rowmm_res_call(f"conv_out{i}", u4, False, W["conv_w_out"], j, x, vec(V["conv_b_out"], j), g_ffn)
            sv.update(a_pre=a_pre, g_pre=g_pre, u1=u1, u2=u2, u4=u4)
        elif kind == 1:
            mix = _pool_mix_call(f"pool_mix{i}", x, vec(V["norm_mix"], i))
            x1, ypre, h2 = _pool_out_call(f"pool_out{i}", mix, wp, vec(V["pool_scale"], j), x, g_ffn)
            sv.update(mix=mix, ypre=ypre)
        else:
            qkv = _colmm_call(f"qkv{i}", hb, W["attn_w_qkv"], j, F32)
            q = _to_heads(qkv[:, :D], n_heads)
            k = _to_heads(qkv[:, D:D + n_kv * HEAD_DIM], n_kv)
            v = _to_heads(qkv[:, D + n_kv * HEAD_DIM:], n_kv)
            o = _attn_fwd_call(f"attn{i}", q, k, v, vec(V["attn_q_norm"], j), vec(V["attn_k_norm"], j), bias_tab, sinks3)
            ob = _from_heads(o)
            x1, h2 = _rowmm_res_call(f"attn_out{i}", ob, False, W["attn_w_o"], j, x, None, g_ffn)
            sv.update(q=q, k=k, v=v, ob=ob)
        gpre, upre, act = _ffn_up_call(f"ffn_up{i}", h2, W["ffn_w_gate"], W["ffn_w_up"], i)
        x2, h3 = _rowmm_res_call(f"ffn_down{i}", act, True, W["ffn_w_down"], i, x1, None, vec(V["norm_ple"], i))
        pp = _colmm_call(f"ple_proj{i}", pb[i], W["ple_w_proj"], i, BF16)
        g_next = vec(V["norm_mix"], i + 1) if i + 1 < depth else None
        x3, gate, hb_next = _ple_gate_call(f"ple_gate{i}", h3, W["ple_w_gate"], i, x2, vec(V["ple_b_gate"], i), pp, g_next)
        sv.update(x1=x1, h2=h2, gpre=gpre, upre=upre, act=act, x2=x2, h3=h3, pp=pp, gate=gate)
        saved.append(sv)
        x, hb = x3, hb_next

    dx, dxb, loss_tile = _loss_call("loss", x, target)
    loss = loss_tile[0, 0]

    GW = {n: [None] * W[n].shape[1] for n in W if n != "pool_w"}
    GV = {n: [None] * V[n].shape[0] for n in ("norm_mix", "norm_ffn", "norm_ple", "ple_b_gate", "conv_b_in", "conv_w_dw", "conv_b_dw", "conv_ln_g", "conv_ln_b", "conv_b_out", "pool_scale")}
    for i in reversed(range(depth)):
        kind, j = i % 3, i // 3
        sv = saved[i]
        dz, dpp, db_gate = _ple_bwd_elt_call(f"ple_bwd{i}", dx, sv["gate"], sv["pp"])
        GV["ple_b_gate"][i] = db_gate
        GW["ple_w_gate"][i] = _grad_w_call(f"g_ple_gate{i}", sv["h3"], "nat", dz, "full")
        GW["ple_w_proj"][i] = _grad_w_call(f"g_ple_proj{i}", pb[i], "full", dpp, "nat")
        dh3 = _dx_rowsharded_call(f"d_h3_{i}", dz, W["ple_w_gate"], i, BF16)
        dx, dxb, dg, _ = _rms_bwd_res_call(f"d_x2_{i}", dh3, sv["x2"], vec(V["norm_ple"], i), dx, False)
        GV["norm_ple"][i] = dg
        dgp, dup = _ffn_bwd_hidden_call(f"ffn_bwd{i}", dxb, W["ffn_w_down"], i, sv["gpre"], sv["upre"])
        GW["ffn_w_down"][i] = _grad_w_call(f"g_ffn_down{i}", sv["act"], "chunk", dxb, "full")
        GW["ffn_w_gate"][i], GW["ffn_w_up"][i] = _grad_w_call(f"g_ffn_up{i}", sv["h2"], "full", [dgp, dup], "chunk")
        dx, dxb, dg, colsum = _dx_colsharded_rms_call(
            f"d_x1_{i}", [dgp, dup], True, [W["ffn_w_gate"], W["ffn_w_up"]], i, sv["x1"], vec(V["norm_ffn"], i), dx, kind == 0)
        GV["norm_ffn"][i] = dg
        g_mix = vec(V["norm_mix"], i)
        if kind == 0:
            GV["conv_b_out"][j] = colsum
            GW["conv_w_out"][j] = _grad_w_call(f"g_conv_out{i}", sv["u4"], "nat", dxb, "full")
            du2, d_ln_g, d_ln_b, d_b_dw = _conv_out_bwd_call(f"conv_out_bwd{i}", dxb, W["conv_w_out"], j, sv["u2"], vec(V["conv_ln_g"], j), vec(V["conv_ln_b"], j))
            du, d_b_in, d_w_dw = _dwconv_bwd_call(f"dwconv_bwd{i}", du2, sv["u1"], V["conv_w_dw"][j], sv["a_pre"], sv["g_pre"])
            GV["conv_ln_g"][j], GV["conv_ln_b"][j], GV["conv_b_dw"][j] = d_ln_g, d_ln_b, d_b_dw
            GV["conv_b_in"][j], GV["conv_w_dw"][j] = d_b_in, d_w_dw[:CONV_WIDTH]
            GW["conv_w_in"][j] = _grad_w_call(f"g_conv_in{i}", sv["h0"], "full", du, "nat")
            dx, dxb, dg, _ = _dx_colsharded_rms_call(f"d_x0_{i}", [du], False, [W["conv_w_in"]], j, sv["x0"], g_mix, dx, False)
        elif kind == 1:
            dmix, dyp, d_scale = _pool_out_bwd_call(f"pool_out_bwd{i}", dx, sv["ypre"], vec(V["pool_scale"], j), wp)
            GV["pool_scale"][j] = d_scale
            GW.setdefault("pool_w", [None])[j] = _pool_w_grad_call(f"g_pool_w{i}", sv["mix"], dyp, ng)
            dx, dxb, dg = _pool_mix_bwd_call(f"pool_mix_bwd{i}", dmix, sv["x0"], g_mix, dx)
        else:
            GW["attn_w_o"][j] = _grad_w_call(f"g_attn_o{i}", sv["ob"], "nat", dxb, "full")
            do = _to_heads(_dx_rowsharded_call(f"d_attn_o{i}", dxb, W["attn_w_o"], j, F32), n_heads)
            dq, dk, dv, dqg, dkg, dbias, dsink = _attn_bwd_call(
                f"attn_bwd{i}", sv["q"], sv["k"], sv["v"], do, vec(V["attn_q_norm"], j), vec(V["attn_k_norm"], j), bias_tab, sinks3)
            GV["attn_q_norm"] = jnp.sum(dqg, axis=0)
            GV["attn_k_norm"] = jnp.sum(dkg, axis=0)
            GV["attn_sinks"] = dsink.reshape(1, n_heads)
            onehot = (buckets.reshape(1, -1) == jnp.arange(NUM_BUCKETS)[:, None]).astype(F32)
            GV["rel_bias"] = _bucket_sum_call(f"g_rel_bias{i}", onehot, dbias.reshape(n_heads, -1))
            dqkv = jnp.concatenate([_from_heads(dq), _from_heads(dk), _from_heads(dv)], axis=-1).astype(BF16)
            GW["attn_w_qkv"][j] = _grad_w_call(f"g_qkv{i}", sv["h0"], "full", dqkv, "nat")
            dx, dxb, dg, _ = _dx_colsharded_rms_call(f"d_x0_{i}", [dqkv], False, [W["attn_w_qkv"]], j, sv["x0"], g_mix, dx, False)
        GV["norm_mix"][i] = dg
    return loss, dx, GW, GV


_BIG = ("conv_w_in", "conv_w_out", "pool_w", "attn_w_qkv", "attn_w_o", "ffn_w_gate", "ffn_w_up", "ffn_w_down", "ple_w_proj", "ple_w_gate")
_SMALL_SHARDED = ("conv_b_in", "conv_w_dw", "conv_b_dw", "conv_ln_g", "conv_ln_b", "conv_b_out")
_SMALL_REPLICATED = ("norm_mix", "norm_ffn", "norm_ple", "pool_scale", "attn_q_norm", "attn_k_norm", "attn_sinks", "rel_bias", "ple_b_gate")
_WEIGHTS = ("norm_mix", "norm_ffn", "norm_ple", "conv_w_in", "conv_b_in", "conv_w_dw", "conv_b_dw", "conv_ln_g", "conv_ln_b", "conv_w_out",
            "conv_b_out", "pool_w", "pool_scale", "attn_w_qkv", "attn_q_norm", "attn_k_norm", "attn_sinks", "attn_w_o", "rel_bias",
            "ffn_w_gate", "ffn_w_up", "ffn_w_down", "ple_w_proj", "ple_w_gate", "ple_b_gate")
PACK_LANES = 128


def _pack(parts):
    flat = jnp.concatenate([t.reshape(-1).astype(F32) for t in parts])
    rows = -(-flat.shape[0] // (8 * PACK_LANES)) * 8
    flat = jnp.pad(flat, (0, rows * PACK_LANES - flat.shape[0]))
    return flat.reshape(rows, PACK_LANES)


def _unpack(pack, shapes, lead=()):
    flat = pack.reshape(*lead, -1)
    out, pos = [], 0
    for s in shapes:
        n = math.prod(s)
        out.append(flat[..., pos:pos + n].reshape(*lead, *s))
        pos += n
    return out


def _as2d(t):
    return t.reshape(-1, t.shape[-1])


def kernel(x, p, norm_mix, norm_ffn, norm_ple, conv_w_in, conv_b_in, conv_w_dw, conv_b_dw, conv_ln_g, conv_ln_b, conv_w_out, conv_b_out, pool_w, pool_scale, attn_w_qkv, attn_q_norm, attn_k_norm, attn_sinks, attn_w_o, rel_bias, ffn_w_gate, ffn_w_up, ffn_w_down, ple_w_proj, ple_w_gate, ple_b_gate, loss_target, m_norm_mix, m_norm_ffn, m_norm_ple, m_conv_w_in, m_conv_b_in, m_conv_w_dw, m_conv_b_dw, m_conv_ln_g, m_conv_ln_b, m_conv_w_out, m_conv_b_out, m_pool_w, m_pool_scale, m_attn_w_qkv, m_attn_q_norm, m_attn_k_norm, m_attn_sinks, m_attn_w_o, m_rel_bias, m_ffn_w_gate, m_ffn_w_up, m_ffn_w_down, m_ple_w_proj, m_ple_w_gate, m_ple_b_gate, v_norm_mix, v_norm_ffn, v_norm_ple, v_conv_w_in, v_conv_b_in, v_conv_w_dw, v_conv_b_dw, v_conv_ln_g, v_conv_ln_b, v_conv_w_out, v_conv_b_out, v_pool_w, v_pool_scale, v_attn_w_qkv, v_attn_q_norm, v_attn_k_norm, v_attn_sinks, v_attn_w_o, v_rel_bias, v_ffn_w_gate, v_ffn_w_up, v_ffn_w_down, v_ple_w_proj, v_ple_w_gate, v_ple_b_gate):
    given = dict(locals())
    w = {n: given[n] for n in _WEIGHTS}
    m = {n: given["m_" + n] for n in _WEIGHTS}
    v = {n: given["v_" + n] for n in _WEIGHTS}
    me = _slot(*_mesh_pos())

    small_shapes = [w[n].shape for n in _SMALL_SHARDED]
    gathered = _all_gather_call("all_gather", [w[n].astype(BF16) for n in _BIG] + [_pack([w[n] for n in _SMALL_SHARDED])])
    W = dict(zip(_BIG, gathered[:-1]))
    pw = W["pool_w"]
    W["pool_w"] = pw[:, 0].transpose(1, 0, 2, 3).reshape(pw.shape[2], pw.shape[3] * N_DEV, pw.shape[4])
    V = {n: w[n] for n in _SMALL_REPLICATED}
    for n, t in zip(_SMALL_SHARDED, _unpack(gathered[-1], small_shapes, lead=(N_DEV,))):
        if n == "conv_w_dw":
            V[n] = t.transpose(1, 2, 0, 3).reshape(t.shape[1], t.shape[2], -1)
        else:
            V[n] = _gathered_vec(t)

    loss, grad_x, GW, GV = _local_step(x[0], p[:, 0], loss_target[0], W, V)
    loss = lax.psum(loss, ("x", "y", "c"))

    small_names = list(_SMALL_REPLICATED) + list(_SMALL_SHARDED)
    small_full = []
    for n in small_names:
        g = GV[n]
        g = jnp.stack([t.reshape(V[n].shape[1:]) for t in g]) if isinstance(g, list) else g.reshape(V[n].shape)
        small_full.append(g)
    reduced = _unpack(_all_reduce_small_call("all_reduce_small", _pack(small_full)), [t.shape for t in small_full])
    small_grad = {}
    for n, g in zip(small_names, reduced):
        if n in _SMALL_SHARDED:
            c = w[n].shape[-1]
            g = lax.dynamic_slice_in_dim(g, me * c, c, axis=g.ndim - 1)
        small_grad[n] = g

    pool_g = GW.pop("pool_w")[0]
    ng, gc, _ = pool_g.shape
    big_names = [n for n in _BIG if n != "pool_w"]
    send = [t for n in big_names for t in GW[n]]
    send.append(pool_g.reshape(ng, N_DEV, gc // N_DEV, gc).transpose(1, 0, 2, 3).astype(BF16))
    landed = _scatter_call("grad_scatter", send)

    out = {}
    pos = 0
    for n in big_names + ["pool_w"]:
        L = 1 if n == "pool_w" else len(GW[n])
        per_layer = []
        for l in range(L):
            land = landed[pos]
            pos += 1
            shard = (w[n][0] if n == "pool_w" else w[n][l])
            r2 = lambda t: _as2d(t[0] if n == "pool_w" else t[l])
            res = _adamw_call(f"adamw_{n}{l}", land.reshape(N_DEV, -1, land.shape[-1]), True, r2(w[n]), r2(m[n]), r2(v[n]))
            per_layer.append([t.reshape(shard.shape) for t in res])
        out[n] = [jnp.stack([pl_[q] for pl_ in per_layer]).reshape(w[n].shape) for q in range(4)]
    for n in small_names:
        res = _adamw_call(f"adamw_{n}", _as2d(small_grad[n]), False, _as2d(w[n]), _as2d(m[n]), _as2d(v[n]))
        out[n] = [t.reshape(w[n].shape) for t in res]

    grads = [out[n][0] for n in _WEIGHTS]
    deltas = [out[n][1] for n in _WEIGHTS]
    new_m = [out[n][2] for n in _WEIGHTS]
    new_v = [out[n][3] for n in _WEIGHTS]
    return (loss, grad_x[None], *grads, *deltas, *new_m, *new_v)
```

```python
import functools
import math

import jax
import jax.numpy as jnp
from jax import lax
from jax.experimental import pallas as pl
from jax.experimental.pallas import tpu as pltpu

F32, BF16 = jnp.float32, jnp.bfloat16
N_DEV = 8
EPS = 1e-6
NEG_INF = -1e30
HEAD_DIM = 64
Q_BLOCK = 128
CHUNK = 64
WINDOW_CHUNKS = 2
CONV_WIDTH = 31
CONV_HALO = 32
POOL_WINDOWS = (2, 4, 8, 16)
POOL_HALO = 16
NUM_BUCKETS = 32
REL_MAX_DIST = 128
ADAM_LR, ADAM_B1, ADAM_B2, ADAM_EPS, ADAM_WD, ADAM_STEP = 0.001, 0.9, 0.999, 1e-08, 0.01, 10
VMEM_LIMIT_BYTES = 44 * 1024 * 1024
BIG_VMEM_LIMIT_BYTES = 52 * 1024 * 1024
ROW_TILE = 512
FULL_ROW_TILE = 256
EPILOGUE_ROWS = 128
ELT_TILE = 256
CONV_TILE = 128
CONV_SUB = 32
CONV_LANES = 512
K_TILE = 512
ADAM_BLOCK_BYTES = 1 << 20

_DIMS = {
    "nn": (((1,), (0,)), ((), ())),
    "nt": (((1,), (1,)), ((), ())),
    "tn": (((0,), (0,)), ((), ())),
}


def _dot(a, b, mode):
    return lax.dot_general(a, b, _DIMS[mode], preferred_element_type=F32)


def _tile(n, t):
    t = min(n, t)
    assert n % t == 0, (n, t)
    return t


def _div_tile(n, t):
    if n <= t:
        return n
    for cand in range(t // 16 * 16, 15, -16):
        if n % cand == 0:
            return cand
    return n


def _sds(shape, dtype):
    return jax.ShapeDtypeStruct(tuple(shape), dtype)


def _call(name, body, grid, in_specs, out_specs, out_shape, ins, scratch=(), comm=None, vmem=VMEM_LIMIT_BYTES):
    params = pltpu.CompilerParams(dimension_semantics=("arbitrary",) * len(grid), vmem_limit_bytes=vmem)
    if comm is None:
        return pl.pallas_call(
            body, name=name, grid=grid, in_specs=list(in_specs), out_specs=out_specs, out_shape=out_shape,
            scratch_shapes=list(scratch), compiler_params=params,
        )(*ins)
    single = not isinstance(out_shape, (list, tuple))
    own_specs = [out_specs] if single else list(out_specs)
    own_shape = [out_shape] if single else list(out_shape)
    n_in, n_out, n_scr = len(ins), len(own_shape), len(scratch)
    n_src, n_dst = len(comm.srcs), len(comm.out_shape)
    hbm = pl.BlockSpec(memory_space=pltpu.HBM)

    def with_comm(*refs):
        a = n_in
        b = a + n_src
        c = b + n_out
        d = c + n_dst
        e = d + n_scr
        src_refs, dst_refs, sem_refs = refs[a:b], refs[c:d], refs[e:]
        first = functools.reduce(jnp.logical_and, [pl.program_id(i) == 0 for i in range(len(grid))])
        last = functools.reduce(jnp.logical_and, [pl.program_id(i) == grid[i] - 1 for i in range(len(grid))])

        @pl.when(first)
        def _():
            comm.start(src_refs, dst_refs, sem_refs)

        body(*refs[:a], *refs[b:c], *refs[d:e])

        @pl.when(last)
        def _():
            comm.wait(src_refs, dst_refs, sem_refs)

    res = pl.pallas_call(
        with_comm, name=name, grid=grid,
        in_specs=list(in_specs) + [hbm] * n_src,
        out_specs=own_specs + [hbm] * n_dst,
        out_shape=own_shape + list(comm.out_shape),
        scratch_shapes=list(scratch) + list(comm.sems),
        compiler_params=params,
    )(*ins, *comm.srcs)
    own = res[0] if single else list(res[:n_out])
    return own, list(res[n_out:])


def _mesh_pos():
    return lax.axis_index("x"), lax.axis_index("y"), lax.axis_index("c")


def _slot(px, py, pc):
    return 4 * px + 2 * py + pc


def _peers(x, y, c):
    flips = [(fx, fy, fc) for fx in (0, 1) for fy in (0, 1) for fc in (0, 1)][1:]
    return [(1 - x if fx else x, 1 - y if fy else y, 1 - c if fc else c) for fx, fy, fc in flips]


def _remote(src, dst, send_sem, recv_sem, to):
    return pltpu.make_async_remote_copy(
        src_ref=src, dst_ref=dst, send_sem=send_sem, recv_sem=recv_sem, device_id=to, device_id_type=pl.DeviceIdType.MESH
    )


class _GatherOwn:
    N_TO = 4

    def __init__(self, shards):
        n = len(shards)
        self.srcs = list(shards)
        self.out_shape = [_sds((N_DEV, *s.shape), s.dtype) for s in shards]
        self.sems = [pltpu.SemaphoreType.DMA((self.N_TO * n,)), pltpu.SemaphoreType.DMA((self.N_TO * n,)), pltpu.SemaphoreType.DMA((n,))]

    def _copies(self, src_refs, dst_refs, sem_refs):
        send_sems, recv_sems, local_sems = sem_refs
        x, y, c = _mesh_pos()
        me = (x, y, c)
        targets = [(x, y, 1 - c), (1 - x, y, c), (x, 1 - y, c), (1 - x, 1 - y, c)]
        sends, recvs, local = [], [], []
        for t, (src, dst) in enumerate(zip(src_refs, dst_refs)):
            local.append(pltpu.make_async_copy(src, dst.at[_slot(*me)], local_sems.at[t]))
            for k, to in enumerate(targets):
                s = self.N_TO * t + k
                sends.append(_remote(src, dst.at[_slot(*me)], send_sems.at[s], recv_sems.at[s], to))
                recvs.append(_remote(src, dst.at[_slot(*to)], send_sems.at[s], recv_sems.at[s], to))
        return sends, recvs, local

    def start(self, src_refs, dst_refs, sem_refs):
        sends, _, local = self._copies(src_refs, dst_refs, sem_refs)
        for cp in local + sends:
            cp.start()

    def wait(self, src_refs, dst_refs, sem_refs):
        sends, recvs, local = self._copies(src_refs, dst_refs, sem_refs)
        for cp in recvs:
            cp.wait_recv()
        for cp in sends:
            cp.wait_send()
        for cp in local:
            cp.wait()


class _ScatterSlots:
    def __init__(self, grads):
        n = len(grads)
        self.srcs = list(grads)
        self.out_shape = [_sds(g.shape, g.dtype) for g in grads]
        self.sems = [pltpu.SemaphoreType.DMA((7 * n,)), pltpu.SemaphoreType.DMA((7 * n,)), pltpu.SemaphoreType.DMA((n,))]

    def _copies(self, src_refs, dst_refs, sem_refs):
        send_sems, recv_sems, local_sems = sem_refs
        x, y, c = _mesh_pos()
        me = _slot(x, y, c)
        sends, recvs, local = [], [], []
        for t, (src, dst) in enumerate(zip(src_refs, dst_refs)):
            local.append(pltpu.make_async_copy(src.at[me], dst.at[me], local_sems.at[t]))
            for k, to in enumerate(_peers(x, y, c)):
                s = 7 * t + k
                sends.append(_remote(src.at[_slot(*to)], dst.at[me], send_sems.at[s], recv_sems.at[s], to))
                recvs.append(_remote(src.at[me], dst.at[_slot(*to)], send_sems.at[s], recv_sems.at[s], to))
        return sends, recvs, local

    start = _GatherOwn.start
    wait = _GatherOwn.wait


def _sig(x):
    return 1.0 / (1.0 + jnp.exp(-x))


def _rms(x, g):
    r = lax.rsqrt(jnp.mean(x * x, axis=-1, keepdims=True) + EPS)
    return x * r * g


def _rms_bwd(dy, x, g):
    r = lax.rsqrt(jnp.mean(x * x, axis=-1, keepdims=True) + EPS)
    xh = x * r
    dg = jnp.sum(dy * xh, axis=0, keepdims=True)
    dxh = dy * g
    dx = r * (dxh - xh * jnp.mean(dxh * xh, axis=-1, keepdims=True))
    return dx, dg


def _accumulate(ref, val, first):
    @pl.when(first)
    def _():
        ref[...] = val

    @pl.when(jnp.logical_not(first))
    def _():
        ref[...] += val


def _kloop(k, nk, acc_refs, contribs, finish):
    @pl.when(k == 0)
    def _():
        for r, c in zip(acc_refs, contribs):
            r[...] = c

    @pl.when(k > 0)
    def _():
        for r, c in zip(acc_refs, contribs):
            r[...] += c

    @pl.when(k == nk - 1)
    def _():
        finish()


def _rms_call(name, x, g):
    S, D = x.shape
    ts = _tile(S, ELT_TILE)

    def body(x_ref, g_ref, o_ref):
        o_ref[...] = _rms(x_ref[...], g_ref[...]).astype(BF16)

    return _call(
        name, body, (S // ts,),
        [pl.BlockSpec((ts, D), lambda i: (i, 0)), pl.BlockSpec((1, D), lambda i: (0, 0))],
        pl.BlockSpec((ts, D), lambda i: (i, 0)), _sds((S, D), BF16), (x, g),
    )


def _conv_in_call(name, hb, w, layer, b_in):
    S, D = hb.shape
    nsh = w.shape[-1]
    half = N_DEV // 2
    assert nsh * half == D
    tm = _tile(S, ROW_TILE)

    def body(h_ref, wa_ref, wg_ref, ba_ref, bg_ref, a_ref, g_ref, u_ref):
        h = h_ref[...]
        a = _dot(h, wa_ref[...], "nn") + ba_ref[...]
        g = _dot(h, wg_ref[...], "nn") + bg_ref[...]
        a_ref[...] = a.astype(BF16)
        g_ref[...] = g.astype(BF16)
        u_ref[...] = a * _sig(g)

    out_spec = pl.BlockSpec((tm, nsh), lambda d, i: (i, d))
    return _call(
        name, body, (half, S // tm),
        [
            pl.BlockSpec((tm, D), lambda d, i: (i, 0)),
            pl.BlockSpec((None, None, D, nsh), lambda d, i: (d, layer, 0, 0)),
            pl.BlockSpec((None, None, D, nsh), lambda d, i: (d + half, layer, 0, 0)),
            pl.BlockSpec((1, nsh), lambda d, i: (0, d)),
            pl.BlockSpec((1, nsh), lambda d, i: (0, d + half)),
        ],
        [out_spec, out_spec, out_spec],
        [_sds((S, D), BF16), _sds((S, D), BF16), _sds((S, D), F32)],
        (hb, w, w, b_in, b_in),
    )


def _conv_taps(ext_ref, w_ref, row0, lane0, offset, reverse):
    acc = None
    for k in range(CONV_WIDTH):
        off = offset - k if reverse else offset + k
        term = w_ref[k:k + 1, lane0:lane0 + CONV_LANES] * ext_ref[pl.ds(row0 + off, CONV_SUB), pl.ds(lane0, CONV_LANES)]
        acc = term if acc is None else acc + term
    return acc


def _dwconv_call(name, u1, w_dw, b_dw, ln_g, ln_b):
    S, D = u1.shape
    tc = _tile(S, CONV_TILE)
    hb = tc // CONV_HALO
    lanes = min(D, CONV_LANES)
    assert lanes == CONV_LANES and D % CONV_LANES == 0 and tc % CONV_SUB == 0

    def body(cur_ref, halo_ref, w_ref, b_ref, g_ref, bb_ref, u2_ref, u4_ref, ext_ref):
        i = pl.program_id(0)
        ext_ref[0:CONV_HALO, :] = jnp.where(i > 0, halo_ref[...], 0.0)
        ext_ref[CONV_HALO:, :] = cur_ref[...]
        first_tap = CONV_HALO - (CONV_WIDTH - 1)
        for r in range(0, tc, CONV_SUB):
            for c in range(0, D, CONV_LANES):
                u2_ref[r:r + CONV_SUB, c:c + CONV_LANES] = (
                    _conv_taps(ext_ref, w_ref, r, c, first_tap, False) + b_ref[:, c:c + CONV_LANES]
                )
        u2 = u2_ref[...]
        mu = jnp.mean(u2, axis=-1, keepdims=True)
        xc = u2 - mu
        u3 = xc * lax.rsqrt(jnp.mean(xc * xc, axis=-1, keepdims=True) + EPS) * g_ref[...] + bb_ref[...]
        u4_ref[...] = (u3 * _sig(u3)).astype(BF16)

    vec = pl.BlockSpec((1, D), lambda i: (0, 0))
    row = pl.BlockSpec((tc, D), lambda i: (i, 0))
    return _call(
        name, body, (S // tc,),
        [
            row,
            pl.BlockSpec((CONV_HALO, D), lambda i: (jnp.maximum(i * hb - 1, 0), 0)),
            pl.BlockSpec((CONV_WIDTH, D), lambda i: (0, 0)),
            vec, vec, vec,
        ],
        [row, row],
        [_sds((S, D), F32), _sds((S, D), BF16)],
        (u1, u1, w_dw, b_dw, ln_g, ln_b),
        scratch=[pltpu.VMEM((tc + CONV_HALO, D), F32)],
    )


def _row_chunks(tm):
    step = min(tm, EPILOGUE_ROWS)
    return [slice(r, r + step) for r in range(0, tm, step)]


def _full_weight_spec(w, layer):
    _, _, ksh, D = w.shape
    return pl.BlockSpec((N_DEV, None, ksh, D), lambda i: (0, layer, 0, 0), pipeline_mode=pl.Buffered(1))


def _ffn_down_call(name, a, w, layer, x_res, g_next, comm=None):
    S, D = x_res.shape
    fsh = w.shape[2]
    tm = _tile(S, ROW_TILE)

    def body(a_ref, w_ref, x_ref, g_ref, xo_ref, ho_ref, acc_ref):
        def finish():
            for rows in _row_chunks(tm):
                xn = x_ref[rows, :] + acc_ref[rows, :]
                xo_ref[rows, :] = xn
                ho_ref[rows, :] = _rms(xn, g_ref[...]).astype(BF16)

        _kloop(pl.program_id(1), N_DEV, [acc_ref], [_dot(a_ref[...], w_ref[...], "nn")], finish)

    row = pl.BlockSpec((tm, D), lambda i, k: (i, 0))
    vec = pl.BlockSpec((1, D), lambda i, k: (0, 0))
    return _call(
        name, body, (S // tm, N_DEV),
        [pl.BlockSpec((None, tm, fsh), lambda i, k: (k, i, 0)), pl.BlockSpec((None, None, fsh, D), lambda i, k: (k, layer, 0, 0)), row, vec],
        [row, row], [_sds((S, D), F32), _sds((S, D), BF16)], (a, w, x_res, g_next),
        scratch=[pltpu.VMEM((tm, D), F32)], comm=comm,
    )


def _fullmm_res_call(name, a, w, layer, x_res, bias, g_next):
    S, D = x_res.shape
    K = a.shape[1]
    tm = _tile(S, FULL_ROW_TILE)
    has_b = bias is not None

    def body(*refs):
        a_ref, w_ref, x_ref = refs[:3]
        b_ref = refs[3] if has_b else None
        g_ref, xo_ref, ho_ref = refs[3 + int(has_b):]
        xn = x_ref[...] + _dot(a_ref[...], w_ref[...].reshape(K, D), "nn")
        if has_b:
            xn = xn + b_ref[...]
        xo_ref[...] = xn
        ho_ref[...] = _rms(xn, g_ref[...]).astype(BF16)

    row = pl.BlockSpec((tm, D), lambda i: (i, 0))
    vec = pl.BlockSpec((1, D), lambda i: (0, 0))
    in_specs = [pl.BlockSpec((tm, K), lambda i: (i, 0)), _full_weight_spec(w, layer), row] + ([vec] if has_b else []) + [vec]
    ins = [a, w, x_res] + ([bias] if has_b else []) + [g_next]
    return _call(name, body, (S // tm,), in_specs, [row, row], [_sds((S, D), F32), _sds((S, D), BF16)], ins)


def _ffn_up_call(name, hb, wg, wu, layer, comm=None):
    S, D = hb.shape
    fsh = wg.shape[-1]
    tm = _tile(S, ROW_TILE)

    def body(h_ref, wg_ref, wu_ref, g_ref, u_ref, a_ref):
        h = h_ref[...]
        g = _dot(h, wg_ref[...], "nn")
        u = _dot(h, wu_ref[...], "nn")
        g_ref[...] = g.astype(BF16)
        u_ref[...] = u.astype(BF16)
        a_ref[...] = (g * _sig(g) * u).astype(BF16)

    wspec = pl.BlockSpec((None, None, D, fsh), lambda d, i: (d, layer, 0, 0))
    ospec = pl.BlockSpec((None, tm, fsh), lambda d, i: (d, i, 0))
    return _call(
        name, body, (N_DEV, S // tm),
        [pl.BlockSpec((tm, D), lambda d, i: (i, 0)), wspec, wspec],
        [ospec] * 3, [_sds((N_DEV, S, fsh), BF16)] * 3, (hb, wg, wu), comm=comm,
    )


def _colmm_call(name, a, w, layer, out_dtype):
    S, K = a.shape
    nsh = w.shape[-1]
    tm = _tile(S, ROW_TILE)

    def body(a_ref, w_ref, o_ref):
        o_ref[...] = _dot(a_ref[...], w_ref[...], "nn").astype(out_dtype)

    return _call(
        name, body, (N_DEV, S // tm),
        [pl.BlockSpec((tm, K), lambda d, i: (i, 0)), pl.BlockSpec((None, None, K, nsh), lambda d, i: (d, layer, 0, 0))],
        pl.BlockSpec((tm, nsh), lambda d, i: (i, d)), _sds((S, N_DEV * nsh), out_dtype), (a, w),
    )


def _ple_gate_call(name, hb, w, layer, x_res, bias, pp, g_next):
    S, D = x_res.shape
    K = hb.shape[1]
    tm = _tile(S, FULL_ROW_TILE)
    has_g = g_next is not None

    def body(*refs):
        a_ref, w_ref, x_ref, b_ref, p_ref = refs[:5]
        g_ref = refs[5] if has_g else None
        outs = refs[5 + int(has_g):]
        gate = _sig(_dot(a_ref[...], w_ref[...].reshape(K, D), "nn") + b_ref[...])
        xn = x_ref[...] + gate * p_ref[...].astype(F32)
        outs[0][...] = xn
        outs[1][...] = gate.astype(BF16)
        if has_g:
            outs[2][...] = _rms(xn, g_ref[...]).astype(BF16)

    row = pl.BlockSpec((tm, D), lambda i: (i, 0))
    vec = pl.BlockSpec((1, D), lambda i: (0, 0))
    in_specs = [pl.BlockSpec((tm, K), lambda i: (i, 0)), _full_weight_spec(w, layer), row, vec, row]
    ins = [hb, w, x_res, bias, pp]
    out_specs, out_shape = [row, row], [_sds((S, D), F32), _sds((S, D), BF16)]
    if has_g:
        in_specs.append(vec)
        ins.append(g_next)
        out_specs.append(row)
        out_shape.append(_sds((S, D), BF16))
    res = _call(name, body, (S // tm,), in_specs, out_specs, out_shape, ins)
    return res if has_g else (res[0], res[1], None)


def _pool_mix_call(name, x, g):
    S, D = x.shape
    ts = _tile(S, ELT_TILE)
    hb = ts // POOL_HALO
    gc = D // len(POOL_WINDOWS)

    def body(cur_ref, halo_ref, g_ref, o_ref, ext_ref):
        i = pl.program_id(0)
        gain = g_ref[...]
        ext_ref[0:POOL_HALO, :] = jnp.where(i > 0, _rms(halo_ref[...], gain), 0.0)
        ext_ref[POOL_HALO:, :] = _rms(cur_ref[...], gain)
        t = i * ts + lax.broadcasted_iota(jnp.int32, (ts, 1), 0)
        for gi, win in enumerate(POOL_WINDOWS):
            lanes = pl.ds(gi * gc, gc)
            h = ext_ref[pl.ds(POOL_HALO, ts), lanes]
            acc = h
            for j in range(1, win):
                acc = acc + ext_ref[pl.ds(POOL_HALO - j, ts), lanes]
            cnt = jnp.minimum(t + 1, win).astype(F32)
            o_ref[:, gi * gc:(gi + 1) * gc] = (acc / cnt - h).astype(BF16)

    row = pl.BlockSpec((ts, D), lambda i: (i, 0))
    return _call(
        name, body, (S // ts,),
        [row, pl.BlockSpec((POOL_HALO, D), lambda i: (jnp.maximum(i * hb - 1, 0), 0)), pl.BlockSpec((1, D), lambda i: (0, 0))],
        row, _sds((S, D), BF16), (x, x, g), scratch=[pltpu.VMEM((ts + POOL_HALO, D), F32)],
    )


def _pool_out_call(name, mix, wp, scale, x_res, g_next):
    S, D = x_res.shape
    ng, gc, _ = wp.shape
    tm = _tile(S, FULL_ROW_TILE)

    def body(m_ref, w_ref, s_ref, x_ref, g_ref, xo_ref, y_ref, h_ref):
        parts = [_dot(m_ref[:, gi * gc:(gi + 1) * gc], w_ref[gi], "nn") for gi in range(ng)]
        ypre = jnp.concatenate(parts, axis=-1)
        y_ref[...] = ypre.astype(BF16)
        xn = x_ref[...] + ypre * s_ref[...]
        xo_ref[...] = xn
        h_ref[...] = _rms(xn, g_ref[...]).astype(BF16)

    row = pl.BlockSpec((tm, D), lambda i: (i, 0))
    vec = pl.BlockSpec((1, D), lambda i: (0, 0))
    return _call(
        name, body, (S // tm,),
        [row, pl.BlockSpec((ng, gc, gc), lambda i: (0, 0, 0)), vec, row, vec],
        [row, row, row], [_sds((S, D), F32), _sds((S, D), BF16), _sds((S, D), BF16)],
        (mix, wp, scale, x_res, g_next),
    )


def _attn_probs(q_ref, kp_ref, kc_ref, qg_ref, kg_ref, bias_ref, sink_ref, n):
    grp = q_ref.shape[0]
    q = q_ref[...]
    qn = _rms(q, qg_ref[...])
    k = jnp.concatenate([kp_ref[...], kc_ref[...]], axis=0)
    kn = _rms(k, kg_ref[...])
    s = _dot(qn.reshape(grp * Q_BLOCK, HEAD_DIM).astype(BF16), kn.astype(BF16), "nt") * (HEAD_DIM ** -0.5)
    s = s.reshape(grp, Q_BLOCK, 2 * Q_BLOCK) + bias_ref[...]
    qi = lax.broadcasted_iota(jnp.int32, (Q_BLOCK, 2 * Q_BLOCK), 0)
    kj = lax.broadcasted_iota(jnp.int32, (Q_BLOCK, 2 * Q_BLOCK), 1)
    qc = qi // CHUNK
    kc = kj // CHUNK - Q_BLOCK // CHUNK
    ok = (kc <= qc) & (kc >= qc - WINDOW_CHUNKS) & ((n > 0) | (kj >= Q_BLOCK))
    s = jnp.where(ok[None], s, NEG_INF)
    sink = sink_ref[...]
    m = jnp.maximum(jnp.max(s, axis=-1, keepdims=True), sink)
    e = jnp.exp(s - m)
    es = jnp.exp(sink - m)
    inv = 1.0 / (jnp.sum(e, axis=-1, keepdims=True) + es)
    return q, qn, k, kn, e * inv, es * inv


def _attn_specs(grp, nb):
    qspec = pl.BlockSpec((grp, Q_BLOCK, HEAD_DIM), lambda j, n: (j, jnp.minimum(n, nb - 1), 0))
    prev = pl.BlockSpec((None, Q_BLOCK, HEAD_DIM), lambda j, n: (j, jnp.maximum(n - 1, 0), 0))
    cur = pl.BlockSpec((None, Q_BLOCK, HEAD_DIM), lambda j, n: (j, jnp.minimum(n, nb - 1), 0))
    gain = pl.BlockSpec((1, HEAD_DIM), lambda j, n: (0, 0))
    bias = pl.BlockSpec((grp, Q_BLOCK, 2 * Q_BLOCK), lambda j, n: (j, 0, 0))
    sink = pl.BlockSpec((grp, 1, 1), lambda j, n: (j, 0, 0))
    return qspec, prev, cur, gain, bias, sink


def _attn_fwd_call(name, q, k, v, qg, kg, bias, sinks):
    H, S, _ = q.shape
    n_kv = k.shape[0]
    grp = H // n_kv
    nb = S // Q_BLOCK

    def body(q_ref, kp_ref, kc_ref, vp_ref, vc_ref, qg_ref, kg_ref, bias_ref, sink_ref, o_ref):
        n = pl.program_id(1)
        _, _, _, _, p, _ = _attn_probs(q_ref, kp_ref, kc_ref, qg_ref, kg_ref, bias_ref, sink_ref, n)
        vv = jnp.concatenate([vp_ref[...], vc_ref[...]], axis=0).astype(BF16)
        o = _dot(p.reshape(grp * Q_BLOCK, 2 * Q_BLOCK).astype(BF16), vv, "nn")
        o_ref[...] = o.reshape(grp, Q_BLOCK, HEAD_DIM).astype(BF16)

    qspec, prev, cur, gain, bspec, sspec = _attn_specs(grp, nb)
    return _call(
        name, body, (n_kv, nb),
        [qspec, prev, cur, prev, cur, gain, gain, bspec, sspec],
        qspec, _sds((H, S, HEAD_DIM), BF16), (q, k, k, v, v, qg, kg, bias, sinks),
    )


def _loss_call(name, y, target):
    S, D = y.shape
    ts = _tile(S, ELT_TILE)

    def body(y_ref, t_ref, d_ref, db_ref, l_ref):
        err = y_ref[...] - t_ref[...]
        dy = err * (1.0 / D)
        d_ref[...] = dy
        db_ref[...] = dy.astype(BF16)
        part = 0.5 * jnp.sum(jnp.sum(err * err, axis=-1, keepdims=True), axis=0, keepdims=True) * (1.0 / D)
        _accumulate(l_ref, jnp.broadcast_to(part, l_ref.shape), pl.program_id(0) == 0)

    row = pl.BlockSpec((ts, D), lambda i: (i, 0))
    return _call(
        name, body, (S // ts,), [row, row],
        [row, row, pl.BlockSpec((8, 128), lambda i: (0, 0))],
        [_sds((S, D), F32), _sds((S, D), BF16), _sds((8, 128), F32)], (y, target),
    )


def _ple_bwd_elt_call(name, dx, gate, pp):
    S, D = dx.shape
    ts = _tile(S, ELT_TILE)

    def body(dx_ref, gt_ref, p_ref, dz_ref, dp_ref, db_ref):
        d = dx_ref[...]
        gt = gt_ref[...].astype(F32)
        dz = d * p_ref[...].astype(F32) * gt * (1.0 - gt)
        dz_ref[...] = dz.astype(BF16)
        dp_ref[...] = (d * gt).astype(BF16)
        _accumulate(db_ref, jnp.sum(dz, axis=0, keepdims=True), pl.program_id(0) == 0)

    row = pl.BlockSpec((ts, D), lambda i: (i, 0))
    return _call(
        name, body, (S // ts,), [row, row, row],
        [row, row, pl.BlockSpec((1, D), lambda i: (0, 0))],
        [_sds((S, D), BF16), _sds((S, D), BF16), _sds((1, D), F32)], (dx, gate, pp),
    )


def _grad_w_call(name, a, a_mode, b, b_mode, comm=None):
    bs = b if isinstance(b, (list, tuple)) else [b]
    S = a.shape[-2]
    tk = _tile(S, K_TILE)
    nk = S // tk

    def spec(arr, mode):
        if mode == "full":
            c = arr.shape[-1]
            return pl.BlockSpec((tk, c), lambda d, k: (k, 0)), c
        if mode == "nat":
            c = arr.shape[-1] // N_DEV
            return pl.BlockSpec((tk, c), lambda d, k: (k, d)), c
        c = arr.shape[-1]
        return pl.BlockSpec((None, tk, c), lambda d, k: (d, k, 0)), c

    a_spec, ca = spec(a, a_mode)
    b_specs, cbs = zip(*[spec(x, b_mode) for x in bs])
    nb = len(bs)

    def body(*refs):
        a_ref = refs[0]
        b_refs = refs[1:1 + nb]
        o_refs = refs[1 + nb:1 + 2 * nb]
        acc_refs = refs[1 + 2 * nb:]
        k = pl.program_id(1)
        av = a_ref[...]

        def finish():
            for o, acc in zip(o_refs, acc_refs):
                o[...] = acc[...].astype(BF16)

        _kloop(k, nk, acc_refs, [_dot(av, br[...], "tn") for br in b_refs], finish)

    res = _call(
        name, body, (N_DEV, nk), [a_spec, *b_specs],
        [pl.BlockSpec((None, ca, cb), lambda d, k: (d, 0, 0)) for cb in cbs],
        [_sds((N_DEV, ca, cb), BF16) for cb in cbs], (a, *bs),
        scratch=[pltpu.VMEM((ca, cb), F32) for cb in cbs], comm=comm,
    )
    own, landed = res if comm is not None else (res, None)
    own = own if isinstance(b, (list, tuple)) else own[0]
    return own if comm is None else (own, landed)


def _dx_full_call(name, dy, w, layer, out_dtype):
    S, D = dy.shape
    K = N_DEV * w.shape[2]
    tm = _tile(S, ROW_TILE)

    def body(dy_ref, w_ref, o_ref):
        o_ref[...] = _dot(dy_ref[...], w_ref[...].reshape(K, D), "nt").astype(out_dtype)

    return _call(
        name, body, (S // tm,), [pl.BlockSpec((tm, D), lambda i: (i, 0)), _full_weight_spec(w, layer)],
        pl.BlockSpec((tm, K), lambda i: (i, 0)), _sds((S, K), out_dtype), (dy, w),
    )


def _dx_full_rms_call(name, dy, w, layer, x, g, dres):
    S, D = x.shape
    K = N_DEV * w.shape[2]
    assert K == D
    tm = _tile(S, FULL_ROW_TILE)

    def body(dy_ref, w_ref, x_ref, g_ref, r_ref, dx_ref, dxb_ref, dg_ref):
        dh, dg = _rms_bwd(_dot(dy_ref[...], w_ref[...].reshape(K, dy_ref.shape[1]), "nt"), x_ref[...], g_ref[...])
        dx = r_ref[...] + dh
        dx_ref[...] = dx
        dxb_ref[...] = dx.astype(BF16)
        _accumulate(dg_ref, dg, pl.program_id(0) == 0)

    row = pl.BlockSpec((tm, D), lambda i: (i, 0))
    vec = pl.BlockSpec((1, D), lambda i: (0, 0))
    return _call(
        name, body, (S // tm,), [pl.BlockSpec((tm, dy.shape[1]), lambda i: (i, 0)), _full_weight_spec(w, layer), row, vec, row],
        [row, row, vec], [_sds((S, D), F32), _sds((S, D), BF16), _sds((1, D), F32)], (dy, w, x, g, dres),
    )


def _dx_colsharded_rms_call(name, dys, dy_chunked, ws, layer, x, g, dres, want_colsum, comm=None):
    S, D = x.shape
    nsh = ws[0].shape[-1]
    tm = _tile(S, ROW_TILE)
    nt = len(dys)

    def body(*refs):
        dy_refs = refs[:nt]
        w_refs = refs[nt:2 * nt]
        x_ref, g_ref, r_ref = refs[2 * nt:2 * nt + 3]
        outs = refs[2 * nt + 3:]
        acc_ref = outs[0]
        i, k = pl.program_id(0), pl.program_id(1)
        contrib = None
        for dr, wr in zip(dy_refs, w_refs):
            c = _dot(dr[...], wr[...], "nt")
            contrib = c if contrib is None else contrib + c

        def finish():
            dg = colsum = None
            for rows in _row_chunks(tm):
                dh, dg_part = _rms_bwd(acc_ref[rows, :], x_ref[rows, :], g_ref[...])
                dx = r_ref[rows, :] + dh
                outs[0][rows, :] = dx
                outs[1][rows, :] = dx.astype(BF16)
                dg = dg_part if dg is None else dg + dg_part
                if want_colsum:
                    part = jnp.sum(dx, axis=0, keepdims=True)
                    colsum = part if colsum is None else colsum + part
            _accumulate(outs[2], dg, i == 0)
            if want_colsum:
                _accumulate(outs[3], colsum, i == 0)

        _kloop(k, N_DEV, [acc_ref], [contrib], finish)

    if dy_chunked:
        dspec = pl.BlockSpec((None, tm, nsh), lambda i, k: (k, i, 0))
    else:
        dspec = pl.BlockSpec((tm, nsh), lambda i, k: (i, k))
    wspec = pl.BlockSpec((None, None, D, nsh), lambda i, k: (k, layer, 0, 0))
    row = pl.BlockSpec((tm, D), lambda i, k: (i, 0))
    row_once = pl.BlockSpec((tm, D), lambda i, k: (i, 0), pipeline_mode=pl.Buffered(1))
    vec = pl.BlockSpec((1, D), lambda i, k: (0, 0))
    out_specs = [row, row, vec] + ([vec] if want_colsum else [])
    out_shape = [_sds((S, D), F32), _sds((S, D), BF16), _sds((1, D), F32)] + ([_sds((1, D), F32)] if want_colsum else [])
    res = _call(
        name, body, (S // tm, N_DEV), [dspec] * nt + [wspec] * nt + [row_once, vec, row_once],
        out_specs, out_shape, (*dys, *ws, x, g, dres), comm=comm, vmem=BIG_VMEM_LIMIT_BYTES,
    )
    own, landed = res if comm is not None else (res, None)
    own = tuple(own) if want_colsum else (*own, None)
    return own if comm is None else (own, landed)


def _ffn_bwd_hidden_call(name, dyb, w, layer, gpre, upre, comm=None):
    S, D = dyb.shape
    fsh = w.shape[2]
    tm = _tile(S, ROW_TILE)

    def body(dy_ref, w_ref, g_ref, u_ref, dg_ref, du_ref):
        da = _dot(dy_ref[...], w_ref[...], "nt")
        g = g_ref[...].astype(F32)
        u = u_ref[...].astype(F32)
        s = _sig(g)
        dg_ref[...] = (da * u * s * (1.0 + g * (1.0 - s))).astype(BF16)
        du_ref[...] = (da * g * s).astype(BF16)

    cspec = pl.BlockSpec((None, tm, fsh), lambda d, i: (d, i, 0))
    return _call(
        name, body, (N_DEV, S // tm),
        [pl.BlockSpec((tm, D), lambda d, i: (i, 0)), pl.BlockSpec((None, None, fsh, D), lambda d, i: (d, layer, 0, 0)), cspec, cspec],
        [cspec, cspec], [_sds((N_DEV, S, fsh), BF16)] * 2, (dyb, w, gpre, upre), comm=comm,
    )


def _conv_out_bwd_call(name, dyb, w, layer, u2, ln_g, ln_b):
    S, D = u2.shape
    ksh = w.shape[2]
    tm = _tile(S, ELT_TILE)

    def body(dy_ref, w_ref, u2_ref, g_ref, b_ref, du2_ref, dg_ref, db_ref, dbdw_ref):
        i = pl.program_id(0)
        dy = dy_ref[...]
        du4 = jnp.concatenate([_dot(dy, w_ref[d], "nt") for d in range(N_DEV)], axis=-1)
        u2 = u2_ref[...]
        mu = jnp.mean(u2, axis=-1, keepdims=True)
        xc = u2 - mu
        r = lax.rsqrt(jnp.mean(xc * xc, axis=-1, keepdims=True) + EPS)
        xh = xc * r
        gain = g_ref[...]
        u3 = xh * gain + b_ref[...]
        s = _sig(u3)
        du3 = du4 * s * (1.0 + u3 * (1.0 - s))
        dxh = du3 * gain
        du2 = r * (dxh - jnp.mean(dxh, axis=-1, keepdims=True) - xh * jnp.mean(dxh * xh, axis=-1, keepdims=True))
        du2_ref[...] = du2
        _accumulate(dg_ref, jnp.sum(du3 * xh, axis=0, keepdims=True), i == 0)
        _accumulate(db_ref, jnp.sum(du3, axis=0, keepdims=True), i == 0)
        _accumulate(dbdw_ref, jnp.sum(du2, axis=0, keepdims=True), i == 0)

    row = pl.BlockSpec((tm, D), lambda i: (i, 0))
    vec = pl.BlockSpec((1, D), lambda i: (0, 0))
    return _call(
        name, body, (S // tm,),
        [row, pl.BlockSpec((N_DEV, None, ksh, D), lambda i: (0, layer, 0, 0)), row, vec, vec],
        [row, vec, vec, vec], [_sds((S, D), F32)] + [_sds((1, D), F32)] * 3, (dyb, w, u2, ln_g, ln_b),
    )


def _dwconv_bwd_call(name, du2, u1, w_dw, a_pre, g_pre):
    S, D = u1.shape
    tc = _tile(S, CONV_TILE)
    hb = tc // CONV_HALO
    n_halo = S // CONV_HALO
    assert D % CONV_LANES == 0 and tc % CONV_SUB == 0
    wrows = CONV_HALO

    def body(d_cur, d_next, u_cur, u_prev, w_ref, a_ref, g_ref, du_ref, dbin_ref, dw_ref, dext_ref, uext_ref, du1_ref, dwacc_ref):
        i = pl.program_id(0)
        last = S // tc - 1
        dext_ref[0:tc, :] = d_cur[...]
        dext_ref[tc:, :] = jnp.where(i < last, d_next[...], 0.0)
        uext_ref[0:CONV_HALO, :] = jnp.where(i > 0, u_prev[...], 0.0)
        uext_ref[CONV_HALO:, :] = u_cur[...]

        @pl.when(i == 0)
        def _():
            dwacc_ref[...] = jnp.zeros_like(dwacc_ref)

        first_tap = CONV_HALO - (CONV_WIDTH - 1)
        for r in range(0, tc, CONV_SUB):
            for c in range(0, D, CONV_LANES):
                du1_ref[r:r + CONV_SUB, c:c + CONV_LANES] = _conv_taps(dext_ref, w_ref, r, c, CONV_WIDTH - 1, True)
                dcur = dext_ref[r:r + CONV_SUB, c:c + CONV_LANES]
                for k in range(CONV_WIDTH):
                    prod = dcur * uext_ref[pl.ds(r + first_tap + k, CONV_SUB), pl.ds(c, CONV_LANES)]
                    part = prod[0:8]
                    for q in range(8, CONV_SUB, 8):
                        part = part + prod[q:q + 8]
                    dwacc_ref[k, :, c:c + CONV_LANES] += part

        du1 = du1_ref[...]
        a = a_ref[...].astype(F32)
        sg = _sig(g_ref[...].astype(F32))
        da = du1 * sg
        dgate = du1 * a * sg * (1.0 - sg)
        du_ref[:, 0:D] = da.astype(BF16)
        du_ref[:, D:2 * D] = dgate.astype(BF16)
        _accumulate(dbin_ref, jnp.concatenate([jnp.sum(da, axis=0, keepdims=True), jnp.sum(dgate, axis=0, keepdims=True)], axis=-1), i == 0)

        @pl.when(i == last)
        def _():
            for k in range(CONV_WIDTH):
                dw_ref[k:k + 1, :] = jnp.sum(dwacc_ref[k], axis=0, keepdims=True)
            dw_ref[CONV_WIDTH:, :] = jnp.zeros((wrows - CONV_WIDTH, D), F32)

    row = pl.BlockSpec((tc, D), lambda i: (i, 0))
    nxt = pl.BlockSpec((CONV_HALO, D), lambda i: (jnp.minimum((i + 1) * hb, n_halo - 1), 0))
    prv = pl.BlockSpec((CONV_HALO, D), lambda i: (jnp.maximum(i * hb - 1, 0), 0))
    return _call(
        name, body, (S // tc,),
        [row, nxt, row, prv, pl.BlockSpec((CONV_WIDTH, D), lambda i: (0, 0)), row, row],
        [pl.BlockSpec((tc, 2 * D), lambda i: (i, 0)), pl.BlockSpec((1, 2 * D), lambda i: (0, 0)), pl.BlockSpec((wrows, D), lambda i: (0, 0))],
        [_sds((S, 2 * D), BF16), _sds((1, 2 * D), F32), _sds((wrows, D), F32)],
        (du2, du2, u1, u1, w_dw, a_pre, g_pre),
        scratch=[
            pltpu.VMEM((tc + CONV_HALO, D), F32), pltpu.VMEM((tc + CONV_HALO, D), F32),
            pltpu.VMEM((tc, D), F32), pltpu.VMEM((CONV_WIDTH, 8, D), F32),
        ],
    )


def _pool_out_bwd_call(name, dy, ypre, scale, wp):
    S, D = dy.shape
    ng, gc, _ = wp.shape
    tm = _tile(S, FULL_ROW_TILE)

    def body(dy_ref, y_ref, s_ref, w_ref, dm_ref, dyp_ref, ds_ref):
        dy = dy_ref[...]
        _accumulate(ds_ref, jnp.sum(dy * y_ref[...].astype(F32), axis=0, keepdims=True), pl.program_id(0) == 0)
        dyp = (dy * s_ref[...]).astype(BF16)
        dyp_ref[...] = dyp
        dm_ref[...] = jnp.concatenate([_dot(dyp[:, gi * gc:(gi + 1) * gc], w_ref[gi], "nt") for gi in range(ng)], axis=-1)

    row = pl.BlockSpec((tm, D), lambda i: (i, 0))
    vec = pl.BlockSpec((1, D), lambda i: (0, 0))
    return _call(
        name, body, (S // tm,), [row, row, vec, pl.BlockSpec((ng, gc, gc), lambda i: (0, 0, 0))],
        [row, row, vec], [_sds((S, D), F32), _sds((S, D), BF16), _sds((1, D), F32)], (dy, ypre, scale, wp),
    )


def _pool_w_grad_call(name, mix, dyp, ng):
    S, D = mix.shape
    gc = D // ng
    tk = _tile(S, K_TILE)
    nk = S // tk

    def body(m_ref, d_ref, o_ref, acc_ref):
        def finish():
            o_ref[...] = acc_ref[...]

        _kloop(pl.program_id(1), nk, [acc_ref], [_dot(m_ref[...], d_ref[...], "tn")], finish)

    blk = pl.BlockSpec((tk, gc), lambda g, k: (k, g))
    return _call(
        name, body, (ng, nk), [blk, blk], pl.BlockSpec((None, gc, gc), lambda g, k: (g, 0, 0)),
        _sds((ng, gc, gc), F32), (mix, dyp), scratch=[pltpu.VMEM((gc, gc), F32)],
    )


def _pool_mix_bwd_call(name, dmix, x, g, dres):
    S, D = x.shape
    ts = _tile(S, ELT_TILE)
    hb = ts // POOL_HALO
    n_halo = S // POOL_HALO
    gc = D // len(POOL_WINDOWS)

    def body(cur_ref, nxt_ref, x_ref, g_ref, r_ref, dx_ref, dxb_ref, dg_ref, ext_ref, dh_ref):
        i = pl.program_id(0)
        last = S // ts - 1
        t = i * ts + lax.broadcasted_iota(jnp.int32, (ts + POOL_HALO, 1), 0)
        for gi, win in enumerate(POOL_WINDOWS):
            lanes = slice(gi * gc, (gi + 1) * gc)
            cnt = jnp.minimum(t + 1, win).astype(F32)
            ext_ref[0:ts, lanes] = cur_ref[:, lanes] / cnt[0:ts]
            ext_ref[ts:, lanes] = jnp.where(i < last, nxt_ref[:, lanes] / cnt[ts:], 0.0)
        for gi, win in enumerate(POOL_WINDOWS):
            lanes = pl.ds(gi * gc, gc)
            acc = ext_ref[pl.ds(0, ts), lanes]
            for j in range(1, win):
                acc = acc + ext_ref[pl.ds(j, ts), lanes]
            dh_ref[:, gi * gc:(gi + 1) * gc] = acc - cur_ref[:, gi * gc:(gi + 1) * gc]
        d, dg = _rms_bwd(dh_ref[...], x_ref[...], g_ref[...])
        dx = r_ref[...] + d
        dx_ref[...] = dx
        dxb_ref[...] = dx.astype(BF16)
        _accumulate(dg_ref, dg, i == 0)

    row = pl.BlockSpec((ts, D), lambda i: (i, 0))
    vec = pl.BlockSpec((1, D), lambda i: (0, 0))
    nxt = pl.BlockSpec((POOL_HALO, D), lambda i: (jnp.minimum((i + 1) * hb, n_halo - 1), 0))
    return _call(
        name, body, (S // ts,), [row, nxt, row, vec, row], [row, row, vec],
        [_sds((S, D), F32), _sds((S, D), BF16), _sds((1, D), F32)], (dmix, dmix, x, g, dres),
        scratch=[pltpu.VMEM((ts + POOL_HALO, D), F32), pltpu.VMEM((ts, D), F32)],
    )


def _attn_bwd_call(name, q, k, v, do, qg, kg, bias, sinks):
    H, S, _ = q.shape
    n_kv = k.shape[0]
    grp = H // n_kv
    nb = S // Q_BLOCK
    scale = HEAD_DIM ** -0.5

    def body(q_ref, kp_ref, kc_ref, vp_ref, vc_ref, do_ref, qg_ref, kg_ref, bias_ref, sink_ref,
             dq_ref, dk_ref, dv_ref, dqg_ref, dkg_ref, dbias_ref, dsink_ref, ck_ref, cv_ref):
        n = pl.program_id(1)

        @pl.when(n == 0)
        def _():
            dqg_ref[...] = jnp.zeros_like(dqg_ref)
            dkg_ref[...] = jnp.zeros_like(dkg_ref)
            dbias_ref[...] = jnp.zeros_like(dbias_ref)
            dsink_ref[...] = jnp.zeros_like(dsink_ref)
            ck_ref[...] = jnp.zeros_like(ck_ref)
            cv_ref[...] = jnp.zeros_like(cv_ref)

        def block_grads():
            q, qn, _, kn, p, ps = _attn_probs(q_ref, kp_ref, kc_ref, qg_ref, kg_ref, bias_ref, sink_ref, n)
            rows = grp * Q_BLOCK
            dob = do_ref[...].reshape(rows, HEAD_DIM).astype(BF16)
            vv = jnp.concatenate([vp_ref[...], vc_ref[...]], axis=0).astype(BF16)
            dp = _dot(dob, vv, "nt").reshape(grp, Q_BLOCK, 2 * Q_BLOCK)
            delta = jnp.sum(p * dp, axis=-1, keepdims=True)
            dl = p * (dp - delta)
            dbias_ref[...] += dl
            dsink_ref[...] += jnp.sum(-ps * delta, axis=1, keepdims=True)
            dlb = dl.reshape(rows, 2 * Q_BLOCK).astype(BF16)
            dqn = (_dot(dlb, kn.astype(BF16), "nn") * scale).reshape(grp, Q_BLOCK, HEAD_DIM)
            dkn = _dot(dlb, qn.reshape(rows, HEAD_DIM).astype(BF16), "tn") * scale
            dvv = _dot(p.reshape(rows, 2 * Q_BLOCK).astype(BF16), dob, "tn")
            qgain = qg_ref[...]
            r = lax.rsqrt(jnp.mean(q * q, axis=-1, keepdims=True) + EPS)
            qh = q * r
            dqg_ref[...] += jnp.sum(jnp.sum(dqn * qh, axis=1), axis=0, keepdims=True)
            dqh = dqn * qgain
            dq_ref[...] = r * (dqh - qh * jnp.mean(dqh * qh, axis=-1, keepdims=True))
            return dkn, dvv

        def finish_prev(dkn_prev, dv_prev):
            kraw = kp_ref[...]
            dk, dkg = _rms_bwd(dkn_prev, kraw, kg_ref[...])
            dk_ref[...] = dk
            dv_ref[...] = dv_prev
            dkg_ref[...] += dkg

        @pl.when(n < nb)
        def _():
            dkn, dvv = block_grads()

            @pl.when(n > 0)
            def _():
                finish_prev(ck_ref[...] + dkn[0:Q_BLOCK], cv_ref[...] + dvv[0:Q_BLOCK])

            ck_ref[...] = dkn[Q_BLOCK:]
            cv_ref[...] = dvv[Q_BLOCK:]

        @pl.when(n == nb)
        def _():
            finish_prev(ck_ref[...], cv_ref[...])

    qspec, prev, cur, gain, bspec, sspec = _attn_specs(grp, nb)
    kout = pl.BlockSpec((None, Q_BLOCK, HEAD_DIM), lambda j, n: (j, jnp.maximum(n - 1, 0), 0))
    gout = pl.BlockSpec((None, 1, HEAD_DIM), lambda j, n: (j, 0, 0))
    return _call(
        name, body, (n_kv, nb + 1),
        [qspec, prev, cur, prev, cur, qspec, gain, gain, bspec, sspec],
        [qspec, kout, kout, gout, gout, bspec, sspec],
        [
            _sds((H, S, HEAD_DIM), F32), _sds((n_kv, S, HEAD_DIM), F32), _sds((n_kv, S, HEAD_DIM), F32),
            _sds((n_kv, 1, HEAD_DIM), F32), _sds((n_kv, 1, HEAD_DIM), F32),
            _sds((H, Q_BLOCK, 2 * Q_BLOCK), F32), _sds((H, 1, 1), F32),
        ],
        (q, k, k, v, v, do, qg, kg, bias, sinks),
        scratch=[pltpu.VMEM((Q_BLOCK, HEAD_DIM), F32), pltpu.VMEM((Q_BLOCK, HEAD_DIM), F32)],
    )


def _bucket_sum_call(name, onehot, dbias):
    nbk, n = onehot.shape
    H = dbias.shape[0]

    def body(o_ref, d_ref, out_ref):
        out_ref[...] = lax.dot_general(o_ref[...], d_ref[...], _DIMS["nt"], precision=lax.Precision.HIGHEST, preferred_element_type=F32)

    return _call(
        name, body, (1,), [pl.BlockSpec((nbk, n), lambda i: (0, 0)), pl.BlockSpec((H, n), lambda i: (0, 0))],
        pl.BlockSpec((nbk, H), lambda i: (0, 0)), _sds((nbk, H), F32), (onehot, dbias),
    )


def _bias_table_call(name, rel_bias, onehot):
    nbk, n = onehot.shape
    H = rel_bias.shape[1]

    def body(r_ref, o_ref, out_ref):
        out_ref[...] = lax.dot_general(r_ref[...], o_ref[...], _DIMS["tn"], precision=lax.Precision.HIGHEST, preferred_element_type=F32)

    return _call(
        name, body, (1,), [pl.BlockSpec((nbk, H), lambda i: (0, 0)), pl.BlockSpec((nbk, n), lambda i: (0, 0))],
        pl.BlockSpec((H, n), lambda i: (0, 0)), _sds((H, n), F32), (rel_bias, onehot),
    )


def _exchange_call(name, comm):
    n_src, n_dst = len(comm.srcs), len(comm.out_shape)

    def body(*refs):
        src_refs, dst_refs, sem_refs = refs[:n_src], refs[n_src:n_src + n_dst], refs[n_src + n_dst:]
        comm.start(src_refs, dst_refs, sem_refs)
        comm.wait(src_refs, dst_refs, sem_refs)

    hbm = pl.BlockSpec(memory_space=pltpu.HBM)
    return pl.pallas_call(
        body, name=name, in_specs=[hbm] * n_src, out_specs=[hbm] * n_dst, out_shape=list(comm.out_shape),
        scratch_shapes=list(comm.sems), compiler_params=pltpu.CompilerParams(has_side_effects=True),
    )(*comm.srcs)


def _gather_forward_call(name, bufs):
    n = len(bufs)

    def body(*refs):
        in_refs, out_refs = refs[:n], refs[n:2 * n]
        send_sems, recv_sems = refs[2 * n:]
        x, y, c = _mesh_pos()
        sibling = (x, y, 1 - c)
        chips = [(1 - x, y), (x, 1 - y), (1 - x, 1 - y)]
        sends, recvs = [], []
        for t in range(n):
            for j, chip in enumerate(chips):
                s = 3 * t + j
                mine, theirs = _slot(*chip, c), _slot(*chip, 1 - c)
                sends.append(_remote(in_refs[t].at[mine], out_refs[t].at[mine], send_sems.at[s], recv_sems.at[s], sibling))
                recvs.append(_remote(in_refs[t].at[mine], out_refs[t].at[theirs], send_sems.at[s], recv_sems.at[s], sibling))
        for cp in sends:
            cp.start()
        for cp in recvs:
            cp.wait_recv()
        for cp in sends:
            cp.wait_send()

    hbm = pl.BlockSpec(memory_space=pltpu.HBM)
    return pl.pallas_call(
        body, name=name, in_specs=[hbm] * n, out_specs=[hbm] * n, out_shape=[_sds(b.shape, b.dtype) for b in bufs],
        scratch_shapes=[pltpu.SemaphoreType.DMA((3 * n,)), pltpu.SemaphoreType.DMA((3 * n,))],
        input_output_aliases={t: t for t in range(n)},
        compiler_params=pltpu.CompilerParams(has_side_effects=True),
    )(*bufs)


def _all_gather_call(name, shards):
    n = len(shards)

    def body(*refs):
        in_refs, out_refs = refs[:n], refs[n:2 * n]
        send_sems, recv_sems, local_sems = refs[2 * n:]
        x, y, c = _mesh_pos()
        me, sibling = (x, y, c), (x, y, 1 - c)
        chips = [(1 - x, y), (x, 1 - y), (1 - x, 1 - y)]

        def copy(t, k, block, to, src=None):
            dst = out_refs[t].at[_slot(*block)]
            return pltpu.make_async_remote_copy(
                src_ref=dst if src is None else src, dst_ref=dst,
                send_sem=send_sems.at[7 * t + k], recv_sem=recv_sems.at[7 * t + k],
                device_id=to, device_id_type=pl.DeviceIdType.MESH,
            )

        mine = [pltpu.make_async_copy(in_refs[t], out_refs[t].at[_slot(*me)], local_sems.at[t]) for t in range(n)]
        for cp in mine:
            cp.start()
        first = []
        for t in range(n):
            first.append(copy(t, 0, me, sibling, src=in_refs[t]))
            first += [copy(t, 1 + j, me, (*chip, c), src=in_refs[t]) for j, chip in enumerate(chips)]
        for cp in first:
            cp.start()
        passed = []
        for j, chip in enumerate(chips):
            for t in range(n):
                copy(t, 1 + j, (*chip, c), me).wait_recv()
                fwd = copy(t, 4 + j, (*chip, c), sibling)
                fwd.start()
                passed.append(fwd)
        for t in range(n):
            copy(t, 0, sibling, me).wait_recv()
            for j, chip in enumerate(chips):
                copy(t, 4 + j, (*chip, 1 - c), me).wait_recv()
        for cp in first + passed:
            cp.wait_send()
        for cp in mine:
            cp.wait()

    hbm = pl.BlockSpec(memory_space=pltpu.HBM)
    return pl.pallas_call(
        body, name=name,
        in_specs=[hbm] * n, out_specs=[hbm] * n,
        out_shape=[_sds((N_DEV, *s.shape), s.dtype) for s in shards],
        scratch_shapes=[pltpu.SemaphoreType.DMA((7 * n,)), pltpu.SemaphoreType.DMA((7 * n,)), pltpu.SemaphoreType.DMA((n,))],
        compiler_params=pltpu.CompilerParams(has_side_effects=True),
    )(*shards)


def _all_reduce_small_call(name, pack):
    R, C = pack.shape

    def body(in_ref, out_ref, land_ref, send_sems, recv_sems):
        x, y, c = _mesh_pos()
        me = _slot(x, y, c)
        peers = _peers(x, y, c)
        land_ref[me] = in_ref[...]
        sends = [
            pltpu.make_async_remote_copy(
                src_ref=in_ref, dst_ref=land_ref.at[me], send_sem=send_sems.at[k], recv_sem=recv_sems.at[k],
                device_id=to, device_id_type=pl.DeviceIdType.MESH,
            )
            for k, to in enumerate(peers)
        ]
        for cp in sends:
            cp.start()
        for k, frm in enumerate(peers):
            pltpu.make_async_remote_copy(
                src_ref=in_ref, dst_ref=land_ref.at[_slot(*frm)], send_sem=send_sems.at[k], recv_sem=recv_sems.at[k],
                device_id=frm, device_id_type=pl.DeviceIdType.MESH,
            ).wait_recv()
        for cp in sends:
            cp.wait_send()
        total = land_ref[0]
        for d in range(1, N_DEV):
            total = total + land_ref[d]
        out_ref[...] = total

    vmem = pl.BlockSpec(memory_space=pltpu.VMEM)
    return pl.pallas_call(
        body, name=name, in_specs=[vmem], out_specs=vmem, out_shape=_sds((R, C), F32),
        scratch_shapes=[pltpu.VMEM((N_DEV, R, C), F32), pltpu.SemaphoreType.DMA((7,)), pltpu.SemaphoreType.DMA((7,))],
        compiler_params=pltpu.CompilerParams(has_side_effects=True, vmem_limit_bytes=VMEM_LIMIT_BYTES),
    )(pack)


def _adamw_call(name, grad, landed, w, m, v):
    R, C = w.shape
    tr = _div_tile(R, ADAM_BLOCK_BYTES // (C * 4))
    c1 = 1.0 / (1.0 - ADAM_B1 ** ADAM_STEP)
    c2 = 1.0 / (1.0 - ADAM_B2 ** ADAM_STEP)

    def body(g_ref, w_ref, m_ref, v_ref, go_ref, d_ref, mo_ref, vo_ref):
        if landed:
            g = g_ref[0].astype(F32)
            for d in range(1, N_DEV):
                g = g + g_ref[d].astype(F32)
        else:
            g = g_ref[...]
        go_ref[...] = g
        mn = ADAM_B1 * m_ref[...] + (1.0 - ADAM_B1) * g
        vn = ADAM_B2 * v_ref[...] + (1.0 - ADAM_B2) * (g * g)
        mo_ref[...] = mn
        vo_ref[...] = vn
        d_ref[...] = -ADAM_LR * ((mn * c1) / (jnp.sqrt(vn * c2) + ADAM_EPS) + ADAM_WD * w_ref[...])

    blk = pl.BlockSpec((tr, C), lambda i: (i, 0))
    gspec = pl.BlockSpec((N_DEV, tr, C), lambda i: (0, i, 0)) if landed else blk
    return _call(name, body, (R // tr,), [gspec, blk, blk, blk], [blk] * 4, [_sds((R, C), F32)] * 4, (grad, w, m, v))


def _t5_bucket(rel):
    nb = NUM_BUCKETS // 2
    n = -rel
    ret = jnp.where(n < 0, nb, 0)
    n = jnp.abs(n)
    max_exact = nb // 2
    nf = jnp.maximum(n, 1).astype(jnp.float32)
    large = max_exact + (jnp.log(nf / max_exact) / math.log(REL_MAX_DIST / max_exact) * (nb - max_exact)).astype(jnp.int32)
    large = jnp.minimum(large, nb - 1)
    return ret + jnp.where(n < max_exact, n, large)


def _band_buckets():
    i = jnp.arange(Q_BLOCK)[:, None]
    j = jnp.arange(2 * Q_BLOCK)[None, :]
    return _t5_bucket(j - Q_BLOCK - i)


def _to_heads(t, n_heads):
    S = t.shape[0]
    return t.reshape(S, n_heads, HEAD_DIM).transpose(1, 0, 2)


def _from_heads(t):
    H, S, _ = t.shape
    return t.transpose(1, 0, 2).reshape(S, H * HEAD_DIM)


def _gathered_vec(t):
    nd, L, n = t.shape
    return t.transpose(1, 0, 2).reshape(L, nd * n)


_MIXER_WEIGHTS = {0: ("conv_w_in", "conv_w_out"), 1: (), 2: ("attn_w_qkv", "attn_w_o")}
_FFN_UP_WEIGHTS = ("ffn_w_gate", "ffn_w_up")
_FFN_REST_WEIGHTS = ("ffn_w_down", "ple_w_proj", "ple_w_gate")
_SMALL_SHARDED = ("conv_b_in", "conv_w_dw", "conv_b_dw", "conv_ln_g", "conv_ln_b", "conv_b_out")


def _mixer_keys(i):
    return [(n, i // 3) for n in _MIXER_WEIGHTS[i % 3]]


def _step(x, p, target, wb, small_pack, small_shapes, V):
    S, D = x.shape
    depth = V["norm_mix"].shape[0]
    n_heads = D // HEAD_DIM
    n_kv = (wb["attn_w_qkv"].shape[-1] * N_DEV - D) // (2 * HEAD_DIM)
    ng = len(POOL_WINDOWS)
    vec = lambda t, i: t[i][None, :]
    shard = lambda key: wb[key[0]][key[1]:key[1] + 1]

    first = [(n, 0) for n in _FFN_UP_WEIGHTS + _FFN_REST_WEIGHTS] + _mixer_keys(0) + [("pool_w", 0)]
    gathered = _all_gather_call("all_gather0", [shard(k) for k in first] + [small_pack])
    W = dict(zip(first, gathered[:-1]))
    pw = W.pop(("pool_w", 0))
    wp = pw[:, 0].transpose(1, 0, 2, 3).reshape(pw.shape[2], pw.shape[3] * N_DEV, pw.shape[4])
    V = dict(V)
    for n, t in zip(_SMALL_SHARDED, _unpack(gathered[-1], small_shapes, lead=(N_DEV,))):
        if n == "conv_w_dw":
            V[n] = t.transpose(1, 2, 0, 3).reshape(t.shape[1], t.shape[2], -1)
        else:
            V[n] = _gathered_vec(t)

    buckets = _band_buckets()
    onehot = (buckets.reshape(1, -1) == jnp.arange(NUM_BUCKETS)[:, None]).astype(F32)
    bias_tab = _bias_table_call("bias_table", V["rel_bias"], onehot).reshape(n_heads, Q_BLOCK, 2 * Q_BLOCK)
    sinks3 = V["attn_sinks"].reshape(n_heads, 1, 1)
    pb = p.astype(BF16)

    def gather_next(keys):
        return _GatherOwn([shard(k) for k in keys]) if keys else None

    def finish_gather(name, keys, bufs):
        W.update(zip(keys, _gather_forward_call(name, bufs)))

    saved = []
    hb = _rms_call("rms_in", x, vec(V["norm_mix"], 0))
    for i in range(depth):
        kind, j = i % 3, i // 3
        sv = {"x0": x, "h0": hb}
        g_ffn = vec(V["norm_ffn"], i)
        if kind == 0:
            a_pre, g_pre, u1 = _conv_in_call(f"conv_in{i}", hb, W["conv_w_in", j], 0, vec(V["conv_b_in"], j))
            u2, u4 = _dwconv_call(f"dwconv{i}", u1, V["conv_w_dw"][j], vec(V["conv_b_dw"], j), vec(V["conv_ln_g"], j), vec(V["conv_ln_b"], j))
            x1, h2 = _fullmm_res_call(f"conv_out{i}", u4, W["conv_w_out", j], 0, x, vec(V["conv_b_out"], j), g_ffn)
            sv.update(a_pre=a_pre, g_pre=g_pre, u1=u1, u2=u2, u4=u4)
        elif kind == 1:
            mix = _pool_mix_call(f"pool_mix{i}", x, vec(V["norm_mix"], i))
            x1, ypre, h2 = _pool_out_call(f"pool_out{i}", mix, wp, vec(V["pool_scale"], j), x, g_ffn)
            sv.update(mix=mix, ypre=ypre)
        else:
            qkv = _colmm_call(f"qkv{i}", hb, W["attn_w_qkv", j], 0, F32)
            q = _to_heads(qkv[:, :D], n_heads)
            k = _to_heads(qkv[:, D:D + n_kv * HEAD_DIM], n_kv)
            v = _to_heads(qkv[:, D + n_kv * HEAD_DIM:], n_kv)
            o = _attn_fwd_call(f"attn{i}", q, k, v, vec(V["attn_q_norm"], j), vec(V["attn_k_norm"], j), bias_tab, sinks3)
            ob = _from_heads(o)
            x1, h2 = _fullmm_res_call(f"attn_out{i}", ob, W["attn_w_o", j], 0, x, None, g_ffn)
            sv.update(q=q, k=k, v=v, ob=ob)
        more = i + 1 < depth
        keys_a = [(n, i + 1) for n in _FFN_UP_WEIGHTS] if more else []
        keys_b = [(n, i + 1) for n in _FFN_REST_WEIGHTS] + _mixer_keys(i + 1) if more else []
        res = _ffn_up_call(f"ffn_up{i}", h2, W["ffn_w_gate", i], W["ffn_w_up", i], 0, comm=gather_next(keys_a))
        if more:
            (gpre, upre, act), bufs = res
            finish_gather(f"gather_fwd_a{i + 1}", keys_a, bufs)
        else:
            gpre, upre, act = res
        res = _ffn_down_call(f"ffn_down{i}", act, W["ffn_w_down", i], 0, x1, vec(V["norm_ple"], i), comm=gather_next(keys_b))
        if more:
            (x2, h3), bufs = res
            finish_gather(f"gather_fwd_b{i + 1}", keys_b, bufs)
        else:
            x2, h3 = res
        pp = _colmm_call(f"ple_proj{i}", pb[i], W["ple_w_proj", i], 0, BF16)
        g_next = vec(V["norm_mix"], i + 1) if more else None
        x3, gate, hb_next = _ple_gate_call(f"ple_gate{i}", h3, W["ple_w_gate", i], 0, x2, vec(V["ple_b_gate"], i), pp, g_next)
        sv.update(x1=x1, h2=h2, gpre=gpre, upre=upre, act=act, x2=x2, h3=h3, pp=pp, gate=gate)
        saved.append(sv)
        x, hb = x3, hb_next

    dx, dxb, loss_tile = _loss_call("loss", x, target)
    loss = loss_tile[0, 0]

    landed = {}
    GV = {n: [None] * V[n].shape[0] for n in ("norm_mix", "norm_ffn", "norm_ple", "ple_b_gate", "conv_b_in", "conv_w_dw", "conv_b_dw", "conv_ln_g", "conv_ln_b", "conv_b_out", "pool_scale")}

    def scatter_of(pending):
        return _ScatterSlots([g for _, g in pending]) if pending else None

    def record(pending, bufs):
        landed.update(zip([k for k, _ in pending], bufs))

    mixer_pending = []
    for i in reversed(range(depth)):
        kind, j = i % 3, i // 3
        sv = saved[i]
        dz, dpp, db_gate = _ple_bwd_elt_call(f"ple_bwd{i}", dx, sv["gate"], sv["pp"])
        GV["ple_b_gate"][i] = db_gate
        pending = [(("ple_w_gate", i), _grad_w_call(f"g_ple_gate{i}", sv["h3"], "nat", dz, "full"))]
        pending.append((("ple_w_proj", i), _grad_w_call(f"g_ple_proj{i}", pb[i], "full", dpp, "nat")))
        dx, dxb, dg = _dx_full_rms_call(f"d_x2_{i}", dz, W["ple_w_gate", i], 0, sv["x2"], vec(V["norm_ple"], i), dx)
        GV["norm_ple"][i] = dg
        res = _ffn_bwd_hidden_call(f"ffn_bwd{i}", dxb, W["ffn_w_down", i], 0, sv["gpre"], sv["upre"], comm=scatter_of(mixer_pending))
        if mixer_pending:
            (dgp, dup), bufs = res
            record(mixer_pending, bufs)
        else:
            dgp, dup = res
        pending.append((("ffn_w_down", i), _grad_w_call(f"g_ffn_down{i}", sv["act"], "chunk", dxb, "full")))
        (g_gate, g_up), bufs = _grad_w_call(f"g_ffn_up{i}", sv["h2"], "full", [dgp, dup], "chunk", comm=scatter_of(pending))
        record(pending, bufs)
        pending = [(("ffn_w_gate", i), g_gate), (("ffn_w_up", i), g_up)]
        (dx, dxb, dg, colsum), bufs = _dx_colsharded_rms_call(
            f"d_x1_{i}", [dgp, dup], True, [W["ffn_w_gate", i], W["ffn_w_up", i]], 0, sv["x1"], vec(V["norm_ffn"], i), dx, kind == 0,
            comm=scatter_of(pending))
        record(pending, bufs)
        GV["norm_ffn"][i] = dg
        g_mix = vec(V["norm_mix"], i)
        if kind == 0:
            GV["conv_b_out"][j] = colsum
            g_out = _grad_w_call(f"g_conv_out{i}", sv["u4"], "nat", dxb, "full")
            du2, d_ln_g, d_ln_b, d_b_dw = _conv_out_bwd_call(f"conv_out_bwd{i}", dxb, W["conv_w_out", j], 0, sv["u2"], vec(V["conv_ln_g"], j), vec(V["conv_ln_b"], j))
            du, d_b_in, d_w_dw = _dwconv_bwd_call(f"dwconv_bwd{i}", du2, sv["u1"], V["conv_w_dw"][j], sv["a_pre"], sv["g_pre"])
            GV["conv_ln_g"][j], GV["conv_ln_b"][j], GV["conv_b_dw"][j] = d_ln_g, d_ln_b, d_b_dw
            GV["conv_b_in"][j], GV["conv_w_dw"][j] = d_b_in, d_w_dw[:CONV_WIDTH]
            g_in = _grad_w_call(f"g_conv_in{i}", sv["h0"], "full", du, "nat")
            mixer_pending = [(("conv_w_in", j), g_in), (("conv_w_out", j), g_out)]
            dx, dxb, dg, _ = _dx_colsharded_rms_call(f"d_x0_{i}", [du], False, [W["conv_w_in", j]], 0, sv["x0"], g_mix, dx, False)
        elif kind == 1:
            dmix, dyp, d_scale = _pool_out_bwd_call(f"pool_out_bwd{i}", dx, sv["ypre"], vec(V["pool_scale"], j), wp)
            GV["pool_scale"][j] = d_scale
            g_pool = _pool_w_grad_call(f"g_pool_w{i}", sv["mix"], dyp, ng)
            gc = g_pool.shape[1]
            g_pool = g_pool.reshape(ng, N_DEV, gc // N_DEV, gc).transpose(1, 0, 2, 3).reshape(N_DEV, ng * (gc // N_DEV), gc)
            mixer_pending = [(("pool_w", j), g_pool.astype(BF16))]
            dx, dxb, dg = _pool_mix_bwd_call(f"pool_mix_bwd{i}", dmix, sv["x0"], g_mix, dx)
        else:
            g_o = _grad_w_call(f"g_attn_o{i}", sv["ob"], "nat", dxb, "full")
            do = _to_heads(_dx_full_call(f"d_attn_o{i}", dxb, W["attn_w_o", j], 0, F32), n_heads)
            dq, dk, dv, dqg, dkg, dbias, dsink = _attn_bwd_call(
                f"attn_bwd{i}", sv["q"], sv["k"], sv["v"], do, vec(V["attn_q_norm"], j), vec(V["attn_k_norm"], j), bias_tab, sinks3)
            GV["attn_q_norm"] = jnp.sum(dqg, axis=0)
            GV["attn_k_norm"] = jnp.sum(dkg, axis=0)
            GV["attn_sinks"] = dsink.reshape(1, n_heads)
            GV["rel_bias"] = _bucket_sum_call(f"g_rel_bias{i}", onehot, dbias.reshape(n_heads, -1))
            dqkv = jnp.concatenate([_from_heads(dq), _from_heads(dk), _from_heads(dv)], axis=-1).astype(BF16)
            g_qkv = _grad_w_call(f"g_qkv{i}", sv["h0"], "full", dqkv, "nat")
            mixer_pending = [(("attn_w_qkv", j), g_qkv), (("attn_w_o", j), g_o)]
            dx, dxb, dg, _ = _dx_colsharded_rms_call(f"d_x0_{i}", [dqkv], False, [W["attn_w_qkv", j]], 0, sv["x0"], g_mix, dx, False)
        GV["norm_mix"][i] = dg
    record(mixer_pending, _exchange_call("grad_scatter_tail", scatter_of(mixer_pending)))
    return loss, dx, landed, GV, V


_BIG = ("conv_w_in", "conv_w_out", "pool_w", "attn_w_qkv", "attn_w_o", "ffn_w_gate", "ffn_w_up", "ffn_w_down", "ple_w_proj", "ple_w_gate")
_SMALL_REPLICATED = ("norm_mix", "norm_ffn", "norm_ple", "pool_scale", "attn_q_norm", "attn_k_norm", "attn_sinks", "rel_bias", "ple_b_gate")
_WEIGHTS = ("norm_mix", "norm_ffn", "norm_ple", "conv_w_in", "conv_b_in", "conv_w_dw", "conv_b_dw", "conv_ln_g", "conv_ln_b", "conv_w_out",
            "conv_b_out", "pool_w", "pool_scale", "attn_w_qkv", "attn_q_norm", "attn_k_norm", "attn_sinks", "attn_w_o", "rel_bias",
            "ffn_w_gate", "ffn_w_up", "ffn_w_down", "ple_w_proj", "ple_w_gate", "ple_b_gate")
PACK_LANES = 128


def _pack(parts):
    flat = jnp.concatenate([t.reshape(-1).astype(F32) for t in parts])
    rows = -(-flat.shape[0] // (8 * PACK_LANES)) * 8
    flat = jnp.pad(flat, (0, rows * PACK_LANES - flat.shape[0]))
    return flat.reshape(rows, PACK_LANES)


def _unpack(pack, shapes, lead=()):
    flat = pack.reshape(*lead, -1)
    out, pos = [], 0
    for s in shapes:
        n = math.prod(s)
        out.append(flat[..., pos:pos + n].reshape(*lead, *s))
        pos += n
    return out


def _as2d(t):
    return t.reshape(-1, t.shape[-1])


def kernel(x, p, norm_mix, norm_ffn, norm_ple, conv_w_in, conv_b_in, conv_w_dw, conv_b_dw, conv_ln_g, conv_ln_b, conv_w_out, conv_b_out, pool_w, pool_scale, attn_w_qkv, attn_q_norm, attn_k_norm, attn_sinks, attn_w_o, rel_bias, ffn_w_gate, ffn_w_up, ffn_w_down, ple_w_proj, ple_w_gate, ple_b_gate, loss_target, m_norm_mix, m_norm_ffn, m_norm_ple, m_conv_w_in, m_conv_b_in, m_conv_w_dw, m_conv_b_dw, m_conv_ln_g, m_conv_ln_b, m_conv_w_out, m_conv_b_out, m_pool_w, m_pool_scale, m_attn_w_qkv, m_attn_q_norm, m_attn_k_norm, m_attn_sinks, m_attn_w_o, m_rel_bias, m_ffn_w_gate, m_ffn_w_up, m_ffn_w_down, m_ple_w_proj, m_ple_w_gate, m_ple_b_gate, v_norm_mix, v_norm_ffn, v_norm_ple, v_conv_w_in, v_conv_b_in, v_conv_w_dw, v_conv_b_dw, v_conv_ln_g, v_conv_ln_b, v_conv_w_out, v_conv_b_out, v_pool_w, v_pool_scale, v_attn_w_qkv, v_attn_q_norm, v_attn_k_norm, v_attn_sinks, v_attn_w_o, v_rel_bias, v_ffn_w_gate, v_ffn_w_up, v_ffn_w_down, v_ple_w_proj, v_ple_w_gate, v_ple_b_gate):
    given = dict(locals())
    w = {n: given[n] for n in _WEIGHTS}
    m = {n: given["m_" + n] for n in _WEIGHTS}
    v = {n: given["v_" + n] for n in _WEIGHTS}
    me = _slot(*_mesh_pos())

    wb = {n: w[n].astype(BF16) for n in _BIG}
    small_pack = _pack([w[n] for n in _SMALL_SHARDED])
    small_shapes = [w[n].shape for n in _SMALL_SHARDED]
    loss, grad_x, landed, GV, V = _step(x[0], p[:, 0], loss_target[0], wb, small_pack, small_shapes, {n: w[n] for n in _SMALL_REPLICATED})
    loss = lax.psum(loss, ("x", "y", "c"))

    small_names = list(_SMALL_REPLICATED) + list(_SMALL_SHARDED)
    small_full = []
    for n in small_names:
        g = GV[n]
        g = jnp.stack([t.reshape(V[n].shape[1:]) for t in g]) if isinstance(g, list) else g.reshape(V[n].shape)
        small_full.append(g)
    reduced = _unpack(_all_reduce_small_call("all_reduce_small", _pack(small_full)), [t.shape for t in small_full])
    small_grad = {}
    for n, g in zip(small_names, reduced):
        if n in _SMALL_SHARDED:
            c = w[n].shape[-1]
            g = lax.dynamic_slice_in_dim(g, me * c, c, axis=g.ndim - 1)
        small_grad[n] = g

    out = {}
    for n in _BIG:
        per_layer = []
        for l in range(w[n].shape[0]):
            land = landed[n, l]
            res = _adamw_call(f"adamw_{n}{l}", land, True, _as2d(w[n][l]), _as2d(m[n][l]), _as2d(v[n][l]))
            per_layer.append([t.reshape(w[n][l].shape) for t in res])
        out[n] = [jnp.stack([pl_[q] for pl_ in per_layer]) for q in range(4)]
    for n in small_names:
        res = _adamw_call(f"adamw_{n}", _as2d(small_grad[n]), False, _as2d(w[n]), _as2d(m[n]), _as2d(v[n]))
        out[n] = [t.reshape(w[n].shape) for t in res]

    grads = [out[n][0] for n in _WEIGHTS]
    deltas = [out[n][1] for n in _WEIGHTS]
    new_m = [out[n][2] for n in _WEIGHTS]
    new_v = [out[n][3] for n in _WEIGHTS]
    return (loss, grad_x[None], *grads, *deltas, *new_m, *new_v)
```

```python
import functools
import math

import jax
import jax.numpy as jnp
from jax import lax
from jax.experimental import pallas as pl
from jax.experimental.pallas import tpu as pltpu

F32, BF16 = jnp.float32, jnp.bfloat16
N_DEV = 8
EPS = 1e-6
NEG_INF = -1e30
HEAD_DIM = 64
Q_BLOCK = 128
CHUNK = 64
WINDOW_CHUNKS = 2
CONV_WIDTH = 31
CONV_HALO = 32
POOL_WINDOWS = (2, 4, 8, 16)
POOL_HALO = 16
NUM_BUCKETS = 32
REL_MAX_DIST = 128
ADAM_LR, ADAM_B1, ADAM_B2, ADAM_EPS, ADAM_WD, ADAM_STEP = 0.001, 0.9, 0.999, 1e-08, 0.01, 10
VMEM_LIMIT_BYTES = 44 * 1024 * 1024
BIG_VMEM_LIMIT_BYTES = 52 * 1024 * 1024
ROW_TILE = 512
FULL_ROW_TILE = 256
EPILOGUE_ROWS = 128
ELT_TILE = 256
CONV_TILE = 128
CONV_SUB = 32
CONV_LANES = 512
COL_TILE = 512
K_TILE = 2048
ADAM_BLOCK_BYTES = 1 << 20

_DIMS = {
    "nn": (((1,), (0,)), ((), ())),
    "nt": (((1,), (1,)), ((), ())),
    "tn": (((0,), (0,)), ((), ())),
}


def _dot(a, b, mode):
    return lax.dot_general(a, b, _DIMS[mode], preferred_element_type=F32)


def _tile(n, t):
    t = min(n, t)
    assert n % t == 0, (n, t)
    return t


def _div_tile(n, t):
    if n <= t:
        return n
    for cand in range(t // 16 * 16, 15, -16):
        if n % cand == 0:
            return cand
    return n


def _sds(shape, dtype):
    return jax.ShapeDtypeStruct(tuple(shape), dtype)


def _call(name, body, grid, in_specs, out_specs, out_shape, ins, scratch=(), comm=None, vmem=VMEM_LIMIT_BYTES):
    params = pltpu.CompilerParams(dimension_semantics=("arbitrary",) * len(grid), vmem_limit_bytes=vmem)
    if comm is None:
        return pl.pallas_call(
            body, name=name, grid=grid, in_specs=list(in_specs), out_specs=out_specs, out_shape=out_shape,
            scratch_shapes=list(scratch), compiler_params=params,
        )(*ins)
    single = not isinstance(out_shape, (list, tuple))
    own_specs = [out_specs] if single else list(out_specs)
    own_shape = [out_shape] if single else list(out_shape)
    n_in, n_out, n_scr = len(ins), len(own_shape), len(scratch)
    n_src, n_dst = len(comm.srcs), len(comm.out_shape)
    hbm = pl.BlockSpec(memory_space=pltpu.HBM)

    def with_comm(*refs):
        a = n_in
        b = a + n_src
        c = b + n_out
        d = c + n_dst
        e = d + n_scr
        src_refs, dst_refs, sem_refs = refs[a:b], refs[c:d], refs[e:]
        first = functools.reduce(jnp.logical_and, [pl.program_id(i) == 0 for i in range(len(grid))])
        last = functools.reduce(jnp.logical_and, [pl.program_id(i) == grid[i] - 1 for i in range(len(grid))])

        @pl.when(first)
        def _():
            comm.start(src_refs, dst_refs, sem_refs)

        body(*refs[:a], *refs[b:c], *refs[d:e])

        @pl.when(last)
        def _():
            comm.wait(src_refs, dst_refs, sem_refs)

    res = pl.pallas_call(
        with_comm, name=name, grid=grid,
        in_specs=list(in_specs) + [hbm] * n_src,
        out_specs=own_specs + [hbm] * n_dst,
        out_shape=own_shape + list(comm.out_shape),
        scratch_shapes=list(scratch) + list(comm.sems),
        compiler_params=params,
    )(*ins, *comm.srcs)
    own = res[0] if single else list(res[:n_out])
    return own, list(res[n_out:])


def _mesh_pos():
    return lax.axis_index("x"), lax.axis_index("y"), lax.axis_index("c")


def _slot(px, py, pc):
    return 4 * px + 2 * py + pc


def _peers(x, y, c):
    flips = [(fx, fy, fc) for fx in (0, 1) for fy in (0, 1) for fc in (0, 1)][1:]
    return [(1 - x if fx else x, 1 - y if fy else y, 1 - c if fc else c) for fx, fy, fc in flips]


def _remote(src, dst, send_sem, recv_sem, to):
    return pltpu.make_async_remote_copy(
        src_ref=src, dst_ref=dst, send_sem=send_sem, recv_sem=recv_sem, device_id=to, device_id_type=pl.DeviceIdType.MESH
    )


class _GatherOwn:
    N_TO = 4

    def __init__(self, shards):
        n = len(shards)
        self.srcs = list(shards)
        self.out_shape = [_sds((N_DEV, *s.shape), s.dtype) for s in shards]
        self.sems = [pltpu.SemaphoreType.DMA((self.N_TO * n,)), pltpu.SemaphoreType.DMA((self.N_TO * n,)), pltpu.SemaphoreType.DMA((n,))]

    def _copies(self, src_refs, dst_refs, sem_refs):
        send_sems, recv_sems, local_sems = sem_refs
        x, y, c = _mesh_pos()
        me = (x, y, c)
        targets = [(x, y, 1 - c), (1 - x, y, c), (x, 1 - y, c), (1 - x, 1 - y, c)]
        sends, recvs, local = [], [], []
        for t, (src, dst) in enumerate(zip(src_refs, dst_refs)):
            local.append(pltpu.make_async_copy(src, dst.at[_slot(*me)], local_sems.at[t]))
            for k, to in enumerate(targets):
                s = self.N_TO * t + k
                sends.append(_remote(src, dst.at[_slot(*me)], send_sems.at[s], recv_sems.at[s], to))
                recvs.append(_remote(src, dst.at[_slot(*to)], send_sems.at[s], recv_sems.at[s], to))
        return sends, recvs, local

    def start(self, src_refs, dst_refs, sem_refs):
        sends, _, local = self._copies(src_refs, dst_refs, sem_refs)
        for cp in local + sends:
            cp.start()

    def wait(self, src_refs, dst_refs, sem_refs):
        sends, recvs, local = self._copies(src_refs, dst_refs, sem_refs)
        for cp in recvs:
            cp.wait_recv()
        for cp in sends:
            cp.wait_send()
        for cp in local:
            cp.wait()


class _ScatterSlots:
    def __init__(self, grads):
        n = len(grads)
        self.srcs = list(grads)
        self.out_shape = [_sds(g.shape, g.dtype) for g in grads]
        self.sems = [pltpu.SemaphoreType.DMA((7 * n,)), pltpu.SemaphoreType.DMA((7 * n,)), pltpu.SemaphoreType.DMA((n,))]

    def _copies(self, src_refs, dst_refs, sem_refs):
        send_sems, recv_sems, local_sems = sem_refs
        x, y, c = _mesh_pos()
        me = _slot(x, y, c)
        sends, recvs, local = [], [], []
        for t, (src, dst) in enumerate(zip(src_refs, dst_refs)):
            local.append(pltpu.make_async_copy(src.at[me], dst.at[me], local_sems.at[t]))
            for k, to in enumerate(_peers(x, y, c)):
                s = 7 * t + k
                sends.append(_remote(src.at[_slot(*to)], dst.at[me], send_sems.at[s], recv_sems.at[s], to))
                recvs.append(_remote(src.at[me], dst.at[_slot(*to)], send_sems.at[s], recv_sems.at[s], to))
        return sends, recvs, local

    start = _GatherOwn.start
    wait = _GatherOwn.wait


def _sig(x):
    return 1.0 / (1.0 + jnp.exp(-x))


def _rms(x, g):
    r = lax.rsqrt(jnp.mean(x * x, axis=-1, keepdims=True) + EPS)
    return x * r * g


def _rms_bwd(dy, x, g):
    r = lax.rsqrt(jnp.mean(x * x, axis=-1, keepdims=True) + EPS)
    xh = x * r
    dg = jnp.sum(dy * xh, axis=0, keepdims=True)
    dxh = dy * g
    dx = r * (dxh - xh * jnp.mean(dxh * xh, axis=-1, keepdims=True))
    return dx, dg


def _accumulate(ref, val, first):
    @pl.when(first)
    def _():
        ref[...] = val

    @pl.when(jnp.logical_not(first))
    def _():
        ref[...] += val


def _kloop(k, nk, acc_refs, contribs, finish):
    @pl.when(k == 0)
    def _():
        for r, c in zip(acc_refs, contribs):
            r[...] = c

    @pl.when(k > 0)
    def _():
        for r, c in zip(acc_refs, contribs):
            r[...] += c

    @pl.when(k == nk - 1)
    def _():
        finish()


def _rms_call(name, x, g):
    S, D = x.shape
    ts = _tile(S, ELT_TILE)

    def body(x_ref, g_ref, o_ref):
        o_ref[...] = _rms(x_ref[...], g_ref[...]).astype(BF16)

    return _call(
        name, body, (S // ts,),
        [pl.BlockSpec((ts, D), lambda i: (i, 0)), pl.BlockSpec((1, D), lambda i: (0, 0))],
        pl.BlockSpec((ts, D), lambda i: (i, 0)), _sds((S, D), BF16), (x, g),
    )


def _conv_in_call(name, hb, w, layer, b_in):
    S, D = hb.shape
    nsh = w.shape[-1]
    half = N_DEV // 2
    assert nsh * half == D
    tm = _tile(S, ROW_TILE)

    def body(h_ref, wa_ref, wg_ref, ba_ref, bg_ref, a_ref, g_ref, u_ref):
        h = h_ref[...]
        a = _dot(h, wa_ref[...], "nn") + ba_ref[...]
        g = _dot(h, wg_ref[...], "nn") + bg_ref[...]
        a_ref[...] = a.astype(BF16)
        g_ref[...] = g.astype(BF16)
        u_ref[...] = a * _sig(g)

    out_spec = pl.BlockSpec((tm, nsh), lambda d, i: (i, d))
    return _call(
        name, body, (half, S // tm),
        [
            pl.BlockSpec((tm, D), lambda d, i: (i, 0)),
            pl.BlockSpec((None, None, D, nsh), lambda d, i: (d, layer, 0, 0)),
            pl.BlockSpec((None, None, D, nsh), lambda d, i: (d + half, layer, 0, 0)),
            pl.BlockSpec((1, nsh), lambda d, i: (0, d)),
            pl.BlockSpec((1, nsh), lambda d, i: (0, d + half)),
        ],
        [out_spec, out_spec, out_spec],
        [_sds((S, D), BF16), _sds((S, D), BF16), _sds((S, D), F32)],
        (hb, w, w, b_in, b_in),
    )


def _conv_taps(ext_ref, w_ref, row0, lane0, offset, reverse):
    acc = None
    for k in range(CONV_WIDTH):
        off = offset - k if reverse else offset + k
        term = w_ref[k:k + 1, lane0:lane0 + CONV_LANES] * ext_ref[pl.ds(row0 + off, CONV_SUB), pl.ds(lane0, CONV_LANES)]
        acc = term if acc is None else acc + term
    return acc


def _dwconv_call(name, u1, w_dw, b_dw, ln_g, ln_b):
    S, D = u1.shape
    tc = _tile(S, CONV_TILE)
    hb = tc // CONV_HALO
    lanes = min(D, CONV_LANES)
    assert lanes == CONV_LANES and D % CONV_LANES == 0 and tc % CONV_SUB == 0

    def body(cur_ref, halo_ref, w_ref, b_ref, g_ref, bb_ref, u2_ref, u4_ref, ext_ref):
        i = pl.program_id(0)
        ext_ref[0:CONV_HALO, :] = jnp.where(i > 0, halo_ref[...], 0.0)
        ext_ref[CONV_HALO:, :] = cur_ref[...]
        first_tap = CONV_HALO - (CONV_WIDTH - 1)
        for r in range(0, tc, CONV_SUB):
            for c in range(0, D, CONV_LANES):
                u2_ref[r:r + CONV_SUB, c:c + CONV_LANES] = (
                    _conv_taps(ext_ref, w_ref, r, c, first_tap, False) + b_ref[:, c:c + CONV_LANES]
                )
        u2 = u2_ref[...]
        mu = jnp.mean(u2, axis=-1, keepdims=True)
        xc = u2 - mu
        u3 = xc * lax.rsqrt(jnp.mean(xc * xc, axis=-1, keepdims=True) + EPS) * g_ref[...] + bb_ref[...]
        u4_ref[...] = (u3 * _sig(u3)).astype(BF16)

    vec = pl.BlockSpec((1, D), lambda i: (0, 0))
    row = pl.BlockSpec((tc, D), lambda i: (i, 0))
    return _call(
        name, body, (S // tc,),
        [
            row,
            pl.BlockSpec((CONV_HALO, D), lambda i: (jnp.maximum(i * hb - 1, 0), 0)),
            pl.BlockSpec((CONV_WIDTH, D), lambda i: (0, 0)),
            vec, vec, vec,
        ],
        [row, row],
        [_sds((S, D), F32), _sds((S, D), BF16)],
        (u1, u1, w_dw, b_dw, ln_g, ln_b),
        scratch=[pltpu.VMEM((tc + CONV_HALO, D), F32)],
    )


def _row_chunks(tm):
    step = min(tm, EPILOGUE_ROWS)
    return [slice(r, r + step) for r in range(0, tm, step)]


def _full_weight_spec(w, layer):
    _, _, ksh, D = w.shape
    return pl.BlockSpec((N_DEV, None, ksh, D), lambda i: (0, layer, 0, 0), pipeline_mode=pl.Buffered(1))


def _fullmm_res_call(name, a, w, layer, x_res, bias, g_next):
    S, D = x_res.shape
    K = a.shape[1]
    tm = _tile(S, FULL_ROW_TILE)
    has_b = bias is not None

    def body(*refs):
        a_ref, w_ref, x_ref = refs[:3]
        b_ref = refs[3] if has_b else None
        g_ref, xo_ref, ho_ref = refs[3 + int(has_b):]
        xn = x_ref[...] + _dot(a_ref[...], w_ref[...].reshape(K, D), "nn")
        if has_b:
            xn = xn + b_ref[...]
        xo_ref[...] = xn
        ho_ref[...] = _rms(xn, g_ref[...]).astype(BF16)

    row = pl.BlockSpec((tm, D), lambda i: (i, 0))
    vec = pl.BlockSpec((1, D), lambda i: (0, 0))
    in_specs = [pl.BlockSpec((tm, K), lambda i: (i, 0)), _full_weight_spec(w, layer), row] + ([vec] if has_b else []) + [vec]
    ins = [a, w, x_res] + ([bias] if has_b else []) + [g_next]
    return _call(name, body, (S // tm,), in_specs, [row, row], [_sds((S, D), F32), _sds((S, D), BF16)], ins)


SHARD_PAIR = 2


def _pair_spec(w, layer):
    _, _, fsh, D = w.shape
    return pl.BlockSpec((SHARD_PAIR, None, fsh, D), lambda d, i: (d, layer, 0, 0))


def _ffn_up_call(name, hb, wgt, wut, layer, comm=None):
    S, D = hb.shape
    fsh = wgt.shape[2]
    tn = SHARD_PAIR * fsh
    tm = _tile(S, ROW_TILE)

    def body(h_ref, wg_ref, wu_ref, g_ref, u_ref, a_ref):
        h = h_ref[...]
        g = _dot(h, wg_ref[...].reshape(tn, D), "nt")
        u = _dot(h, wu_ref[...].reshape(tn, D), "nt")
        g_ref[...] = g.astype(BF16)
        u_ref[...] = u.astype(BF16)
        a_ref[...] = (g * _sig(g) * u).astype(BF16)

    ospec = pl.BlockSpec((tm, tn), lambda d, i: (i, d))
    return _call(
        name, body, (N_DEV // SHARD_PAIR, S // tm),
        [pl.BlockSpec((tm, D), lambda d, i: (i, 0)), _pair_spec(wgt, layer), _pair_spec(wut, layer)],
        [ospec] * 3, [_sds((S, N_DEV * fsh), BF16)] * 3, (hb, wgt, wut), comm=comm,
    )


def _mm_rows_call(name, a, w, layer, add, comm=None):
    S, K = a.shape
    _, _, ksh, N = w.shape
    assert K == N_DEV * ksh
    tm = _tile(S, ROW_TILE)
    tn = _tile(N, COL_TILE)
    has_add = add is not None

    def body(*refs):
        a_ref, w_ref = refs[:2]
        o_ref = refs[-1]
        y = _dot(a_ref[...], w_ref[...].reshape(K, tn), "nn")
        o_ref[...] = y + refs[2][...] if has_add else y

    tile = pl.BlockSpec((tm, tn), lambda i, n: (i, n))
    in_specs = [pl.BlockSpec((tm, K), lambda i, n: (i, 0)), pl.BlockSpec((N_DEV, None, ksh, tn), lambda i, n: (0, layer, 0, n))]
    return _call(
        name, body, (S // tm, N // tn), in_specs + ([tile] if has_add else []), tile, _sds((S, N), F32),
        (a, w) + ((add,) if has_add else ()), comm=comm,
    )


def _res_rms_call(name, y, x_res, g_next):
    S, D = x_res.shape
    ts = _tile(S, ELT_TILE)

    def body(y_ref, x_ref, g_ref, xo_ref, ho_ref):
        xn = x_ref[...] + y_ref[...]
        xo_ref[...] = xn
        ho_ref[...] = _rms(xn, g_ref[...]).astype(BF16)

    row = pl.BlockSpec((ts, D), lambda i: (i, 0))
    return _call(
        name, body, (S // ts,), [row, row, pl.BlockSpec((1, D), lambda i: (0, 0))],
        [row, row], [_sds((S, D), F32), _sds((S, D), BF16)], (y, x_res, g_next),
    )


def _colmm_call(name, a, w, layer, out_dtype):
    S, K = a.shape
    nsh = w.shape[-1]
    tm = _tile(S, ROW_TILE)

    def body(a_ref, w_ref, o_ref):
        o_ref[...] = _dot(a_ref[...], w_ref[...], "nn").astype(out_dtype)

    return _call(
        name, body, (N_DEV, S // tm),
        [pl.BlockSpec((tm, K), lambda d, i: (i, 0)), pl.BlockSpec((None, None, K, nsh), lambda d, i: (d, layer, 0, 0))],
        pl.BlockSpec((tm, nsh), lambda d, i: (i, d)), _sds((S, N_DEV * nsh), out_dtype), (a, w),
    )


def _ple_gate_call(name, hb, w, layer, x_res, bias, pp, g_next):
    S, D = x_res.shape
    K = hb.shape[1]
    tm = _tile(S, FULL_ROW_TILE)
    has_g = g_next is not None

    def body(*refs):
        a_ref, w_ref, x_ref, b_ref, p_ref = refs[:5]
        g_ref = refs[5] if has_g else None
        outs = refs[5 + int(has_g):]
        gate = _sig(_dot(a_ref[...], w_ref[...].reshape(K, D), "nn") + b_ref[...])
        xn = x_ref[...] + gate * p_ref[...].astype(F32)
        outs[0][...] = xn
        outs[1][...] = gate.astype(BF16)
        if has_g:
            outs[2][...] = _rms(xn, g_ref[...]).astype(BF16)

    row = pl.BlockSpec((tm, D), lambda i: (i, 0))
    vec = pl.BlockSpec((1, D), lambda i: (0, 0))
    in_specs = [pl.BlockSpec((tm, K), lambda i: (i, 0)), _full_weight_spec(w, layer), row, vec, row]
    ins = [hb, w, x_res, bias, pp]
    out_specs, out_shape = [row, row], [_sds((S, D), F32), _sds((S, D), BF16)]
    if has_g:
        in_specs.append(vec)
        ins.append(g_next)
        out_specs.append(row)
        out_shape.append(_sds((S, D), BF16))
    res = _call(name, body, (S // tm,), in_specs, out_specs, out_shape, ins)
    return res if has_g else (res[0], res[1], None)


def _pool_mix_call(name, x, g):
    S, D = x.shape
    ts = _tile(S, ELT_TILE)
    hb = ts // POOL_HALO
    gc = D // len(POOL_WINDOWS)

    def body(cur_ref, halo_ref, g_ref, o_ref, ext_ref):
        i = pl.program_id(0)
        gain = g_ref[...]
        ext_ref[0:POOL_HALO, :] = jnp.where(i > 0, _rms(halo_ref[...], gain), 0.0)
        ext_ref[POOL_HALO:, :] = _rms(cur_ref[...], gain)
        t = i * ts + lax.broadcasted_iota(jnp.int32, (ts, 1), 0)
        for gi, win in enumerate(POOL_WINDOWS):
            lanes = pl.ds(gi * gc, gc)
            h = ext_ref[pl.ds(POOL_HALO, ts), lanes]
            acc = h
            for j in range(1, win):
                acc = acc + ext_ref[pl.ds(POOL_HALO - j, ts), lanes]
            cnt = jnp.minimum(t + 1, win).astype(F32)
            o_ref[:, gi * gc:(gi + 1) * gc] = (acc / cnt - h).astype(BF16)

    row = pl.BlockSpec((ts, D), lambda i: (i, 0))
    return _call(
        name, body, (S // ts,),
        [row, pl.BlockSpec((POOL_HALO, D), lambda i: (jnp.maximum(i * hb - 1, 0), 0)), pl.BlockSpec((1, D), lambda i: (0, 0))],
        row, _sds((S, D), BF16), (x, x, g), scratch=[pltpu.VMEM((ts + POOL_HALO, D), F32)],
    )


def _pool_out_call(name, mix, wp, scale, x_res, g_next):
    S, D = x_res.shape
    ng, gc, _ = wp.shape
    tm = _tile(S, FULL_ROW_TILE)

    def body(m_ref, w_ref, s_ref, x_ref, g_ref, xo_ref, y_ref, h_ref):
        parts = [_dot(m_ref[:, gi * gc:(gi + 1) * gc], w_ref[gi], "nn") for gi in range(ng)]
        ypre = jnp.concatenate(parts, axis=-1)
        y_ref[...] = ypre.astype(BF16)
        xn = x_ref[...] + ypre * s_ref[...]
        xo_ref[...] = xn
        h_ref[...] = _rms(xn, g_ref[...]).astype(BF16)

    row = pl.BlockSpec((tm, D), lambda i: (i, 0))
    vec = pl.BlockSpec((1, D), lambda i: (0, 0))
    return _call(
        name, body, (S // tm,),
        [row, pl.BlockSpec((ng, gc, gc), lambda i: (0, 0, 0)), vec, row, vec],
        [row, row, row], [_sds((S, D), F32), _sds((S, D), BF16), _sds((S, D), BF16)],
        (mix, wp, scale, x_res, g_next),
    )


def _attn_probs(q_ref, kp_ref, kc_ref, qg_ref, kg_ref, bias_ref, sink_ref, n):
    grp = q_ref.shape[0]
    q = q_ref[...]
    qn = _rms(q, qg_ref[...])
    k = jnp.concatenate([kp_ref[...], kc_ref[...]], axis=0)
    kn = _rms(k, kg_ref[...])
    s = _dot(qn.reshape(grp * Q_BLOCK, HEAD_DIM).astype(BF16), kn.astype(BF16), "nt") * (HEAD_DIM ** -0.5)
    s = s.reshape(grp, Q_BLOCK, 2 * Q_BLOCK) + bias_ref[...]
    qi = lax.broadcasted_iota(jnp.int32, (Q_BLOCK, 2 * Q_BLOCK), 0)
    kj = lax.broadcasted_iota(jnp.int32, (Q_BLOCK, 2 * Q_BLOCK), 1)
    qc = qi // CHUNK
    kc = kj // CHUNK - Q_BLOCK // CHUNK
    ok = (kc <= qc) & (kc >= qc - WINDOW_CHUNKS) & ((n > 0) | (kj >= Q_BLOCK))
    s = jnp.where(ok[None], s, NEG_INF)
    sink = sink_ref[...]
    m = jnp.maximum(jnp.max(s, axis=-1, keepdims=True), sink)
    e = jnp.exp(s - m)
    es = jnp.exp(sink - m)
    inv = 1.0 / (jnp.sum(e, axis=-1, keepdims=True) + es)
    return q, qn, k, kn, e * inv, es * inv


def _attn_specs(grp, nb):
    qspec = pl.BlockSpec((grp, Q_BLOCK, HEAD_DIM), lambda j, n: (j, jnp.minimum(n, nb - 1), 0))
    prev = pl.BlockSpec((None, Q_BLOCK, HEAD_DIM), lambda j, n: (j, jnp.maximum(n - 1, 0), 0))
    cur = pl.BlockSpec((None, Q_BLOCK, HEAD_DIM), lambda j, n: (j, jnp.minimum(n, nb - 1), 0))
    gain = pl.BlockSpec((1, HEAD_DIM), lambda j, n: (0, 0))
    bias = pl.BlockSpec((grp, Q_BLOCK, 2 * Q_BLOCK), lambda j, n: (j, 0, 0))
    sink = pl.BlockSpec((grp, 1, 1), lambda j, n: (j, 0, 0))
    return qspec, prev, cur, gain, bias, sink


def _attn_fwd_call(name, q, k, v, qg, kg, bias, sinks):
    H, S, _ = q.shape
    n_kv = k.shape[0]
    grp = H // n_kv
    nb = S // Q_BLOCK

    def body(q_ref, kp_ref, kc_ref, vp_ref, vc_ref, qg_ref, kg_ref, bias_ref, sink_ref, o_ref):
        n = pl.program_id(1)
        _, _, _, _, p, _ = _attn_probs(q_ref, kp_ref, kc_ref, qg_ref, kg_ref, bias_ref, sink_ref, n)
        vv = jnp.concatenate([vp_ref[...], vc_ref[...]], axis=0).astype(BF16)
        o = _dot(p.reshape(grp * Q_BLOCK, 2 * Q_BLOCK).astype(BF16), vv, "nn")
        o_ref[...] = o.reshape(grp, Q_BLOCK, HEAD_DIM).astype(BF16)

    qspec, prev, cur, gain, bspec, sspec = _attn_specs(grp, nb)
    return _call(
        name, body, (n_kv, nb),
        [qspec, prev, cur, prev, cur, gain, gain, bspec, sspec],
        qspec, _sds((H, S, HEAD_DIM), BF16), (q, k, k, v, v, qg, kg, bias, sinks),
    )


def _loss_call(name, y, target):
    S, D = y.shape
    ts = _tile(S, ELT_TILE)

    def body(y_ref, t_ref, d_ref, db_ref, l_ref):
        err = y_ref[...] - t_ref[...]
        dy = err * (1.0 / D)
        d_ref[...] = dy
        db_ref[...] = dy.astype(BF16)
        part = 0.5 * jnp.sum(jnp.sum(err * err, axis=-1, keepdims=True), axis=0, keepdims=True) * (1.0 / D)
        _accumulate(l_ref, jnp.broadcast_to(part, l_ref.shape), pl.program_id(0) == 0)

    row = pl.BlockSpec((ts, D), lambda i: (i, 0))
    return _call(
        name, body, (S // ts,), [row, row],
        [row, row, pl.BlockSpec((8, 128), lambda i: (0, 0))],
        [_sds((S, D), F32), _sds((S, D), BF16), _sds((8, 128), F32)], (y, target),
    )


def _ple_bwd_elt_call(name, dx, gate, pp):
    S, D = dx.shape
    ts = _tile(S, ELT_TILE)

    def body(dx_ref, gt_ref, p_ref, dz_ref, dp_ref, db_ref):
        d = dx_ref[...]
        gt = gt_ref[...].astype(F32)
        dz = d * p_ref[...].astype(F32) * gt * (1.0 - gt)
        dz_ref[...] = dz.astype(BF16)
        dp_ref[...] = (d * gt).astype(BF16)
        _accumulate(db_ref, jnp.sum(dz, axis=0, keepdims=True), pl.program_id(0) == 0)

    row = pl.BlockSpec((ts, D), lambda i: (i, 0))
    return _call(
        name, body, (S // ts,), [row, row, row],
        [row, row, pl.BlockSpec((1, D), lambda i: (0, 0))],
        [_sds((S, D), BF16), _sds((S, D), BF16), _sds((1, D), F32)], (dx, gate, pp),
    )


def _grad_w_call(name, a, a_mode, b, b_mode, comm=None):
    bs = b if isinstance(b, (list, tuple)) else [b]
    S = a.shape[-2]
    tk = _tile(S, K_TILE)
    nk = S // tk

    def spec(arr, mode):
        if mode == "full":
            c = arr.shape[-1]
            return pl.BlockSpec((tk, c), lambda d, k: (k, 0)), c
        if mode == "nat":
            c = arr.shape[-1] // N_DEV
            return pl.BlockSpec((tk, c), lambda d, k: (k, d)), c
        c = arr.shape[-1]
        return pl.BlockSpec((None, tk, c), lambda d, k: (d, k, 0)), c

    a_spec, ca = spec(a, a_mode)
    b_specs, cbs = zip(*[spec(x, b_mode) for x in bs])
    nb = len(bs)

    def body(*refs):
        a_ref = refs[0]
        b_refs = refs[1:1 + nb]
        o_refs = refs[1 + nb:1 + 2 * nb]
        acc_refs = refs[1 + 2 * nb:]
        k = pl.program_id(1)
        av = a_ref[...]

        def finish():
            for o, acc in zip(o_refs, acc_refs):
                o[...] = acc[...].astype(BF16)

        _kloop(k, nk, acc_refs, [_dot(av, br[...], "tn") for br in b_refs], finish)

    res = _call(
        name, body, (N_DEV, nk), [a_spec, *b_specs],
        [pl.BlockSpec((None, ca, cb), lambda d, k: (d, 0, 0)) for cb in cbs],
        [_sds((N_DEV, ca, cb), BF16) for cb in cbs], (a, *bs),
        scratch=[pltpu.VMEM((ca, cb), F32) for cb in cbs], comm=comm,
    )
    own, landed = res if comm is not None else (res, None)
    own = own if isinstance(b, (list, tuple)) else own[0]
    return own if comm is None else (own, landed)


def _dx_full_call(name, dy, w, layer, out_dtype):
    S, D = dy.shape
    K = N_DEV * w.shape[2]
    tm = _tile(S, ROW_TILE)

    def body(dy_ref, w_ref, o_ref):
        o_ref[...] = _dot(dy_ref[...], w_ref[...].reshape(K, D), "nt").astype(out_dtype)

    return _call(
        name, body, (S // tm,), [pl.BlockSpec((tm, D), lambda i: (i, 0)), _full_weight_spec(w, layer)],
        pl.BlockSpec((tm, K), lambda i: (i, 0)), _sds((S, K), out_dtype), (dy, w),
    )


def _dx_full_rms_call(name, dy, w, layer, x, g, dres):
    S, D = x.shape
    K = N_DEV * w.shape[2]
    assert K == D
    tm = _tile(S, FULL_ROW_TILE)

    def body(dy_ref, w_ref, x_ref, g_ref, r_ref, dx_ref, dxb_ref, dg_ref):
        dh, dg = _rms_bwd(_dot(dy_ref[...], w_ref[...].reshape(K, dy_ref.shape[1]), "nt"), x_ref[...], g_ref[...])
        dx = r_ref[...] + dh
        dx_ref[...] = dx
        dxb_ref[...] = dx.astype(BF16)
        _accumulate(dg_ref, dg, pl.program_id(0) == 0)

    row = pl.BlockSpec((tm, D), lambda i: (i, 0))
    vec = pl.BlockSpec((1, D), lambda i: (0, 0))
    return _call(
        name, body, (S // tm,), [pl.BlockSpec((tm, dy.shape[1]), lambda i: (i, 0)), _full_weight_spec(w, layer), row, vec, row],
        [row, row, vec], [_sds((S, D), F32), _sds((S, D), BF16), _sds((1, D), F32)], (dy, w, x, g, dres),
    )


def _dx_colsharded_rms_call(name, dys, dy_chunked, ws, layer, x, g, dres, want_colsum, comm=None):
    S, D = x.shape
    nsh = ws[0].shape[-1]
    tm = _tile(S, ROW_TILE)
    nt = len(dys)

    def body(*refs):
        dy_refs = refs[:nt]
        w_refs = refs[nt:2 * nt]
        x_ref, g_ref, r_ref = refs[2 * nt:2 * nt + 3]
        outs = refs[2 * nt + 3:]
        acc_ref = outs[0]
        i, k = pl.program_id(0), pl.program_id(1)
        contrib = None
        for dr, wr in zip(dy_refs, w_refs):
            c = _dot(dr[...], wr[...], "nt")
            contrib = c if contrib is None else contrib + c

        def finish():
            dg = colsum = None
            for rows in _row_chunks(tm):
                dh, dg_part = _rms_bwd(acc_ref[rows, :], x_ref[rows, :], g_ref[...])
                dx = r_ref[rows, :] + dh
                outs[0][rows, :] = dx
                outs[1][rows, :] = dx.astype(BF16)
                dg = dg_part if dg is None else dg + dg_part
                if want_colsum:
                    part = jnp.sum(dx, axis=0, keepdims=True)
                    colsum = part if colsum is None else colsum + part
            _accumulate(outs[2], dg, i == 0)
            if want_colsum:
                _accumulate(outs[3], colsum, i == 0)

        _kloop(k, N_DEV, [acc_ref], [contrib], finish)

    if dy_chunked:
        dspec = pl.BlockSpec((None, tm, nsh), lambda i, k: (k, i, 0))
    else:
        dspec = pl.BlockSpec((tm, nsh), lambda i, k: (i, k))
    wspec = pl.BlockSpec((None, None, D, nsh), lambda i, k: (k, layer, 0, 0))
    row = pl.BlockSpec((tm, D), lambda i, k: (i, 0))
    row_once = pl.BlockSpec((tm, D), lambda i, k: (i, 0), pipeline_mode=pl.Buffered(1))
    vec = pl.BlockSpec((1, D), lambda i, k: (0, 0))
    out_specs = [row, row, vec] + ([vec] if want_colsum else [])
    out_shape = [_sds((S, D), F32), _sds((S, D), BF16), _sds((1, D), F32)] + ([_sds((1, D), F32)] if want_colsum else [])
    res = _call(
        name, body, (S // tm, N_DEV), [dspec] * nt + [wspec] * nt + [row_once, vec, row_once],
        out_specs, out_shape, (*dys, *ws, x, g, dres), comm=comm, vmem=BIG_VMEM_LIMIT_BYTES,
    )
    own, landed = res if comm is not None else (res, None)
    own = tuple(own) if want_colsum else (*own, None)
    return own if comm is None else (own, landed)


def _ffn_bwd_hidden_call(name, dyb, w, layer, gpre, upre, comm=None):
    S, D = dyb.shape
    fsh = w.shape[2]
    tn = SHARD_PAIR * fsh
    tm = _tile(S, ROW_TILE)

    def body(dy_ref, w_ref, g_ref, u_ref, dg_ref, du_ref):
        da = _dot(dy_ref[...], w_ref[...].reshape(tn, D), "nt")
        g = g_ref[...].astype(F32)
        u = u_ref[...].astype(F32)
        s = _sig(g)
        dg_ref[...] = (da * u * s * (1.0 + g * (1.0 - s))).astype(BF16)
        du_ref[...] = (da * g * s).astype(BF16)

    cspec = pl.BlockSpec((tm, tn), lambda d, i: (i, d))
    return _call(
        name, body, (N_DEV // SHARD_PAIR, S // tm),
        [pl.BlockSpec((tm, D), lambda d, i: (i, 0)), _pair_spec(w, layer), cspec, cspec],
        [cspec, cspec], [_sds((S, N_DEV * fsh), BF16)] * 2, (dyb, w, gpre, upre), comm=comm,
    )


def _grad_rows_call(name, a, b, comm=None):
    S, F = a.shape
    N = b.shape[1]
    fsh = F // N_DEV
    tr = SHARD_PAIR * fsh
    tn = _tile(N, 2 * COL_TILE)
    tk = _tile(S, K_TILE)
    nk = S // tk

    def body(a_ref, b_ref, o_ref, acc_ref):
        def finish():
            o_ref[...] = acc_ref[...].astype(BF16).reshape(SHARD_PAIR, fsh, tn)

        _kloop(pl.program_id(2), nk, [acc_ref], [_dot(a_ref[...], b_ref[...], "tn")], finish)

    res = _call(
        name, body, (N_DEV // SHARD_PAIR, N // tn, nk),
        [pl.BlockSpec((tk, tr), lambda d, n, k: (k, d)), pl.BlockSpec((tk, tn), lambda d, n, k: (k, n))],
        pl.BlockSpec((SHARD_PAIR, fsh, tn), lambda d, n, k: (d, 0, n)), _sds((N_DEV, fsh, N), BF16), (a, b),
        scratch=[pltpu.VMEM((tr, tn), F32)], comm=comm,
    )
    return res


def _rms_bwd_res_call(name, dh, x, g, dres, want_colsum):
    S, D = x.shape
    ts = _tile(S, ELT_TILE)

    def body(dh_ref, x_ref, g_ref, r_ref, *outs):
        i = pl.program_id(0)
        d, dg = _rms_bwd(dh_ref[...], x_ref[...], g_ref[...])
        dx = r_ref[...] + d
        outs[0][...] = dx
        outs[1][...] = dx.astype(BF16)
        _accumulate(outs[2], dg, i == 0)
        if want_colsum:
            _accumulate(outs[3], jnp.sum(dx, axis=0, keepdims=True), i == 0)

    row = pl.BlockSpec((ts, D), lambda i: (i, 0))
    vec = pl.BlockSpec((1, D), lambda i: (0, 0))
    out_specs = [row, row, vec] + ([vec] if want_colsum else [])
    out_shape = [_sds((S, D), F32), _sds((S, D), BF16), _sds((1, D), F32)] + ([_sds((1, D), F32)] if want_colsum else [])
    res = _call(name, body, (S // ts,), [row, row, vec, row], out_specs, out_shape, (dh, x, g, dres))
    return tuple(res) if want_colsum else (*res, None)


def _sum_slots_call(name, land):
    _, R, C = land.shape
    tr = _div_tile(R, ADAM_BLOCK_BYTES // (C * 4))

    def body(l_ref, o_ref):
        g = l_ref[0].astype(F32)
        for d in range(1, N_DEV):
            g = g + l_ref[d].astype(F32)
        o_ref[...] = g

    return _call(
        name, body, (R // tr,), [pl.BlockSpec((N_DEV, tr, C), lambda i: (0, i, 0))],
        pl.BlockSpec((tr, C), lambda i: (i, 0)), _sds((R, C), F32), (land,),
    )


def _conv_out_bwd_call(name, dyb, w, layer, u2, ln_g, ln_b):
    S, D = u2.shape
    ksh = w.shape[2]
    tm = _tile(S, ELT_TILE)

    def body(dy_ref, w_ref, u2_ref, g_ref, b_ref, du2_ref, dg_ref, db_ref, dbdw_ref):
        i = pl.program_id(0)
        dy = dy_ref[...]
        du4 = jnp.concatenate([_dot(dy, w_ref[d], "nt") for d in range(N_DEV)], axis=-1)
        u2 = u2_ref[...]
        mu = jnp.mean(u2, axis=-1, keepdims=True)
        xc = u2 - mu
        r = lax.rsqrt(jnp.mean(xc * xc, axis=-1, keepdims=True) + EPS)
        xh = xc * r
        gain = g_ref[...]
        u3 = xh * gain + b_ref[...]
        s = _sig(u3)
        du3 = du4 * s * (1.0 + u3 * (1.0 - s))
        dxh = du3 * gain
        du2 = r * (dxh - jnp.mean(dxh, axis=-1, keepdims=True) - xh * jnp.mean(dxh * xh, axis=-1, keepdims=True))
        du2_ref[...] = du2
        _accumulate(dg_ref, jnp.sum(du3 * xh, axis=0, keepdims=True), i == 0)
        _accumulate(db_ref, jnp.sum(du3, axis=0, keepdims=True), i == 0)
        _accumulate(dbdw_ref, jnp.sum(du2, axis=0, keepdims=True), i == 0)

    row = pl.BlockSpec((tm, D), lambda i: (i, 0))
    vec = pl.BlockSpec((1, D), lambda i: (0, 0))
    return _call(
        name, body, (S // tm,),
        [row, pl.BlockSpec((N_DEV, None, ksh, D), lambda i: (0, layer, 0, 0)), row, vec, vec],
        [row, vec, vec, vec], [_sds((S, D), F32)] + [_sds((1, D), F32)] * 3, (dyb, w, u2, ln_g, ln_b),
    )


def _dwconv_bwd_call(name, du2, u1, w_dw, a_pre, g_pre):
    S, D = u1.shape
    tc = _tile(S, CONV_TILE)
    hb = tc // CONV_HALO
    n_halo = S // CONV_HALO
    assert D % CONV_LANES == 0 and tc % CONV_SUB == 0
    wrows = CONV_HALO

    def body(d_cur, d_next, u_cur, u_prev, w_ref, a_ref, g_ref, du_ref, dbin_ref, dw_ref, dext_ref, uext_ref, du1_ref, dwacc_ref):
        i = pl.program_id(0)
        last = S // tc - 1
        dext_ref[0:tc, :] = d_cur[...]
        dext_ref[tc:, :] = jnp.where(i < last, d_next[...], 0.0)
        uext_ref[0:CONV_HALO, :] = jnp.where(i > 0, u_prev[...], 0.0)
        uext_ref[CONV_HALO:, :] = u_cur[...]

        @pl.when(i == 0)
        def _():
            dwacc_ref[...] = jnp.zeros_like(dwacc_ref)

        first_tap = CONV_HALO - (CONV_WIDTH - 1)
        for r in range(0, tc, CONV_SUB):
            for c in range(0, D, CONV_LANES):
                du1_ref[r:r + CONV_SUB, c:c + CONV_LANES] = _conv_taps(dext_ref, w_ref, r, c, CONV_WIDTH - 1, True)
                dcur = dext_ref[r:r + CONV_SUB, c:c + CONV_LANES]
                for k in range(CONV_WIDTH):
                    prod = dcur * uext_ref[pl.ds(r + first_tap + k, CONV_SUB), pl.ds(c, CONV_LANES)]
                    part = prod[0:8]
                    for q in range(8, CONV_SUB, 8):
                        part = part + prod[q:q + 8]
                    dwacc_ref[k, :, c:c + CONV_LANES] += part

        du1 = du1_ref[...]
        a = a_ref[...].astype(F32)
        sg = _sig(g_ref[...].astype(F32))
        da = du1 * sg
        dgate = du1 * a * sg * (1.0 - sg)
        du_ref[:, 0:D] = da.astype(BF16)
        du_ref[:, D:2 * D] = dgate.astype(BF16)
        _accumulate(dbin_ref, jnp.concatenate([jnp.sum(da, axis=0, keepdims=True), jnp.sum(dgate, axis=0, keepdims=True)], axis=-1), i == 0)

        @pl.when(i == last)
        def _():
            for k in range(CONV_WIDTH):
                dw_ref[k:k + 1, :] = jnp.sum(dwacc_ref[k], axis=0, keepdims=True)
            dw_ref[CONV_WIDTH:, :] = jnp.zeros((wrows - CONV_WIDTH, D), F32)

    row = pl.BlockSpec((tc, D), lambda i: (i, 0))
    nxt = pl.BlockSpec((CONV_HALO, D), lambda i: (jnp.minimum((i + 1) * hb, n_halo - 1), 0))
    prv = pl.BlockSpec((CONV_HALO, D), lambda i: (jnp.maximum(i * hb - 1, 0), 0))
    return _call(
        name, body, (S // tc,),
        [row, nxt, row, prv, pl.BlockSpec((CONV_WIDTH, D), lambda i: (0, 0)), row, row],
        [pl.BlockSpec((tc, 2 * D), lambda i: (i, 0)), pl.BlockSpec((1, 2 * D), lambda i: (0, 0)), pl.BlockSpec((wrows, D), lambda i: (0, 0))],
        [_sds((S, 2 * D), BF16), _sds((1, 2 * D), F32), _sds((wrows, D), F32)],
        (du2, du2, u1, u1, w_dw, a_pre, g_pre),
        scratch=[
            pltpu.VMEM((tc + CONV_HALO, D), F32), pltpu.VMEM((tc + CONV_HALO, D), F32),
            pltpu.VMEM((tc, D), F32), pltpu.VMEM((CONV_WIDTH, 8, D), F32),
        ],
    )


def _pool_out_bwd_call(name, dy, ypre, scale, wp):
    S, D = dy.shape
    ng, gc, _ = wp.shape
    tm = _tile(S, FULL_ROW_TILE)

    def body(dy_ref, y_ref, s_ref, w_ref, dm_ref, dyp_ref, ds_ref):
        dy = dy_ref[...]
        _accumulate(ds_ref, jnp.sum(dy * y_ref[...].astype(F32), axis=0, keepdims=True), pl.program_id(0) == 0)
        dyp = (dy * s_ref[...]).astype(BF16)
        dyp_ref[...] = dyp
        dm_ref[...] = jnp.concatenate([_dot(dyp[:, gi * gc:(gi + 1) * gc], w_ref[gi], "nt") for gi in range(ng)], axis=-1)

    row = pl.BlockSpec((tm, D), lambda i: (i, 0))
    vec = pl.BlockSpec((1, D), lambda i: (0, 0))
    return _call(
        name, body, (S // tm,), [row, row, vec, pl.BlockSpec((ng, gc, gc), lambda i: (0, 0, 0))],
        [row, row, vec], [_sds((S, D), F32), _sds((S, D), BF16), _sds((1, D), F32)], (dy, ypre, scale, wp),
    )


def _pool_w_grad_call(name, mix, dyp, ng):
    S, D = mix.shape
    gc = D // ng
    tk = _tile(S, K_TILE)
    nk = S // tk

    def body(m_ref, d_ref, o_ref, acc_ref):
        def finish():
            o_ref[...] = acc_ref[...]

        _kloop(pl.program_id(1), nk, [acc_ref], [_dot(m_ref[...], d_ref[...], "tn")], finish)

    blk = pl.BlockSpec((tk, gc), lambda g, k: (k, g))
    return _call(
        name, body, (ng, nk), [blk, blk], pl.BlockSpec((None, gc, gc), lambda g, k: (g, 0, 0)),
        _sds((ng, gc, gc), F32), (mix, dyp), scratch=[pltpu.VMEM((gc, gc), F32)],
    )


def _pool_mix_bwd_call(name, dmix, x, g, dres):
    S, D = x.shape
    ts = _tile(S, ELT_TILE)
    hb = ts // POOL_HALO
    n_halo = S // POOL_HALO
    gc = D // len(POOL_WINDOWS)

    def body(cur_ref, nxt_ref, x_ref, g_ref, r_ref, dx_ref, dxb_ref, dg_ref, ext_ref, dh_ref):
        i = pl.program_id(0)
        last = S // ts - 1
        t = i * ts + lax.broadcasted_iota(jnp.int32, (ts + POOL_HALO, 1), 0)
        for gi, win in enumerate(POOL_WINDOWS):
            lanes = slice(gi * gc, (gi + 1) * gc)
            cnt = jnp.minimum(t + 1, win).astype(F32)
            ext_ref[0:ts, lanes] = cur_ref[:, lanes] / cnt[0:ts]
            ext_ref[ts:, lanes] = jnp.where(i < last, nxt_ref[:, lanes] / cnt[ts:], 0.0)
        for gi, win in enumerate(POOL_WINDOWS):
            lanes = pl.ds(gi * gc, gc)
            acc = ext_ref[pl.ds(0, ts), lanes]
            for j in range(1, win):
                acc = acc + ext_ref[pl.ds(j, ts), lanes]
            dh_ref[:, gi * gc:(gi + 1) * gc] = acc - cur_ref[:, gi * gc:(gi + 1) * gc]
        d, dg = _rms_bwd(dh_ref[...], x_ref[...], g_ref[...])
        dx = r_ref[...] + d
        dx_ref[...] = dx
        dxb_ref[...] = dx.astype(BF16)
        _accumulate(dg_ref, dg, i == 0)

    row = pl.BlockSpec((ts, D), lambda i: (i, 0))
    vec = pl.BlockSpec((1, D), lambda i: (0, 0))
    nxt = pl.BlockSpec((POOL_HALO, D), lambda i: (jnp.minimum((i + 1) * hb, n_halo - 1), 0))
    return _call(
        name, body, (S // ts,), [row, nxt, row, vec, row], [row, row, vec],
        [_sds((S, D), F32), _sds((S, D), BF16), _sds((1, D), F32)], (dmix, dmix, x, g, dres),
        scratch=[pltpu.VMEM((ts + POOL_HALO, D), F32), pltpu.VMEM((ts, D), F32)],
    )


def _attn_bwd_call(name, q, k, v, do, qg, kg, bias, sinks):
    H, S, _ = q.shape
    n_kv = k.shape[0]
    grp = H // n_kv
    nb = S // Q_BLOCK
    scale = HEAD_DIM ** -0.5

    def body(q_ref, kp_ref, kc_ref, vp_ref, vc_ref, do_ref, qg_ref, kg_ref, bias_ref, sink_ref,
             dq_ref, dk_ref, dv_ref, dqg_ref, dkg_ref, dbias_ref, dsink_ref, ck_ref, cv_ref):
        n = pl.program_id(1)

        @pl.when(n == 0)
        def _():
            dqg_ref[...] = jnp.zeros_like(dqg_ref)
            dkg_ref[...] = jnp.zeros_like(dkg_ref)
            dbias_ref[...] = jnp.zeros_like(dbias_ref)
            dsink_ref[...] = jnp.zeros_like(dsink_ref)
            ck_ref[...] = jnp.zeros_like(ck_ref)
            cv_ref[...] = jnp.zeros_like(cv_ref)

        def block_grads():
            q, qn, _, kn, p, ps = _attn_probs(q_ref, kp_ref, kc_ref, qg_ref, kg_ref, bias_ref, sink_ref, n)
            rows = grp * Q_BLOCK
            dob = do_ref[...].reshape(rows, HEAD_DIM).astype(BF16)
            vv = jnp.concatenate([vp_ref[...], vc_ref[...]], axis=0).astype(BF16)
            dp = _dot(dob, vv, "nt").reshape(grp, Q_BLOCK, 2 * Q_BLOCK)
            delta = jnp.sum(p * dp, axis=-1, keepdims=True)
            dl = p * (dp - delta)
            dbias_ref[...] += dl
            dsink_ref[...] += jnp.sum(-ps * delta, axis=1, keepdims=True)
            dlb = dl.reshape(rows, 2 * Q_BLOCK).astype(BF16)
            dqn = (_dot(dlb, kn.astype(BF16), "nn") * scale).reshape(grp, Q_BLOCK, HEAD_DIM)
            dkn = _dot(dlb, qn.reshape(rows, HEAD_DIM).astype(BF16), "tn") * scale
            dvv = _dot(p.reshape(rows, 2 * Q_BLOCK).astype(BF16), dob, "tn")
            qgain = qg_ref[...]
            r = lax.rsqrt(jnp.mean(q * q, axis=-1, keepdims=True) + EPS)
            qh = q * r
            dqg_ref[...] += jnp.sum(jnp.sum(dqn * qh, axis=1), axis=0, keepdims=True)
            dqh = dqn * qgain
            dq_ref[...] = r * (dqh - qh * jnp.mean(dqh * qh, axis=-1, keepdims=True))
            return dkn, dvv

        def finish_prev(dkn_prev, dv_prev):
            kraw = kp_ref[...]
            dk, dkg = _rms_bwd(dkn_prev, kraw, kg_ref[...])
            dk_ref[...] = dk
            dv_ref[...] = dv_prev
            dkg_ref[...] += dkg

        @pl.when(n < nb)
        def _():
            dkn, dvv = block_grads()

            @pl.when(n > 0)
            def _():
                finish_prev(ck_ref[...] + dkn[0:Q_BLOCK], cv_ref[...] + dvv[0:Q_BLOCK])

            ck_ref[...] = dkn[Q_BLOCK:]
            cv_ref[...] = dvv[Q_BLOCK:]

        @pl.when(n == nb)
        def _():
            finish_prev(ck_ref[...], cv_ref[...])

    qspec, prev, cur, gain, bspec, sspec = _attn_specs(grp, nb)
    kout = pl.BlockSpec((None, Q_BLOCK, HEAD_DIM), lambda j, n: (j, jnp.maximum(n - 1, 0), 0))
    gout = pl.BlockSpec((None, 1, HEAD_DIM), lambda j, n: (j, 0, 0))
    return _call(
        name, body, (n_kv, nb + 1),
        [qspec, prev, cur, prev, cur, qspec, gain, gain, bspec, sspec],
        [qspec, kout, kout, gout, gout, bspec, sspec],
        [
            _sds((H, S, HEAD_DIM), F32), _sds((n_kv, S, HEAD_DIM), F32), _sds((n_kv, S, HEAD_DIM), F32),
            _sds((n_kv, 1, HEAD_DIM), F32), _sds((n_kv, 1, HEAD_DIM), F32),
            _sds((H, Q_BLOCK, 2 * Q_BLOCK), F32), _sds((H, 1, 1), F32),
        ],
        (q, k, k, v, v, do, qg, kg, bias, sinks),
        scratch=[pltpu.VMEM((Q_BLOCK, HEAD_DIM), F32), pltpu.VMEM((Q_BLOCK, HEAD_DIM), F32)],
    )


def _bucket_sum_call(name, onehot, dbias):
    nbk, n = onehot.shape
    H = dbias.shape[0]

    def body(o_ref, d_ref, out_ref):
        out_ref[...] = lax.dot_general(o_ref[...], d_ref[...], _DIMS["nt"], precision=lax.Precision.HIGHEST, preferred_element_type=F32)

    return _call(
        name, body, (1,), [pl.BlockSpec((nbk, n), lambda i: (0, 0)), pl.BlockSpec((H, n), lambda i: (0, 0))],
        pl.BlockSpec((nbk, H), lambda i: (0, 0)), _sds((nbk, H), F32), (onehot, dbias),
    )


def _bias_table_call(name, rel_bias, onehot):
    nbk, n = onehot.shape
    H = rel_bias.shape[1]

    def body(r_ref, o_ref, out_ref):
        out_ref[...] = lax.dot_general(r_ref[...], o_ref[...], _DIMS["tn"], precision=lax.Precision.HIGHEST, preferred_element_type=F32)

    return _call(
        name, body, (1,), [pl.BlockSpec((nbk, H), lambda i: (0, 0)), pl.BlockSpec((nbk, n), lambda i: (0, 0))],
        pl.BlockSpec((H, n), lambda i: (0, 0)), _sds((H, n), F32), (rel_bias, onehot),
    )


def _exchange_call(name, comm):
    n_src, n_dst = len(comm.srcs), len(comm.out_shape)

    def body(*refs):
        src_refs, dst_refs, sem_refs = refs[:n_src], refs[n_src:n_src + n_dst], refs[n_src + n_dst:]
        comm.start(src_refs, dst_refs, sem_refs)
        comm.wait(src_refs, dst_refs, sem_refs)

    hbm = pl.BlockSpec(memory_space=pltpu.HBM)
    return pl.pallas_call(
        body, name=name, in_specs=[hbm] * n_src, out_specs=[hbm] * n_dst, out_shape=list(comm.out_shape),
        scratch_shapes=list(comm.sems), compiler_params=pltpu.CompilerParams(has_side_effects=True),
    )(*comm.srcs)


def _gather_forward_call(name, bufs):
    n = len(bufs)

    def body(*refs):
        in_refs, out_refs = refs[:n], refs[n:2 * n]
        send_sems, recv_sems = refs[2 * n:]
        x, y, c = _mesh_pos()
        sibling = (x, y, 1 - c)
        chips = [(1 - x, y), (x, 1 - y), (1 - x, 1 - y)]
        sends, recvs = [], []
        for t in range(n):
            for j, chip in enumerate(chips):
                s = 3 * t + j
                mine, theirs = _slot(*chip, c), _slot(*chip, 1 - c)
                sends.append(_remote(in_refs[t].at[mine], out_refs[t].at[mine], send_sems.at[s], recv_sems.at[s], sibling))
                recvs.append(_remote(in_refs[t].at[mine], out_refs[t].at[theirs], send_sems.at[s], recv_sems.at[s], sibling))
        for cp in sends:
            cp.start()
        for cp in recvs:
            cp.wait_recv()
        for cp in sends:
            cp.wait_send()

    hbm = pl.BlockSpec(memory_space=pltpu.HBM)
    return pl.pallas_call(
        body, name=name, in_specs=[hbm] * n, out_specs=[hbm] * n, out_shape=[_sds(b.shape, b.dtype) for b in bufs],
        scratch_shapes=[pltpu.SemaphoreType.DMA((3 * n,)), pltpu.SemaphoreType.DMA((3 * n,))],
        input_output_aliases={t: t for t in range(n)},
        compiler_params=pltpu.CompilerParams(has_side_effects=True),
    )(*bufs)


def _all_gather_call(name, shards):
    n = len(shards)

    def body(*refs):
        in_refs, out_refs = refs[:n], refs[n:2 * n]
        send_sems, recv_sems, local_sems = refs[2 * n:]
        x, y, c = _mesh_pos()
        me, sibling = (x, y, c), (x, y, 1 - c)
        chips = [(1 - x, y), (x, 1 - y), (1 - x, 1 - y)]

        def copy(t, k, block, to, src=None):
            dst = out_refs[t].at[_slot(*block)]
            return pltpu.make_async_remote_copy(
                src_ref=dst if src is None else src, dst_ref=dst,
                send_sem=send_sems.at[7 * t + k], recv_sem=recv_sems.at[7 * t + k],
                device_id=to, device_id_type=pl.DeviceIdType.MESH,
            )

        mine = [pltpu.make_async_copy(in_refs[t], out_refs[t].at[_slot(*me)], local_sems.at[t]) for t in range(n)]
        for cp in mine:
            cp.start()
        first = []
        for t in range(n):
            first.append(copy(t, 0, me, sibling, src=in_refs[t]))
            first += [copy(t, 1 + j, me, (*chip, c), src=in_refs[t]) for j, chip in enumerate(chips)]
        for cp in first:
            cp.start()
        passed = []
        for j, chip in enumerate(chips):
            for t in range(n):
                copy(t, 1 + j, (*chip, c), me).wait_recv()
                fwd = copy(t, 4 + j, (*chip, c), sibling)
                fwd.start()
                passed.append(fwd)
        for t in range(n):
            copy(t, 0, sibling, me).wait_recv()
            for j, chip in enumerate(chips):
                copy(t, 4 + j, (*chip, 1 - c), me).wait_recv()
        for cp in first + passed:
            cp.wait_send()
        for cp in mine:
            cp.wait()

    hbm = pl.BlockSpec(memory_space=pltpu.HBM)
    return pl.pallas_call(
        body, name=name,
        in_specs=[hbm] * n, out_specs=[hbm] * n,
        out_shape=[_sds((N_DEV, *s.shape), s.dtype) for s in shards],
        scratch_shapes=[pltpu.SemaphoreType.DMA((7 * n,)), pltpu.SemaphoreType.DMA((7 * n,)), pltpu.SemaphoreType.DMA((n,))],
        compiler_params=pltpu.CompilerParams(has_side_effects=True),
    )(*shards)


def _all_reduce_small_call(name, pack):
    R, C = pack.shape

    def body(in_ref, out_ref, land_ref, send_sems, recv_sems):
        x, y, c = _mesh_pos()
        me = _slot(x, y, c)
        peers = _peers(x, y, c)
        land_ref[me] = in_ref[...]
        sends = [
            pltpu.make_async_remote_copy(
                src_ref=in_ref, dst_ref=land_ref.at[me], send_sem=send_sems.at[k], recv_sem=recv_sems.at[k],
                device_id=to, device_id_type=pl.DeviceIdType.MESH,
            )
            for k, to in enumerate(peers)
        ]
        for cp in sends:
            cp.start()
        for k, frm in enumerate(peers):
            pltpu.make_async_remote_copy(
                src_ref=in_ref, dst_ref=land_ref.at[_slot(*frm)], send_sem=send_sems.at[k], recv_sem=recv_sems.at[k],
                device_id=frm, device_id_type=pl.DeviceIdType.MESH,
            ).wait_recv()
        for cp in sends:
            cp.wait_send()
        total = land_ref[0]
        for d in range(1, N_DEV):
            total = total + land_ref[d]
        out_ref[...] = total

    vmem = pl.BlockSpec(memory_space=pltpu.VMEM)
    return pl.pallas_call(
        body, name=name, in_specs=[vmem], out_specs=vmem, out_shape=_sds((R, C), F32),
        scratch_shapes=[pltpu.VMEM((N_DEV, R, C), F32), pltpu.SemaphoreType.DMA((7,)), pltpu.SemaphoreType.DMA((7,))],
        compiler_params=pltpu.CompilerParams(has_side_effects=True, vmem_limit_bytes=VMEM_LIMIT_BYTES),
    )(pack)


def _adamw_call(name, grad, landed, w, m, v):
    R, C = w.shape
    tr = _div_tile(R, ADAM_BLOCK_BYTES // (C * 4))
    c1 = 1.0 / (1.0 - ADAM_B1 ** ADAM_STEP)
    c2 = 1.0 / (1.0 - ADAM_B2 ** ADAM_STEP)

    def body(g_ref, w_ref, m_ref, v_ref, go_ref, d_ref, mo_ref, vo_ref):
        if landed:
            g = g_ref[0].astype(F32)
            for d in range(1, N_DEV):
                g = g + g_ref[d].astype(F32)
        else:
            g = g_ref[...]
        go_ref[...] = g
        mn = ADAM_B1 * m_ref[...] + (1.0 - ADAM_B1) * g
        vn = ADAM_B2 * v_ref[...] + (1.0 - ADAM_B2) * (g * g)
        mo_ref[...] = mn
        vo_ref[...] = vn
        d_ref[...] = -ADAM_LR * ((mn * c1) / (jnp.sqrt(vn * c2) + ADAM_EPS) + ADAM_WD * w_ref[...])

    blk = pl.BlockSpec((tr, C), lambda i: (i, 0))
    gspec = pl.BlockSpec((N_DEV, tr, C), lambda i: (0, i, 0)) if landed else blk
    return _call(name, body, (R // tr,), [gspec, blk, blk, blk], [blk] * 4, [_sds((R, C), F32)] * 4, (grad, w, m, v))


def _t5_bucket(rel):
    nb = NUM_BUCKETS // 2
    n = -rel
    ret = jnp.where(n < 0, nb, 0)
    n = jnp.abs(n)
    max_exact = nb // 2
    nf = jnp.maximum(n, 1).astype(jnp.float32)
    large = max_exact + (jnp.log(nf / max_exact) / math.log(REL_MAX_DIST / max_exact) * (nb - max_exact)).astype(jnp.int32)
    large = jnp.minimum(large, nb - 1)
    return ret + jnp.where(n < max_exact, n, large)


def _band_buckets():
    i = jnp.arange(Q_BLOCK)[:, None]
    j = jnp.arange(2 * Q_BLOCK)[None, :]
    return _t5_bucket(j - Q_BLOCK - i)


def _to_heads(t, n_heads):
    S = t.shape[0]
    return t.reshape(S, n_heads, HEAD_DIM).transpose(1, 0, 2)


def _from_heads(t):
    H, S, _ = t.shape
    return t.transpose(1, 0, 2).reshape(S, H * HEAD_DIM)


def _gathered_vec(t):
    nd, L, n = t.shape
    return t.transpose(1, 0, 2).reshape(L, nd * n)


_MIXER_WEIGHTS = {0: ("conv_w_in", "conv_w_out"), 1: (), 2: ("attn_w_qkv", "attn_w_o")}
_FFN_UP_WEIGHTS = ("ffn_w_gate", "ffn_w_up")
_FFN_REST_WEIGHTS = ("ffn_w_down", "ple_w_proj", "ple_w_gate")
_SMALL_SHARDED = ("conv_b_in", "conv_w_dw", "conv_b_dw", "conv_ln_g", "conv_ln_b", "conv_b_out")


def _mixer_keys(i):
    return [(n, i // 3) for n in _MIXER_WEIGHTS[i % 3]]


def _step(x, p, target, wb, small_pack, small_shapes, V):
    S, D = x.shape
    depth = V["norm_mix"].shape[0]
    n_heads = D // HEAD_DIM
    n_kv = (wb["attn_w_qkv"].shape[-1] * N_DEV - D) // (2 * HEAD_DIM)
    ng = len(POOL_WINDOWS)
    vec = lambda t, i: t[i][None, :]
    shard = lambda key: wb[key[0]][key[1]:key[1] + 1]

    first = [(n, 0) for n in _FFN_UP_WEIGHTS + _FFN_REST_WEIGHTS] + _mixer_keys(0) + [("pool_w", 0)]
    gathered = _all_gather_call("all_gather0", [shard(k) for k in first] + [small_pack])
    W = dict(zip(first, gathered[:-1]))
    pw = W.pop(("pool_w", 0))
    wp = pw[:, 0].transpose(1, 0, 2, 3).reshape(pw.shape[2], pw.shape[3] * N_DEV, pw.shape[4])
    V = dict(V)
    for n, t in zip(_SMALL_SHARDED, _unpack(gathered[-1], small_shapes, lead=(N_DEV,))):
        if n == "conv_w_dw":
            V[n] = t.transpose(1, 2, 0, 3).reshape(t.shape[1], t.shape[2], -1)
        else:
            V[n] = _gathered_vec(t)

    buckets = _band_buckets()
    onehot = (buckets.reshape(1, -1) == jnp.arange(NUM_BUCKETS)[:, None]).astype(F32)
    bias_tab = _bias_table_call("bias_table", V["rel_bias"], onehot).reshape(n_heads, Q_BLOCK, 2 * Q_BLOCK)
    sinks3 = V["attn_sinks"].reshape(n_heads, 1, 1)
    pb = p.astype(BF16)

    def gather_next(keys):
        return _GatherOwn([shard(k) for k in keys]) if keys else None

    def finish_gather(name, keys, bufs):
        W.update(zip(keys, _gather_forward_call(name, bufs)))

    saved = []
    hb = _rms_call("rms_in", x, vec(V["norm_mix"], 0))
    for i in range(depth):
        kind, j = i % 3, i // 3
        sv = {"x0": x, "h0": hb}
        g_ffn = vec(V["norm_ffn"], i)
        if kind == 0:
            a_pre, g_pre, u1 = _conv_in_call(f"conv_in{i}", hb, W["conv_w_in", j], 0, vec(V["conv_b_in"], j))
            u2, u4 = _dwconv_call(f"dwconv{i}", u1, V["conv_w_dw"][j], vec(V["conv_b_dw"], j), vec(V["conv_ln_g"], j), vec(V["conv_ln_b"], j))
            x1, h2 = _fullmm_res_call(f"conv_out{i}", u4, W["conv_w_out", j], 0, x, vec(V["conv_b_out"], j), g_ffn)
            sv.update(a_pre=a_pre, g_pre=g_pre, u1=u1, u2=u2, u4=u4)
        elif kind == 1:
            mix = _pool_mix_call(f"pool_mix{i}", x, vec(V["norm_mix"], i))
            x1, ypre, h2 = _pool_out_call(f"pool_out{i}", mix, wp, vec(V["pool_scale"], j), x, g_ffn)
            sv.update(mix=mix, ypre=ypre)
        else:
            qkv = _colmm_call(f"qkv{i}", hb, W["attn_w_qkv", j], 0, F32)
            q = _to_heads(qkv[:, :D], n_heads)
            k = _to_heads(qkv[:, D:D + n_kv * HEAD_DIM], n_kv)
            v = _to_heads(qkv[:, D + n_kv * HEAD_DIM:], n_kv)
            o = _attn_fwd_call(f"attn{i}", q, k, v, vec(V["attn_q_norm"], j), vec(V["attn_k_norm"], j), bias_tab, sinks3)
            ob = _from_heads(o)
            x1, h2 = _fullmm_res_call(f"attn_out{i}", ob, W["attn_w_o", j], 0, x, None, g_ffn)
            sv.update(q=q, k=k, v=v, ob=ob)
        more = i + 1 < depth
        keys_a = [(n, i + 1) for n in _FFN_UP_WEIGHTS] if more else []
        keys_b = [(n, i + 1) for n in _FFN_REST_WEIGHTS] + _mixer_keys(i + 1) if more else []
        res = _ffn_up_call(f"ffn_up{i}", h2, W["ffn_w_gate", i], W["ffn_w_up", i], 0, comm=gather_next(keys_a))
        if more:
            (gpre, upre, act), bufs = res
            finish_gather(f"gather_fwd_a{i + 1}", keys_a, bufs)
        else:
            gpre, upre, act = res
        res = _mm_rows_call(f"ffn_down{i}", act, W["ffn_w_down", i], 0, None, comm=gather_next(keys_b))
        if more:
            y, bufs = res
            finish_gather(f"gather_fwd_b{i + 1}", keys_b, bufs)
        else:
            y = res
        x2, h3 = _res_rms_call(f"ffn_res{i}", y, x1, vec(V["norm_ple"], i))
        pp = _colmm_call(f"ple_proj{i}", pb[i], W["ple_w_proj", i], 0, BF16)
        g_next = vec(V["norm_mix"], i + 1) if more else None
        x3, gate, hb_next = _ple_gate_call(f"ple_gate{i}", h3, W["ple_w_gate", i], 0, x2, vec(V["ple_b_gate"], i), pp, g_next)
        sv.update(x1=x1, h2=h2, gpre=gpre, upre=upre, act=act, x2=x2, h3=h3, pp=pp, gate=gate)
        saved.append(sv)
        x, hb = x3, hb_next

    dx, dxb, loss_tile = _loss_call("loss", x, target)
    loss = loss_tile[0, 0]

    landed = {}
    GV = {n: [None] * V[n].shape[0] for n in ("norm_mix", "norm_ffn", "norm_ple", "ple_b_gate", "conv_b_in", "conv_w_dw", "conv_b_dw", "conv_ln_g", "conv_ln_b", "conv_b_out", "pool_scale")}

    def scatter_of(pending):
        return _ScatterSlots([g for _, g in pending]) if pending else None

    def record(pending, bufs):
        landed.update(zip([k for k, _ in pending], bufs))

    mixer_pending = []
    for i in reversed(range(depth)):
        kind, j = i % 3, i // 3
        sv = saved[i]
        dz, dpp, db_gate = _ple_bwd_elt_call(f"ple_bwd{i}", dx, sv["gate"], sv["pp"])
        GV["ple_b_gate"][i] = db_gate
        pending = [(("ple_w_gate", i), _grad_w_call(f"g_ple_gate{i}", sv["h3"], "nat", dz, "full"))]
        pending.append((("ple_w_proj", i), _grad_w_call(f"g_ple_proj{i}", pb[i], "full", dpp, "nat")))
        dx, dxb, dg = _dx_full_rms_call(f"d_x2_{i}", dz, W["ple_w_gate", i], 0, sv["x2"], vec(V["norm_ple"], i), dx)
        GV["norm_ple"][i] = dg
        res = _ffn_bwd_hidden_call(f"ffn_bwd{i}", dxb, W["ffn_w_down", i], 0, sv["gpre"], sv["upre"], comm=scatter_of(mixer_pending))
        if mixer_pending:
            (dgp, dup), bufs = res
            record(mixer_pending, bufs)
        else:
            dgp, dup = res
        g_down = _grad_rows_call(f"g_ffn_down{i}", sv["act"], dxb)
        g_gate, bufs = _grad_rows_call(f"g_ffn_gate{i}", dgp, sv["h2"], comm=scatter_of(pending))
        record(pending, bufs)
        pending = [(("ffn_w_down", i), g_down)]
        g_up, bufs = _grad_rows_call(f"g_ffn_up{i}", dup, sv["h2"], comm=scatter_of(pending))
        record(pending, bufs)
        pending = [(("ffn_w_gate", i), g_gate)]
        dh, bufs = _mm_rows_call(f"d_h2g_{i}", dgp, W["ffn_w_gate", i], 0, None, comm=scatter_of(pending))
        record(pending, bufs)
        pending = [(("ffn_w_up", i), g_up)]
        dh, bufs = _mm_rows_call(f"d_h2_{i}", dup, W["ffn_w_up", i], 0, dh, comm=scatter_of(pending))
        record(pending, bufs)
        dx, dxb, dg, colsum = _rms_bwd_res_call(f"d_x1_{i}", dh, sv["x1"], vec(V["norm_ffn"], i), dx, kind == 0)
        GV["norm_ffn"][i] = dg
        g_mix = vec(V["norm_mix"], i)
        if kind == 0:
            GV["conv_b_out"][j] = colsum
            g_out = _grad_w_call(f"g_conv_out{i}", sv["u4"], "nat", dxb, "full")
            du2, d_ln_g, d_ln_b, d_b_dw = _conv_out_bwd_call(f"conv_out_bwd{i}", dxb, W["conv_w_out", j], 0, sv["u2"], vec(V["conv_ln_g"], j), vec(V["conv_ln_b"], j))
            du, d_b_in, d_w_dw = _dwconv_bwd_call(f"dwconv_bwd{i}", du2, sv["u1"], V["conv_w_dw"][j], sv["a_pre"], sv["g_pre"])
            GV["conv_ln_g"][j], GV["conv_ln_b"][j], GV["conv_b_dw"][j] = d_ln_g, d_ln_b, d_b_dw
            GV["conv_b_in"][j], GV["conv_w_dw"][j] = d_b_in, d_w_dw[:CONV_WIDTH]
            g_in = _grad_w_call(f"g_conv_in{i}", sv["h0"], "full", du, "nat")
            mixer_pending = [(("conv_w_in", j), g_in), (("conv_w_out", j), g_out)]
            dx, dxb, dg, _ = _dx_colsharded_rms_call(f"d_x0_{i}", [du], False, [W["conv_w_in", j]], 0, sv["x0"], g_mix, dx, False)
        elif kind == 1:
            dmix, dyp, d_scale = _pool_out_bwd_call(f"pool_out_bwd{i}", dx, sv["ypre"], vec(V["pool_scale"], j), wp)
            GV["pool_scale"][j] = d_scale
            g_pool = _pool_w_grad_call(f"g_pool_w{i}", sv["mix"], dyp, ng)
            gc = g_pool.shape[1]
            g_pool = g_pool.reshape(ng, N_DEV, gc // N_DEV, gc).transpose(1, 0, 2, 3).reshape(N_DEV, ng * (gc // N_DEV), gc)
            mixer_pending = [(("pool_w", j), g_pool.astype(BF16))]
            dx, dxb, dg = _pool_mix_bwd_call(f"pool_mix_bwd{i}", dmix, sv["x0"], g_mix, dx)
        else:
            g_o = _grad_w_call(f"g_attn_o{i}", sv["ob"], "nat", dxb, "full")
            do = _to_heads(_dx_full_call(f"d_attn_o{i}", dxb, W["attn_w_o", j], 0, F32), n_heads)
            dq, dk, dv, dqg, dkg, dbias, dsink = _attn_bwd_call(
                f"attn_bwd{i}", sv["q"], sv["k"], sv["v"], do, vec(V["attn_q_norm"], j), vec(V["attn_k_norm"], j), bias_tab, sinks3)
            GV["attn_q_norm"] = jnp.sum(dqg, axis=0)
            GV["attn_k_norm"] = jnp.sum(dkg, axis=0)
            GV["attn_sinks"] = dsink.reshape(1, n_heads)
            GV["rel_bias"] = _bucket_sum_call(f"g_rel_bias{i}", onehot, dbias.reshape(n_heads, -1))
            dqkv = jnp.concatenate([_from_heads(dq), _from_heads(dk), _from_heads(dv)], axis=-1).astype(BF16)
            g_qkv = _grad_w_call(f"g_qkv{i}", sv["h0"], "full", dqkv, "nat")
            mixer_pending = [(("attn_w_qkv", j), g_qkv), (("attn_w_o", j), g_o)]
            dx, dxb, dg, _ = _dx_colsharded_rms_call(f"d_x0_{i}", [dqkv], False, [W["attn_w_qkv", j]], 0, sv["x0"], g_mix, dx, False)
        GV["norm_mix"][i] = dg
    record(mixer_pending, _exchange_call("grad_scatter_tail", scatter_of(mixer_pending)))
    return loss, dx, landed, GV, V


_BIG = ("conv_w_in", "conv_w_out", "pool_w", "attn_w_qkv", "attn_w_o", "ffn_w_gate", "ffn_w_up", "ffn_w_down", "ple_w_proj", "ple_w_gate")
_SMALL_REPLICATED = ("norm_mix", "norm_ffn", "norm_ple", "pool_scale", "attn_q_norm", "attn_k_norm", "attn_sinks", "rel_bias", "ple_b_gate")
_WEIGHTS = ("norm_mix", "norm_ffn", "norm_ple", "conv_w_in", "conv_b_in", "conv_w_dw", "conv_b_dw", "conv_ln_g", "conv_ln_b", "conv_w_out",
            "conv_b_out", "pool_w", "pool_scale", "attn_w_qkv", "attn_q_norm", "attn_k_norm", "attn_sinks", "attn_w_o", "rel_bias",
            "ffn_w_gate", "ffn_w_up", "ffn_w_down", "ple_w_proj", "ple_w_gate", "ple_b_gate")
PACK_LANES = 128


def _pack(parts):
    flat = jnp.concatenate([t.reshape(-1).astype(F32) for t in parts])
    rows = -(-flat.shape[0] // (8 * PACK_LANES)) * 8
    flat = jnp.pad(flat, (0, rows * PACK_LANES - flat.shape[0]))
    return flat.reshape(rows, PACK_LANES)


def _unpack(pack, shapes, lead=()):
    flat = pack.reshape(*lead, -1)
    out, pos = [], 0
    for s in shapes:
        n = math.prod(s)
        out.append(flat[..., pos:pos + n].reshape(*lead, *s))
        pos += n
    return out


def _as2d(t):
    return t.reshape(-1, t.shape[-1])


def kernel(x, p, norm_mix, norm_ffn, norm_ple, conv_w_in, conv_b_in, conv_w_dw, conv_b_dw, conv_ln_g, conv_ln_b, conv_w_out, conv_b_out, pool_w, pool_scale, attn_w_qkv, attn_q_norm, attn_k_norm, attn_sinks, attn_w_o, rel_bias, ffn_w_gate, ffn_w_up, ffn_w_down, ple_w_proj, ple_w_gate, ple_b_gate, loss_target, m_norm_mix, m_norm_ffn, m_norm_ple, m_conv_w_in, m_conv_b_in, m_conv_w_dw, m_conv_b_dw, m_conv_ln_g, m_conv_ln_b, m_conv_w_out, m_conv_b_out, m_pool_w, m_pool_scale, m_attn_w_qkv, m_attn_q_norm, m_attn_k_norm, m_attn_sinks, m_attn_w_o, m_rel_bias, m_ffn_w_gate, m_ffn_w_up, m_ffn_w_down, m_ple_w_proj, m_ple_w_gate, m_ple_b_gate, v_norm_mix, v_norm_ffn, v_norm_ple, v_conv_w_in, v_conv_b_in, v_conv_w_dw, v_conv_b_dw, v_conv_ln_g, v_conv_ln_b, v_conv_w_out, v_conv_b_out, v_pool_w, v_pool_scale, v_attn_w_qkv, v_attn_q_norm, v_attn_k_norm, v_attn_sinks, v_attn_w_o, v_rel_bias, v_ffn_w_gate, v_ffn_w_up, v_ffn_w_down, v_ple_w_proj, v_ple_w_gate, v_ple_b_gate):
    given = dict(locals())
    w = {n: given[n] for n in _WEIGHTS}
    m = {n: given["m_" + n] for n in _WEIGHTS}
    v = {n: given["v_" + n] for n in _WEIGHTS}
    me = _slot(*_mesh_pos())

    wb = {n: w[n].astype(BF16) for n in _BIG}
    for n in _FFN_UP_WEIGHTS:
        wb[n] = wb[n].transpose(0, 2, 1)
    small_pack = _pack([w[n] for n in _SMALL_SHARDED])
    small_shapes = [w[n].shape for n in _SMALL_SHARDED]
    loss, grad_x, landed, GV, V = _step(x[0], p[:, 0], loss_target[0], wb, small_pack, small_shapes, {n: w[n] for n in _SMALL_REPLICATED})
    loss = lax.psum(loss, ("x", "y", "c"))

    small_names = list(_SMALL_REPLICATED) + list(_SMALL_SHARDED)
    small_full = []
    for n in small_names:
        g = GV[n]
        g = jnp.stack([t.reshape(V[n].shape[1:]) for t in g]) if isinstance(g, list) else g.reshape(V[n].shape)
        small_full.append(g)
    reduced = _unpack(_all_reduce_small_call("all_reduce_small", _pack(small_full)), [t.shape for t in small_full])
    small_grad = {}
    for n, g in zip(small_names, reduced):
        if n in _SMALL_SHARDED:
            c = w[n].shape[-1]
            g = lax.dynamic_slice_in_dim(g, me * c, c, axis=g.ndim - 1)
        small_grad[n] = g

    out = {}
    for n in _BIG:
        per_layer = []
        for l in range(w[n].shape[0]):
            land = landed[n, l]
            if n in _FFN_UP_WEIGHTS:
                g = _sum_slots_call(f"gsum_{n}{l}", land).T
                res = _adamw_call(f"adamw_{n}{l}", g, False, w[n][l], m[n][l], v[n][l])
            else:
                res = _adamw_call(f"adamw_{n}{l}", land, True, _as2d(w[n][l]), _as2d(m[n][l]), _as2d(v[n][l]))
            per_layer.append([t.reshape(w[n][l].shape) for t in res])
        out[n] = [jnp.stack([pl_[q] for pl_ in per_layer]) for q in range(4)]
    for n in small_names:
        res = _adamw_call(f"adamw_{n}", _as2d(small_grad[n]), False, _as2d(w[n]), _as2d(m[n]), _as2d(v[n]))
        out[n] = [t.reshape(w[n].shape) for t in res]

    grads = [out[n][0] for n in _WEIGHTS]
    deltas = [out[n][1] for n in _WEIGHTS]
    new_m = [out[n][2] for n in _WEIGHTS]
    new_v = [out[n][3] for n in _WEIGHTS]
    return (loss, grad_x[None], *grads, *deltas, *new_m, *new_v)
```

```python
import functools
import math

import jax
import jax.numpy as jnp
from jax import lax
from jax.experimental import pallas as pl
from jax.experimental.pallas import tpu as pltpu

F32, BF16 = jnp.float32, jnp.bfloat16
N_DEV = 8
EPS = 1e-6
NEG_INF = -1e30
HEAD_DIM = 64
Q_BLOCK = 128
CHUNK = 64
WINDOW_CHUNKS = 2
CONV_WIDTH = 31
CONV_HALO = 32
POOL_WINDOWS = (2, 4, 8, 16)
POOL_HALO = 16
NUM_BUCKETS = 32
REL_MAX_DIST = 128
ADAM_LR, ADAM_B1, ADAM_B2, ADAM_EPS, ADAM_WD, ADAM_STEP = 0.001, 0.9, 0.999, 1e-08, 0.01, 10
VMEM_LIMIT_BYTES = 44 * 1024 * 1024
BIG_VMEM_LIMIT_BYTES = 52 * 1024 * 1024
ROW_TILE = 512
FULL_ROW_TILE = 256
EPILOGUE_ROWS = 128
ELT_TILE = 256
CONV_TILE = 128
CONV_SUB = 32
CONV_LANES = 512
COL_TILE = 512
K_TILE = 2048
ADAM_BLOCK_BYTES = 1 << 20

_DIMS = {
    "nn": (((1,), (0,)), ((), ())),
    "nt": (((1,), (1,)), ((), ())),
    "tn": (((0,), (0,)), ((), ())),
}


def _dot(a, b, mode):
    return lax.dot_general(a, b, _DIMS[mode], preferred_element_type=F32)


def _tile(n, t):
    t = min(n, t)
    assert n % t == 0, (n, t)
    return t


def _div_tile(n, t):
    if n <= t:
        return n
    for cand in range(t // 16 * 16, 15, -16):
        if n % cand == 0:
            return cand
    return n


def _sds(shape, dtype):
    return jax.ShapeDtypeStruct(tuple(shape), dtype)


def _call(name, body, grid, in_specs, out_specs, out_shape, ins, scratch=(), comm=None, vmem=VMEM_LIMIT_BYTES):
    params = pltpu.CompilerParams(dimension_semantics=("arbitrary",) * len(grid), vmem_limit_bytes=vmem)
    if comm is None:
        return pl.pallas_call(
            body, name=name, grid=grid, in_specs=list(in_specs), out_specs=out_specs, out_shape=out_shape,
            scratch_shapes=list(scratch), compiler_params=params,
        )(*ins)
    single = not isinstance(out_shape, (list, tuple))
    own_specs = [out_specs] if single else list(out_specs)
    own_shape = [out_shape] if single else list(out_shape)
    n_in, n_out, n_scr = len(ins), len(own_shape), len(scratch)
    n_src, n_dst = len(comm.srcs), len(comm.out_shape)
    hbm = pl.BlockSpec(memory_space=pltpu.HBM)

    def with_comm(*refs):
        a = n_in
        b = a + n_src
        c = b + n_out
        d = c + n_dst
        e = d + n_scr
        src_refs, dst_refs, sem_refs = refs[a:b], refs[c:d], refs[e:]
        first = functools.reduce(jnp.logical_and, [pl.program_id(i) == 0 for i in range(len(grid))])
        last = functools.reduce(jnp.logical_and, [pl.program_id(i) == grid[i] - 1 for i in range(len(grid))])

        @pl.when(first)
        def _():
            comm.start(src_refs, dst_refs, sem_refs)

        body(*refs[:a], *refs[b:c], *refs[d:e])

        @pl.when(last)
        def _():
            comm.wait(src_refs, dst_refs, sem_refs)

    res = pl.pallas_call(
        with_comm, name=name, grid=grid,
        in_specs=list(in_specs) + [hbm] * n_src,
        out_specs=own_specs + [hbm] * n_dst,
        out_shape=own_shape + list(comm.out_shape),
        scratch_shapes=list(scratch) + list(comm.sems),
        compiler_params=params,
    )(*ins, *comm.srcs)
    own = res[0] if single else list(res[:n_out])
    return own, list(res[n_out:])


def _mesh_pos():
    return lax.axis_index("x"), lax.axis_index("y"), lax.axis_index("c")


def _slot(px, py, pc):
    return 4 * px + 2 * py + pc


def _peers(x, y, c):
    flips = [(fx, fy, fc) for fx in (0, 1) for fy in (0, 1) for fc in (0, 1)][1:]
    return [(1 - x if fx else x, 1 - y if fy else y, 1 - c if fc else c) for fx, fy, fc in flips]


def _remote(src, dst, send_sem, recv_sem, to):
    return pltpu.make_async_remote_copy(
        src_ref=src, dst_ref=dst, send_sem=send_sem, recv_sem=recv_sem, device_id=to, device_id_type=pl.DeviceIdType.MESH
    )


class _GatherOwn:
    N_TO = 4

    def __init__(self, shards):
        n = len(shards)
        self.srcs = list(shards)
        self.out_shape = [_sds((N_DEV, *s.shape), s.dtype) for s in shards]
        self.sems = [pltpu.SemaphoreType.DMA((self.N_TO * n,)), pltpu.SemaphoreType.DMA((self.N_TO * n,)), pltpu.SemaphoreType.DMA((n,))]

    def _copies(self, src_refs, dst_refs, sem_refs):
        send_sems, recv_sems, local_sems = sem_refs
        x, y, c = _mesh_pos()
        me = (x, y, c)
        targets = [(x, y, 1 - c), (1 - x, y, c), (x, 1 - y, c), (1 - x, 1 - y, c)]
        sends, recvs, local = [], [], []
        for t, (src, dst) in enumerate(zip(src_refs, dst_refs)):
            local.append(pltpu.make_async_copy(src, dst.at[_slot(*me)], local_sems.at[t]))
            for k, to in enumerate(targets):
                s = self.N_TO * t + k
                sends.append(_remote(src, dst.at[_slot(*me)], send_sems.at[s], recv_sems.at[s], to))
                recvs.append(_remote(src, dst.at[_slot(*to)], send_sems.at[s], recv_sems.at[s], to))
        return sends, recvs, local

    def start(self, src_refs, dst_refs, sem_refs):
        sends, _, local = self._copies(src_refs, dst_refs, sem_refs)
        for cp in local + sends:
            cp.start()

    def wait(self, src_refs, dst_refs, sem_refs):
        sends, recvs, local = self._copies(src_refs, dst_refs, sem_refs)
        for cp in recvs:
            cp.wait_recv()
        for cp in sends:
            cp.wait_send()
        for cp in local:
            cp.wait()


class _ScatterSlots:
    def __init__(self, grads):
        n = len(grads)
        self.srcs = list(grads)
        self.out_shape = [_sds(g.shape, g.dtype) for g in grads]
        self.sems = [pltpu.SemaphoreType.DMA((7 * n,)), pltpu.SemaphoreType.DMA((7 * n,)), pltpu.SemaphoreType.DMA((n,))]

    def _copies(self, src_refs, dst_refs, sem_refs):
        send_sems, recv_sems, local_sems = sem_refs
        x, y, c = _mesh_pos()
        me = _slot(x, y, c)
        sends, recvs, local = [], [], []
        for t, (src, dst) in enumerate(zip(src_refs, dst_refs)):
            local.append(pltpu.make_async_copy(src.at[me], dst.at[me], local_sems.at[t]))
            for k, to in enumerate(_peers(x, y, c)):
                s = 7 * t + k
                sends.append(_remote(src.at[_slot(*to)], dst.at[me], send_sems.at[s], recv_sems.at[s], to))
                recvs.append(_remote(src.at[me], dst.at[_slot(*to)], send_sems.at[s], recv_sems.at[s], to))
        return sends, recvs, local

    start = _GatherOwn.start
    wait = _GatherOwn.wait


def _sig(x):
    return 1.0 / (1.0 + jnp.exp(-x))


def _rms(x, g):
    r = lax.rsqrt(jnp.mean(x * x, axis=-1, keepdims=True) + EPS)
    return x * r * g


def _rms_bwd(dy, x, g):
    r = lax.rsqrt(jnp.mean(x * x, axis=-1, keepdims=True) + EPS)
    xh = x * r
    dg = jnp.sum(dy * xh, axis=0, keepdims=True)
    dxh = dy * g
    dx = r * (dxh - xh * jnp.mean(dxh * xh, axis=-1, keepdims=True))
    return dx, dg


def _accumulate(ref, val, first):
    @pl.when(first)
    def _():
        ref[...] = val

    @pl.when(jnp.logical_not(first))
    def _():
        ref[...] += val


def _kloop(k, nk, acc_refs, contribs, finish):
    @pl.when(k == 0)
    def _():
        for r, c in zip(acc_refs, contribs):
            r[...] = c

    @pl.when(k > 0)
    def _():
        for r, c in zip(acc_refs, contribs):
            r[...] += c

    @pl.when(k == nk - 1)
    def _():
        finish()


def _rms_call(name, x, g):
    S, D = x.shape
    ts = _tile(S, ELT_TILE)

    def body(x_ref, g_ref, o_ref):
        o_ref[...] = _rms(x_ref[...], g_ref[...]).astype(BF16)

    return _call(
        name, body, (S // ts,),
        [pl.BlockSpec((ts, D), lambda i: (i, 0)), pl.BlockSpec((1, D), lambda i: (0, 0))],
        pl.BlockSpec((ts, D), lambda i: (i, 0)), _sds((S, D), BF16), (x, g),
    )


def _conv_in_call(name, hb, w, layer, b_in, comm=None):
    S, D = hb.shape
    nsh = w.shape[-1]
    half = N_DEV // 2
    assert nsh * half == D
    tm = _tile(S, ROW_TILE)

    def body(h_ref, wa_ref, wg_ref, ba_ref, bg_ref, a_ref, g_ref, u_ref):
        h = h_ref[...]
        a = _dot(h, wa_ref[...], "nn") + ba_ref[...]
        g = _dot(h, wg_ref[...], "nn") + bg_ref[...]
        a_ref[...] = a.astype(BF16)
        g_ref[...] = g.astype(BF16)
        u_ref[...] = a * _sig(g)

    out_spec = pl.BlockSpec((tm, nsh), lambda d, i: (i, d))
    return _call(
        name, body, (half, S // tm),
        [
            pl.BlockSpec((tm, D), lambda d, i: (i, 0)),
            pl.BlockSpec((None, None, D, nsh), lambda d, i: (d, layer, 0, 0)),
            pl.BlockSpec((None, None, D, nsh), lambda d, i: (d + half, layer, 0, 0)),
            pl.BlockSpec((1, nsh), lambda d, i: (0, d)),
            pl.BlockSpec((1, nsh), lambda d, i: (0, d + half)),
        ],
        [out_spec, out_spec, out_spec],
        [_sds((S, D), BF16), _sds((S, D), BF16), _sds((S, D), F32)],
        (hb, w, w, b_in, b_in), comm=comm,
    )


SUBLANES = 8


def _fill_shifted(sh_ref, ext_ref, rows):
    ext_ref[rows:, :] = jnp.zeros((SUBLANES, ext_ref.shape[1]), F32)
    for s in range(SUBLANES):
        sh_ref[s] = ext_ref[pl.ds(s, rows), :]


def _shifted_window(sh_ref, row, lane0):
    s = row % SUBLANES
    return sh_ref[s, pl.ds(row - s, CONV_SUB), pl.ds(lane0, CONV_LANES)]


def _conv_taps(sh_ref, w_ref, row0, lane0, offset, reverse):
    acc = None
    for k in range(CONV_WIDTH):
        off = offset - k if reverse else offset + k
        term = w_ref[k:k + 1, lane0:lane0 + CONV_LANES] * _shifted_window(sh_ref, row0 + off, lane0)
        acc = term if acc is None else acc + term
    return acc


def _dwconv_call(name, u1, w_dw, b_dw, ln_g, ln_b, comm=None):
    S, D = u1.shape
    tc = _tile(S, CONV_TILE)
    hb = tc // CONV_HALO
    lanes = min(D, CONV_LANES)
    assert lanes == CONV_LANES and D % CONV_LANES == 0 and tc % CONV_SUB == 0

    def body(cur_ref, halo_ref, w_ref, b_ref, g_ref, bb_ref, u2_ref, u4_ref, ext_ref, sh_ref):
        i = pl.program_id(0)
        ext_ref[0:CONV_HALO, :] = jnp.where(i > 0, halo_ref[...], 0.0)
        ext_ref[CONV_HALO:tc + CONV_HALO, :] = cur_ref[...]
        _fill_shifted(sh_ref, ext_ref, tc + CONV_HALO)
        first_tap = CONV_HALO - (CONV_WIDTH - 1)
        for r in range(0, tc, CONV_SUB):
            for c in range(0, D, CONV_LANES):
                u2_ref[r:r + CONV_SUB, c:c + CONV_LANES] = (
                    _conv_taps(sh_ref, w_ref, r, c, first_tap, False) + b_ref[:, c:c + CONV_LANES]
                )
        u2 = u2_ref[...]
        mu = jnp.mean(u2, axis=-1, keepdims=True)
        xc = u2 - mu
        u3 = xc * lax.rsqrt(jnp.mean(xc * xc, axis=-1, keepdims=True) + EPS) * g_ref[...] + bb_ref[...]
        u4_ref[...] = (u3 * _sig(u3)).astype(BF16)

    vec = pl.BlockSpec((1, D), lambda i: (0, 0))
    row = pl.BlockSpec((tc, D), lambda i: (i, 0))
    return _call(
        name, body, (S // tc,),
        [
            row,
            pl.BlockSpec((CONV_HALO, D), lambda i: (jnp.maximum(i * hb - 1, 0), 0)),
            pl.BlockSpec((CONV_WIDTH, D), lambda i: (0, 0)),
            vec, vec, vec,
        ],
        [row, row],
        [_sds((S, D), F32), _sds((S, D), BF16)],
        (u1, u1, w_dw, b_dw, ln_g, ln_b),
        scratch=[pltpu.VMEM((tc + CONV_HALO + SUBLANES, D), F32), pltpu.VMEM((SUBLANES, tc + CONV_HALO, D), F32)], comm=comm,
    )


def _row_chunks(tm):
    step = min(tm, EPILOGUE_ROWS)
    return [slice(r, r + step) for r in range(0, tm, step)]


def _full_weight_spec(w, layer):
    _, _, ksh, D = w.shape
    return pl.BlockSpec((N_DEV, None, ksh, D), lambda i: (0, layer, 0, 0), pipeline_mode=pl.Buffered(1))


def _fullmm_res_call(name, a, w, layer, x_res, bias, g_next):
    S, D = x_res.shape
    K = a.shape[1]
    tm = _tile(S, FULL_ROW_TILE)
    has_b = bias is not None

    def body(*refs):
        a_ref, w_ref, x_ref = refs[:3]
        b_ref = refs[3] if has_b else None
        g_ref, xo_ref, ho_ref = refs[3 + int(has_b):]
        xn = x_ref[...] + _dot(a_ref[...], w_ref[...].reshape(K, D), "nn")
        if has_b:
            xn = xn + b_ref[...]
        xo_ref[...] = xn
        ho_ref[...] = _rms(xn, g_ref[...]).astype(BF16)

    row = pl.BlockSpec((tm, D), lambda i: (i, 0))
    vec = pl.BlockSpec((1, D), lambda i: (0, 0))
    in_specs = [pl.BlockSpec((tm, K), lambda i: (i, 0)), _full_weight_spec(w, layer), row] + ([vec] if has_b else []) + [vec]
    ins = [a, w, x_res] + ([bias] if has_b else []) + [g_next]
    return _call(name, body, (S // tm,), in_specs, [row, row], [_sds((S, D), F32), _sds((S, D), BF16)], ins)


SHARD_PAIR = 2


def _pair_spec(w, layer):
    _, _, fsh, D = w.shape
    return pl.BlockSpec((SHARD_PAIR, None, fsh, D), lambda d, i: (d, layer, 0, 0))


def _ffn_up_call(name, hb, wgt, wut, layer, comm=None):
    S, D = hb.shape
    fsh = wgt.shape[2]
    tn = SHARD_PAIR * fsh
    tm = _tile(S, ROW_TILE)

    def body(h_ref, wg_ref, wu_ref, g_ref, u_ref, a_ref):
        h = h_ref[...]
        g = _dot(h, wg_ref[...].reshape(tn, D), "nt")
        u = _dot(h, wu_ref[...].reshape(tn, D), "nt")
        g_ref[...] = g.astype(BF16)
        u_ref[...] = u.astype(BF16)
        a_ref[...] = (g * _sig(g) * u).astype(BF16)

    ospec = pl.BlockSpec((tm, tn), lambda d, i: (i, d))
    return _call(
        name, body, (N_DEV // SHARD_PAIR, S // tm),
        [pl.BlockSpec((tm, D), lambda d, i: (i, 0)), _pair_spec(wgt, layer), _pair_spec(wut, layer)],
        [ospec] * 3, [_sds((S, N_DEV * fsh), BF16)] * 3, (hb, wgt, wut), comm=comm,
    )


def _mm_rows_call(name, a, w, layer, add, comm=None):
    S, K = a.shape
    _, _, ksh, N = w.shape
    assert K == N_DEV * ksh
    tm = _tile(S, ROW_TILE)
    tn = _tile(N, COL_TILE)
    has_add = add is not None

    def body(*refs):
        a_ref, w_ref = refs[:2]
        o_ref = refs[-1]
        y = _dot(a_ref[...], w_ref[...].reshape(K, tn), "nn")
        o_ref[...] = y + refs[2][...] if has_add else y

    tile = pl.BlockSpec((tm, tn), lambda i, n: (i, n))
    in_specs = [pl.BlockSpec((tm, K), lambda i, n: (i, 0)), pl.BlockSpec((N_DEV, None, ksh, tn), lambda i, n: (0, layer, 0, n))]
    return _call(
        name, body, (S // tm, N // tn), in_specs + ([tile] if has_add else []), tile, _sds((S, N), F32),
        (a, w) + ((add,) if has_add else ()), comm=comm,
    )


def _res_rms_call(name, y, x_res, g_next):
    S, D = x_res.shape
    ts = _tile(S, ELT_TILE)

    def body(y_ref, x_ref, g_ref, xo_ref, ho_ref):
        xn = x_ref[...] + y_ref[...]
        xo_ref[...] = xn
        ho_ref[...] = _rms(xn, g_ref[...]).astype(BF16)

    row = pl.BlockSpec((ts, D), lambda i: (i, 0))
    return _call(
        name, body, (S // ts,), [row, row, pl.BlockSpec((1, D), lambda i: (0, 0))],
        [row, row], [_sds((S, D), F32), _sds((S, D), BF16)], (y, x_res, g_next),
    )


def _colmm_call(name, a, w, layer, out_dtype):
    S, K = a.shape
    nsh = w.shape[-1]
    tm = _tile(S, ROW_TILE)

    def body(a_ref, w_ref, o_ref):
        o_ref[...] = _dot(a_ref[...], w_ref[...], "nn").astype(out_dtype)

    return _call(
        name, body, (N_DEV, S // tm),
        [pl.BlockSpec((tm, K), lambda d, i: (i, 0)), pl.BlockSpec((None, None, K, nsh), lambda d, i: (d, layer, 0, 0))],
        pl.BlockSpec((tm, nsh), lambda d, i: (i, d)), _sds((S, N_DEV * nsh), out_dtype), (a, w),
    )


def _ple_gate_call(name, hb, w, layer, x_res, bias, pp, g_next):
    S, D = x_res.shape
    K = hb.shape[1]
    tm = _tile(S, FULL_ROW_TILE)
    has_g = g_next is not None

    def body(*refs):
        a_ref, w_ref, x_ref, b_ref, p_ref = refs[:5]
        g_ref = refs[5] if has_g else None
        outs = refs[5 + int(has_g):]
        gate = _sig(_dot(a_ref[...], w_ref[...].reshape(K, D), "nn") + b_ref[...])
        xn = x_ref[...] + gate * p_ref[...].astype(F32)
        outs[0][...] = xn
        outs[1][...] = gate.astype(BF16)
        if has_g:
            outs[2][...] = _rms(xn, g_ref[...]).astype(BF16)

    row = pl.BlockSpec((tm, D), lambda i: (i, 0))
    vec = pl.BlockSpec((1, D), lambda i: (0, 0))
    in_specs = [pl.BlockSpec((tm, K), lambda i: (i, 0)), _full_weight_spec(w, layer), row, vec, row]
    ins = [hb, w, x_res, bias, pp]
    out_specs, out_shape = [row, row], [_sds((S, D), F32), _sds((S, D), BF16)]
    if has_g:
        in_specs.append(vec)
        ins.append(g_next)
        out_specs.append(row)
        out_shape.append(_sds((S, D), BF16))
    res = _call(name, body, (S // tm,), in_specs, out_specs, out_shape, ins)
    return res if has_g else (res[0], res[1], None)


def _pool_mix_call(name, x, g):
    S, D = x.shape
    ts = _tile(S, ELT_TILE)
    hb = ts // POOL_HALO
    gc = D // len(POOL_WINDOWS)

    def body(cur_ref, halo_ref, g_ref, o_ref, ext_ref):
        i = pl.program_id(0)
        gain = g_ref[...]
        ext_ref[0:POOL_HALO, :] = jnp.where(i > 0, _rms(halo_ref[...], gain), 0.0)
        ext_ref[POOL_HALO:, :] = _rms(cur_ref[...], gain)
        t = i * ts + lax.broadcasted_iota(jnp.int32, (ts, 1), 0)
        for gi, win in enumerate(POOL_WINDOWS):
            lanes = pl.ds(gi * gc, gc)
            h = ext_ref[pl.ds(POOL_HALO, ts), lanes]
            acc = h
            for j in range(1, win):
                acc = acc + ext_ref[pl.ds(POOL_HALO - j, ts), lanes]
            cnt = jnp.minimum(t + 1, win).astype(F32)
            o_ref[:, gi * gc:(gi + 1) * gc] = (acc / cnt - h).astype(BF16)

    row = pl.BlockSpec((ts, D), lambda i: (i, 0))
    return _call(
        name, body, (S // ts,),
        [row, pl.BlockSpec((POOL_HALO, D), lambda i: (jnp.maximum(i * hb - 1, 0), 0)), pl.BlockSpec((1, D), lambda i: (0, 0))],
        row, _sds((S, D), BF16), (x, x, g), scratch=[pltpu.VMEM((ts + POOL_HALO, D), F32)],
    )


def _pool_out_call(name, mix, wp, scale, x_res, g_next):
    S, D = x_res.shape
    ng, gc, _ = wp.shape
    tm = _tile(S, FULL_ROW_TILE)

    def body(m_ref, w_ref, s_ref, x_ref, g_ref, xo_ref, y_ref, h_ref):
        parts = [_dot(m_ref[:, gi * gc:(gi + 1) * gc], w_ref[gi], "nn") for gi in range(ng)]
        ypre = jnp.concatenate(parts, axis=-1)
        y_ref[...] = ypre.astype(BF16)
        xn = x_ref[...] + ypre * s_ref[...]
        xo_ref[...] = xn
        h_ref[...] = _rms(xn, g_ref[...]).astype(BF16)

    row = pl.BlockSpec((tm, D), lambda i: (i, 0))
    vec = pl.BlockSpec((1, D), lambda i: (0, 0))
    return _call(
        name, body, (S // tm,),
        [row, pl.BlockSpec((ng, gc, gc), lambda i: (0, 0, 0)), vec, row, vec],
        [row, row, row], [_sds((S, D), F32), _sds((S, D), BF16), _sds((S, D), BF16)],
        (mix, wp, scale, x_res, g_next),
    )


def _attn_probs(q_ref, kp_ref, kc_ref, qg_ref, kg_ref, bias_ref, sink_ref, n):
    grp = q_ref.shape[0]
    q = q_ref[...]
    qn = _rms(q, qg_ref[...])
    k = jnp.concatenate([kp_ref[...], kc_ref[...]], axis=0)
    kn = _rms(k, kg_ref[...])
    s = _dot(qn.reshape(grp * Q_BLOCK, HEAD_DIM).astype(BF16), kn.astype(BF16), "nt") * (HEAD_DIM ** -0.5)
    s = s.reshape(grp, Q_BLOCK, 2 * Q_BLOCK) + bias_ref[...]
    qi = lax.broadcasted_iota(jnp.int32, (Q_BLOCK, 2 * Q_BLOCK), 0)
    kj = lax.broadcasted_iota(jnp.int32, (Q_BLOCK, 2 * Q_BLOCK), 1)
    qc = qi // CHUNK
    kc = kj // CHUNK - Q_BLOCK // CHUNK
    ok = (kc <= qc) & (kc >= qc - WINDOW_CHUNKS) & ((n > 0) | (kj >= Q_BLOCK))
    s = jnp.where(ok[None], s, NEG_INF)
    sink = sink_ref[...]
    m = jnp.maximum(jnp.max(s, axis=-1, keepdims=True), sink)
    e = jnp.exp(s - m)
    es = jnp.exp(sink - m)
    inv = 1.0 / (jnp.sum(e, axis=-1, keepdims=True) + es)
    return q, qn, k, kn, e * inv, es * inv


def _attn_specs(grp, nb):
    qspec = pl.BlockSpec((grp, Q_BLOCK, HEAD_DIM), lambda j, n: (j, jnp.minimum(n, nb - 1), 0))
    prev = pl.BlockSpec((None, Q_BLOCK, HEAD_DIM), lambda j, n: (j, jnp.maximum(n - 1, 0), 0))
    cur = pl.BlockSpec((None, Q_BLOCK, HEAD_DIM), lambda j, n: (j, jnp.minimum(n, nb - 1), 0))
    gain = pl.BlockSpec((1, HEAD_DIM), lambda j, n: (0, 0))
    bias = pl.BlockSpec((grp, Q_BLOCK, 2 * Q_BLOCK), lambda j, n: (j, 0, 0))
    sink = pl.BlockSpec((grp, 1, 1), lambda j, n: (j, 0, 0))
    return qspec, prev, cur, gain, bias, sink


def _attn_fwd_call(name, q, k, v, qg, kg, bias, sinks):
    H, S, _ = q.shape
    n_kv = k.shape[0]
    grp = H // n_kv
    nb = S // Q_BLOCK

    def body(q_ref, kp_ref, kc_ref, vp_ref, vc_ref, qg_ref, kg_ref, bias_ref, sink_ref, o_ref):
        n = pl.program_id(1)
        _, _, _, _, p, _ = _attn_probs(q_ref, kp_ref, kc_ref, qg_ref, kg_ref, bias_ref, sink_ref, n)
        vv = jnp.concatenate([vp_ref[...], vc_ref[...]], axis=0).astype(BF16)
        o = _dot(p.reshape(grp * Q_BLOCK, 2 * Q_BLOCK).astype(BF16), vv, "nn")
        o_ref[...] = o.reshape(grp, Q_BLOCK, HEAD_DIM).astype(BF16)

    qspec, prev, cur, gain, bspec, sspec = _attn_specs(grp, nb)
    return _call(
        name, body, (n_kv, nb),
        [qspec, prev, cur, prev, cur, gain, gain, bspec, sspec],
        qspec, _sds((H, S, HEAD_DIM), BF16), (q, k, k, v, v, qg, kg, bias, sinks),
    )


def _loss_call(name, y, target):
    S, D = y.shape
    ts = _tile(S, ELT_TILE)

    def body(y_ref, t_ref, d_ref, db_ref, l_ref):
        err = y_ref[...] - t_ref[...]
        dy = err * (1.0 / D)
        d_ref[...] = dy
        db_ref[...] = dy.astype(BF16)
        part = 0.5 * jnp.sum(jnp.sum(err * err, axis=-1, keepdims=True), axis=0, keepdims=True) * (1.0 / D)
        _accumulate(l_ref, jnp.broadcast_to(part, l_ref.shape), pl.program_id(0) == 0)

    row = pl.BlockSpec((ts, D), lambda i: (i, 0))
    return _call(
        name, body, (S // ts,), [row, row],
        [row, row, pl.BlockSpec((8, 128), lambda i: (0, 0))],
        [_sds((S, D), F32), _sds((S, D), BF16), _sds((8, 128), F32)], (y, target),
    )


def _ple_bwd_elt_call(name, dx, gate, pp):
    S, D = dx.shape
    ts = _tile(S, ELT_TILE)

    def body(dx_ref, gt_ref, p_ref, dz_ref, dp_ref, db_ref):
        d = dx_ref[...]
        gt = gt_ref[...].astype(F32)
        dz = d * p_ref[...].astype(F32) * gt * (1.0 - gt)
        dz_ref[...] = dz.astype(BF16)
        dp_ref[...] = (d * gt).astype(BF16)
        _accumulate(db_ref, jnp.sum(dz, axis=0, keepdims=True), pl.program_id(0) == 0)

    row = pl.BlockSpec((ts, D), lambda i: (i, 0))
    return _call(
        name, body, (S // ts,), [row, row, row],
        [row, row, pl.BlockSpec((1, D), lambda i: (0, 0))],
        [_sds((S, D), BF16), _sds((S, D), BF16), _sds((1, D), F32)], (dx, gate, pp),
    )


def _grad_w_call(name, a, a_mode, b, b_mode, comm=None):
    bs = b if isinstance(b, (list, tuple)) else [b]
    S = a.shape[-2]
    tk = _tile(S, K_TILE)
    nk = S // tk

    def spec(arr, mode):
        if mode == "full":
            c = arr.shape[-1]
            return pl.BlockSpec((tk, c), lambda d, k: (k, 0)), c
        if mode == "nat":
            c = arr.shape[-1] // N_DEV
            return pl.BlockSpec((tk, c), lambda d, k: (k, d)), c
        c = arr.shape[-1]
        return pl.BlockSpec((None, tk, c), lambda d, k: (d, k, 0)), c

    a_spec, ca = spec(a, a_mode)
    b_specs, cbs = zip(*[spec(x, b_mode) for x in bs])
    nb = len(bs)

    def body(*refs):
        a_ref = refs[0]
        b_refs = refs[1:1 + nb]
        o_refs = refs[1 + nb:1 + 2 * nb]
        acc_refs = refs[1 + 2 * nb:]
        k = pl.program_id(1)
        av = a_ref[...]

        def finish():
            for o, acc in zip(o_refs, acc_refs):
                o[...] = acc[...].astype(BF16)

        _kloop(k, nk, acc_refs, [_dot(av, br[...], "tn") for br in b_refs], finish)

    res = _call(
        name, body, (N_DEV, nk), [a_spec, *b_specs],
        [pl.BlockSpec((None, ca, cb), lambda d, k: (d, 0, 0)) for cb in cbs],
        [_sds((N_DEV, ca, cb), BF16) for cb in cbs], (a, *bs),
        scratch=[pltpu.VMEM((ca, cb), F32) for cb in cbs], comm=comm,
    )
    own, landed = res if comm is not None else (res, None)
    own = own if isinstance(b, (list, tuple)) else own[0]
    return own if comm is None else (own, landed)


def _dx_full_call(name, dy, w, layer, out_dtype):
    S, D = dy.shape
    K = N_DEV * w.shape[2]
    tm = _tile(S, ROW_TILE)

    def body(dy_ref, w_ref, o_ref):
        o_ref[...] = _dot(dy_ref[...], w_ref[...].reshape(K, D), "nt").astype(out_dtype)

    return _call(
        name, body, (S // tm,), [pl.BlockSpec((tm, D), lambda i: (i, 0)), _full_weight_spec(w, layer)],
        pl.BlockSpec((tm, K), lambda i: (i, 0)), _sds((S, K), out_dtype), (dy, w),
    )


def _dx_full_rms_call(name, dy, w, layer, x, g, dres):
    S, D = x.shape
    K = N_DEV * w.shape[2]
    assert K == D
    tm = _tile(S, FULL_ROW_TILE)

    def body(dy_ref, w_ref, x_ref, g_ref, r_ref, dx_ref, dxb_ref, dg_ref):
        dh, dg = _rms_bwd(_dot(dy_ref[...], w_ref[...].reshape(K, dy_ref.shape[1]), "nt"), x_ref[...], g_ref[...])
        dx = r_ref[...] + dh
        dx_ref[...] = dx
        dxb_ref[...] = dx.astype(BF16)
        _accumulate(dg_ref, dg, pl.program_id(0) == 0)

    row = pl.BlockSpec((tm, D), lambda i: (i, 0))
    vec = pl.BlockSpec((1, D), lambda i: (0, 0))
    return _call(
        name, body, (S // tm,), [pl.BlockSpec((tm, dy.shape[1]), lambda i: (i, 0)), _full_weight_spec(w, layer), row, vec, row],
        [row, row, vec], [_sds((S, D), F32), _sds((S, D), BF16), _sds((1, D), F32)], (dy, w, x, g, dres),
    )


def _dx_colsharded_rms_call(name, dys, dy_chunked, ws, layer, x, g, dres, want_colsum, comm=None):
    S, D = x.shape
    nsh = ws[0].shape[-1]
    tm = _tile(S, ROW_TILE)
    nt = len(dys)

    def body(*refs):
        dy_refs = refs[:nt]
        w_refs = refs[nt:2 * nt]
        x_ref, g_ref, r_ref = refs[2 * nt:2 * nt + 3]
        outs = refs[2 * nt + 3:]
        acc_ref = outs[0]
        i, k = pl.program_id(0), pl.program_id(1)
        contrib = None
        for dr, wr in zip(dy_refs, w_refs):
            c = _dot(dr[...], wr[...], "nt")
            contrib = c if contrib is None else contrib + c

        def finish():
            dg = colsum = None
            for rows in _row_chunks(tm):
                dh, dg_part = _rms_bwd(acc_ref[rows, :], x_ref[rows, :], g_ref[...])
                dx = r_ref[rows, :] + dh
                outs[0][rows, :] = dx
                outs[1][rows, :] = dx.astype(BF16)
                dg = dg_part if dg is None else dg + dg_part
                if want_colsum:
                    part = jnp.sum(dx, axis=0, keepdims=True)
                    colsum = part if colsum is None else colsum + part
            _accumulate(outs[2], dg, i == 0)
            if want_colsum:
                _accumulate(outs[3], colsum, i == 0)

        _kloop(k, N_DEV, [acc_ref], [contrib], finish)

    if dy_chunked:
        dspec = pl.BlockSpec((None, tm, nsh), lambda i, k: (k, i, 0))
    else:
        dspec = pl.BlockSpec((tm, nsh), lambda i, k: (i, k))
    wspec = pl.BlockSpec((None, None, D, nsh), lambda i, k: (k, layer, 0, 0))
    row = pl.BlockSpec((tm, D), lambda i, k: (i, 0))
    row_once = pl.BlockSpec((tm, D), lambda i, k: (i, 0), pipeline_mode=pl.Buffered(1))
    vec = pl.BlockSpec((1, D), lambda i, k: (0, 0))
    out_specs = [row, row, vec] + ([vec] if want_colsum else [])
    out_shape = [_sds((S, D), F32), _sds((S, D), BF16), _sds((1, D), F32)] + ([_sds((1, D), F32)] if want_colsum else [])
    res = _call(
        name, body, (S // tm, N_DEV), [dspec] * nt + [wspec] * nt + [row_once, vec, row_once],
        out_specs, out_shape, (*dys, *ws, x, g, dres), comm=comm, vmem=BIG_VMEM_LIMIT_BYTES,
    )
    own, landed = res if comm is not None else (res, None)
    own = tuple(own) if want_colsum else (*own, None)
    return own if comm is None else (own, landed)


def _ffn_bwd_hidden_call(name, dyb, w, layer, gpre, upre, comm=None):
    S, D = dyb.shape
    fsh = w.shape[2]
    tn = SHARD_PAIR * fsh
    tm = _tile(S, ROW_TILE)

    def body(dy_ref, w_ref, g_ref, u_ref, dg_ref, du_ref):
        da = _dot(dy_ref[...], w_ref[...].reshape(tn, D), "nt")
        g = g_ref[...].astype(F32)
        u = u_ref[...].astype(F32)
        s = _sig(g)
        dg_ref[...] = (da * u * s * (1.0 + g * (1.0 - s))).astype(BF16)
        du_ref[...] = (da * g * s).astype(BF16)

    cspec = pl.BlockSpec((tm, tn), lambda d, i: (i, d))
    return _call(
        name, body, (N_DEV // SHARD_PAIR, S // tm),
        [pl.BlockSpec((tm, D), lambda d, i: (i, 0)), _pair_spec(w, layer), cspec, cspec],
        [cspec, cspec], [_sds((S, N_DEV * fsh), BF16)] * 2, (dyb, w, gpre, upre), comm=comm,
    )


def _grad_rows_call(name, a, b, comm=None):
    S, F = a.shape
    N = b.shape[1]
    fsh = F // N_DEV
    tr = SHARD_PAIR * fsh
    tn = _tile(N, 2 * COL_TILE)
    tk = _tile(S, K_TILE)
    nk = S // tk

    def body(a_ref, b_ref, o_ref, acc_ref):
        def finish():
            o_ref[...] = acc_ref[...].astype(BF16).reshape(SHARD_PAIR, fsh, tn)

        _kloop(pl.program_id(2), nk, [acc_ref], [_dot(a_ref[...], b_ref[...], "tn")], finish)

    res = _call(
        name, body, (N_DEV // SHARD_PAIR, N // tn, nk),
        [pl.BlockSpec((tk, tr), lambda d, n, k: (k, d)), pl.BlockSpec((tk, tn), lambda d, n, k: (k, n))],
        pl.BlockSpec((SHARD_PAIR, fsh, tn), lambda d, n, k: (d, 0, n)), _sds((N_DEV, fsh, N), BF16), (a, b),
        scratch=[pltpu.VMEM((tr, tn), F32)], comm=comm,
    )
    return res


def _rms_bwd_res_call(name, dh, x, g, dres, want_colsum):
    S, D = x.shape
    ts = _tile(S, ELT_TILE)

    def body(dh_ref, x_ref, g_ref, r_ref, *outs):
        i = pl.program_id(0)
        d, dg = _rms_bwd(dh_ref[...], x_ref[...], g_ref[...])
        dx = r_ref[...] + d
        outs[0][...] = dx
        outs[1][...] = dx.astype(BF16)
        _accumulate(outs[2], dg, i == 0)
        if want_colsum:
            _accumulate(outs[3], jnp.sum(dx, axis=0, keepdims=True), i == 0)

    row = pl.BlockSpec((ts, D), lambda i: (i, 0))
    vec = pl.BlockSpec((1, D), lambda i: (0, 0))
    out_specs = [row, row, vec] + ([vec] if want_colsum else [])
    out_shape = [_sds((S, D), F32), _sds((S, D), BF16), _sds((1, D), F32)] + ([_sds((1, D), F32)] if want_colsum else [])
    res = _call(name, body, (S // ts,), [row, row, vec, row], out_specs, out_shape, (dh, x, g, dres))
    return tuple(res) if want_colsum else (*res, None)


def _sum_slots_call(name, land):
    _, R, C = land.shape
    tr = _div_tile(R, ADAM_BLOCK_BYTES // (C * 4))

    def body(l_ref, o_ref):
        g = l_ref[0].astype(F32)
        for d in range(1, N_DEV):
            g = g + l_ref[d].astype(F32)
        o_ref[...] = g

    return _call(
        name, body, (R // tr,), [pl.BlockSpec((N_DEV, tr, C), lambda i: (0, i, 0))],
        pl.BlockSpec((tr, C), lambda i: (i, 0)), _sds((R, C), F32), (land,),
    )


def _conv_out_bwd_call(name, dyb, w, layer, u2, ln_g, ln_b):
    S, D = u2.shape
    ksh = w.shape[2]
    tm = _tile(S, ELT_TILE)

    def body(dy_ref, w_ref, u2_ref, g_ref, b_ref, du2_ref, dg_ref, db_ref, dbdw_ref):
        i = pl.program_id(0)
        dy = dy_ref[...]
        du4 = jnp.concatenate([_dot(dy, w_ref[d], "nt") for d in range(N_DEV)], axis=-1)
        u2 = u2_ref[...]
        mu = jnp.mean(u2, axis=-1, keepdims=True)
        xc = u2 - mu
        r = lax.rsqrt(jnp.mean(xc * xc, axis=-1, keepdims=True) + EPS)
        xh = xc * r
        gain = g_ref[...]
        u3 = xh * gain + b_ref[...]
        s = _sig(u3)
        du3 = du4 * s * (1.0 + u3 * (1.0 - s))
        dxh = du3 * gain
        du2 = r * (dxh - jnp.mean(dxh, axis=-1, keepdims=True) - xh * jnp.mean(dxh * xh, axis=-1, keepdims=True))
        du2_ref[...] = du2
        _accumulate(dg_ref, jnp.sum(du3 * xh, axis=0, keepdims=True), i == 0)
        _accumulate(db_ref, jnp.sum(du3, axis=0, keepdims=True), i == 0)
        _accumulate(dbdw_ref, jnp.sum(du2, axis=0, keepdims=True), i == 0)

    row = pl.BlockSpec((tm, D), lambda i: (i, 0))
    vec = pl.BlockSpec((1, D), lambda i: (0, 0))
    return _call(
        name, body, (S // tm,),
        [row, pl.BlockSpec((N_DEV, None, ksh, D), lambda i: (0, layer, 0, 0)), row, vec, vec],
        [row, vec, vec, vec], [_sds((S, D), F32)] + [_sds((1, D), F32)] * 3, (dyb, w, u2, ln_g, ln_b),
    )


def _dwconv_bwd_call(name, du2, u1, w_dw, a_pre, g_pre):
    S, D = u1.shape
    tc = _tile(S, CONV_TILE)
    hb = tc // CONV_HALO
    n_halo = S // CONV_HALO
    assert D % CONV_LANES == 0 and tc % CONV_SUB == 0
    wrows = CONV_HALO

    def body(d_cur, d_next, u_cur, u_prev, w_ref, a_ref, g_ref, du_ref, dbin_ref, dw_ref, dext_ref, uext_ref, du1_ref, dwacc_ref, dsh_ref, ush_ref):
        i = pl.program_id(0)
        last = S // tc - 1
        dext_ref[0:tc, :] = d_cur[...]
        dext_ref[tc:tc + CONV_HALO, :] = jnp.where(i < last, d_next[...], 0.0)
        uext_ref[0:CONV_HALO, :] = jnp.where(i > 0, u_prev[...], 0.0)
        uext_ref[CONV_HALO:tc + CONV_HALO, :] = u_cur[...]
        _fill_shifted(dsh_ref, dext_ref, tc + CONV_HALO)
        _fill_shifted(ush_ref, uext_ref, tc + CONV_HALO)

        @pl.when(i == 0)
        def _():
            dwacc_ref[...] = jnp.zeros_like(dwacc_ref)

        first_tap = CONV_HALO - (CONV_WIDTH - 1)
        for r in range(0, tc, CONV_SUB):
            for c in range(0, D, CONV_LANES):
                du1_ref[r:r + CONV_SUB, c:c + CONV_LANES] = _conv_taps(dsh_ref, w_ref, r, c, CONV_WIDTH - 1, True)
                dcur = dext_ref[r:r + CONV_SUB, c:c + CONV_LANES]
                for k in range(CONV_WIDTH):
                    prod = dcur * _shifted_window(ush_ref, r + first_tap + k, c)
                    part = prod[0:8]
                    for q in range(8, CONV_SUB, 8):
                        part = part + prod[q:q + 8]
                    dwacc_ref[k, :, c:c + CONV_LANES] += part

        du1 = du1_ref[...]
        a = a_ref[...].astype(F32)
        sg = _sig(g_ref[...].astype(F32))
        da = du1 * sg
        dgate = du1 * a * sg * (1.0 - sg)
        du_ref[:, 0:D] = da.astype(BF16)
        du_ref[:, D:2 * D] = dgate.astype(BF16)
        _accumulate(dbin_ref, jnp.concatenate([jnp.sum(da, axis=0, keepdims=True), jnp.sum(dgate, axis=0, keepdims=True)], axis=-1), i == 0)

        @pl.when(i == last)
        def _():
            for k in range(CONV_WIDTH):
                dw_ref[k:k + 1, :] = jnp.sum(dwacc_ref[k], axis=0, keepdims=True)
            dw_ref[CONV_WIDTH:, :] = jnp.zeros((wrows - CONV_WIDTH, D), F32)

    row = pl.BlockSpec((tc, D), lambda i: (i, 0))
    nxt = pl.BlockSpec((CONV_HALO, D), lambda i: (jnp.minimum((i + 1) * hb, n_halo - 1), 0))
    prv = pl.BlockSpec((CONV_HALO, D), lambda i: (jnp.maximum(i * hb - 1, 0), 0))
    return _call(
        name, body, (S // tc,),
        [row, nxt, row, prv, pl.BlockSpec((CONV_WIDTH, D), lambda i: (0, 0)), row, row],
        [pl.BlockSpec((tc, 2 * D), lambda i: (i, 0)), pl.BlockSpec((1, 2 * D), lambda i: (0, 0)), pl.BlockSpec((wrows, D), lambda i: (0, 0))],
        [_sds((S, 2 * D), BF16), _sds((1, 2 * D), F32), _sds((wrows, D), F32)],
        (du2, du2, u1, u1, w_dw, a_pre, g_pre),
        scratch=[
            pltpu.VMEM((tc + CONV_HALO + SUBLANES, D), F32), pltpu.VMEM((tc + CONV_HALO + SUBLANES, D), F32),
            pltpu.VMEM((tc, D), F32), pltpu.VMEM((CONV_WIDTH, 8, D), F32),
            pltpu.VMEM((SUBLANES, tc + CONV_HALO, D), F32), pltpu.VMEM((SUBLANES, tc + CONV_HALO, D), F32),
        ],
    )


def _pool_out_bwd_call(name, dy, ypre, scale, wp):
    S, D = dy.shape
    ng, gc, _ = wp.shape
    tm = _tile(S, FULL_ROW_TILE)

    def body(dy_ref, y_ref, s_ref, w_ref, dm_ref, dyp_ref, ds_ref):
        dy = dy_ref[...]
        _accumulate(ds_ref, jnp.sum(dy * y_ref[...].astype(F32), axis=0, keepdims=True), pl.program_id(0) == 0)
        dyp = (dy * s_ref[...]).astype(BF16)
        dyp_ref[...] = dyp
        dm_ref[...] = jnp.concatenate([_dot(dyp[:, gi * gc:(gi + 1) * gc], w_ref[gi], "nt") for gi in range(ng)], axis=-1)

    row = pl.BlockSpec((tm, D), lambda i: (i, 0))
    vec = pl.BlockSpec((1, D), lambda i: (0, 0))
    return _call(
        name, body, (S // tm,), [row, row, vec, pl.BlockSpec((ng, gc, gc), lambda i: (0, 0, 0))],
        [row, row, vec], [_sds((S, D), F32), _sds((S, D), BF16), _sds((1, D), F32)], (dy, ypre, scale, wp),
    )


def _pool_w_grad_call(name, mix, dyp, ng):
    S, D = mix.shape
    gc = D // ng
    tk = _tile(S, K_TILE)
    nk = S // tk

    def body(m_ref, d_ref, o_ref, acc_ref):
        def finish():
            o_ref[...] = acc_ref[...]

        _kloop(pl.program_id(1), nk, [acc_ref], [_dot(m_ref[...], d_ref[...], "tn")], finish)

    blk = pl.BlockSpec((tk, gc), lambda g, k: (k, g))
    return _call(
        name, body, (ng, nk), [blk, blk], pl.BlockSpec((None, gc, gc), lambda g, k: (g, 0, 0)),
        _sds((ng, gc, gc), F32), (mix, dyp), scratch=[pltpu.VMEM((gc, gc), F32)],
    )


def _pool_mix_bwd_call(name, dmix, x, g, dres):
    S, D = x.shape
    ts = _tile(S, ELT_TILE)
    hb = ts // POOL_HALO
    n_halo = S // POOL_HALO
    gc = D // len(POOL_WINDOWS)

    def body(cur_ref, nxt_ref, x_ref, g_ref, r_ref, dx_ref, dxb_ref, dg_ref, ext_ref, dh_ref):
        i = pl.program_id(0)
        last = S // ts - 1
        t = i * ts + lax.broadcasted_iota(jnp.int32, (ts + POOL_HALO, 1), 0)
        for gi, win in enumerate(POOL_WINDOWS):
            lanes = slice(gi * gc, (gi + 1) * gc)
            cnt = jnp.minimum(t + 1, win).astype(F32)
            ext_ref[0:ts, lanes] = cur_ref[:, lanes] / cnt[0:ts]
            ext_ref[ts:, lanes] = jnp.where(i < last, nxt_ref[:, lanes] / cnt[ts:], 0.0)
        for gi, win in enumerate(POOL_WINDOWS):
            lanes = pl.ds(gi * gc, gc)
            acc = ext_ref[pl.ds(0, ts), lanes]
            for j in range(1, win):
                acc = acc + ext_ref[pl.ds(j, ts), lanes]
            dh_ref[:, gi * gc:(gi + 1) * gc] = acc - cur_ref[:, gi * gc:(gi + 1) * gc]
        d, dg = _rms_bwd(dh_ref[...], x_ref[...], g_ref[...])
        dx = r_ref[...] + d
        dx_ref[...] = dx
        dxb_ref[...] = dx.astype(BF16)
        _accumulate(dg_ref, dg, i == 0)

    row = pl.BlockSpec((ts, D), lambda i: (i, 0))
    vec = pl.BlockSpec((1, D), lambda i: (0, 0))
    nxt = pl.BlockSpec((POOL_HALO, D), lambda i: (jnp.minimum((i + 1) * hb, n_halo - 1), 0))
    return _call(
        name, body, (S // ts,), [row, nxt, row, vec, row], [row, row, vec],
        [_sds((S, D), F32), _sds((S, D), BF16), _sds((1, D), F32)], (dmix, dmix, x, g, dres),
        scratch=[pltpu.VMEM((ts + POOL_HALO, D), F32), pltpu.VMEM((ts, D), F32)],
    )


def _attn_bwd_call(name, q, k, v, do, qg, kg, bias, sinks):
    H, S, _ = q.shape
    n_kv = k.shape[0]
    grp = H // n_kv
    nb = S // Q_BLOCK
    scale = HEAD_DIM ** -0.5

    def body(q_ref, kp_ref, kc_ref, vp_ref, vc_ref, do_ref, qg_ref, kg_ref, bias_ref, sink_ref,
             dq_ref, dk_ref, dv_ref, dqg_ref, dkg_ref, dbias_ref, dsink_ref, ck_ref, cv_ref):
        n = pl.program_id(1)

        @pl.when(n == 0)
        def _():
            dqg_ref[...] = jnp.zeros_like(dqg_ref)
            dkg_ref[...] = jnp.zeros_like(dkg_ref)
            dbias_ref[...] = jnp.zeros_like(dbias_ref)
            dsink_ref[...] = jnp.zeros_like(dsink_ref)
            ck_ref[...] = jnp.zeros_like(ck_ref)
            cv_ref[...] = jnp.zeros_like(cv_ref)

        def block_grads():
            q, qn, _, kn, p, ps = _attn_probs(q_ref, kp_ref, kc_ref, qg_ref, kg_ref, bias_ref, sink_ref, n)
            rows = grp * Q_BLOCK
            dob = do_ref[...].reshape(rows, HEAD_DIM).astype(BF16)
            vv = jnp.concatenate([vp_ref[...], vc_ref[...]], axis=0).astype(BF16)
            dp = _dot(dob, vv, "nt").reshape(grp, Q_BLOCK, 2 * Q_BLOCK)
            delta = jnp.sum(p * dp, axis=-1, keepdims=True)
            dl = p * (dp - delta)
            dbias_ref[...] += dl
            dsink_ref[...] += jnp.sum(-ps * delta, axis=1, keepdims=True)
            dlb = dl.reshape(rows, 2 * Q_BLOCK).astype(BF16)
            dqn = (_dot(dlb, kn.astype(BF16), "nn") * scale).reshape(grp, Q_BLOCK, HEAD_DIM)
            dkn = _dot(dlb, qn.reshape(rows, HEAD_DIM).astype(BF16), "tn") * scale
            dvv = _dot(p.reshape(rows, 2 * Q_BLOCK).astype(BF16), dob, "tn")
            qgain = qg_ref[...]
            r = lax.rsqrt(jnp.mean(q * q, axis=-1, keepdims=True) + EPS)
            qh = q * r
            dqg_ref[...] += jnp.sum(jnp.sum(dqn * qh, axis=1), axis=0, keepdims=True)
            dqh = dqn * qgain
            dq_ref[...] = r * (dqh - qh * jnp.mean(dqh * qh, axis=-1, keepdims=True))
            return dkn, dvv

        def finish_prev(dkn_prev, dv_prev):
            kraw = kp_ref[...]
            dk, dkg = _rms_bwd(dkn_prev, kraw, kg_ref[...])
            dk_ref[...] = dk
            dv_ref[...] = dv_prev
            dkg_ref[...] += dkg

        @pl.when(n < nb)
        def _():
            dkn, dvv = block_grads()

            @pl.when(n > 0)
            def _():
                finish_prev(ck_ref[...] + dkn[0:Q_BLOCK], cv_ref[...] + dvv[0:Q_BLOCK])

            ck_ref[...] = dkn[Q_BLOCK:]
            cv_ref[...] = dvv[Q_BLOCK:]

        @pl.when(n == nb)
        def _():
            finish_prev(ck_ref[...], cv_ref[...])

    qspec, prev, cur, gain, bspec, sspec = _attn_specs(grp, nb)
    kout = pl.BlockSpec((None, Q_BLOCK, HEAD_DIM), lambda j, n: (j, jnp.maximum(n - 1, 0), 0))
    gout = pl.BlockSpec((None, 1, HEAD_DIM), lambda j, n: (j, 0, 0))
    return _call(
        name, body, (n_kv, nb + 1),
        [qspec, prev, cur, prev, cur, qspec, gain, gain, bspec, sspec],
        [qspec, kout, kout, gout, gout, bspec, sspec],
        [
            _sds((H, S, HEAD_DIM), F32), _sds((n_kv, S, HEAD_DIM), F32), _sds((n_kv, S, HEAD_DIM), F32),
            _sds((n_kv, 1, HEAD_DIM), F32), _sds((n_kv, 1, HEAD_DIM), F32),
            _sds((H, Q_BLOCK, 2 * Q_BLOCK), F32), _sds((H, 1, 1), F32),
        ],
        (q, k, k, v, v, do, qg, kg, bias, sinks),
        scratch=[pltpu.VMEM((Q_BLOCK, HEAD_DIM), F32), pltpu.VMEM((Q_BLOCK, HEAD_DIM), F32)],
    )


def _bucket_sum_call(name, onehot, dbias):
    nbk, n = onehot.shape
    H = dbias.shape[0]

    def body(o_ref, d_ref, out_ref):
        out_ref[...] = lax.dot_general(o_ref[...], d_ref[...], _DIMS["nt"], precision=lax.Precision.HIGHEST, preferred_element_type=F32)

    return _call(
        name, body, (1,), [pl.BlockSpec((nbk, n), lambda i: (0, 0)), pl.BlockSpec((H, n), lambda i: (0, 0))],
        pl.BlockSpec((nbk, H), lambda i: (0, 0)), _sds((nbk, H), F32), (onehot, dbias),
    )


def _bias_table_call(name, rel_bias, onehot):
    nbk, n = onehot.shape
    H = rel_bias.shape[1]

    def body(r_ref, o_ref, out_ref):
        out_ref[...] = lax.dot_general(r_ref[...], o_ref[...], _DIMS["tn"], precision=lax.Precision.HIGHEST, preferred_element_type=F32)

    return _call(
        name, body, (1,), [pl.BlockSpec((nbk, H), lambda i: (0, 0)), pl.BlockSpec((nbk, n), lambda i: (0, 0))],
        pl.BlockSpec((H, n), lambda i: (0, 0)), _sds((H, n), F32), (rel_bias, onehot),
    )


def _exchange_call(name, comm):
    n_src, n_dst = len(comm.srcs), len(comm.out_shape)

    def body(*refs):
        src_refs, dst_refs, sem_refs = refs[:n_src], refs[n_src:n_src + n_dst], refs[n_src + n_dst:]
        comm.start(src_refs, dst_refs, sem_refs)
        comm.wait(src_refs, dst_refs, sem_refs)

    hbm = pl.BlockSpec(memory_space=pltpu.HBM)
    return pl.pallas_call(
        body, name=name, in_specs=[hbm] * n_src, out_specs=[hbm] * n_dst, out_shape=list(comm.out_shape),
        scratch_shapes=list(comm.sems), compiler_params=pltpu.CompilerParams(has_side_effects=True),
    )(*comm.srcs)


def _gather_forward_call(name, bufs):
    n = len(bufs)

    def body(*refs):
        in_refs, out_refs = refs[:n], refs[n:2 * n]
        send_sems, recv_sems = refs[2 * n:]
        x, y, c = _mesh_pos()
        sibling = (x, y, 1 - c)
        chips = [(1 - x, y), (x, 1 - y), (1 - x, 1 - y)]
        sends, recvs = [], []
        for t in range(n):
            for j, chip in enumerate(chips):
                s = 3 * t + j
                mine, theirs = _slot(*chip, c), _slot(*chip, 1 - c)
                sends.append(_remote(in_refs[t].at[mine], out_refs[t].at[mine], send_sems.at[s], recv_sems.at[s], sibling))
                recvs.append(_remote(in_refs[t].at[mine], out_refs[t].at[theirs], send_sems.at[s], recv_sems.at[s], sibling))
        for cp in sends:
            cp.start()
        for cp in recvs:
            cp.wait_recv()
        for cp in sends:
            cp.wait_send()

    hbm = pl.BlockSpec(memory_space=pltpu.HBM)
    return pl.pallas_call(
        body, name=name, in_specs=[hbm] * n, out_specs=[hbm] * n, out_shape=[_sds(b.shape, b.dtype) for b in bufs],
        scratch_shapes=[pltpu.SemaphoreType.DMA((3 * n,)), pltpu.SemaphoreType.DMA((3 * n,))],
        input_output_aliases={t: t for t in range(n)},
        compiler_params=pltpu.CompilerParams(has_side_effects=True),
    )(*bufs)


def _all_gather_call(name, shards):
    n = len(shards)

    def body(*refs):
        in_refs, out_refs = refs[:n], refs[n:2 * n]
        send_sems, recv_sems, local_sems = refs[2 * n:]
        x, y, c = _mesh_pos()
        me, sibling = (x, y, c), (x, y, 1 - c)
        chips = [(1 - x, y), (x, 1 - y), (1 - x, 1 - y)]

        def copy(t, k, block, to, src=None):
            dst = out_refs[t].at[_slot(*block)]
            return pltpu.make_async_remote_copy(
                src_ref=dst if src is None else src, dst_ref=dst,
                send_sem=send_sems.at[7 * t + k], recv_sem=recv_sems.at[7 * t + k],
                device_id=to, device_id_type=pl.DeviceIdType.MESH,
            )

        mine = [pltpu.make_async_copy(in_refs[t], out_refs[t].at[_slot(*me)], local_sems.at[t]) for t in range(n)]
        for cp in mine:
            cp.start()
        first = []
        for t in range(n):
            first.append(copy(t, 0, me, sibling, src=in_refs[t]))
            first += [copy(t, 1 + j, me, (*chip, c), src=in_refs[t]) for j, chip in enumerate(chips)]
        for cp in first:
            cp.start()
        passed = []
        for j, chip in enumerate(chips):
            for t in range(n):
                copy(t, 1 + j, (*chip, c), me).wait_recv()
                fwd = copy(t, 4 + j, (*chip, c), sibling)
                fwd.start()
                passed.append(fwd)
        for t in range(n):
            copy(t, 0, sibling, me).wait_recv()
            for j, chip in enumerate(chips):
                copy(t, 4 + j, (*chip, 1 - c), me).wait_recv()
        for cp in first + passed:
            cp.wait_send()
        for cp in mine:
            cp.wait()

    hbm = pl.BlockSpec(memory_space=pltpu.HBM)
    return pl.pallas_call(
        body, name=name,
        in_specs=[hbm] * n, out_specs=[hbm] * n,
        out_shape=[_sds((N_DEV, *s.shape), s.dtype) for s in shards],
        scratch_shapes=[pltpu.SemaphoreType.DMA((7 * n,)), pltpu.SemaphoreType.DMA((7 * n,)), pltpu.SemaphoreType.DMA((n,))],
        compiler_params=pltpu.CompilerParams(has_side_effects=True),
    )(*shards)


def _all_reduce_small_call(name, pack):
    R, C = pack.shape

    def body(in_ref, out_ref, land_ref, send_sems, recv_sems):
        x, y, c = _mesh_pos()
        me = _slot(x, y, c)
        peers = _peers(x, y, c)
        land_ref[me] = in_ref[...]
        sends = [
            pltpu.make_async_remote_copy(
                src_ref=in_ref, dst_ref=land_ref.at[me], send_sem=send_sems.at[k], recv_sem=recv_sems.at[k],
                device_id=to, device_id_type=pl.DeviceIdType.MESH,
            )
            for k, to in enumerate(peers)
        ]
        for cp in sends:
            cp.start()
        for k, frm in enumerate(peers):
            pltpu.make_async_remote_copy(
                src_ref=in_ref, dst_ref=land_ref.at[_slot(*frm)], send_sem=send_sems.at[k], recv_sem=recv_sems.at[k],
                device_id=frm, device_id_type=pl.DeviceIdType.MESH,
            ).wait_recv()
        for cp in sends:
            cp.wait_send()
        total = land_ref[0]
        for d in range(1, N_DEV):
            total = total + land_ref[d]
        out_ref[...] = total

    vmem = pl.BlockSpec(memory_space=pltpu.VMEM)
    return pl.pallas_call(
        body, name=name, in_specs=[vmem], out_specs=vmem, out_shape=_sds((R, C), F32),
        scratch_shapes=[pltpu.VMEM((N_DEV, R, C), F32), pltpu.SemaphoreType.DMA((7,)), pltpu.SemaphoreType.DMA((7,))],
        compiler_params=pltpu.CompilerParams(has_side_effects=True, vmem_limit_bytes=VMEM_LIMIT_BYTES),
    )(pack)


def _adamw_call(name, grad, landed, w, m, v):
    R, C = w.shape
    tr = _div_tile(R, ADAM_BLOCK_BYTES // (C * 4))
    c1 = 1.0 / (1.0 - ADAM_B1 ** ADAM_STEP)
    c2 = 1.0 / (1.0 - ADAM_B2 ** ADAM_STEP)

    def body(g_ref, w_ref, m_ref, v_ref, go_ref, d_ref, mo_ref, vo_ref):
        if landed:
            g = g_ref[0].astype(F32)
            for d in range(1, N_DEV):
                g = g + g_ref[d].astype(F32)
        else:
            g = g_ref[...]
        go_ref[...] = g
        mn = ADAM_B1 * m_ref[...] + (1.0 - ADAM_B1) * g
        vn = ADAM_B2 * v_ref[...] + (1.0 - ADAM_B2) * (g * g)
        mo_ref[...] = mn
        vo_ref[...] = vn
        d_ref[...] = -ADAM_LR * ((mn * c1) / (jnp.sqrt(vn * c2) + ADAM_EPS) + ADAM_WD * w_ref[...])

    blk = pl.BlockSpec((tr, C), lambda i: (i, 0))
    gspec = pl.BlockSpec((N_DEV, tr, C), lambda i: (0, i, 0)) if landed else blk
    return _call(name, body, (R // tr,), [gspec, blk, blk, blk], [blk] * 4, [_sds((R, C), F32)] * 4, (grad, w, m, v))


def _t5_bucket(rel):
    nb = NUM_BUCKETS // 2
    n = -rel
    ret = jnp.where(n < 0, nb, 0)
    n = jnp.abs(n)
    max_exact = nb // 2
    nf = jnp.maximum(n, 1).astype(jnp.float32)
    large = max_exact + (jnp.log(nf / max_exact) / math.log(REL_MAX_DIST / max_exact) * (nb - max_exact)).astype(jnp.int32)
    large = jnp.minimum(large, nb - 1)
    return ret + jnp.where(n < max_exact, n, large)


def _band_buckets():
    i = jnp.arange(Q_BLOCK)[:, None]
    j = jnp.arange(2 * Q_BLOCK)[None, :]
    return _t5_bucket(j - Q_BLOCK - i)


def _to_heads(t, n_heads):
    S = t.shape[0]
    return t.reshape(S, n_heads, HEAD_DIM).transpose(1, 0, 2)


def _from_heads(t):
    H, S, _ = t.shape
    return t.transpose(1, 0, 2).reshape(S, H * HEAD_DIM)


def _gathered_vec(t):
    nd, L, n = t.shape
    return t.transpose(1, 0, 2).reshape(L, nd * n)


_MIXER_WEIGHTS = {0: ("conv_w_in", "conv_w_out"), 1: (), 2: ("attn_w_qkv", "attn_w_o")}
_FFN_UP_WEIGHTS = ("ffn_w_gate", "ffn_w_up")
_FFN_REST_WEIGHTS = ("ffn_w_down", "ple_w_proj", "ple_w_gate")
_SMALL_SHARDED = ("conv_b_in", "conv_w_dw", "conv_b_dw", "conv_ln_g", "conv_ln_b", "conv_b_out")


def _mixer_keys(i):
    return [(n, i // 3) for n in _MIXER_WEIGHTS[i % 3]]


def _step(x, p, target, wb, small_pack, small_shapes, V):
    S, D = x.shape
    depth = V["norm_mix"].shape[0]
    n_heads = D // HEAD_DIM
    n_kv = (wb["attn_w_qkv"].shape[-1] * N_DEV - D) // (2 * HEAD_DIM)
    ng = len(POOL_WINDOWS)
    vec = lambda t, i: t[i][None, :]
    shard = lambda key: wb[key[0]][key[1]:key[1] + 1]

    keys_a0 = [(n, 0) for n in _FFN_UP_WEIGHTS]
    keys_b0 = [(n, 0) for n in _FFN_REST_WEIGHTS]
    first = _mixer_keys(0) + [("pool_w", 0)]
    gathered = _all_gather_call("all_gather0", [shard(k) for k in first] + [small_pack])
    W = dict(zip(first, gathered[:-1]))
    pw = W.pop(("pool_w", 0))
    wp = pw[:, 0].transpose(1, 0, 2, 3).reshape(pw.shape[2], pw.shape[3] * N_DEV, pw.shape[4])
    V = dict(V)
    for n, t in zip(_SMALL_SHARDED, _unpack(gathered[-1], small_shapes, lead=(N_DEV,))):
        if n == "conv_w_dw":
            V[n] = t.transpose(1, 2, 0, 3).reshape(t.shape[1], t.shape[2], -1)
        else:
            V[n] = _gathered_vec(t)

    buckets = _band_buckets()
    onehot = (buckets.reshape(1, -1) == jnp.arange(NUM_BUCKETS)[:, None]).astype(F32)
    bias_tab = _bias_table_call("bias_table", V["rel_bias"], onehot).reshape(n_heads, Q_BLOCK, 2 * Q_BLOCK)
    sinks3 = V["attn_sinks"].reshape(n_heads, 1, 1)
    pb = p.astype(BF16)

    def gather_next(keys):
        return _GatherOwn([shard(k) for k in keys]) if keys else None

    def finish_gather(name, keys, bufs):
        W.update(zip(keys, _gather_forward_call(name, bufs)))

    saved = []
    hb = _rms_call("rms_in", x, vec(V["norm_mix"], 0))
    for i in range(depth):
        kind, j = i % 3, i // 3
        sv = {"x0": x, "h0": hb}
        g_ffn = vec(V["norm_ffn"], i)
        if kind == 0:
            ride = i == 0
            res = _conv_in_call(f"conv_in{i}", hb, W["conv_w_in", j], 0, vec(V["conv_b_in"], j), comm=gather_next(keys_b0) if ride else None)
            if ride:
                res, bufs = res
                finish_gather("gather_fwd_b0", keys_b0, bufs)
            a_pre, g_pre, u1 = res
            res = _dwconv_call(
                f"dwconv{i}", u1, V["conv_w_dw"][j], vec(V["conv_b_dw"], j), vec(V["conv_ln_g"], j), vec(V["conv_ln_b"], j),
                comm=gather_next(keys_a0) if ride else None)
            if ride:
                res, bufs = res
                finish_gather("gather_fwd_a0", keys_a0, bufs)
            u2, u4 = res
            x1, h2 = _fullmm_res_call(f"conv_out{i}", u4, W["conv_w_out", j], 0, x, vec(V["conv_b_out"], j), g_ffn)
            sv.update(a_pre=a_pre, g_pre=g_pre, u1=u1, u2=u2, u4=u4)
        elif kind == 1:
            mix = _pool_mix_call(f"pool_mix{i}", x, vec(V["norm_mix"], i))
            x1, ypre, h2 = _pool_out_call(f"pool_out{i}", mix, wp, vec(V["pool_scale"], j), x, g_ffn)
            sv.update(mix=mix, ypre=ypre)
        else:
            qkv = _colmm_call(f"qkv{i}", hb, W["attn_w_qkv", j], 0, F32)
            q = _to_heads(qkv[:, :D], n_heads)
            k = _to_heads(qkv[:, D:D + n_kv * HEAD_DIM], n_kv)
            v = _to_heads(qkv[:, D + n_kv * HEAD_DIM:], n_kv)
            o = _attn_fwd_call(f"attn{i}", q, k, v, vec(V["attn_q_norm"], j), vec(V["attn_k_norm"], j), bias_tab, sinks3)
            ob = _from_heads(o)
            x1, h2 = _fullmm_res_call(f"attn_out{i}", ob, W["attn_w_o", j], 0, x, None, g_ffn)
            sv.update(q=q, k=k, v=v, ob=ob)
        more = i + 1 < depth
        keys_a = [(n, i + 1) for n in _FFN_UP_WEIGHTS] if more else []
        keys_b = [(n, i + 1) for n in _FFN_REST_WEIGHTS] + _mixer_keys(i + 1) if more else []
        res = _ffn_up_call(f"ffn_up{i}", h2, W["ffn_w_gate", i], W["ffn_w_up", i], 0, comm=gather_next(keys_a))
        if more:
            (gpre, upre, act), bufs = res
            finish_gather(f"gather_fwd_a{i + 1}", keys_a, bufs)
        else:
            gpre, upre, act = res
        res = _mm_rows_call(f"ffn_down{i}", act, W["ffn_w_down", i], 0, None, comm=gather_next(keys_b))
        if more:
            y, bufs = res
            finish_gather(f"gather_fwd_b{i + 1}", keys_b, bufs)
        else:
            y = res
        x2, h3 = _res_rms_call(f"ffn_res{i}", y, x1, vec(V["norm_ple"], i))
        pp = _colmm_call(f"ple_proj{i}", pb[i], W["ple_w_proj", i], 0, BF16)
        g_next = vec(V["norm_mix"], i + 1) if more else None
        x3, gate, hb_next = _ple_gate_call(f"ple_gate{i}", h3, W["ple_w_gate", i], 0, x2, vec(V["ple_b_gate"], i), pp, g_next)
        sv.update(x1=x1, h2=h2, gpre=gpre, upre=upre, act=act, x2=x2, h3=h3, pp=pp, gate=gate)
        saved.append(sv)
        x, hb = x3, hb_next

    dx, dxb, loss_tile = _loss_call("loss", x, target)
    loss = loss_tile[0, 0]

    landed = {}
    GV = {n: [None] * V[n].shape[0] for n in ("norm_mix", "norm_ffn", "norm_ple", "ple_b_gate", "conv_b_in", "conv_w_dw", "conv_b_dw", "conv_ln_g", "conv_ln_b", "conv_b_out", "pool_scale")}

    def scatter_of(pending):
        return _ScatterSlots([g for _, g in pending]) if pending else None

    def record(pending, bufs):
        landed.update(zip([k for k, _ in pending], bufs))

    mixer_pending = []
    for i in reversed(range(depth)):
        kind, j = i % 3, i // 3
        sv = saved[i]
        dz, dpp, db_gate = _ple_bwd_elt_call(f"ple_bwd{i}", dx, sv["gate"], sv["pp"])
        GV["ple_b_gate"][i] = db_gate
        pending = [(("ple_w_gate", i), _grad_w_call(f"g_ple_gate{i}", sv["h3"], "nat", dz, "full"))]
        pending.append((("ple_w_proj", i), _grad_w_call(f"g_ple_proj{i}", pb[i], "full", dpp, "nat")))
        dx, dxb, dg = _dx_full_rms_call(f"d_x2_{i}", dz, W["ple_w_gate", i], 0, sv["x2"], vec(V["norm_ple"], i), dx)
        GV["norm_ple"][i] = dg
        res = _ffn_bwd_hidden_call(f"ffn_bwd{i}", dxb, W["ffn_w_down", i], 0, sv["gpre"], sv["upre"], comm=scatter_of(mixer_pending))
        if mixer_pending:
            (dgp, dup), bufs = res
            record(mixer_pending, bufs)
        else:
            dgp, dup = res
        g_down = _grad_rows_call(f"g_ffn_down{i}", sv["act"], dxb)
        g_gate, bufs = _grad_rows_call(f"g_ffn_gate{i}", dgp, sv["h2"], comm=scatter_of(pending))
        record(pending, bufs)
        pending = [(("ffn_w_down", i), g_down)]
        g_up, bufs = _grad_rows_call(f"g_ffn_up{i}", dup, sv["h2"], comm=scatter_of(pending))
        record(pending, bufs)
        pending = [(("ffn_w_gate", i), g_gate)]
        dh, bufs = _mm_rows_call(f"d_h2g_{i}", dgp, W["ffn_w_gate", i], 0, None, comm=scatter_of(pending))
        record(pending, bufs)
        pending = [(("ffn_w_up", i), g_up)]
        dh, bufs = _mm_rows_call(f"d_h2_{i}", dup, W["ffn_w_up", i], 0, dh, comm=scatter_of(pending))
        record(pending, bufs)
        dx, dxb, dg, colsum = _rms_bwd_res_call(f"d_x1_{i}", dh, sv["x1"], vec(V["norm_ffn"], i), dx, kind == 0)
        GV["norm_ffn"][i] = dg
        g_mix = vec(V["norm_mix"], i)
        if kind == 0:
            GV["conv_b_out"][j] = colsum
            g_out = _grad_w_call(f"g_conv_out{i}", sv["u4"], "nat", dxb, "full")
            du2, d_ln_g, d_ln_b, d_b_dw = _conv_out_bwd_call(f"conv_out_bwd{i}", dxb, W["conv_w_out", j], 0, sv["u2"], vec(V["conv_ln_g"], j), vec(V["conv_ln_b"], j))
            du, d_b_in, d_w_dw = _dwconv_bwd_call(f"dwconv_bwd{i}", du2, sv["u1"], V["conv_w_dw"][j], sv["a_pre"], sv["g_pre"])
            GV["conv_ln_g"][j], GV["conv_ln_b"][j], GV["conv_b_dw"][j] = d_ln_g, d_ln_b, d_b_dw
            GV["conv_b_in"][j], GV["conv_w_dw"][j] = d_b_in, d_w_dw[:CONV_WIDTH]
            g_in = _grad_w_call(f"g_conv_in{i}", sv["h0"], "full", du, "nat")
            mixer_pending = [(("conv_w_in", j), g_in), (("conv_w_out", j), g_out)]
            res = _dx_colsharded_rms_call(
                f"d_x0_{i}", [du], False, [W["conv_w_in", j]], 0, sv["x0"], g_mix, dx, False, comm=scatter_of(mixer_pending) if i == 0 else None)
            if i == 0:
                res, bufs = res
                record(mixer_pending, bufs)
                mixer_pending = []
            dx, dxb, dg, _ = res
        elif kind == 1:
            dmix, dyp, d_scale = _pool_out_bwd_call(f"pool_out_bwd{i}", dx, sv["ypre"], vec(V["pool_scale"], j), wp)
            GV["pool_scale"][j] = d_scale
            g_pool = _pool_w_grad_call(f"g_pool_w{i}", sv["mix"], dyp, ng)
            gc = g_pool.shape[1]
            g_pool = g_pool.reshape(ng, N_DEV, gc // N_DEV, gc).transpose(1, 0, 2, 3).reshape(N_DEV, ng * (gc // N_DEV), gc)
            mixer_pending = [(("pool_w", j), g_pool.astype(BF16))]
            dx, dxb, dg = _pool_mix_bwd_call(f"pool_mix_bwd{i}", dmix, sv["x0"], g_mix, dx)
        else:
            g_o = _grad_w_call(f"g_attn_o{i}", sv["ob"], "nat", dxb, "full")
            do = _to_heads(_dx_full_call(f"d_attn_o{i}", dxb, W["attn_w_o", j], 0, F32), n_heads)
            dq, dk, dv, dqg, dkg, dbias, dsink = _attn_bwd_call(
                f"attn_bwd{i}", sv["q"], sv["k"], sv["v"], do, vec(V["attn_q_norm"], j), vec(V["attn_k_norm"], j), bias_tab, sinks3)
            GV["attn_q_norm"] = jnp.sum(dqg, axis=0)
            GV["attn_k_norm"] = jnp.sum(dkg, axis=0)
            GV["attn_sinks"] = dsink.reshape(1, n_heads)
            GV["rel_bias"] = _bucket_sum_call(f"g_rel_bias{i}", onehot, dbias.reshape(n_heads, -1))
            dqkv = jnp.concatenate([_from_heads(dq), _from_heads(dk), _from_heads(dv)], axis=-1).astype(BF16)
            g_qkv = _grad_w_call(f"g_qkv{i}", sv["h0"], "full", dqkv, "nat")
            mixer_pending = [(("attn_w_qkv", j), g_qkv), (("attn_w_o", j), g_o)]
            dx, dxb, dg, _ = _dx_colsharded_rms_call(f"d_x0_{i}", [dqkv], False, [W["attn_w_qkv", j]], 0, sv["x0"], g_mix, dx, False)
        GV["norm_mix"][i] = dg
    if mixer_pending:
        record(mixer_pending, _exchange_call("grad_scatter_tail", scatter_of(mixer_pending)))
    return loss, dx, landed, GV, V


_BIG = ("conv_w_in", "conv_w_out", "pool_w", "attn_w_qkv", "attn_w_o", "ffn_w_gate", "ffn_w_up", "ffn_w_down", "ple_w_proj", "ple_w_gate")
_SMALL_REPLICATED = ("norm_mix", "norm_ffn", "norm_ple", "pool_scale", "attn_q_norm", "attn_k_norm", "attn_sinks", "rel_bias", "ple_b_gate")
_WEIGHTS = ("norm_mix", "norm_ffn", "norm_ple", "conv_w_in", "conv_b_in", "conv_w_dw", "conv_b_dw", "conv_ln_g", "conv_ln_b", "conv_w_out",
            "conv_b_out", "pool_w", "pool_scale", "attn_w_qkv", "attn_q_norm", "attn_k_norm", "attn_sinks", "attn_w_o", "rel_bias",
            "ffn_w_gate", "ffn_w_up", "ffn_w_down", "ple_w_proj", "ple_w_gate", "ple_b_gate")
PACK_LANES = 128


def _pack(parts):
    flat = jnp.concatenate([t.reshape(-1).astype(F32) for t in parts])
    rows = -(-flat.shape[0] // (8 * PACK_LANES)) * 8
    flat = jnp.pad(flat, (0, rows * PACK_LANES - flat.shape[0]))
    return flat.reshape(rows, PACK_LANES)


def _unpack(pack, shapes, lead=()):
    flat = pack.reshape(*lead, -1)
    out, pos = [], 0
    for s in shapes:
        n = math.prod(s)
        out.append(flat[..., pos:pos + n].reshape(*lead, *s))
        pos += n
    return out


def _as2d(t):
    return t.reshape(-1, t.shape[-1])


def kernel(x, p, norm_mix, norm_ffn, norm_ple, conv_w_in, conv_b_in, conv_w_dw, conv_b_dw, conv_ln_g, conv_ln_b, conv_w_out, conv_b_out, pool_w, pool_scale, attn_w_qkv, attn_q_norm, attn_k_norm, attn_sinks, attn_w_o, rel_bias, ffn_w_gate, ffn_w_up, ffn_w_down, ple_w_proj, ple_w_gate, ple_b_gate, loss_target, m_norm_mix, m_norm_ffn, m_norm_ple, m_conv_w_in, m_conv_b_in, m_conv_w_dw, m_conv_b_dw, m_conv_ln_g, m_conv_ln_b, m_conv_w_out, m_conv_b_out, m_pool_w, m_pool_scale, m_attn_w_qkv, m_attn_q_norm, m_attn_k_norm, m_attn_sinks, m_attn_w_o, m_rel_bias, m_ffn_w_gate, m_ffn_w_up, m_ffn_w_down, m_ple_w_proj, m_ple_w_gate, m_ple_b_gate, v_norm_mix, v_norm_ffn, v_norm_ple, v_conv_w_in, v_conv_b_in, v_conv_w_dw, v_conv_b_dw, v_conv_ln_g, v_conv_ln_b, v_conv_w_out, v_conv_b_out, v_pool_w, v_pool_scale, v_attn_w_qkv, v_attn_q_norm, v_attn_k_norm, v_attn_sinks, v_attn_w_o, v_rel_bias, v_ffn_w_gate, v_ffn_w_up, v_ffn_w_down, v_ple_w_proj, v_ple_w_gate, v_ple_b_gate):
    given = dict(locals())
    w = {n: given[n] for n in _WEIGHTS}
    m = {n: given["m_" + n] for n in _WEIGHTS}
    v = {n: given["v_" + n] for n in _WEIGHTS}
    me = _slot(*_mesh_pos())

    wb = {n: w[n].astype(BF16) for n in _BIG}
    for n in _FFN_UP_WEIGHTS:
        wb[n] = wb[n].transpose(0, 2, 1)
    small_pack = _pack([w[n] for n in _SMALL_SHARDED])
    small_shapes = [w[n].shape for n in _SMALL_SHARDED]
    loss, grad_x, landed, GV, V = _step(x[0], p[:, 0], loss_target[0], wb, small_pack, small_shapes, {n: w[n] for n in _SMALL_REPLICATED})
    loss = lax.psum(loss, ("x", "y", "c"))

    small_names = list(_SMALL_REPLICATED) + list(_SMALL_SHARDED)
    small_full = []
    for n in small_names:
        g = GV[n]
        g = jnp.stack([t.reshape(V[n].shape[1:]) for t in g]) if isinstance(g, list) else g.reshape(V[n].shape)
        small_full.append(g)
    reduced = _unpack(_all_reduce_small_call("all_reduce_small", _pack(small_full)), [t.shape for t in small_full])
    small_grad = {}
    for n, g in zip(small_names, reduced):
        if n in _SMALL_SHARDED:
            c = w[n].shape[-1]
            g = lax.dynamic_slice_in_dim(g, me * c, c, axis=g.ndim - 1)
        small_grad[n] = g

    out = {}
    for n in _BIG:
        per_layer = []
        for l in range(w[n].shape[0]):
            land = landed[n, l]
            if n in _FFN_UP_WEIGHTS:
                g = _sum_slots_call(f"gsum_{n}{l}", land).T
                res = _adamw_call(f"adamw_{n}{l}", g, False, w[n][l], m[n][l], v[n][l])
            else:
                res = _adamw_call(f"adamw_{n}{l}", land, True, _as2d(w[n][l]), _as2d(m[n][l]), _as2d(v[n][l]))
            per_layer.append([t.reshape(w[n][l].shape) for t in res])
        out[n] = [jnp.stack([pl_[q] for pl_ in per_layer]) for q in range(4)]
    for n in small_names:
        res = _adamw_call(f"adamw_{n}", _as2d(small_grad[n]), False, _as2d(w[n]), _as2d(m[n]), _as2d(v[n]))
        out[n] = [t.reshape(w[n].shape) for t in res]

    grads = [out[n][0] for n in _WEIGHTS]
    deltas = [out[n][1] for n in _WEIGHTS]
    new_m = [out[n][2] for n in _WEIGHTS]
    new_v = [out[n][3] for n in _WEIGHTS]
    return (loss, grad_x[None], *grads, *deltas, *new_m, *new_v)
```

```python
import functools
import math

import jax
import jax.numpy as jnp
from jax import lax
from jax.experimental import pallas as pl
from jax.experimental.pallas import tpu as pltpu

F32, BF16 = jnp.float32, jnp.bfloat16
N_DEV = 8
EPS = 1e-6
NEG_INF = -1e30
HEAD_DIM = 64
Q_BLOCK = 128
CHUNK = 64
WINDOW_CHUNKS = 2
CONV_WIDTH = 31
CONV_HALO = 32
POOL_WINDOWS = (2, 4, 8, 16)
POOL_HALO = 16
NUM_BUCKETS = 32
REL_MAX_DIST = 128
ADAM_LR, ADAM_B1, ADAM_B2, ADAM_EPS, ADAM_WD, ADAM_STEP = 0.001, 0.9, 0.999, 1e-08, 0.01, 10
VMEM_LIMIT_BYTES = 44 * 1024 * 1024
BIG_VMEM_LIMIT_BYTES = 52 * 1024 * 1024
ROW_TILE = 512
FULL_ROW_TILE = 256
EPILOGUE_ROWS = 128
ELT_TILE = 256
CONV_TILE = 128
CONV_SUB = 32
CONV_LANES = 512
COL_TILE = 512
K_TILE = 2048
ADAM_BLOCK_BYTES = 1 << 20

_DIMS = {
    "nn": (((1,), (0,)), ((), ())),
    "nt": (((1,), (1,)), ((), ())),
    "tn": (((0,), (0,)), ((), ())),
}


def _dot(a, b, mode):
    return lax.dot_general(a, b, _DIMS[mode], preferred_element_type=F32)


def _tile(n, t):
    t = min(n, t)
    assert n % t == 0, (n, t)
    return t


def _div_tile(n, t):
    if n <= t:
        return n
    for cand in range(t // 16 * 16, 15, -16):
        if n % cand == 0:
            return cand
    return n


def _sds(shape, dtype):
    return jax.ShapeDtypeStruct(tuple(shape), dtype)


def _call(name, body, grid, in_specs, out_specs, out_shape, ins, scratch=(), comm=None, vmem=VMEM_LIMIT_BYTES, aliases=None):
    params = pltpu.CompilerParams(dimension_semantics=("arbitrary",) * len(grid), vmem_limit_bytes=vmem)
    if comm is None:
        return pl.pallas_call(
            body, name=name, grid=grid, in_specs=list(in_specs), out_specs=out_specs, out_shape=out_shape,
            scratch_shapes=list(scratch), compiler_params=params, input_output_aliases=aliases or {},
        )(*ins)
    assert not aliases
    single = not isinstance(out_shape, (list, tuple))
    own_specs = [out_specs] if single else list(out_specs)
    own_shape = [out_shape] if single else list(out_shape)
    n_in, n_out, n_scr = len(ins), len(own_shape), len(scratch)
    n_src, n_dst = len(comm.srcs), len(comm.out_shape)
    hbm = pl.BlockSpec(memory_space=pltpu.HBM)

    def with_comm(*refs):
        a = n_in
        b = a + n_src
        c = b + n_out
        d = c + n_dst
        e = d + n_scr
        src_refs, dst_refs, sem_refs = refs[a:b], refs[c:d], refs[e:]
        first = functools.reduce(jnp.logical_and, [pl.program_id(i) == 0 for i in range(len(grid))])
        last = functools.reduce(jnp.logical_and, [pl.program_id(i) == grid[i] - 1 for i in range(len(grid))])

        @pl.when(first)
        def _():
            comm.start(src_refs, dst_refs, sem_refs)

        body(*refs[:a], *refs[b:c], *refs[d:e])

        @pl.when(last)
        def _():
            comm.wait(src_refs, dst_refs, sem_refs)

    res = pl.pallas_call(
        with_comm, name=name, grid=grid,
        in_specs=list(in_specs) + [hbm] * n_src,
        out_specs=own_specs + [hbm] * n_dst,
        out_shape=own_shape + list(comm.out_shape),
        scratch_shapes=list(scratch) + list(comm.sems),
        compiler_params=params,
    )(*ins, *comm.srcs)
    own = res[0] if single else list(res[:n_out])
    return own, list(res[n_out:])


def _mesh_pos():
    return lax.axis_index("x"), lax.axis_index("y"), lax.axis_index("c")


def _slot(px, py, pc):
    return 4 * px + 2 * py + pc


def _peers(x, y, c):
    flips = [(fx, fy, fc) for fx in (0, 1) for fy in (0, 1) for fc in (0, 1)][1:]
    return [(1 - x if fx else x, 1 - y if fy else y, 1 - c if fc else c) for fx, fy, fc in flips]


def _remote(src, dst, send_sem, recv_sem, to):
    return pltpu.make_async_remote_copy(
        src_ref=src, dst_ref=dst, send_sem=send_sem, recv_sem=recv_sem, device_id=to, device_id_type=pl.DeviceIdType.MESH
    )


class _GatherOwn:
    N_TO = 4

    def __init__(self, shards):
        n = len(shards)
        self.srcs = list(shards)
        self.out_shape = [_sds((N_DEV, *s.shape), s.dtype) for s in shards]
        self.sems = [pltpu.SemaphoreType.DMA((self.N_TO * n,)), pltpu.SemaphoreType.DMA((self.N_TO * n,)), pltpu.SemaphoreType.DMA((n,))]

    def _copies(self, src_refs, dst_refs, sem_refs):
        send_sems, recv_sems, local_sems = sem_refs
        x, y, c = _mesh_pos()
        me = (x, y, c)
        targets = [(x, y, 1 - c), (1 - x, y, c), (x, 1 - y, c), (1 - x, 1 - y, c)]
        sends, recvs, local = [], [], []
        for t, (src, dst) in enumerate(zip(src_refs, dst_refs)):
            local.append(pltpu.make_async_copy(src, dst.at[_slot(*me)], local_sems.at[t]))
            for k, to in enumerate(targets):
                s = self.N_TO * t + k
                sends.append(_remote(src, dst.at[_slot(*me)], send_sems.at[s], recv_sems.at[s], to))
                recvs.append(_remote(src, dst.at[_slot(*to)], send_sems.at[s], recv_sems.at[s], to))
        return sends, recvs, local

    def start(self, src_refs, dst_refs, sem_refs):
        sends, _, local = self._copies(src_refs, dst_refs, sem_refs)
        for cp in local + sends:
            cp.start()

    def wait(self, src_refs, dst_refs, sem_refs):
        sends, recvs, local = self._copies(src_refs, dst_refs, sem_refs)
        for cp in recvs:
            cp.wait_recv()
        for cp in sends:
            cp.wait_send()
        for cp in local:
            cp.wait()


class _ScatterSlots:
    def __init__(self, grads):
        n = len(grads)
        self.srcs = list(grads)
        self.out_shape = [_sds(g.shape, g.dtype) for g in grads]
        self.sems = [pltpu.SemaphoreType.DMA((7 * n,)), pltpu.SemaphoreType.DMA((7 * n,)), pltpu.SemaphoreType.DMA((n,))]

    def _copies(self, src_refs, dst_refs, sem_refs):
        send_sems, recv_sems, local_sems = sem_refs
        x, y, c = _mesh_pos()
        me = _slot(x, y, c)
        sends, recvs, local = [], [], []
        for t, (src, dst) in enumerate(zip(src_refs, dst_refs)):
            local.append(pltpu.make_async_copy(src.at[me], dst.at[me], local_sems.at[t]))
            for k, to in enumerate(_peers(x, y, c)):
                s = 7 * t + k
                sends.append(_remote(src.at[_slot(*to)], dst.at[me], send_sems.at[s], recv_sems.at[s], to))
                recvs.append(_remote(src.at[me], dst.at[_slot(*to)], send_sems.at[s], recv_sems.at[s], to))
        return sends, recvs, local

    start = _GatherOwn.start
    wait = _GatherOwn.wait


def _sig(x):
    return 1.0 / (1.0 + jnp.exp(-x))


def _rms(x, g):
    r = lax.rsqrt(jnp.mean(x * x, axis=-1, keepdims=True) + EPS)
    return x * r * g


def _rms_bwd(dy, x, g):
    r = lax.rsqrt(jnp.mean(x * x, axis=-1, keepdims=True) + EPS)
    xh = x * r
    dg = jnp.sum(dy * xh, axis=0, keepdims=True)
    dxh = dy * g
    dx = r * (dxh - xh * jnp.mean(dxh * xh, axis=-1, keepdims=True))
    return dx, dg


def _accumulate(ref, val, first):
    @pl.when(first)
    def _():
        ref[...] = val

    @pl.when(jnp.logical_not(first))
    def _():
        ref[...] += val


def _kloop(k, nk, acc_refs, contribs, finish):
    @pl.when(k == 0)
    def _():
        for r, c in zip(acc_refs, contribs):
            r[...] = c

    @pl.when(k > 0)
    def _():
        for r, c in zip(acc_refs, contribs):
            r[...] += c

    @pl.when(k == nk - 1)
    def _():
        finish()


def _rms_call(name, x, g):
    S, D = x.shape
    ts = _tile(S, ELT_TILE)

    def body(x_ref, g_ref, o_ref):
        o_ref[...] = _rms(x_ref[...], g_ref[...]).astype(BF16)

    return _call(
        name, body, (S // ts,),
        [pl.BlockSpec((ts, D), lambda i: (i, 0)), pl.BlockSpec((1, D), lambda i: (0, 0))],
        pl.BlockSpec((ts, D), lambda i: (i, 0)), _sds((S, D), BF16), (x, g),
    )


def _conv_in_call(name, hb, w, layer, b_in, comm=None):
    S, D = hb.shape
    nsh = w.shape[-1]
    half = N_DEV // 2
    assert nsh * half == D
    tm = _tile(S, ROW_TILE)

    def body(h_ref, wa_ref, wg_ref, ba_ref, bg_ref, a_ref, g_ref, u_ref):
        h = h_ref[...]
        a = _dot(h, wa_ref[...], "nn") + ba_ref[...]
        g = _dot(h, wg_ref[...], "nn") + bg_ref[...]
        a_ref[...] = a.astype(BF16)
        g_ref[...] = g.astype(BF16)
        u_ref[...] = a * _sig(g)

    out_spec = pl.BlockSpec((tm, nsh), lambda d, i: (i, d))
    return _call(
        name, body, (half, S // tm),
        [
            pl.BlockSpec((tm, D), lambda d, i: (i, 0)),
            pl.BlockSpec((None, None, D, nsh), lambda d, i: (d, layer, 0, 0)),
            pl.BlockSpec((None, None, D, nsh), lambda d, i: (d + half, layer, 0, 0)),
            pl.BlockSpec((1, nsh), lambda d, i: (0, d)),
            pl.BlockSpec((1, nsh), lambda d, i: (0, d + half)),
        ],
        [out_spec, out_spec, out_spec],
        [_sds((S, D), BF16), _sds((S, D), BF16), _sds((S, D), F32)],
        (hb, w, w, b_in, b_in), comm=comm,
    )


SUBLANES = 8


def _fill_shifted(sh_ref, ext_ref, rows):
    ext_ref[rows:, :] = jnp.zeros((SUBLANES, ext_ref.shape[1]), F32)
    for s in range(SUBLANES):
        sh_ref[s] = ext_ref[pl.ds(s, rows), :]


def _shifted_window(sh_ref, row, lane0):
    s = row % SUBLANES
    return sh_ref[s, pl.ds(row - s, CONV_SUB), pl.ds(lane0, CONV_LANES)]


def _conv_taps(sh_ref, w_ref, row0, lane0, offset, reverse):
    acc = None
    for k in range(CONV_WIDTH):
        off = offset - k if reverse else offset + k
        term = w_ref[k:k + 1, lane0:lane0 + CONV_LANES] * _shifted_window(sh_ref, row0 + off, lane0)
        acc = term if acc is None else acc + term
    return acc


def _dwconv_call(name, u1, w_dw, b_dw, ln_g, ln_b, comm=None):
    S, D = u1.shape
    tc = _tile(S, CONV_TILE)
    hb = tc // CONV_HALO
    lanes = min(D, CONV_LANES)
    assert lanes == CONV_LANES and D % CONV_LANES == 0 and tc % CONV_SUB == 0

    def body(cur_ref, halo_ref, w_ref, b_ref, g_ref, bb_ref, u2_ref, u4_ref, ext_ref, sh_ref):
        i = pl.program_id(0)
        ext_ref[0:CONV_HALO, :] = jnp.where(i > 0, halo_ref[...], 0.0)
        ext_ref[CONV_HALO:tc + CONV_HALO, :] = cur_ref[...]
        _fill_shifted(sh_ref, ext_ref, tc + CONV_HALO)
        first_tap = CONV_HALO - (CONV_WIDTH - 1)
        for r in range(0, tc, CONV_SUB):
            for c in range(0, D, CONV_LANES):
                u2_ref[r:r + CONV_SUB, c:c + CONV_LANES] = (
                    _conv_taps(sh_ref, w_ref, r, c, first_tap, False) + b_ref[:, c:c + CONV_LANES]
                )
        u2 = u2_ref[...]
        mu = jnp.mean(u2, axis=-1, keepdims=True)
        xc = u2 - mu
        u3 = xc * lax.rsqrt(jnp.mean(xc * xc, axis=-1, keepdims=True) + EPS) * g_ref[...] + bb_ref[...]
        u4_ref[...] = (u3 * _sig(u3)).astype(BF16)

    vec = pl.BlockSpec((1, D), lambda i: (0, 0))
    row = pl.BlockSpec((tc, D), lambda i: (i, 0))
    return _call(
        name, body, (S // tc,),
        [
            row,
            pl.BlockSpec((CONV_HALO, D), lambda i: (jnp.maximum(i * hb - 1, 0), 0)),
            pl.BlockSpec((CONV_WIDTH, D), lambda i: (0, 0)),
            vec, vec, vec,
        ],
        [row, row],
        [_sds((S, D), F32), _sds((S, D), BF16)],
        (u1, u1, w_dw, b_dw, ln_g, ln_b),
        scratch=[pltpu.VMEM((tc + CONV_HALO + SUBLANES, D), F32), pltpu.VMEM((SUBLANES, tc + CONV_HALO, D), F32)], comm=comm,
    )


def _row_chunks(tm):
    step = min(tm, EPILOGUE_ROWS)
    return [slice(r, r + step) for r in range(0, tm, step)]


def _full_weight_spec(w, layer):
    _, _, ksh, D = w.shape
    return pl.BlockSpec((N_DEV, None, ksh, D), lambda i: (0, layer, 0, 0), pipeline_mode=pl.Buffered(1))


def _fullmm_res_call(name, a, w, layer, x_res, bias, g_next):
    S, D = x_res.shape
    K = a.shape[1]
    tm = _tile(S, FULL_ROW_TILE)
    has_b = bias is not None

    def body(*refs):
        a_ref, w_ref, x_ref = refs[:3]
        b_ref = refs[3] if has_b else None
        g_ref, xo_ref, ho_ref = refs[3 + int(has_b):]
        xn = x_ref[...] + _dot(a_ref[...], w_ref[...].reshape(K, D), "nn")
        if has_b:
            xn = xn + b_ref[...]
        xo_ref[...] = xn
        ho_ref[...] = _rms(xn, g_ref[...]).astype(BF16)

    row = pl.BlockSpec((tm, D), lambda i: (i, 0))
    vec = pl.BlockSpec((1, D), lambda i: (0, 0))
    in_specs = [pl.BlockSpec((tm, K), lambda i: (i, 0)), _full_weight_spec(w, layer), row] + ([vec] if has_b else []) + [vec]
    ins = [a, w, x_res] + ([bias] if has_b else []) + [g_next]
    return _call(name, body, (S // tm,), in_specs, [row, row], [_sds((S, D), F32), _sds((S, D), BF16)], ins)


SHARD_PAIR = 2


def _pair_spec(w, layer):
    _, _, fsh, D = w.shape
    return pl.BlockSpec((SHARD_PAIR, None, fsh, D), lambda d, i: (d, layer, 0, 0))


def _ffn_up_call(name, hb, wgt, wut, layer, comm=None):
    S, D = hb.shape
    fsh = wgt.shape[2]
    tn = SHARD_PAIR * fsh
    tm = _tile(S, ROW_TILE)

    def body(h_ref, wg_ref, wu_ref, g_ref, u_ref, a_ref):
        h = h_ref[...]
        g = _dot(h, wg_ref[...].reshape(tn, D), "nt")
        u = _dot(h, wu_ref[...].reshape(tn, D), "nt")
        g_ref[...] = g.astype(BF16)
        u_ref[...] = u.astype(BF16)
        a_ref[...] = (g * _sig(g) * u).astype(BF16)

    ospec = pl.BlockSpec((tm, tn), lambda d, i: (i, d))
    return _call(
        name, body, (N_DEV // SHARD_PAIR, S // tm),
        [pl.BlockSpec((tm, D), lambda d, i: (i, 0)), _pair_spec(wgt, layer), _pair_spec(wut, layer)],
        [ospec] * 3, [_sds((S, N_DEV * fsh), BF16)] * 3, (hb, wgt, wut), comm=comm,
    )


def _mm_rows_call(name, a, w, layer, add, comm=None):
    S, K = a.shape
    _, _, ksh, N = w.shape
    assert K == N_DEV * ksh
    tm = _tile(S, ROW_TILE)
    tn = _tile(N, COL_TILE)
    has_add = add is not None

    def body(*refs):
        a_ref, w_ref = refs[:2]
        o_ref = refs[-1]
        y = _dot(a_ref[...], w_ref[...].reshape(K, tn), "nn")
        o_ref[...] = y + refs[2][...] if has_add else y

    tile = pl.BlockSpec((tm, tn), lambda i, n: (i, n))
    in_specs = [pl.BlockSpec((tm, K), lambda i, n: (i, 0)), pl.BlockSpec((N_DEV, None, ksh, tn), lambda i, n: (0, layer, 0, n))]
    return _call(
        name, body, (S // tm, N // tn), in_specs + ([tile] if has_add else []), tile, _sds((S, N), F32),
        (a, w) + ((add,) if has_add else ()), comm=comm,
    )


def _res_rms_call(name, y, x_res, g_next):
    S, D = x_res.shape
    ts = _tile(S, ELT_TILE)

    def body(y_ref, x_ref, g_ref, xo_ref, ho_ref):
        xn = x_ref[...] + y_ref[...]
        xo_ref[...] = xn
        ho_ref[...] = _rms(xn, g_ref[...]).astype(BF16)

    row = pl.BlockSpec((ts, D), lambda i: (i, 0))
    return _call(
        name, body, (S // ts,), [row, row, pl.BlockSpec((1, D), lambda i: (0, 0))],
        [row, row], [_sds((S, D), F32), _sds((S, D), BF16)], (y, x_res, g_next),
    )


def _colmm_call(name, a, w, layer, out_dtype):
    S, K = a.shape
    nsh = w.shape[-1]
    tm = _tile(S, ROW_TILE)

    def body(a_ref, w_ref, o_ref):
        o_ref[...] = _dot(a_ref[...], w_ref[...], "nn").astype(out_dtype)

    return _call(
        name, body, (N_DEV, S // tm),
        [pl.BlockSpec((tm, K), lambda d, i: (i, 0)), pl.BlockSpec((None, None, K, nsh), lambda d, i: (d, layer, 0, 0))],
        pl.BlockSpec((tm, nsh), lambda d, i: (i, d)), _sds((S, N_DEV * nsh), out_dtype), (a, w),
    )


def _ple_gate_call(name, hb, w, layer, x_res, bias, pp, g_next):
    S, D = x_res.shape
    K = hb.shape[1]
    tm = _tile(S, FULL_ROW_TILE)
    has_g = g_next is not None

    def body(*refs):
        a_ref, w_ref, x_ref, b_ref, p_ref = refs[:5]
        g_ref = refs[5] if has_g else None
        outs = refs[5 + int(has_g):]
        gate = _sig(_dot(a_ref[...], w_ref[...].reshape(K, D), "nn") + b_ref[...])
        xn = x_ref[...] + gate * p_ref[...].astype(F32)
        outs[0][...] = xn
        outs[1][...] = gate.astype(BF16)
        if has_g:
            outs[2][...] = _rms(xn, g_ref[...]).astype(BF16)

    row = pl.BlockSpec((tm, D), lambda i: (i, 0))
    vec = pl.BlockSpec((1, D), lambda i: (0, 0))
    in_specs = [pl.BlockSpec((tm, K), lambda i: (i, 0)), _full_weight_spec(w, layer), row, vec, row]
    ins = [hb, w, x_res, bias, pp]
    out_specs, out_shape = [row, row], [_sds((S, D), F32), _sds((S, D), BF16)]
    if has_g:
        in_specs.append(vec)
        ins.append(g_next)
        out_specs.append(row)
        out_shape.append(_sds((S, D), BF16))
    res = _call(name, body, (S // tm,), in_specs, out_specs, out_shape, ins)
    return res if has_g else (res[0], res[1], None)


def _pool_mix_call(name, x, g):
    S, D = x.shape
    ts = _tile(S, ELT_TILE)
    hb = ts // POOL_HALO
    gc = D // len(POOL_WINDOWS)

    def body(cur_ref, halo_ref, g_ref, o_ref, ext_ref):
        i = pl.program_id(0)
        gain = g_ref[...]
        ext_ref[0:POOL_HALO, :] = jnp.where(i > 0, _rms(halo_ref[...], gain), 0.0)
        ext_ref[POOL_HALO:, :] = _rms(cur_ref[...], gain)
        t = i * ts + lax.broadcasted_iota(jnp.int32, (ts, 1), 0)
        for gi, win in enumerate(POOL_WINDOWS):
            lanes = pl.ds(gi * gc, gc)
            h = ext_ref[pl.ds(POOL_HALO, ts), lanes]
            acc = h
            for j in range(1, win):
                acc = acc + ext_ref[pl.ds(POOL_HALO - j, ts), lanes]
            cnt = jnp.minimum(t + 1, win).astype(F32)
            o_ref[:, gi * gc:(gi + 1) * gc] = (acc / cnt - h).astype(BF16)

    row = pl.BlockSpec((ts, D), lambda i: (i, 0))
    return _call(
        name, body, (S // ts,),
        [row, pl.BlockSpec((POOL_HALO, D), lambda i: (jnp.maximum(i * hb - 1, 0), 0)), pl.BlockSpec((1, D), lambda i: (0, 0))],
        row, _sds((S, D), BF16), (x, x, g), scratch=[pltpu.VMEM((ts + POOL_HALO, D), F32)],
    )


def _pool_out_call(name, mix, wp, scale, x_res, g_next):
    S, D = x_res.shape
    ng, gc, _ = wp.shape
    tm = _tile(S, FULL_ROW_TILE)

    def body(m_ref, w_ref, s_ref, x_ref, g_ref, xo_ref, y_ref, h_ref):
        parts = [_dot(m_ref[:, gi * gc:(gi + 1) * gc], w_ref[gi], "nn") for gi in range(ng)]
        ypre = jnp.concatenate(parts, axis=-1)
        y_ref[...] = ypre.astype(BF16)
        xn = x_ref[...] + ypre * s_ref[...]
        xo_ref[...] = xn
        h_ref[...] = _rms(xn, g_ref[...]).astype(BF16)

    row = pl.BlockSpec((tm, D), lambda i: (i, 0))
    vec = pl.BlockSpec((1, D), lambda i: (0, 0))
    return _call(
        name, body, (S // tm,),
        [row, pl.BlockSpec((ng, gc, gc), lambda i: (0, 0, 0)), vec, row, vec],
        [row, row, row], [_sds((S, D), F32), _sds((S, D), BF16), _sds((S, D), BF16)],
        (mix, wp, scale, x_res, g_next),
    )


KV_STEP = 2


def _attn_probs(q, kprev, kcur, qg, kg, bias, sink, n):
    grp = q.shape[0]
    qn = _rms(q, qg)
    k = jnp.concatenate([kprev, kcur], axis=0)
    kn = _rms(k, kg)
    s = _dot(qn.reshape(grp * Q_BLOCK, HEAD_DIM).astype(BF16), kn.astype(BF16), "nt") * (HEAD_DIM ** -0.5)
    s = s.reshape(grp, Q_BLOCK, 2 * Q_BLOCK) + bias
    qi = lax.broadcasted_iota(jnp.int32, (Q_BLOCK, 2 * Q_BLOCK), 0)
    kj = lax.broadcasted_iota(jnp.int32, (Q_BLOCK, 2 * Q_BLOCK), 1)
    qc = qi // CHUNK
    kc = kj // CHUNK - Q_BLOCK // CHUNK
    ok = (kc <= qc) & (kc >= qc - WINDOW_CHUNKS) & ((n > 0) | (kj >= Q_BLOCK))
    s = jnp.where(ok[None], s, NEG_INF)
    m = jnp.maximum(jnp.max(s, axis=-1, keepdims=True), sink)
    e = jnp.exp(s - m)
    es = jnp.exp(sink - m)
    inv = 1.0 / (jnp.sum(e, axis=-1, keepdims=True) + es)
    return q, qn, k, kn, e * inv, es * inv


def _attn_specs(grp, nb):
    heads = KV_STEP * grp
    qspec = pl.BlockSpec((heads, Q_BLOCK, HEAD_DIM), lambda j, n: (j, jnp.minimum(n, nb - 1), 0))
    prev = pl.BlockSpec((KV_STEP, Q_BLOCK, HEAD_DIM), lambda j, n: (j, jnp.maximum(n - 1, 0), 0))
    cur = pl.BlockSpec((KV_STEP, Q_BLOCK, HEAD_DIM), lambda j, n: (j, jnp.minimum(n, nb - 1), 0))
    gain = pl.BlockSpec((1, HEAD_DIM), lambda j, n: (0, 0))
    bias = pl.BlockSpec((heads, Q_BLOCK, 2 * Q_BLOCK), lambda j, n: (j, 0, 0))
    sink = pl.BlockSpec((heads, 1, 1), lambda j, n: (j, 0, 0))
    return qspec, prev, cur, gain, bias, sink


def _attn_fwd_call(name, q, k, v, qg, kg, bias, sinks):
    H, S, _ = q.shape
    n_kv = k.shape[0]
    grp = H // n_kv
    nb = S // Q_BLOCK
    assert n_kv % KV_STEP == 0

    def body(q_ref, kp_ref, kc_ref, vp_ref, vc_ref, qg_ref, kg_ref, bias_ref, sink_ref, o_ref):
        n = pl.program_id(1)
        for hh in range(KV_STEP):
            hs = slice(hh * grp, (hh + 1) * grp)
            _, _, _, _, p, _ = _attn_probs(q_ref[hs], kp_ref[hh], kc_ref[hh], qg_ref[...], kg_ref[...], bias_ref[hs], sink_ref[hs], n)
            vv = jnp.concatenate([vp_ref[hh], vc_ref[hh]], axis=0).astype(BF16)
            o = _dot(p.reshape(grp * Q_BLOCK, 2 * Q_BLOCK).astype(BF16), vv, "nn")
            o_ref[hs] = o.reshape(grp, Q_BLOCK, HEAD_DIM).astype(BF16)

    qspec, prev, cur, gain, bspec, sspec = _attn_specs(grp, nb)
    return _call(
        name, body, (n_kv // KV_STEP, nb),
        [qspec, prev, cur, prev, cur, gain, gain, bspec, sspec],
        qspec, _sds((H, S, HEAD_DIM), BF16), (q, k, k, v, v, qg, kg, bias, sinks),
    )


def _loss_call(name, y, target):
    S, D = y.shape
    ts = _tile(S, ELT_TILE)

    def body(y_ref, t_ref, d_ref, db_ref, l_ref):
        err = y_ref[...] - t_ref[...]
        dy = err * (1.0 / D)
        d_ref[...] = dy
        db_ref[...] = dy.astype(BF16)
        part = 0.5 * jnp.sum(jnp.sum(err * err, axis=-1, keepdims=True), axis=0, keepdims=True) * (1.0 / D)
        _accumulate(l_ref, jnp.broadcast_to(part, l_ref.shape), pl.program_id(0) == 0)

    row = pl.BlockSpec((ts, D), lambda i: (i, 0))
    return _call(
        name, body, (S // ts,), [row, row],
        [row, row, pl.BlockSpec((8, 128), lambda i: (0, 0))],
        [_sds((S, D), F32), _sds((S, D), BF16), _sds((8, 128), F32)], (y, target),
    )


def _ple_bwd_elt_call(name, dx, gate, pp):
    S, D = dx.shape
    ts = _tile(S, ELT_TILE)

    def body(dx_ref, gt_ref, p_ref, dz_ref, dp_ref, db_ref):
        d = dx_ref[...]
        gt = gt_ref[...].astype(F32)
        dz = d * p_ref[...].astype(F32) * gt * (1.0 - gt)
        dz_ref[...] = dz.astype(BF16)
        dp_ref[...] = (d * gt).astype(BF16)
        _accumulate(db_ref, jnp.sum(dz, axis=0, keepdims=True), pl.program_id(0) == 0)

    row = pl.BlockSpec((ts, D), lambda i: (i, 0))
    return _call(
        name, body, (S // ts,), [row, row, row],
        [row, row, pl.BlockSpec((1, D), lambda i: (0, 0))],
        [_sds((S, D), BF16), _sds((S, D), BF16), _sds((1, D), F32)], (dx, gate, pp),
    )


def _grad_w_call(name, a, a_mode, b, b_mode, comm=None):
    bs = b if isinstance(b, (list, tuple)) else [b]
    S = a.shape[-2]
    tk = _tile(S, K_TILE)
    nk = S // tk

    def spec(arr, mode):
        if mode == "full":
            c = arr.shape[-1]
            return pl.BlockSpec((tk, c), lambda d, k: (k, 0)), c
        if mode == "nat":
            c = arr.shape[-1] // N_DEV
            return pl.BlockSpec((tk, c), lambda d, k: (k, d)), c
        c = arr.shape[-1]
        return pl.BlockSpec((None, tk, c), lambda d, k: (d, k, 0)), c

    a_spec, ca = spec(a, a_mode)
    b_specs, cbs = zip(*[spec(x, b_mode) for x in bs])
    nb = len(bs)

    def body(*refs):
        a_ref = refs[0]
        b_refs = refs[1:1 + nb]
        o_refs = refs[1 + nb:1 + 2 * nb]
        acc_refs = refs[1 + 2 * nb:]
        k = pl.program_id(1)
        av = a_ref[...]

        def finish():
            for o, acc in zip(o_refs, acc_refs):
                o[...] = acc[...].astype(BF16)

        _kloop(k, nk, acc_refs, [_dot(av, br[...], "tn") for br in b_refs], finish)

    res = _call(
        name, body, (N_DEV, nk), [a_spec, *b_specs],
        [pl.BlockSpec((None, ca, cb), lambda d, k: (d, 0, 0)) for cb in cbs],
        [_sds((N_DEV, ca, cb), BF16) for cb in cbs], (a, *bs),
        scratch=[pltpu.VMEM((ca, cb), F32) for cb in cbs], comm=comm,
    )
    own, landed = res if comm is not None else (res, None)
    own = own if isinstance(b, (list, tuple)) else own[0]
    return own if comm is None else (own, landed)


def _dx_full_call(name, dy, w, layer, out_dtype):
    S, D = dy.shape
    K = N_DEV * w.shape[2]
    tm = _tile(S, ROW_TILE)

    def body(dy_ref, w_ref, o_ref):
        o_ref[...] = _dot(dy_ref[...], w_ref[...].reshape(K, D), "nt").astype(out_dtype)

    return _call(
        name, body, (S // tm,), [pl.BlockSpec((tm, D), lambda i: (i, 0)), _full_weight_spec(w, layer)],
        pl.BlockSpec((tm, K), lambda i: (i, 0)), _sds((S, K), out_dtype), (dy, w),
    )


def _dx_full_rms_call(name, dy, w, layer, x, g, dres):
    S, D = x.shape
    K = N_DEV * w.shape[2]
    assert K == D
    tm = _tile(S, FULL_ROW_TILE)

    def body(dy_ref, w_ref, x_ref, g_ref, r_ref, dx_ref, dxb_ref, dg_ref):
        dh, dg = _rms_bwd(_dot(dy_ref[...], w_ref[...].reshape(K, dy_ref.shape[1]), "nt"), x_ref[...], g_ref[...])
        dx = r_ref[...] + dh
        dx_ref[...] = dx
        dxb_ref[...] = dx.astype(BF16)
        _accumulate(dg_ref, dg, pl.program_id(0) == 0)

    row = pl.BlockSpec((tm, D), lambda i: (i, 0))
    vec = pl.BlockSpec((1, D), lambda i: (0, 0))
    return _call(
        name, body, (S // tm,), [pl.BlockSpec((tm, dy.shape[1]), lambda i: (i, 0)), _full_weight_spec(w, layer), row, vec, row],
        [row, row, vec], [_sds((S, D), F32), _sds((S, D), BF16), _sds((1, D), F32)], (dy, w, x, g, dres),
    )


def _dx_colsharded_rms_call(name, dys, dy_chunked, ws, layer, x, g, dres, want_colsum, comm=None):
    S, D = x.shape
    nsh = ws[0].shape[-1]
    tm = _tile(S, ROW_TILE)
    nt = len(dys)

    def body(*refs):
        dy_refs = refs[:nt]
        w_refs = refs[nt:2 * nt]
        x_ref, g_ref, r_ref = refs[2 * nt:2 * nt + 3]
        outs = refs[2 * nt + 3:]
        acc_ref = outs[0]
        i, k = pl.program_id(0), pl.program_id(1)
        contrib = None
        for dr, wr in zip(dy_refs, w_refs):
            c = _dot(dr[...], wr[...], "nt")
            contrib = c if contrib is None else contrib + c

        def finish():
            dg = colsum = None
            for rows in _row_chunks(tm):
                dh, dg_part = _rms_bwd(acc_ref[rows, :], x_ref[rows, :], g_ref[...])
                dx = r_ref[rows, :] + dh
                outs[0][rows, :] = dx
                outs[1][rows, :] = dx.astype(BF16)
                dg = dg_part if dg is None else dg + dg_part
                if want_colsum:
                    part = jnp.sum(dx, axis=0, keepdims=True)
                    colsum = part if colsum is None else colsum + part
            _accumulate(outs[2], dg, i == 0)
            if want_colsum:
                _accumulate(outs[3], colsum, i == 0)

        _kloop(k, N_DEV, [acc_ref], [contrib], finish)

    if dy_chunked:
        dspec = pl.BlockSpec((None, tm, nsh), lambda i, k: (k, i, 0))
    else:
        dspec = pl.BlockSpec((tm, nsh), lambda i, k: (i, k))
    wspec = pl.BlockSpec((None, None, D, nsh), lambda i, k: (k, layer, 0, 0))
    row = pl.BlockSpec((tm, D), lambda i, k: (i, 0))
    row_once = pl.BlockSpec((tm, D), lambda i, k: (i, 0), pipeline_mode=pl.Buffered(1))
    vec = pl.BlockSpec((1, D), lambda i, k: (0, 0))
    out_specs = [row, row, vec] + ([vec] if want_colsum else [])
    out_shape = [_sds((S, D), F32), _sds((S, D), BF16), _sds((1, D), F32)] + ([_sds((1, D), F32)] if want_colsum else [])
    res = _call(
        name, body, (S // tm, N_DEV), [dspec] * nt + [wspec] * nt + [row_once, vec, row_once],
        out_specs, out_shape, (*dys, *ws, x, g, dres), comm=comm, vmem=BIG_VMEM_LIMIT_BYTES,
    )
    own, landed = res if comm is not None else (res, None)
    own = tuple(own) if want_colsum else (*own, None)
    return own if comm is None else (own, landed)


def _ffn_bwd_hidden_call(name, dyb, w, layer, gpre, upre, comm=None):
    S, D = dyb.shape
    fsh = w.shape[2]
    tn = SHARD_PAIR * fsh
    tm = _tile(S, ROW_TILE)

    def body(dy_ref, w_ref, g_ref, u_ref, dg_ref, du_ref):
        w = w_ref[...].reshape(tn, D)
        half = max(tm // 2, 8)
        for r in range(0, tm, half):
            rows = slice(r, r + half)
            da = _dot(dy_ref[rows, :], w, "nt")
            g = g_ref[rows, :].astype(F32)
            u = u_ref[rows, :].astype(F32)
            s = _sig(g)
            dg_ref[rows, :] = (da * u * s * (1.0 + g * (1.0 - s))).astype(BF16)
            du_ref[rows, :] = (da * g * s).astype(BF16)

    cspec = pl.BlockSpec((tm, tn), lambda d, i: (i, d))
    return _call(
        name, body, (N_DEV // SHARD_PAIR, S // tm),
        [pl.BlockSpec((tm, D), lambda d, i: (i, 0)), _pair_spec(w, layer), cspec, cspec],
        [cspec, cspec], [_sds((S, N_DEV * fsh), BF16)] * 2, (dyb, w, gpre, upre), comm=comm,
    )


def _grad_rows_call(name, a, b, comm=None):
    S, F = a.shape
    N = b.shape[1]
    fsh = F // N_DEV
    tr = SHARD_PAIR * fsh
    tn = _tile(N, 2 * COL_TILE)
    tk = _tile(S, K_TILE)
    nk = S // tk

    def body(a_ref, b_ref, o_ref, acc_ref):
        def finish():
            o_ref[...] = acc_ref[...].astype(BF16).reshape(SHARD_PAIR, fsh, tn)

        _kloop(pl.program_id(2), nk, [acc_ref], [_dot(a_ref[...], b_ref[...], "tn")], finish)

    res = _call(
        name, body, (N_DEV // SHARD_PAIR, N // tn, nk),
        [pl.BlockSpec((tk, tr), lambda d, n, k: (k, d)), pl.BlockSpec((tk, tn), lambda d, n, k: (k, n))],
        pl.BlockSpec((SHARD_PAIR, fsh, tn), lambda d, n, k: (d, 0, n)), _sds((N_DEV, fsh, N), BF16), (a, b),
        scratch=[pltpu.VMEM((tr, tn), F32)], comm=comm,
    )
    return res


def _rms_bwd_res_call(name, dh, x, g, dres, want_colsum):
    S, D = x.shape
    ts = _tile(S, ELT_TILE)

    def body(dh_ref, x_ref, g_ref, r_ref, *outs):
        i = pl.program_id(0)
        d, dg = _rms_bwd(dh_ref[...], x_ref[...], g_ref[...])
        dx = r_ref[...] + d
        outs[0][...] = dx
        outs[1][...] = dx.astype(BF16)
        _accumulate(outs[2], dg, i == 0)
        if want_colsum:
            _accumulate(outs[3], jnp.sum(dx, axis=0, keepdims=True), i == 0)

    row = pl.BlockSpec((ts, D), lambda i: (i, 0))
    vec = pl.BlockSpec((1, D), lambda i: (0, 0))
    out_specs = [row, row, vec] + ([vec] if want_colsum else [])
    out_shape = [_sds((S, D), F32), _sds((S, D), BF16), _sds((1, D), F32)] + ([_sds((1, D), F32)] if want_colsum else [])
    res = _call(name, body, (S // ts,), [row, row, vec, row], out_specs, out_shape, (dh, x, g, dres))
    return tuple(res) if want_colsum else (*res, None)


def _sum_slots_call(name, land):
    _, R, C = land.shape
    tr = _div_tile(R, ADAM_BLOCK_BYTES // (C * 4))

    def body(l_ref, o_ref):
        g = l_ref[0].astype(F32)
        for d in range(1, N_DEV):
            g = g + l_ref[d].astype(F32)
        o_ref[...] = g

    return _call(
        name, body, (R // tr,), [pl.BlockSpec((N_DEV, tr, C), lambda i: (0, i, 0))],
        pl.BlockSpec((tr, C), lambda i: (i, 0)), _sds((R, C), F32), (land,),
    )


def _conv_out_bwd_call(name, dyb, w, layer, u2, ln_g, ln_b):
    S, D = u2.shape
    ksh = w.shape[2]
    tm = _tile(S, ELT_TILE)

    def body(dy_ref, w_ref, u2_ref, g_ref, b_ref, du2_ref, dg_ref, db_ref, dbdw_ref):
        i = pl.program_id(0)
        dy = dy_ref[...]
        du4 = jnp.concatenate([_dot(dy, w_ref[d], "nt") for d in range(N_DEV)], axis=-1)
        u2 = u2_ref[...]
        mu = jnp.mean(u2, axis=-1, keepdims=True)
        xc = u2 - mu
        r = lax.rsqrt(jnp.mean(xc * xc, axis=-1, keepdims=True) + EPS)
        xh = xc * r
        gain = g_ref[...]
        u3 = xh * gain + b_ref[...]
        s = _sig(u3)
        du3 = du4 * s * (1.0 + u3 * (1.0 - s))
        dxh = du3 * gain
        du2 = r * (dxh - jnp.mean(dxh, axis=-1, keepdims=True) - xh * jnp.mean(dxh * xh, axis=-1, keepdims=True))
        du2_ref[...] = du2
        _accumulate(dg_ref, jnp.sum(du3 * xh, axis=0, keepdims=True), i == 0)
        _accumulate(db_ref, jnp.sum(du3, axis=0, keepdims=True), i == 0)
        _accumulate(dbdw_ref, jnp.sum(du2, axis=0, keepdims=True), i == 0)

    row = pl.BlockSpec((tm, D), lambda i: (i, 0))
    vec = pl.BlockSpec((1, D), lambda i: (0, 0))
    return _call(
        name, body, (S // tm,),
        [row, pl.BlockSpec((N_DEV, None, ksh, D), lambda i: (0, layer, 0, 0)), row, vec, vec],
        [row, vec, vec, vec], [_sds((S, D), F32)] + [_sds((1, D), F32)] * 3, (dyb, w, u2, ln_g, ln_b),
    )


def _dwconv_bwd_call(name, du2, u1, w_dw, a_pre, g_pre):
    S, D = u1.shape
    tc = _tile(S, CONV_TILE)
    hb = tc // CONV_HALO
    n_halo = S // CONV_HALO
    assert D % CONV_LANES == 0 and tc % CONV_SUB == 0
    wrows = CONV_HALO

    def body(d_cur, d_next, u_cur, u_prev, w_ref, a_ref, g_ref, du_ref, dbin_ref, dw_ref, dext_ref, uext_ref, du1_ref, dwacc_ref, dsh_ref, ush_ref):
        i = pl.program_id(0)
        last = S // tc - 1
        dext_ref[0:tc, :] = d_cur[...]
        dext_ref[tc:tc + CONV_HALO, :] = jnp.where(i < last, d_next[...], 0.0)
        uext_ref[0:CONV_HALO, :] = jnp.where(i > 0, u_prev[...], 0.0)
        uext_ref[CONV_HALO:tc + CONV_HALO, :] = u_cur[...]
        _fill_shifted(dsh_ref, dext_ref, tc + CONV_HALO)
        _fill_shifted(ush_ref, uext_ref, tc + CONV_HALO)

        @pl.when(i == 0)
        def _():
            dwacc_ref[...] = jnp.zeros_like(dwacc_ref)

        first_tap = CONV_HALO - (CONV_WIDTH - 1)
        for r in range(0, tc, CONV_SUB):
            for c in range(0, D, CONV_LANES):
                du1_ref[r:r + CONV_SUB, c:c + CONV_LANES] = _conv_taps(dsh_ref, w_ref, r, c, CONV_WIDTH - 1, True)
                dcur = dext_ref[r:r + CONV_SUB, c:c + CONV_LANES]
                for k in range(CONV_WIDTH):
                    prod = dcur * _shifted_window(ush_ref, r + first_tap + k, c)
                    part = prod[0:8]
                    for q in range(8, CONV_SUB, 8):
                        part = part + prod[q:q + 8]
                    dwacc_ref[k, :, c:c + CONV_LANES] += part

        du1 = du1_ref[...]
        a = a_ref[...].astype(F32)
        sg = _sig(g_ref[...].astype(F32))
        da = du1 * sg
        dgate = du1 * a * sg * (1.0 - sg)
        du_ref[:, 0:D] = da.astype(BF16)
        du_ref[:, D:2 * D] = dgate.astype(BF16)
        _accumulate(dbin_ref, jnp.concatenate([jnp.sum(da, axis=0, keepdims=True), jnp.sum(dgate, axis=0, keepdims=True)], axis=-1), i == 0)

        @pl.when(i == last)
        def _():
            for k in range(CONV_WIDTH):
                dw_ref[k:k + 1, :] = jnp.sum(dwacc_ref[k], axis=0, keepdims=True)
            dw_ref[CONV_WIDTH:, :] = jnp.zeros((wrows - CONV_WIDTH, D), F32)

    row = pl.BlockSpec((tc, D), lambda i: (i, 0))
    nxt = pl.BlockSpec((CONV_HALO, D), lambda i: (jnp.minimum((i + 1) * hb, n_halo - 1), 0))
    prv = pl.BlockSpec((CONV_HALO, D), lambda i: (jnp.maximum(i * hb - 1, 0), 0))
    return _call(
        name, body, (S // tc,),
        [row, nxt, row, prv, pl.BlockSpec((CONV_WIDTH, D), lambda i: (0, 0)), row, row],
        [pl.BlockSpec((tc, 2 * D), lambda i: (i, 0)), pl.BlockSpec((1, 2 * D), lambda i: (0, 0)), pl.BlockSpec((wrows, D), lambda i: (0, 0))],
        [_sds((S, 2 * D), BF16), _sds((1, 2 * D), F32), _sds((wrows, D), F32)],
        (du2, du2, u1, u1, w_dw, a_pre, g_pre),
        scratch=[
            pltpu.VMEM((tc + CONV_HALO + SUBLANES, D), F32), pltpu.VMEM((tc + CONV_HALO + SUBLANES, D), F32),
            pltpu.VMEM((tc, D), F32), pltpu.VMEM((CONV_WIDTH, 8, D), F32),
            pltpu.VMEM((SUBLANES, tc + CONV_HALO, D), F32), pltpu.VMEM((SUBLANES, tc + CONV_HALO, D), F32),
        ],
    )


def _pool_out_bwd_call(name, dy, ypre, scale, wp):
    S, D = dy.shape
    ng, gc, _ = wp.shape
    tm = _tile(S, FULL_ROW_TILE)

    def body(dy_ref, y_ref, s_ref, w_ref, dm_ref, dyp_ref, ds_ref):
        dy = dy_ref[...]
        _accumulate(ds_ref, jnp.sum(dy * y_ref[...].astype(F32), axis=0, keepdims=True), pl.program_id(0) == 0)
        dyp = (dy * s_ref[...]).astype(BF16)
        dyp_ref[...] = dyp
        dm_ref[...] = jnp.concatenate([_dot(dyp[:, gi * gc:(gi + 1) * gc], w_ref[gi], "nt") for gi in range(ng)], axis=-1)

    row = pl.BlockSpec((tm, D), lambda i: (i, 0))
    vec = pl.BlockSpec((1, D), lambda i: (0, 0))
    return _call(
        name, body, (S // tm,), [row, row, vec, pl.BlockSpec((ng, gc, gc), lambda i: (0, 0, 0))],
        [row, row, vec], [_sds((S, D), F32), _sds((S, D), BF16), _sds((1, D), F32)], (dy, ypre, scale, wp),
    )


def _pool_w_grad_call(name, mix, dyp, ng):
    S, D = mix.shape
    gc = D // ng
    tk = _tile(S, K_TILE)
    nk = S // tk

    def body(m_ref, d_ref, o_ref, acc_ref):
        def finish():
            o_ref[...] = acc_ref[...]

        _kloop(pl.program_id(1), nk, [acc_ref], [_dot(m_ref[...], d_ref[...], "tn")], finish)

    blk = pl.BlockSpec((tk, gc), lambda g, k: (k, g))
    return _call(
        name, body, (ng, nk), [blk, blk], pl.BlockSpec((None, gc, gc), lambda g, k: (g, 0, 0)),
        _sds((ng, gc, gc), F32), (mix, dyp), scratch=[pltpu.VMEM((gc, gc), F32)],
    )


def _pool_mix_bwd_call(name, dmix, x, g, dres):
    S, D = x.shape
    ts = _tile(S, ELT_TILE)
    hb = ts // POOL_HALO
    n_halo = S // POOL_HALO
    gc = D // len(POOL_WINDOWS)

    def body(cur_ref, nxt_ref, x_ref, g_ref, r_ref, dx_ref, dxb_ref, dg_ref, ext_ref, dh_ref):
        i = pl.program_id(0)
        last = S // ts - 1
        t = i * ts + lax.broadcasted_iota(jnp.int32, (ts + POOL_HALO, 1), 0)
        for gi, win in enumerate(POOL_WINDOWS):
            lanes = slice(gi * gc, (gi + 1) * gc)
            cnt = jnp.minimum(t + 1, win).astype(F32)
            ext_ref[0:ts, lanes] = cur_ref[:, lanes] / cnt[0:ts]
            ext_ref[ts:, lanes] = jnp.where(i < last, nxt_ref[:, lanes] / cnt[ts:], 0.0)
        for gi, win in enumerate(POOL_WINDOWS):
            lanes = pl.ds(gi * gc, gc)
            acc = ext_ref[pl.ds(0, ts), lanes]
            for j in range(1, win):
                acc = acc + ext_ref[pl.ds(j, ts), lanes]
            dh_ref[:, gi * gc:(gi + 1) * gc] = acc - cur_ref[:, gi * gc:(gi + 1) * gc]
        d, dg = _rms_bwd(dh_ref[...], x_ref[...], g_ref[...])
        dx = r_ref[...] + d
        dx_ref[...] = dx
        dxb_ref[...] = dx.astype(BF16)
        _accumulate(dg_ref, dg, i == 0)

    row = pl.BlockSpec((ts, D), lambda i: (i, 0))
    vec = pl.BlockSpec((1, D), lambda i: (0, 0))
    nxt = pl.BlockSpec((POOL_HALO, D), lambda i: (jnp.minimum((i + 1) * hb, n_halo - 1), 0))
    return _call(
        name, body, (S // ts,), [row, nxt, row, vec, row], [row, row, vec],
        [_sds((S, D), F32), _sds((S, D), BF16), _sds((1, D), F32)], (dmix, dmix, x, g, dres),
        scratch=[pltpu.VMEM((ts + POOL_HALO, D), F32), pltpu.VMEM((ts, D), F32)],
    )


def _attn_bwd_call(name, q, k, v, do, qg, kg, bias, sinks):
    H, S, _ = q.shape
    n_kv = k.shape[0]
    grp = H // n_kv
    nb = S // Q_BLOCK
    scale = HEAD_DIM ** -0.5

    def body(q_ref, kp_ref, kc_ref, vp_ref, vc_ref, do_ref, qg_ref, kg_ref, bias_ref, sink_ref,
             dq_ref, dk_ref, dv_ref, dqg_ref, dkg_ref, dbias_ref, dsink_ref, ck_ref, cv_ref):
        n = pl.program_id(1)

        @pl.when(n == 0)
        def _():
            dqg_ref[...] = jnp.zeros_like(dqg_ref)
            dkg_ref[...] = jnp.zeros_like(dkg_ref)
            dbias_ref[...] = jnp.zeros_like(dbias_ref)
            dsink_ref[...] = jnp.zeros_like(dsink_ref)
            ck_ref[...] = jnp.zeros_like(ck_ref)
            cv_ref[...] = jnp.zeros_like(cv_ref)

        def block_grads(hh):
            hs = slice(hh * grp, (hh + 1) * grp)
            q, qn, _, kn, p, ps = _attn_probs(q_ref[hs], kp_ref[hh], kc_ref[hh], qg_ref[...], kg_ref[...], bias_ref[hs], sink_ref[hs], n)
            rows = grp * Q_BLOCK
            dob = do_ref[hs].reshape(rows, HEAD_DIM).astype(BF16)
            vv = jnp.concatenate([vp_ref[hh], vc_ref[hh]], axis=0).astype(BF16)
            dp = _dot(dob, vv, "nt").reshape(grp, Q_BLOCK, 2 * Q_BLOCK)
            delta = jnp.sum(p * dp, axis=-1, keepdims=True)
            dl = p * (dp - delta)
            dbias_ref[hs] += dl
            dsink_ref[hs] += jnp.sum(-ps * delta, axis=1, keepdims=True)
            dlb = dl.reshape(rows, 2 * Q_BLOCK).astype(BF16)
            dqn = (_dot(dlb, kn.astype(BF16), "nn") * scale).reshape(grp, Q_BLOCK, HEAD_DIM)
            dkn = _dot(dlb, qn.reshape(rows, HEAD_DIM).astype(BF16), "tn") * scale
            dvv = _dot(p.reshape(rows, 2 * Q_BLOCK).astype(BF16), dob, "tn")
            qgain = qg_ref[...]
            r = lax.rsqrt(jnp.mean(q * q, axis=-1, keepdims=True) + EPS)
            qh = q * r
            dqg_ref[hh] += jnp.sum(jnp.sum(dqn * qh, axis=1), axis=0, keepdims=True)
            dqh = dqn * qgain
            dq_ref[hs] = r * (dqh - qh * jnp.mean(dqh * qh, axis=-1, keepdims=True))
            return dkn, dvv

        def finish_prev(hh, dkn_prev, dv_prev):
            dk, dkg = _rms_bwd(dkn_prev, kp_ref[hh], kg_ref[...])
            dk_ref[hh] = dk
            dv_ref[hh] = dv_prev
            dkg_ref[hh] += dkg

        @pl.when(n < nb)
        def _():
            grads = [block_grads(hh) for hh in range(KV_STEP)]

            @pl.when(n > 0)
            def _():
                for hh, (dkn, dvv) in enumerate(grads):
                    finish_prev(hh, ck_ref[hh] + dkn[0:Q_BLOCK], cv_ref[hh] + dvv[0:Q_BLOCK])

            for hh, (dkn, dvv) in enumerate(grads):
                ck_ref[hh] = dkn[Q_BLOCK:]
                cv_ref[hh] = dvv[Q_BLOCK:]

        @pl.when(n == nb)
        def _():
            for hh in range(KV_STEP):
                finish_prev(hh, ck_ref[hh], cv_ref[hh])

    qspec, prev, cur, gain, bspec, sspec = _attn_specs(grp, nb)
    kout = pl.BlockSpec((KV_STEP, Q_BLOCK, HEAD_DIM), lambda j, n: (j, jnp.maximum(n - 1, 0), 0))
    gout = pl.BlockSpec((KV_STEP, 1, HEAD_DIM), lambda j, n: (j, 0, 0))
    return _call(
        name, body, (n_kv // KV_STEP, nb + 1),
        [qspec, prev, cur, prev, cur, qspec, gain, gain, bspec, sspec],
        [qspec, kout, kout, gout, gout, bspec, sspec],
        [
            _sds((H, S, HEAD_DIM), F32), _sds((n_kv, S, HEAD_DIM), F32), _sds((n_kv, S, HEAD_DIM), F32),
            _sds((n_kv, 1, HEAD_DIM), F32), _sds((n_kv, 1, HEAD_DIM), F32),
            _sds((H, Q_BLOCK, 2 * Q_BLOCK), F32), _sds((H, 1, 1), F32),
        ],
        (q, k, k, v, v, do, qg, kg, bias, sinks),
        scratch=[pltpu.VMEM((KV_STEP, Q_BLOCK, HEAD_DIM), F32), pltpu.VMEM((KV_STEP, Q_BLOCK, HEAD_DIM), F32)],
    )


def _bucket_sum_call(name, onehot, dbias):
    nbk, n = onehot.shape
    H = dbias.shape[0]

    def body(o_ref, d_ref, out_ref):
        out_ref[...] = lax.dot_general(o_ref[...], d_ref[...], _DIMS["nt"], precision=lax.Precision.HIGHEST, preferred_element_type=F32)

    return _call(
        name, body, (1,), [pl.BlockSpec((nbk, n), lambda i: (0, 0)), pl.BlockSpec((H, n), lambda i: (0, 0))],
        pl.BlockSpec((nbk, H), lambda i: (0, 0)), _sds((nbk, H), F32), (onehot, dbias),
    )


def _bias_table_call(name, rel_bias, onehot):
    nbk, n = onehot.shape
    H = rel_bias.shape[1]

    def body(r_ref, o_ref, out_ref):
        out_ref[...] = lax.dot_general(r_ref[...], o_ref[...], _DIMS["tn"], precision=lax.Precision.HIGHEST, preferred_element_type=F32)

    return _call(
        name, body, (1,), [pl.BlockSpec((nbk, H), lambda i: (0, 0)), pl.BlockSpec((nbk, n), lambda i: (0, 0))],
        pl.BlockSpec((H, n), lambda i: (0, 0)), _sds((H, n), F32), (rel_bias, onehot),
    )


def _exchange_call(name, comm):
    n_src, n_dst = len(comm.srcs), len(comm.out_shape)

    def body(*refs):
        src_refs, dst_refs, sem_refs = refs[:n_src], refs[n_src:n_src + n_dst], refs[n_src + n_dst:]
        comm.start(src_refs, dst_refs, sem_refs)
        comm.wait(src_refs, dst_refs, sem_refs)

    hbm = pl.BlockSpec(memory_space=pltpu.HBM)
    return pl.pallas_call(
        body, name=name, in_specs=[hbm] * n_src, out_specs=[hbm] * n_dst, out_shape=list(comm.out_shape),
        scratch_shapes=list(comm.sems), compiler_params=pltpu.CompilerParams(has_side_effects=True),
    )(*comm.srcs)


def _gather_forward_call(name, bufs):
    n = len(bufs)

    def body(*refs):
        in_refs, out_refs = refs[:n], refs[n:2 * n]
        send_sems, recv_sems = refs[2 * n:]
        x, y, c = _mesh_pos()
        sibling = (x, y, 1 - c)
        chips = [(1 - x, y), (x, 1 - y), (1 - x, 1 - y)]
        sends, recvs = [], []
        for t in range(n):
            for j, chip in enumerate(chips):
                s = 3 * t + j
                mine, theirs = _slot(*chip, c), _slot(*chip, 1 - c)
                sends.append(_remote(in_refs[t].at[mine], out_refs[t].at[mine], send_sems.at[s], recv_sems.at[s], sibling))
                recvs.append(_remote(in_refs[t].at[mine], out_refs[t].at[theirs], send_sems.at[s], recv_sems.at[s], sibling))
        for cp in sends:
            cp.start()
        for cp in recvs:
            cp.wait_recv()
        for cp in sends:
            cp.wait_send()

    hbm = pl.BlockSpec(memory_space=pltpu.HBM)
    return pl.pallas_call(
        body, name=name, in_specs=[hbm] * n, out_specs=[hbm] * n, out_shape=[_sds(b.shape, b.dtype) for b in bufs],
        scratch_shapes=[pltpu.SemaphoreType.DMA((3 * n,)), pltpu.SemaphoreType.DMA((3 * n,))],
        input_output_aliases={t: t for t in range(n)},
        compiler_params=pltpu.CompilerParams(has_side_effects=True),
    )(*bufs)


def _all_gather_call(name, shards):
    n = len(shards)

    def body(*refs):
        in_refs, out_refs = refs[:n], refs[n:2 * n]
        send_sems, recv_sems, local_sems = refs[2 * n:]
        x, y, c = _mesh_pos()
        me, sibling = (x, y, c), (x, y, 1 - c)
        chips = [(1 - x, y), (x, 1 - y), (1 - x, 1 - y)]

        def copy(t, k, block, to, src=None):
            dst = out_refs[t].at[_slot(*block)]
            return pltpu.make_async_remote_copy(
                src_ref=dst if src is None else src, dst_ref=dst,
                send_sem=send_sems.at[7 * t + k], recv_sem=recv_sems.at[7 * t + k],
                device_id=to, device_id_type=pl.DeviceIdType.MESH,
            )

        mine = [pltpu.make_async_copy(in_refs[t], out_refs[t].at[_slot(*me)], local_sems.at[t]) for t in range(n)]
        for cp in mine:
            cp.start()
        first = []
        for t in range(n):
            first.append(copy(t, 0, me, sibling, src=in_refs[t]))
            first += [copy(t, 1 + j, me, (*chip, c), src=in_refs[t]) for j, chip in enumerate(chips)]
        for cp in first:
            cp.start()
        passed = []
        for j, chip in enumerate(chips):
            for t in range(n):
                copy(t, 1 + j, (*chip, c), me).wait_recv()
                fwd = copy(t, 4 + j, (*chip, c), sibling)
                fwd.start()
                passed.append(fwd)
        for t in range(n):
            copy(t, 0, sibling, me).wait_recv()
            for j, chip in enumerate(chips):
                copy(t, 4 + j, (*chip, 1 - c), me).wait_recv()
        for cp in first + passed:
            cp.wait_send()
        for cp in mine:
            cp.wait()

    hbm = pl.BlockSpec(memory_space=pltpu.HBM)
    return pl.pallas_call(
        body, name=name,
        in_specs=[hbm] * n, out_specs=[hbm] * n,
        out_shape=[_sds((N_DEV, *s.shape), s.dtype) for s in shards],
        scratch_shapes=[pltpu.SemaphoreType.DMA((7 * n,)), pltpu.SemaphoreType.DMA((7 * n,)), pltpu.SemaphoreType.DMA((n,))],
        compiler_params=pltpu.CompilerParams(has_side_effects=True),
    )(*shards)


def _all_reduce_small_call(name, pack):
    R, C = pack.shape

    def body(in_ref, out_ref, land_ref, send_sems, recv_sems):
        x, y, c = _mesh_pos()
        me = _slot(x, y, c)
        peers = _peers(x, y, c)
        land_ref[me] = in_ref[...]
        sends = [
            pltpu.make_async_remote_copy(
                src_ref=in_ref, dst_ref=land_ref.at[me], send_sem=send_sems.at[k], recv_sem=recv_sems.at[k],
                device_id=to, device_id_type=pl.DeviceIdType.MESH,
            )
            for k, to in enumerate(peers)
        ]
        for cp in sends:
            cp.start()
        for k, frm in enumerate(peers):
            pltpu.make_async_remote_copy(
                src_ref=in_ref, dst_ref=land_ref.at[_slot(*frm)], send_sem=send_sems.at[k], recv_sem=recv_sems.at[k],
                device_id=frm, device_id_type=pl.DeviceIdType.MESH,
            ).wait_recv()
        for cp in sends:
            cp.wait_send()
        total = land_ref[0]
        for d in range(1, N_DEV):
            total = total + land_ref[d]
        out_ref[...] = total

    vmem = pl.BlockSpec(memory_space=pltpu.VMEM)
    return pl.pallas_call(
        body, name=name, in_specs=[vmem], out_specs=vmem, out_shape=_sds((R, C), F32),
        scratch_shapes=[pltpu.VMEM((N_DEV, R, C), F32), pltpu.SemaphoreType.DMA((7,)), pltpu.SemaphoreType.DMA((7,))],
        compiler_params=pltpu.CompilerParams(has_side_effects=True, vmem_limit_bytes=VMEM_LIMIT_BYTES),
    )(pack)


def _adamw_call(name, grad, landed, w, m, v, layer, prev):
    L, R, C = w.shape
    tr = _div_tile(R, ADAM_BLOCK_BYTES // (C * 4))
    c1 = 1.0 / (1.0 - ADAM_B1 ** ADAM_STEP)
    c2 = 1.0 / (1.0 - ADAM_B2 ** ADAM_STEP)

    def body(g_ref, w_ref, m_ref, v_ref, *rest):
        go_ref, d_ref, mo_ref, vo_ref = rest[-4:]
        if landed:
            g = g_ref[0].astype(F32)
            for d in range(1, N_DEV):
                g = g + g_ref[d].astype(F32)
        else:
            g = g_ref[...]
        go_ref[...] = g
        mn = ADAM_B1 * m_ref[...] + (1.0 - ADAM_B1) * g
        vn = ADAM_B2 * v_ref[...] + (1.0 - ADAM_B2) * (g * g)
        mo_ref[...] = mn
        vo_ref[...] = vn
        d_ref[...] = -ADAM_LR * ((mn * c1) / (jnp.sqrt(vn * c2) + ADAM_EPS) + ADAM_WD * w_ref[...])

    blk = pl.BlockSpec((None, tr, C), lambda i: (layer, i, 0))
    gspec = pl.BlockSpec((N_DEV, tr, C), lambda i: (0, i, 0)) if landed else pl.BlockSpec((tr, C), lambda i: (i, 0))
    in_specs, ins, aliases = [gspec, blk, blk, blk], [grad, w, m, v], {}
    if prev is not None:
        in_specs += [pl.BlockSpec(memory_space=pl.ANY)] * 4
        ins += list(prev)
        aliases = {4 + q: q for q in range(4)}
    return _call(name, body, (R // tr,), in_specs, [blk] * 4, [_sds((L, R, C), F32)] * 4, ins, aliases=aliases)


def _t5_bucket(rel):
    nb = NUM_BUCKETS // 2
    n = -rel
    ret = jnp.where(n < 0, nb, 0)
    n = jnp.abs(n)
    max_exact = nb // 2
    nf = jnp.maximum(n, 1).astype(jnp.float32)
    large = max_exact + (jnp.log(nf / max_exact) / math.log(REL_MAX_DIST / max_exact) * (nb - max_exact)).astype(jnp.int32)
    large = jnp.minimum(large, nb - 1)
    return ret + jnp.where(n < max_exact, n, large)


def _band_buckets():
    i = jnp.arange(Q_BLOCK)[:, None]
    j = jnp.arange(2 * Q_BLOCK)[None, :]
    return _t5_bucket(j - Q_BLOCK - i)


def _to_heads(t, n_heads):
    S = t.shape[0]
    return t.reshape(S, n_heads, HEAD_DIM).transpose(1, 0, 2)


def _from_heads(t):
    H, S, _ = t.shape
    return t.transpose(1, 0, 2).reshape(S, H * HEAD_DIM)


def _gathered_vec(t):
    nd, L, n = t.shape
    return t.transpose(1, 0, 2).reshape(L, nd * n)


_MIXER_WEIGHTS = {0: ("conv_w_in", "conv_w_out"), 1: (), 2: ("attn_w_qkv", "attn_w_o")}
_FFN_UP_WEIGHTS = ("ffn_w_gate", "ffn_w_up")
_FFN_REST_WEIGHTS = ("ffn_w_down", "ple_w_proj", "ple_w_gate")
_SMALL_SHARDED = ("conv_b_in", "conv_w_dw", "conv_b_dw", "conv_ln_g", "conv_ln_b", "conv_b_out")


def _mixer_keys(i):
    return [(n, i // 3) for n in _MIXER_WEIGHTS[i % 3]]


def _step(x, p, target, wb, small_pack, small_shapes, V):
    S, D = x.shape
    depth = V["norm_mix"].shape[0]
    n_heads = D // HEAD_DIM
    n_kv = (wb["attn_w_qkv"].shape[-1] * N_DEV - D) // (2 * HEAD_DIM)
    ng = len(POOL_WINDOWS)
    vec = lambda t, i: t[i][None, :]
    shard = lambda key: wb[key[0]][key[1]:key[1] + 1]

    keys_a0 = [(n, 0) for n in _FFN_UP_WEIGHTS]
    keys_b0 = [(n, 0) for n in _FFN_REST_WEIGHTS]
    first = _mixer_keys(0) + [("pool_w", 0)]
    gathered = _all_gather_call("all_gather0", [shard(k) for k in first] + [small_pack])
    W = dict(zip(first, gathered[:-1]))
    pw = W.pop(("pool_w", 0))
    wp = pw[:, 0].transpose(1, 0, 2, 3).reshape(pw.shape[2], pw.shape[3] * N_DEV, pw.shape[4])
    V = dict(V)
    for n, t in zip(_SMALL_SHARDED, _unpack(gathered[-1], small_shapes, lead=(N_DEV,))):
        if n == "conv_w_dw":
            V[n] = t.transpose(1, 2, 0, 3).reshape(t.shape[1], t.shape[2], -1)
        else:
            V[n] = _gathered_vec(t)

    buckets = _band_buckets()
    onehot = (buckets.reshape(1, -1) == jnp.arange(NUM_BUCKETS)[:, None]).astype(F32)
    bias_tab = _bias_table_call("bias_table", V["rel_bias"], onehot).reshape(n_heads, Q_BLOCK, 2 * Q_BLOCK)
    sinks3 = V["attn_sinks"].reshape(n_heads, 1, 1)
    pb = p.astype(BF16)

    def gather_next(keys):
        return _GatherOwn([shard(k) for k in keys]) if keys else None

    def finish_gather(name, keys, bufs):
        W.update(zip(keys, _gather_forward_call(name, bufs)))

    saved = []
    hb = _rms_call("rms_in", x, vec(V["norm_mix"], 0))
    for i in range(depth):
        kind, j = i % 3, i // 3
        sv = {"x0": x, "h0": hb}
        g_ffn = vec(V["norm_ffn"], i)
        if kind == 0:
            ride = i == 0
            res = _conv_in_call(f"conv_in{i}", hb, W["conv_w_in", j], 0, vec(V["conv_b_in"], j), comm=gather_next(keys_b0) if ride else None)
            if ride:
                res, bufs = res
                finish_gather("gather_fwd_b0", keys_b0, bufs)
            a_pre, g_pre, u1 = res
            res = _dwconv_call(
                f"dwconv{i}", u1, V["conv_w_dw"][j], vec(V["conv_b_dw"], j), vec(V["conv_ln_g"], j), vec(V["conv_ln_b"], j),
                comm=gather_next(keys_a0) if ride else None)
            if ride:
                res, bufs = res
                finish_gather("gather_fwd_a0", keys_a0, bufs)
            u2, u4 = res
            x1, h2 = _fullmm_res_call(f"conv_out{i}", u4, W["conv_w_out", j], 0, x, vec(V["conv_b_out"], j), g_ffn)
            sv.update(a_pre=a_pre, g_pre=g_pre, u1=u1, u2=u2, u4=u4)
        elif kind == 1:
            mix = _pool_mix_call(f"pool_mix{i}", x, vec(V["norm_mix"], i))
            x1, ypre, h2 = _pool_out_call(f"pool_out{i}", mix, wp, vec(V["pool_scale"], j), x, g_ffn)
            sv.update(mix=mix, ypre=ypre)
        else:
            qkv = _colmm_call(f"qkv{i}", hb, W["attn_w_qkv", j], 0, F32)
            q = _to_heads(qkv[:, :D], n_heads)
            k = _to_heads(qkv[:, D:D + n_kv * HEAD_DIM], n_kv)
            v = _to_heads(qkv[:, D + n_kv * HEAD_DIM:], n_kv)
            o = _attn_fwd_call(f"attn{i}", q, k, v, vec(V["attn_q_norm"], j), vec(V["attn_k_norm"], j), bias_tab, sinks3)
            ob = _from_heads(o)
            x1, h2 = _fullmm_res_call(f"attn_out{i}", ob, W["attn_w_o", j], 0, x, None, g_ffn)
            sv.update(q=q, k=k, v=v, ob=ob)
        more = i + 1 < depth
        keys_a = [(n, i + 1) for n in _FFN_UP_WEIGHTS] if more else []
        keys_b = [(n, i + 1) for n in _FFN_REST_WEIGHTS] + _mixer_keys(i + 1) if more else []
        res = _ffn_up_call(f"ffn_up{i}", h2, W["ffn_w_gate", i], W["ffn_w_up", i], 0, comm=gather_next(keys_a))
        if more:
            (gpre, upre, act), bufs = res
            finish_gather(f"gather_fwd_a{i + 1}", keys_a, bufs)
        else:
            gpre, upre, act = res
        res = _mm_rows_call(f"ffn_down{i}", act, W["ffn_w_down", i], 0, None, comm=gather_next(keys_b))
        if more:
            y, bufs = res
            finish_gather(f"gather_fwd_b{i + 1}", keys_b, bufs)
        else:
            y = res
        x2, h3 = _res_rms_call(f"ffn_res{i}", y, x1, vec(V["norm_ple"], i))
        pp = _colmm_call(f"ple_proj{i}", pb[i], W["ple_w_proj", i], 0, BF16)
        g_next = vec(V["norm_mix"], i + 1) if more else None
        x3, gate, hb_next = _ple_gate_call(f"ple_gate{i}", h3, W["ple_w_gate", i], 0, x2, vec(V["ple_b_gate"], i), pp, g_next)
        sv.update(x1=x1, h2=h2, gpre=gpre, upre=upre, act=act, x2=x2, h3=h3, pp=pp, gate=gate)
        saved.append(sv)
        x, hb = x3, hb_next

    dx, dxb, loss_tile = _loss_call("loss", x, target)
    loss = loss_tile[0, 0]

    landed = {}
    GV = {n: [None] * V[n].shape[0] for n in ("norm_mix", "norm_ffn", "norm_ple", "ple_b_gate", "conv_b_in", "conv_w_dw", "conv_b_dw", "conv_ln_g", "conv_ln_b", "conv_b_out", "pool_scale")}

    def scatter_of(pending):
        return _ScatterSlots([g for _, g in pending]) if pending else None

    def record(pending, bufs):
        landed.update(zip([k for k, _ in pending], bufs))

    mixer_pending = []
    for i in reversed(range(depth)):
        kind, j = i % 3, i // 3
        sv = saved[i]
        dz, dpp, db_gate = _ple_bwd_elt_call(f"ple_bwd{i}", dx, sv["gate"], sv["pp"])
        GV["ple_b_gate"][i] = db_gate
        pending = [(("ple_w_gate", i), _grad_w_call(f"g_ple_gate{i}", sv["h3"], "nat", dz, "full"))]
        pending.append((("ple_w_proj", i), _grad_w_call(f"g_ple_proj{i}", pb[i], "full", dpp, "nat")))
        dx, dxb, dg = _dx_full_rms_call(f"d_x2_{i}", dz, W["ple_w_gate", i], 0, sv["x2"], vec(V["norm_ple"], i), dx)
        GV["norm_ple"][i] = dg
        res = _ffn_bwd_hidden_call(f"ffn_bwd{i}", dxb, W["ffn_w_down", i], 0, sv["gpre"], sv["upre"], comm=scatter_of(mixer_pending))
        if mixer_pending:
            (dgp, dup), bufs = res
            record(mixer_pending, bufs)
        else:
            dgp, dup = res
        g_down = _grad_rows_call(f"g_ffn_down{i}", sv["act"], dxb)
        g_gate, bufs = _grad_rows_call(f"g_ffn_gate{i}", dgp, sv["h2"], comm=scatter_of(pending))
        record(pending, bufs)
        pending = [(("ffn_w_down", i), g_down)]
        g_up, bufs = _grad_rows_call(f"g_ffn_up{i}", dup, sv["h2"], comm=scatter_of(pending))
        record(pending, bufs)
        pending = [(("ffn_w_gate", i), g_gate)]
        dh, bufs = _mm_rows_call(f"d_h2g_{i}", dgp, W["ffn_w_gate", i], 0, None, comm=scatter_of(pending))
        record(pending, bufs)
        pending = [(("ffn_w_up", i), g_up)]
        dh, bufs = _mm_rows_call(f"d_h2_{i}", dup, W["ffn_w_up", i], 0, dh, comm=scatter_of(pending))
        record(pending, bufs)
        dx, dxb, dg, colsum = _rms_bwd_res_call(f"d_x1_{i}", dh, sv["x1"], vec(V["norm_ffn"], i), dx, kind == 0)
        GV["norm_ffn"][i] = dg
        g_mix = vec(V["norm_mix"], i)
        if kind == 0:
            GV["conv_b_out"][j] = colsum
            g_out = _grad_w_call(f"g_conv_out{i}", sv["u4"], "nat", dxb, "full")
            du2, d_ln_g, d_ln_b, d_b_dw = _conv_out_bwd_call(f"conv_out_bwd{i}", dxb, W["conv_w_out", j], 0, sv["u2"], vec(V["conv_ln_g"], j), vec(V["conv_ln_b"], j))
            du, d_b_in, d_w_dw = _dwconv_bwd_call(f"dwconv_bwd{i}", du2, sv["u1"], V["conv_w_dw"][j], sv["a_pre"], sv["g_pre"])
            GV["conv_ln_g"][j], GV["conv_ln_b"][j], GV["conv_b_dw"][j] = d_ln_g, d_ln_b, d_b_dw
            GV["conv_b_in"][j], GV["conv_w_dw"][j] = d_b_in, d_w_dw[:CONV_WIDTH]
            g_in = _grad_w_call(f"g_conv_in{i}", sv["h0"], "full", du, "nat")
            mixer_pending = [(("conv_w_in", j), g_in), (("conv_w_out", j), g_out)]
            res = _dx_colsharded_rms_call(
                f"d_x0_{i}", [du], False, [W["conv_w_in", j]], 0, sv["x0"], g_mix, dx, False, comm=scatter_of(mixer_pending) if i == 0 else None)
            if i == 0:
                res, bufs = res
                record(mixer_pending, bufs)
                mixer_pending = []
            dx, dxb, dg, _ = res
        elif kind == 1:
            dmix, dyp, d_scale = _pool_out_bwd_call(f"pool_out_bwd{i}", dx, sv["ypre"], vec(V["pool_scale"], j), wp)
            GV["pool_scale"][j] = d_scale
            g_pool = _pool_w_grad_call(f"g_pool_w{i}", sv["mix"], dyp, ng)
            gc = g_pool.shape[1]
            g_pool = g_pool.reshape(ng, N_DEV, gc // N_DEV, gc).transpose(1, 0, 2, 3).reshape(N_DEV, ng * (gc // N_DEV), gc)
            mixer_pending = [(("pool_w", j), g_pool.astype(BF16))]
            dx, dxb, dg = _pool_mix_bwd_call(f"pool_mix_bwd{i}", dmix, sv["x0"], g_mix, dx)
        else:
            g_o = _grad_w_call(f"g_attn_o{i}", sv["ob"], "nat", dxb, "full")
            do = _to_heads(_dx_full_call(f"d_attn_o{i}", dxb, W["attn_w_o", j], 0, F32), n_heads)
            dq, dk, dv, dqg, dkg, dbias, dsink = _attn_bwd_call(
                f"attn_bwd{i}", sv["q"], sv["k"], sv["v"], do, vec(V["attn_q_norm"], j), vec(V["attn_k_norm"], j), bias_tab, sinks3)
            GV["attn_q_norm"] = jnp.sum(dqg, axis=0)
            GV["attn_k_norm"] = jnp.sum(dkg, axis=0)
            GV["attn_sinks"] = dsink.reshape(1, n_heads)
            GV["rel_bias"] = _bucket_sum_call(f"g_rel_bias{i}", onehot, dbias.reshape(n_heads, -1))
            dqkv = jnp.concatenate([_from_heads(dq), _from_heads(dk), _from_heads(dv)], axis=-1).astype(BF16)
            g_qkv = _grad_w_call(f"g_qkv{i}", sv["h0"], "full", dqkv, "nat")
            mixer_pending = [(("attn_w_qkv", j), g_qkv), (("attn_w_o", j), g_o)]
            dx, dxb, dg, _ = _dx_colsharded_rms_call(f"d_x0_{i}", [dqkv], False, [W["attn_w_qkv", j]], 0, sv["x0"], g_mix, dx, False)
        GV["norm_mix"][i] = dg
    if mixer_pending:
        record(mixer_pending, _exchange_call("grad_scatter_tail", scatter_of(mixer_pending)))
    return loss, dx, landed, GV, V


_BIG = ("conv_w_in", "conv_w_out", "pool_w", "attn_w_qkv", "attn_w_o", "ffn_w_gate", "ffn_w_up", "ffn_w_down", "ple_w_proj", "ple_w_gate")
_SMALL_REPLICATED = ("norm_mix", "norm_ffn", "norm_ple", "pool_scale", "attn_q_norm", "attn_k_norm", "attn_sinks", "rel_bias", "ple_b_gate")
_WEIGHTS = ("norm_mix", "norm_ffn", "norm_ple", "conv_w_in", "conv_b_in", "conv_w_dw", "conv_b_dw", "conv_ln_g", "conv_ln_b", "conv_w_out",
            "conv_b_out", "pool_w", "pool_scale", "attn_w_qkv", "attn_q_norm", "attn_k_norm", "attn_sinks", "attn_w_o", "rel_bias",
            "ffn_w_gate", "ffn_w_up", "ffn_w_down", "ple_w_proj", "ple_w_gate", "ple_b_gate")
PACK_LANES = 128


def _pack(parts):
    flat = jnp.concatenate([t.reshape(-1).astype(F32) for t in parts])
    rows = -(-flat.shape[0] // (8 * PACK_LANES)) * 8
    flat = jnp.pad(flat, (0, rows * PACK_LANES - flat.shape[0]))
    return flat.reshape(rows, PACK_LANES)


def _unpack(pack, shapes, lead=()):
    flat = pack.reshape(*lead, -1)
    out, pos = [], 0
    for s in shapes:
        n = math.prod(s)
        out.append(flat[..., pos:pos + n].reshape(*lead, *s))
        pos += n
    return out


def _as2d(t):
    return t.reshape(-1, t.shape[-1])


def kernel(x, p, norm_mix, norm_ffn, norm_ple, conv_w_in, conv_b_in, conv_w_dw, conv_b_dw, conv_ln_g, conv_ln_b, conv_w_out, conv_b_out, pool_w, pool_scale, attn_w_qkv, attn_q_norm, attn_k_norm, attn_sinks, attn_w_o, rel_bias, ffn_w_gate, ffn_w_up, ffn_w_down, ple_w_proj, ple_w_gate, ple_b_gate, loss_target, m_norm_mix, m_norm_ffn, m_norm_ple, m_conv_w_in, m_conv_b_in, m_conv_w_dw, m_conv_b_dw, m_conv_ln_g, m_conv_ln_b, m_conv_w_out, m_conv_b_out, m_pool_w, m_pool_scale, m_attn_w_qkv, m_attn_q_norm, m_attn_k_norm, m_attn_sinks, m_attn_w_o, m_rel_bias, m_ffn_w_gate, m_ffn_w_up, m_ffn_w_down, m_ple_w_proj, m_ple_w_gate, m_ple_b_gate, v_norm_mix, v_norm_ffn, v_norm_ple, v_conv_w_in, v_conv_b_in, v_conv_w_dw, v_conv_b_dw, v_conv_ln_g, v_conv_ln_b, v_conv_w_out, v_conv_b_out, v_pool_w, v_pool_scale, v_attn_w_qkv, v_attn_q_norm, v_attn_k_norm, v_attn_sinks, v_attn_w_o, v_rel_bias, v_ffn_w_gate, v_ffn_w_up, v_ffn_w_down, v_ple_w_proj, v_ple_w_gate, v_ple_b_gate):
    given = dict(locals())
    w = {n: given[n] for n in _WEIGHTS}
    m = {n: given["m_" + n] for n in _WEIGHTS}
    v = {n: given["v_" + n] for n in _WEIGHTS}
    me = _slot(*_mesh_pos())

    wb = {n: w[n].astype(BF16) for n in _BIG}
    for n in _FFN_UP_WEIGHTS:
        wb[n] = wb[n].transpose(0, 2, 1)
    small_pack = _pack([w[n] for n in _SMALL_SHARDED])
    small_shapes = [w[n].shape for n in _SMALL_SHARDED]
    loss, grad_x, landed, GV, V = _step(x[0], p[:, 0], loss_target[0], wb, small_pack, small_shapes, {n: w[n] for n in _SMALL_REPLICATED})
    loss = lax.psum(loss, ("x", "y", "c"))

    small_names = list(_SMALL_REPLICATED) + list(_SMALL_SHARDED)
    small_full = []
    for n in small_names:
        g = GV[n]
        g = jnp.stack([t.reshape(V[n].shape[1:]) for t in g]) if isinstance(g, list) else g.reshape(V[n].shape)
        small_full.append(g)
    reduced = _unpack(_all_reduce_small_call("all_reduce_small", _pack(small_full)), [t.shape for t in small_full])
    small_grad = {}
    for n, g in zip(small_names, reduced):
        if n in _SMALL_SHARDED:
            c = w[n].shape[-1]
            g = lax.dynamic_slice_in_dim(g, me * c, c, axis=g.ndim - 1)
        small_grad[n] = g

    out = {}
    for n in _BIG:
        layers = w[n].shape[0]
        as3d = lambda t: t.reshape(layers, -1, t.shape[-1])
        res = None
        for l in range(layers):
            land = landed[n, l]
            if n in _FFN_UP_WEIGHTS:
                g = _sum_slots_call(f"gsum_{n}{l}", land).T
                res = _adamw_call(f"adamw_{n}{l}", g, False, as3d(w[n]), as3d(m[n]), as3d(v[n]), l, res)
            else:
                res = _adamw_call(f"adamw_{n}{l}", land, True, as3d(w[n]), as3d(m[n]), as3d(v[n]), l, res)
        out[n] = [t.reshape(w[n].shape) for t in res]
    for n in small_names:
        res = _adamw_call(f"adamw_{n}", _as2d(small_grad[n]), False, _as2d(w[n])[None], _as2d(m[n])[None], _as2d(v[n])[None], 0, None)
        out[n] = [t.reshape(w[n].shape) for t in res]

    grads = [out[n][0] for n in _WEIGHTS]
    deltas = [out[n][1] for n in _WEIGHTS]
    new_m = [out[n][2] for n in _WEIGHTS]
    new_v = [out[n][3] for n in _WEIGHTS]
    return (loss, grad_x[None], *grads, *deltas, *new_m, *new_v)
```

```python
import functools
import math

import jax
import jax.numpy as jnp
from jax import lax
from jax.experimental import pallas as pl
from jax.experimental.pallas import tpu as pltpu

F32, BF16 = jnp.float32, jnp.bfloat16
N_DEV = 8
EPS = 1e-6
NEG_INF = -1e30
HEAD_DIM = 64
Q_BLOCK = 128
CHUNK = 64
WINDOW_CHUNKS = 2
CONV_WIDTH = 31
CONV_HALO = 32
POOL_WINDOWS = (2, 4, 8, 16)
POOL_HALO = 16
NUM_BUCKETS = 32
REL_MAX_DIST = 128
ADAM_LR, ADAM_B1, ADAM_B2, ADAM_EPS, ADAM_WD, ADAM_STEP = 0.001, 0.9, 0.999, 1e-08, 0.01, 10
VMEM_LIMIT_BYTES = 44 * 1024 * 1024
BIG_VMEM_LIMIT_BYTES = 52 * 1024 * 1024
ROW_TILE = 512
FULL_ROW_TILE = 256
EPILOGUE_ROWS = 128
ELT_TILE = 256
CONV_TILE = 128
CONV_SUB = 32
CONV_LANES = 512
COL_TILE = 1024
GRAD_COL_TILE = 1024
K_TILE = 2048
ADAM_BLOCK_BYTES = 1 << 20

_DIMS = {
    "nn": (((1,), (0,)), ((), ())),
    "nt": (((1,), (1,)), ((), ())),
    "tn": (((0,), (0,)), ((), ())),
}


def _dot(a, b, mode):
    return lax.dot_general(a, b, _DIMS[mode], preferred_element_type=F32)


def _tile(n, t):
    t = min(n, t)
    assert n % t == 0, (n, t)
    return t


def _div_tile(n, t):
    if n <= t:
        return n
    for cand in range(t // 16 * 16, 15, -16):
        if n % cand == 0:
            return cand
    return n


def _sds(shape, dtype):
    return jax.ShapeDtypeStruct(tuple(shape), dtype)


def _call(name, body, grid, in_specs, out_specs, out_shape, ins, scratch=(), comm=None, vmem=VMEM_LIMIT_BYTES, aliases=None):
    params = pltpu.CompilerParams(dimension_semantics=("arbitrary",) * len(grid), vmem_limit_bytes=vmem)
    if comm is None:
        return pl.pallas_call(
            body, name=name, grid=grid, in_specs=list(in_specs), out_specs=out_specs, out_shape=out_shape,
            scratch_shapes=list(scratch), compiler_params=params, input_output_aliases=aliases or {},
        )(*ins)
    assert not aliases
    single = not isinstance(out_shape, (list, tuple))
    own_specs = [out_specs] if single else list(out_specs)
    own_shape = [out_shape] if single else list(out_shape)
    n_in, n_out, n_scr = len(ins), len(own_shape), len(scratch)
    n_src, n_dst = len(comm.srcs), len(comm.out_shape)
    hbm = pl.BlockSpec(memory_space=pltpu.HBM)

    def with_comm(*refs):
        a = n_in
        b = a + n_src
        c = b + n_out
        d = c + n_dst
        e = d + n_scr
        src_refs, dst_refs, sem_refs = refs[a:b], refs[c:d], refs[e:]
        first = functools.reduce(jnp.logical_and, [pl.program_id(i) == 0 for i in range(len(grid))])
        last = functools.reduce(jnp.logical_and, [pl.program_id(i) == grid[i] - 1 for i in range(len(grid))])

        @pl.when(first)
        def _():
            comm.start(src_refs, dst_refs, sem_refs)

        body(*refs[:a], *refs[b:c], *refs[d:e])

        @pl.when(last)
        def _():
            comm.wait(src_refs, dst_refs, sem_refs)

    res = pl.pallas_call(
        with_comm, name=name, grid=grid,
        in_specs=list(in_specs) + [hbm] * n_src,
        out_specs=own_specs + [hbm] * n_dst,
        out_shape=own_shape + list(comm.out_shape),
        scratch_shapes=list(scratch) + list(comm.sems),
        compiler_params=params,
    )(*ins, *comm.srcs)
    own = res[0] if single else list(res[:n_out])
    return own, list(res[n_out:])


def _mesh_pos():
    return lax.axis_index("x"), lax.axis_index("y"), lax.axis_index("c")


def _slot(px, py, pc):
    return 4 * px + 2 * py + pc


def _peers(x, y, c):
    flips = [(fx, fy, fc) for fx in (0, 1) for fy in (0, 1) for fc in (0, 1)][1:]
    return [(1 - x if fx else x, 1 - y if fy else y, 1 - c if fc else c) for fx, fy, fc in flips]


def _remote(src, dst, send_sem, recv_sem, to):
    return pltpu.make_async_remote_copy(
        src_ref=src, dst_ref=dst, send_sem=send_sem, recv_sem=recv_sem, device_id=to, device_id_type=pl.DeviceIdType.MESH
    )


class _GatherOwn:
    N_TO = 4

    def __init__(self, shards):
        n = len(shards)
        self.srcs = list(shards)
        self.out_shape = [_sds((N_DEV, *s.shape), s.dtype) for s in shards]
        self.sems = [pltpu.SemaphoreType.DMA((self.N_TO * n,)), pltpu.SemaphoreType.DMA((self.N_TO * n,)), pltpu.SemaphoreType.DMA((n,))]

    def _copies(self, src_refs, dst_refs, sem_refs):
        send_sems, recv_sems, local_sems = sem_refs
        x, y, c = _mesh_pos()
        me = (x, y, c)
        targets = [(x, y, 1 - c), (1 - x, y, c), (x, 1 - y, c), (1 - x, 1 - y, c)]
        sends, recvs, local = [], [], []
        for t, (src, dst) in enumerate(zip(src_refs, dst_refs)):
            local.append(pltpu.make_async_copy(src, dst.at[_slot(*me)], local_sems.at[t]))
            for k, to in enumerate(targets):
                s = self.N_TO * t + k
                sends.append(_remote(src, dst.at[_slot(*me)], send_sems.at[s], recv_sems.at[s], to))
                recvs.append(_remote(src, dst.at[_slot(*to)], send_sems.at[s], recv_sems.at[s], to))
        return sends, recvs, local

    def start(self, src_refs, dst_refs, sem_refs):
        sends, _, local = self._copies(src_refs, dst_refs, sem_refs)
        for cp in local + sends:
            cp.start()

    def wait(self, src_refs, dst_refs, sem_refs):
        sends, recvs, local = self._copies(src_refs, dst_refs, sem_refs)
        for cp in recvs:
            cp.wait_recv()
        for cp in sends:
            cp.wait_send()
        for cp in local:
            cp.wait()


class _ScatterSlots:
    def __init__(self, grads):
        n = len(grads)
        self.srcs = list(grads)
        self.out_shape = [_sds(g.shape, g.dtype) for g in grads]
        self.sems = [pltpu.SemaphoreType.DMA((7 * n,)), pltpu.SemaphoreType.DMA((7 * n,)), pltpu.SemaphoreType.DMA((n,))]

    def _copies(self, src_refs, dst_refs, sem_refs):
        send_sems, recv_sems, local_sems = sem_refs
        x, y, c = _mesh_pos()
        me = _slot(x, y, c)
        sends, recvs, local = [], [], []
        for t, (src, dst) in enumerate(zip(src_refs, dst_refs)):
            local.append(pltpu.make_async_copy(src.at[me], dst.at[me], local_sems.at[t]))
            for k, to in enumerate(_peers(x, y, c)):
                s = 7 * t + k
                sends.append(_remote(src.at[_slot(*to)], dst.at[me], send_sems.at[s], recv_sems.at[s], to))
                recvs.append(_remote(src.at[me], dst.at[_slot(*to)], send_sems.at[s], recv_sems.at[s], to))
        return sends, recvs, local

    start = _GatherOwn.start
    wait = _GatherOwn.wait


def _sig(x):
    return 1.0 / (1.0 + jnp.exp(-x))


def _rms(x, g):
    r = lax.rsqrt(jnp.mean(x * x, axis=-1, keepdims=True) + EPS)
    return x * r * g


def _rms_bwd(dy, x, g):
    r = lax.rsqrt(jnp.mean(x * x, axis=-1, keepdims=True) + EPS)
    xh = x * r
    dg = jnp.sum(dy * xh, axis=0, keepdims=True)
    dxh = dy * g
    dx = r * (dxh - xh * jnp.mean(dxh * xh, axis=-1, keepdims=True))
    return dx, dg


def _accumulate(ref, val, first):
    @pl.when(first)
    def _():
        ref[...] = val

    @pl.when(jnp.logical_not(first))
    def _():
        ref[...] += val


def _kloop(k, nk, acc_refs, contribs, finish):
    @pl.when(k == 0)
    def _():
        for r, c in zip(acc_refs, contribs):
            r[...] = c

    @pl.when(k > 0)
    def _():
        for r, c in zip(acc_refs, contribs):
            r[...] += c

    @pl.when(k == nk - 1)
    def _():
        finish()


def _rms_call(name, x, g):
    S, D = x.shape
    ts = _tile(S, ELT_TILE)

    def body(x_ref, g_ref, o_ref):
        o_ref[...] = _rms(x_ref[...], g_ref[...]).astype(BF16)

    return _call(
        name, body, (S // ts,),
        [pl.BlockSpec((ts, D), lambda i: (i, 0)), pl.BlockSpec((1, D), lambda i: (0, 0))],
        pl.BlockSpec((ts, D), lambda i: (i, 0)), _sds((S, D), BF16), (x, g),
    )


def _conv_in_call(name, hb, w, layer, b_in, comm=None):
    S, D = hb.shape
    nsh = w.shape[-1]
    half = N_DEV // 2
    assert nsh * half == D
    tm = _tile(S, ROW_TILE)

    def body(h_ref, wa_ref, wg_ref, ba_ref, bg_ref, a_ref, g_ref, u_ref):
        h = h_ref[...]
        a = _dot(h, wa_ref[...], "nn") + ba_ref[...]
        g = _dot(h, wg_ref[...], "nn") + bg_ref[...]
        a_ref[...] = a.astype(BF16)
        g_ref[...] = g.astype(BF16)
        u_ref[...] = a * _sig(g)

    out_spec = pl.BlockSpec((tm, nsh), lambda d, i: (i, d))
    return _call(
        name, body, (half, S // tm),
        [
            pl.BlockSpec((tm, D), lambda d, i: (i, 0)),
            pl.BlockSpec((None, None, D, nsh), lambda d, i: (d, layer, 0, 0)),
            pl.BlockSpec((None, None, D, nsh), lambda d, i: (d + half, layer, 0, 0)),
            pl.BlockSpec((1, nsh), lambda d, i: (0, d)),
            pl.BlockSpec((1, nsh), lambda d, i: (0, d + half)),
        ],
        [out_spec, out_spec, out_spec],
        [_sds((S, D), BF16), _sds((S, D), BF16), _sds((S, D), F32)],
        (hb, w, w, b_in, b_in), comm=comm,
    )


SUBLANES = 8


def _fill_shifted(sh_ref, ext_ref, rows):
    ext_ref[rows:, :] = jnp.zeros((SUBLANES, ext_ref.shape[1]), F32)
    for s in range(SUBLANES):
        sh_ref[s] = ext_ref[pl.ds(s, rows), :]


def _shifted_window(sh_ref, row, lane0):
    s = row % SUBLANES
    return sh_ref[s, pl.ds(row - s, CONV_SUB), pl.ds(lane0, CONV_LANES)]


def _conv_taps(sh_ref, w_ref, row0, lane0, offset, reverse):
    acc = None
    for k in range(CONV_WIDTH):
        off = offset - k if reverse else offset + k
        term = w_ref[k:k + 1, lane0:lane0 + CONV_LANES] * _shifted_window(sh_ref, row0 + off, lane0)
        acc = term if acc is None else acc + term
    return acc


def _dwconv_call(name, u1, w_dw, b_dw, ln_g, ln_b, comm=None):
    S, D = u1.shape
    tc = _tile(S, CONV_TILE)
    hb = tc // CONV_HALO
    lanes = min(D, CONV_LANES)
    assert lanes == CONV_LANES and D % CONV_LANES == 0 and tc % CONV_SUB == 0

    def body(cur_ref, halo_ref, w_ref, b_ref, g_ref, bb_ref, u2_ref, u4_ref, ext_ref, sh_ref):
        i = pl.program_id(0)
        ext_ref[0:CONV_HALO, :] = jnp.where(i > 0, halo_ref[...], 0.0)
        ext_ref[CONV_HALO:tc + CONV_HALO, :] = cur_ref[...]
        _fill_shifted(sh_ref, ext_ref, tc + CONV_HALO)
        first_tap = CONV_HALO - (CONV_WIDTH - 1)
        for r in range(0, tc, CONV_SUB):
            for c in range(0, D, CONV_LANES):
                u2_ref[r:r + CONV_SUB, c:c + CONV_LANES] = (
                    _conv_taps(sh_ref, w_ref, r, c, first_tap, False) + b_ref[:, c:c + CONV_LANES]
                )
        u2 = u2_ref[...]
        mu = jnp.mean(u2, axis=-1, keepdims=True)
        xc = u2 - mu
        u3 = xc * lax.rsqrt(jnp.mean(xc * xc, axis=-1, keepdims=True) + EPS) * g_ref[...] + bb_ref[...]
        u4_ref[...] = (u3 * _sig(u3)).astype(BF16)

    vec = pl.BlockSpec((1, D), lambda i: (0, 0))
    row = pl.BlockSpec((tc, D), lambda i: (i, 0))
    return _call(
        name, body, (S // tc,),
        [
            row,
            pl.BlockSpec((CONV_HALO, D), lambda i: (jnp.maximum(i * hb - 1, 0), 0)),
            pl.BlockSpec((CONV_WIDTH, D), lambda i: (0, 0)),
            vec, vec, vec,
        ],
        [row, row],
        [_sds((S, D), F32), _sds((S, D), BF16)],
        (u1, u1, w_dw, b_dw, ln_g, ln_b),
        scratch=[pltpu.VMEM((tc + CONV_HALO + SUBLANES, D), F32), pltpu.VMEM((SUBLANES, tc + CONV_HALO, D), F32)], comm=comm,
    )


def _row_chunks(tm):
    step = min(tm, EPILOGUE_ROWS)
    return [slice(r, r + step) for r in range(0, tm, step)]


def _full_weight_spec(w, layer):
    _, _, ksh, D = w.shape
    return pl.BlockSpec((N_DEV, None, ksh, D), lambda i: (0, layer, 0, 0), pipeline_mode=pl.Buffered(1))


def _fullmm_res_call(name, a, w, layer, x_res, bias, g_next):
    S, D = x_res.shape
    K = a.shape[1]
    tm = _tile(S, FULL_ROW_TILE)
    has_b = bias is not None

    def body(*refs):
        a_ref, w_ref, x_ref = refs[:3]
        b_ref = refs[3] if has_b else None
        g_ref, xo_ref, ho_ref = refs[3 + int(has_b):]
        xn = x_ref[...] + _dot(a_ref[...], w_ref[...].reshape(K, D), "nn")
        if has_b:
            xn = xn + b_ref[...]
        xo_ref[...] = xn
        ho_ref[...] = _rms(xn, g_ref[...]).astype(BF16)

    row = pl.BlockSpec((tm, D), lambda i: (i, 0))
    vec = pl.BlockSpec((1, D), lambda i: (0, 0))
    in_specs = [pl.BlockSpec((tm, K), lambda i: (i, 0)), _full_weight_spec(w, layer), row] + ([vec] if has_b else []) + [vec]
    ins = [a, w, x_res] + ([bias] if has_b else []) + [g_next]
    return _call(name, body, (S // tm,), in_specs, [row, row], [_sds((S, D), F32), _sds((S, D), BF16)], ins)


SHARD_PAIR = 2


def _pair_spec(w, layer):
    _, _, fsh, D = w.shape
    return pl.BlockSpec((SHARD_PAIR, None, fsh, D), lambda d, i: (d, layer, 0, 0))


def _ffn_up_call(name, hb, wgt, wut, layer, comm=None):
    S, D = hb.shape
    fsh = wgt.shape[2]
    tn = SHARD_PAIR * fsh
    tm = _tile(S, ROW_TILE)

    def body(h_ref, wg_ref, wu_ref, g_ref, u_ref, a_ref):
        h = h_ref[...]
        g = _dot(h, wg_ref[...].reshape(tn, D), "nt")
        u = _dot(h, wu_ref[...].reshape(tn, D), "nt")
        g_ref[...] = g.astype(BF16)
        u_ref[...] = u.astype(BF16)
        a_ref[...] = (g * _sig(g) * u).astype(BF16)

    ospec = pl.BlockSpec((tm, tn), lambda d, i: (i, d))
    return _call(
        name, body, (N_DEV // SHARD_PAIR, S // tm),
        [pl.BlockSpec((tm, D), lambda d, i: (i, 0)), _pair_spec(wgt, layer), _pair_spec(wut, layer)],
        [ospec] * 3, [_sds((S, N_DEV * fsh), BF16)] * 3, (hb, wgt, wut), comm=comm,
    )


def _mm_rows_call(name, a, w, layer, add, comm=None):
    S, K = a.shape
    _, _, ksh, N = w.shape
    assert K == N_DEV * ksh
    tm = _tile(S, ROW_TILE)
    tn = _tile(N, COL_TILE)
    has_add = add is not None

    def body(*refs):
        a_ref, w_ref = refs[:2]
        o_ref = refs[-1]
        y = _dot(a_ref[...], w_ref[...].reshape(K, tn), "nn")
        o_ref[...] = y + refs[2][...] if has_add else y

    tile = pl.BlockSpec((tm, tn), lambda i, n: (i, n))
    in_specs = [pl.BlockSpec((tm, K), lambda i, n: (i, 0)), pl.BlockSpec((N_DEV, None, ksh, tn), lambda i, n: (0, layer, 0, n))]
    return _call(
        name, body, (S // tm, N // tn), in_specs + ([tile] if has_add else []), tile, _sds((S, N), F32),
        (a, w) + ((add,) if has_add else ()), comm=comm, vmem=BIG_VMEM_LIMIT_BYTES,
    )


def _res_rms_call(name, y, x_res, g_next):
    S, D = x_res.shape
    ts = _tile(S, ELT_TILE)

    def body(y_ref, x_ref, g_ref, xo_ref, ho_ref):
        xn = x_ref[...] + y_ref[...]
        xo_ref[...] = xn
        ho_ref[...] = _rms(xn, g_ref[...]).astype(BF16)

    row = pl.BlockSpec((ts, D), lambda i: (i, 0))
    return _call(
        name, body, (S // ts,), [row, row, pl.BlockSpec((1, D), lambda i: (0, 0))],
        [row, row], [_sds((S, D), F32), _sds((S, D), BF16)], (y, x_res, g_next),
    )


def _colmm_call(name, a, w, layer, out_dtype):
    S, K = a.shape
    nsh = w.shape[-1]
    tm = _tile(S, ROW_TILE)

    def body(a_ref, w_ref, o_ref):
        o_ref[...] = _dot(a_ref[...], w_ref[...], "nn").astype(out_dtype)

    return _call(
        name, body, (N_DEV, S // tm),
        [pl.BlockSpec((tm, K), lambda d, i: (i, 0)), pl.BlockSpec((None, None, K, nsh), lambda d, i: (d, layer, 0, 0))],
        pl.BlockSpec((tm, nsh), lambda d, i: (i, d)), _sds((S, N_DEV * nsh), out_dtype), (a, w),
    )


def _ple_gate_call(name, hb, w, layer, x_res, bias, pp, g_next):
    S, D = x_res.shape
    K = hb.shape[1]
    tm = _tile(S, FULL_ROW_TILE)
    has_g = g_next is not None

    def body(*refs):
        a_ref, w_ref, x_ref, b_ref, p_ref = refs[:5]
        g_ref = refs[5] if has_g else None
        outs = refs[5 + int(has_g):]
        gate = _sig(_dot(a_ref[...], w_ref[...].reshape(K, D), "nn") + b_ref[...])
        xn = x_ref[...] + gate * p_ref[...].astype(F32)
        outs[0][...] = xn
        outs[1][...] = gate.astype(BF16)
        if has_g:
            outs[2][...] = _rms(xn, g_ref[...]).astype(BF16)

    row = pl.BlockSpec((tm, D), lambda i: (i, 0))
    vec = pl.BlockSpec((1, D), lambda i: (0, 0))
    in_specs = [pl.BlockSpec((tm, K), lambda i: (i, 0)), _full_weight_spec(w, layer), row, vec, row]
    ins = [hb, w, x_res, bias, pp]
    out_specs, out_shape = [row, row], [_sds((S, D), F32), _sds((S, D), BF16)]
    if has_g:
        in_specs.append(vec)
        ins.append(g_next)
        out_specs.append(row)
        out_shape.append(_sds((S, D), BF16))
    res = _call(name, body, (S // tm,), in_specs, out_specs, out_shape, ins)
    return res if has_g else (res[0], res[1], None)


def _pool_mix_call(name, x, g):
    S, D = x.shape
    ts = _tile(S, ELT_TILE)
    hb = ts // POOL_HALO
    gc = D // len(POOL_WINDOWS)

    def body(cur_ref, halo_ref, g_ref, o_ref, ext_ref):
        i = pl.program_id(0)
        gain = g_ref[...]
        ext_ref[0:POOL_HALO, :] = jnp.where(i > 0, _rms(halo_ref[...], gain), 0.0)
        ext_ref[POOL_HALO:, :] = _rms(cur_ref[...], gain)
        t = i * ts + lax.broadcasted_iota(jnp.int32, (ts, 1), 0)
        for gi, win in enumerate(POOL_WINDOWS):
            lanes = pl.ds(gi * gc, gc)
            h = ext_ref[pl.ds(POOL_HALO, ts), lanes]
            acc = h
            for j in range(1, win):
                acc = acc + ext_ref[pl.ds(POOL_HALO - j, ts), lanes]
            cnt = jnp.minimum(t + 1, win).astype(F32)
            o_ref[:, gi * gc:(gi + 1) * gc] = (acc / cnt - h).astype(BF16)

    row = pl.BlockSpec((ts, D), lambda i: (i, 0))
    return _call(
        name, body, (S // ts,),
        [row, pl.BlockSpec((POOL_HALO, D), lambda i: (jnp.maximum(i * hb - 1, 0), 0)), pl.BlockSpec((1, D), lambda i: (0, 0))],
        row, _sds((S, D), BF16), (x, x, g), scratch=[pltpu.VMEM((ts + POOL_HALO, D), F32)],
    )


def _pool_out_call(name, mix, wp, scale, x_res, g_next):
    S, D = x_res.shape
    ng, gc, _ = wp.shape
    tm = _tile(S, FULL_ROW_TILE)

    def body(m_ref, w_ref, s_ref, x_ref, g_ref, xo_ref, y_ref, h_ref):
        parts = [_dot(m_ref[:, gi * gc:(gi + 1) * gc], w_ref[gi], "nn") for gi in range(ng)]
        ypre = jnp.concatenate(parts, axis=-1)
        y_ref[...] = ypre.astype(BF16)
        xn = x_ref[...] + ypre * s_ref[...]
        xo_ref[...] = xn
        h_ref[...] = _rms(xn, g_ref[...]).astype(BF16)

    row = pl.BlockSpec((tm, D), lambda i: (i, 0))
    vec = pl.BlockSpec((1, D), lambda i: (0, 0))
    return _call(
        name, body, (S // tm,),
        [row, pl.BlockSpec((ng, gc, gc), lambda i: (0, 0, 0)), vec, row, vec],
        [row, row, row], [_sds((S, D), F32), _sds((S, D), BF16), _sds((S, D), BF16)],
        (mix, wp, scale, x_res, g_next),
    )


KV_STEP = 2


def _attn_probs(q, kprev, kcur, qg, kg, bias, sink, n):
    grp = q.shape[0]
    qn = _rms(q, qg)
    k = jnp.concatenate([kprev, kcur], axis=0)
    kn = _rms(k, kg)
    s = _dot(qn.reshape(grp * Q_BLOCK, HEAD_DIM).astype(BF16), kn.astype(BF16), "nt") * (HEAD_DIM ** -0.5)
    s = s.reshape(grp, Q_BLOCK, 2 * Q_BLOCK) + bias
    qi = lax.broadcasted_iota(jnp.int32, (Q_BLOCK, 2 * Q_BLOCK), 0)
    kj = lax.broadcasted_iota(jnp.int32, (Q_BLOCK, 2 * Q_BLOCK), 1)
    qc = qi // CHUNK
    kc = kj // CHUNK - Q_BLOCK // CHUNK
    ok = (kc <= qc) & (kc >= qc - WINDOW_CHUNKS) & ((n > 0) | (kj >= Q_BLOCK))
    s = jnp.where(ok[None], s, NEG_INF)
    m = jnp.maximum(jnp.max(s, axis=-1, keepdims=True), sink)
    e = jnp.exp(s - m)
    es = jnp.exp(sink - m)
    inv = 1.0 / (jnp.sum(e, axis=-1, keepdims=True) + es)
    return q, qn, k, kn, e * inv, es * inv


def _attn_specs(grp, nb):
    heads = KV_STEP * grp
    qspec = pl.BlockSpec((heads, Q_BLOCK, HEAD_DIM), lambda j, n: (j, jnp.minimum(n, nb - 1), 0))
    prev = pl.BlockSpec((KV_STEP, Q_BLOCK, HEAD_DIM), lambda j, n: (j, jnp.maximum(n - 1, 0), 0))
    cur = pl.BlockSpec((KV_STEP, Q_BLOCK, HEAD_DIM), lambda j, n: (j, jnp.minimum(n, nb - 1), 0))
    gain = pl.BlockSpec((1, HEAD_DIM), lambda j, n: (0, 0))
    bias = pl.BlockSpec((heads, Q_BLOCK, 2 * Q_BLOCK), lambda j, n: (j, 0, 0))
    sink = pl.BlockSpec((heads, 1, 1), lambda j, n: (j, 0, 0))
    return qspec, prev, cur, gain, bias, sink


def _attn_fwd_call(name, q, k, v, qg, kg, bias, sinks):
    H, S, _ = q.shape
    n_kv = k.shape[0]
    grp = H // n_kv
    nb = S // Q_BLOCK
    assert n_kv % KV_STEP == 0

    def body(q_ref, kp_ref, kc_ref, vp_ref, vc_ref, qg_ref, kg_ref, bias_ref, sink_ref, o_ref):
        n = pl.program_id(1)
        for hh in range(KV_STEP):
            hs = slice(hh * grp, (hh + 1) * grp)
            _, _, _, _, p, _ = _attn_probs(q_ref[hs], kp_ref[hh], kc_ref[hh], qg_ref[...], kg_ref[...], bias_ref[hs], sink_ref[hs], n)
            vv = jnp.concatenate([vp_ref[hh], vc_ref[hh]], axis=0).astype(BF16)
            o = _dot(p.reshape(grp * Q_BLOCK, 2 * Q_BLOCK).astype(BF16), vv, "nn")
            o_ref[hs] = o.reshape(grp, Q_BLOCK, HEAD_DIM).astype(BF16)

    qspec, prev, cur, gain, bspec, sspec = _attn_specs(grp, nb)
    return _call(
        name, body, (n_kv // KV_STEP, nb),
        [qspec, prev, cur, prev, cur, gain, gain, bspec, sspec],
        qspec, _sds((H, S, HEAD_DIM), BF16), (q, k, k, v, v, qg, kg, bias, sinks),
    )


def _loss_call(name, y, target):
    S, D = y.shape
    ts = _tile(S, ELT_TILE)

    def body(y_ref, t_ref, d_ref, db_ref, l_ref):
        err = y_ref[...] - t_ref[...]
        dy = err * (1.0 / D)
        d_ref[...] = dy
        db_ref[...] = dy.astype(BF16)
        part = 0.5 * jnp.sum(jnp.sum(err * err, axis=-1, keepdims=True), axis=0, keepdims=True) * (1.0 / D)
        _accumulate(l_ref, jnp.broadcast_to(part, l_ref.shape), pl.program_id(0) == 0)

    row = pl.BlockSpec((ts, D), lambda i: (i, 0))
    return _call(
        name, body, (S // ts,), [row, row],
        [row, row, pl.BlockSpec((8, 128), lambda i: (0, 0))],
        [_sds((S, D), F32), _sds((S, D), BF16), _sds((8, 128), F32)], (y, target),
    )


def _ple_bwd_elt_call(name, dx, gate, pp):
    S, D = dx.shape
    ts = _tile(S, ELT_TILE)

    def body(dx_ref, gt_ref, p_ref, dz_ref, dp_ref, db_ref):
        d = dx_ref[...]
        gt = gt_ref[...].astype(F32)
        dz = d * p_ref[...].astype(F32) * gt * (1.0 - gt)
        dz_ref[...] = dz.astype(BF16)
        dp_ref[...] = (d * gt).astype(BF16)
        _accumulate(db_ref, jnp.sum(dz, axis=0, keepdims=True), pl.program_id(0) == 0)

    row = pl.BlockSpec((ts, D), lambda i: (i, 0))
    return _call(
        name, body, (S // ts,), [row, row, row],
        [row, row, pl.BlockSpec((1, D), lambda i: (0, 0))],
        [_sds((S, D), BF16), _sds((S, D), BF16), _sds((1, D), F32)], (dx, gate, pp),
    )


def _grad_w_call(name, a, a_mode, b, b_mode, comm=None):
    bs = b if isinstance(b, (list, tuple)) else [b]
    S = a.shape[-2]
    tk = _tile(S, K_TILE)
    nk = S // tk

    def spec(arr, mode):
        if mode == "full":
            c = arr.shape[-1]
            return pl.BlockSpec((tk, c), lambda d, k: (k, 0)), c
        if mode == "nat":
            c = arr.shape[-1] // N_DEV
            return pl.BlockSpec((tk, c), lambda d, k: (k, d)), c
        c = arr.shape[-1]
        return pl.BlockSpec((None, tk, c), lambda d, k: (d, k, 0)), c

    a_spec, ca = spec(a, a_mode)
    b_specs, cbs = zip(*[spec(x, b_mode) for x in bs])
    nb = len(bs)

    def body(*refs):
        a_ref = refs[0]
        b_refs = refs[1:1 + nb]
        o_refs = refs[1 + nb:1 + 2 * nb]
        acc_refs = refs[1 + 2 * nb:]
        k = pl.program_id(1)
        av = a_ref[...]

        def finish():
            for o, acc in zip(o_refs, acc_refs):
                o[...] = acc[...].astype(BF16)

        _kloop(k, nk, acc_refs, [_dot(av, br[...], "tn") for br in b_refs], finish)

    res = _call(
        name, body, (N_DEV, nk), [a_spec, *b_specs],
        [pl.BlockSpec((None, ca, cb), lambda d, k: (d, 0, 0)) for cb in cbs],
        [_sds((N_DEV, ca, cb), BF16) for cb in cbs], (a, *bs),
        scratch=[pltpu.VMEM((ca, cb), F32) for cb in cbs], comm=comm,
    )
    own, landed = res if comm is not None else (res, None)
    own = own if isinstance(b, (list, tuple)) else own[0]
    return own if comm is None else (own, landed)


def _dx_full_call(name, dy, w, layer, out_dtype):
    S, D = dy.shape
    K = N_DEV * w.shape[2]
    tm = _tile(S, ROW_TILE)

    def body(dy_ref, w_ref, o_ref):
        o_ref[...] = _dot(dy_ref[...], w_ref[...].reshape(K, D), "nt").astype(out_dtype)

    return _call(
        name, body, (S // tm,), [pl.BlockSpec((tm, D), lambda i: (i, 0)), _full_weight_spec(w, layer)],
        pl.BlockSpec((tm, K), lambda i: (i, 0)), _sds((S, K), out_dtype), (dy, w),
    )


def _dx_full_rms_call(name, dy, w, layer, x, g, dres):
    S, D = x.shape
    K = N_DEV * w.shape[2]
    assert K == D
    tm = _tile(S, FULL_ROW_TILE)

    def body(dy_ref, w_ref, x_ref, g_ref, r_ref, dx_ref, dxb_ref, dg_ref):
        dh, dg = _rms_bwd(_dot(dy_ref[...], w_ref[...].reshape(K, dy_ref.shape[1]), "nt"), x_ref[...], g_ref[...])
        dx = r_ref[...] + dh
        dx_ref[...] = dx
        dxb_ref[...] = dx.astype(BF16)
        _accumulate(dg_ref, dg, pl.program_id(0) == 0)

    row = pl.BlockSpec((tm, D), lambda i: (i, 0))
    vec = pl.BlockSpec((1, D), lambda i: (0, 0))
    return _call(
        name, body, (S // tm,), [pl.BlockSpec((tm, dy.shape[1]), lambda i: (i, 0)), _full_weight_spec(w, layer), row, vec, row],
        [row, row, vec], [_sds((S, D), F32), _sds((S, D), BF16), _sds((1, D), F32)], (dy, w, x, g, dres),
    )


def _dx_colsharded_rms_call(name, dys, dy_chunked, ws, layer, x, g, dres, want_colsum, comm=None):
    S, D = x.shape
    nsh = ws[0].shape[-1]
    tm = _tile(S, ROW_TILE)
    nt = len(dys)

    def body(*refs):
        dy_refs = refs[:nt]
        w_refs = refs[nt:2 * nt]
        x_ref, g_ref, r_ref = refs[2 * nt:2 * nt + 3]
        outs = refs[2 * nt + 3:]
        acc_ref = outs[0]
        i, k = pl.program_id(0), pl.program_id(1)
        contrib = None
        for dr, wr in zip(dy_refs, w_refs):
            c = _dot(dr[...], wr[...], "nt")
            contrib = c if contrib is None else contrib + c

        def finish():
            dg = colsum = None
            for rows in _row_chunks(tm):
                dh, dg_part = _rms_bwd(acc_ref[rows, :], x_ref[rows, :], g_ref[...])
                dx = r_ref[rows, :] + dh
                outs[0][rows, :] = dx
                outs[1][rows, :] = dx.astype(BF16)
                dg = dg_part if dg is None else dg + dg_part
                if want_colsum:
                    part = jnp.sum(dx, axis=0, keepdims=True)
                    colsum = part if colsum is None else colsum + part
            _accumulate(outs[2], dg, i == 0)
            if want_colsum:
                _accumulate(outs[3], colsum, i == 0)

        _kloop(k, N_DEV, [acc_ref], [contrib], finish)

    if dy_chunked:
        dspec = pl.BlockSpec((None, tm, nsh), lambda i, k: (k, i, 0))
    else:
        dspec = pl.BlockSpec((tm, nsh), lambda i, k: (i, k))
    wspec = pl.BlockSpec((None, None, D, nsh), lambda i, k: (k, layer, 0, 0))
    row = pl.BlockSpec((tm, D), lambda i, k: (i, 0))
    row_once = pl.BlockSpec((tm, D), lambda i, k: (i, 0), pipeline_mode=pl.Buffered(1))
    vec = pl.BlockSpec((1, D), lambda i, k: (0, 0))
    out_specs = [row, row, vec] + ([vec] if want_colsum else [])
    out_shape = [_sds((S, D), F32), _sds((S, D), BF16), _sds((1, D), F32)] + ([_sds((1, D), F32)] if want_colsum else [])
    res = _call(
        name, body, (S // tm, N_DEV), [dspec] * nt + [wspec] * nt + [row_once, vec, row_once],
        out_specs, out_shape, (*dys, *ws, x, g, dres), comm=comm, vmem=BIG_VMEM_LIMIT_BYTES,
    )
    own, landed = res if comm is not None else (res, None)
    own = tuple(own) if want_colsum else (*own, None)
    return own if comm is None else (own, landed)


def _ffn_bwd_hidden_call(name, dyb, w, layer, gpre, upre, comm=None):
    S, D = dyb.shape
    fsh = w.shape[2]
    tn = SHARD_PAIR * fsh
    tm = _tile(S, ROW_TILE)

    def body(dy_ref, w_ref, g_ref, u_ref, dg_ref, du_ref):
        w = w_ref[...].reshape(tn, D)
        half = max(tm // 2, 8)
        for r in range(0, tm, half):
            rows = slice(r, r + half)
            da = _dot(dy_ref[rows, :], w, "nt")
            g = g_ref[rows, :].astype(F32)
            u = u_ref[rows, :].astype(F32)
            s = _sig(g)
            dg_ref[rows, :] = (da * u * s * (1.0 + g * (1.0 - s))).astype(BF16)
            du_ref[rows, :] = (da * g * s).astype(BF16)

    cspec = pl.BlockSpec((tm, tn), lambda d, i: (i, d))
    return _call(
        name, body, (N_DEV // SHARD_PAIR, S // tm),
        [pl.BlockSpec((tm, D), lambda d, i: (i, 0)), _pair_spec(w, layer), cspec, cspec],
        [cspec, cspec], [_sds((S, N_DEV * fsh), BF16)] * 2, (dyb, w, gpre, upre), comm=comm,
    )


def _grad_rows_call(name, a, b, comm=None):
    S, F = a.shape
    N = b.shape[1]
    fsh = F // N_DEV
    tr = SHARD_PAIR * fsh
    tn = _tile(N, GRAD_COL_TILE)
    tk = _tile(S, K_TILE)
    nk = S // tk

    def body(a_ref, b_ref, o_ref, acc_ref):
        def finish():
            o_ref[...] = acc_ref[...].astype(BF16).reshape(SHARD_PAIR, fsh, tn)

        _kloop(pl.program_id(2), nk, [acc_ref], [_dot(a_ref[...], b_ref[...], "tn")], finish)

    res = _call(
        name, body, (N_DEV // SHARD_PAIR, N // tn, nk),
        [pl.BlockSpec((tk, tr), lambda d, n, k: (k, d)), pl.BlockSpec((tk, tn), lambda d, n, k: (k, n))],
        pl.BlockSpec((SHARD_PAIR, fsh, tn), lambda d, n, k: (d, 0, n)), _sds((N_DEV, fsh, N), BF16), (a, b),
        scratch=[pltpu.VMEM((tr, tn), F32)], comm=comm,
    )
    return res


def _rms_bwd_res_call(name, dh, x, g, dres, want_colsum):
    S, D = x.shape
    ts = _tile(S, ELT_TILE)

    def body(dh_ref, x_ref, g_ref, r_ref, *outs):
        i = pl.program_id(0)
        d, dg = _rms_bwd(dh_ref[...], x_ref[...], g_ref[...])
        dx = r_ref[...] + d
        outs[0][...] = dx
        outs[1][...] = dx.astype(BF16)
        _accumulate(outs[2], dg, i == 0)
        if want_colsum:
            _accumulate(outs[3], jnp.sum(dx, axis=0, keepdims=True), i == 0)

    row = pl.BlockSpec((ts, D), lambda i: (i, 0))
    vec = pl.BlockSpec((1, D), lambda i: (0, 0))
    out_specs = [row, row, vec] + ([vec] if want_colsum else [])
    out_shape = [_sds((S, D), F32), _sds((S, D), BF16), _sds((1, D), F32)] + ([_sds((1, D), F32)] if want_colsum else [])
    res = _call(name, body, (S // ts,), [row, row, vec, row], out_specs, out_shape, (dh, x, g, dres))
    return tuple(res) if want_colsum else (*res, None)


def _conv_out_bwd_call(name, dyb, w, layer, u2, ln_g, ln_b):
    S, D = u2.shape
    ksh = w.shape[2]
    tm = _tile(S, ELT_TILE)

    def body(dy_ref, w_ref, u2_ref, g_ref, b_ref, du2_ref, dg_ref, db_ref, dbdw_ref):
        i = pl.program_id(0)
        dy = dy_ref[...]
        du4 = jnp.concatenate([_dot(dy, w_ref[d], "nt") for d in range(N_DEV)], axis=-1)
        u2 = u2_ref[...]
        mu = jnp.mean(u2, axis=-1, keepdims=True)
        xc = u2 - mu
        r = lax.rsqrt(jnp.mean(xc * xc, axis=-1, keepdims=True) + EPS)
        xh = xc * r
        gain = g_ref[...]
        u3 = xh * gain + b_ref[...]
        s = _sig(u3)
        du3 = du4 * s * (1.0 + u3 * (1.0 - s))
        dxh = du3 * gain
        du2 = r * (dxh - jnp.mean(dxh, axis=-1, keepdims=True) - xh * jnp.mean(dxh * xh, axis=-1, keepdims=True))
        du2_ref[...] = du2
        _accumulate(dg_ref, jnp.sum(du3 * xh, axis=0, keepdims=True), i == 0)
        _accumulate(db_ref, jnp.sum(du3, axis=0, keepdims=True), i == 0)
        _accumulate(dbdw_ref, jnp.sum(du2, axis=0, keepdims=True), i == 0)

    row = pl.BlockSpec((tm, D), lambda i: (i, 0))
    vec = pl.BlockSpec((1, D), lambda i: (0, 0))
    return _call(
        name, body, (S // tm,),
        [row, pl.BlockSpec((N_DEV, None, ksh, D), lambda i: (0, layer, 0, 0)), row, vec, vec],
        [row, vec, vec, vec], [_sds((S, D), F32)] + [_sds((1, D), F32)] * 3, (dyb, w, u2, ln_g, ln_b),
    )


def _dwconv_bwd_call(name, du2, u1, w_dw, a_pre, g_pre):
    S, D = u1.shape
    tc = _tile(S, CONV_TILE)
    hb = tc // CONV_HALO
    n_halo = S // CONV_HALO
    assert D % CONV_LANES == 0 and tc % CONV_SUB == 0
    wrows = CONV_HALO

    def body(d_cur, d_next, u_cur, u_prev, w_ref, a_ref, g_ref, du_ref, dbin_ref, dw_ref, dext_ref, uext_ref, du1_ref, dwacc_ref, dsh_ref, ush_ref):
        i = pl.program_id(0)
        last = S // tc - 1
        dext_ref[0:tc, :] = d_cur[...]
        dext_ref[tc:tc + CONV_HALO, :] = jnp.where(i < last, d_next[...], 0.0)
        uext_ref[0:CONV_HALO, :] = jnp.where(i > 0, u_prev[...], 0.0)
        uext_ref[CONV_HALO:tc + CONV_HALO, :] = u_cur[...]
        _fill_shifted(dsh_ref, dext_ref, tc + CONV_HALO)
        _fill_shifted(ush_ref, uext_ref, tc + CONV_HALO)

        @pl.when(i == 0)
        def _():
            dwacc_ref[...] = jnp.zeros_like(dwacc_ref)

        first_tap = CONV_HALO - (CONV_WIDTH - 1)
        for r in range(0, tc, CONV_SUB):
            for c in range(0, D, CONV_LANES):
                du1_ref[r:r + CONV_SUB, c:c + CONV_LANES] = _conv_taps(dsh_ref, w_ref, r, c, CONV_WIDTH - 1, True)
                dcur = dext_ref[r:r + CONV_SUB, c:c + CONV_LANES]
                for k in range(CONV_WIDTH):
                    prod = dcur * _shifted_window(ush_ref, r + first_tap + k, c)
                    part = prod[0:8]
                    for q in range(8, CONV_SUB, 8):
                        part = part + prod[q:q + 8]
                    dwacc_ref[k, :, c:c + CONV_LANES] += part

        du1 = du1_ref[...]
        a = a_ref[...].astype(F32)
        sg = _sig(g_ref[...].astype(F32))
        da = du1 * sg
        dgate = du1 * a * sg * (1.0 - sg)
        du_ref[:, 0:D] = da.astype(BF16)
        du_ref[:, D:2 * D] = dgate.astype(BF16)
        _accumulate(dbin_ref, jnp.concatenate([jnp.sum(da, axis=0, keepdims=True), jnp.sum(dgate, axis=0, keepdims=True)], axis=-1), i == 0)

        @pl.when(i == last)
        def _():
            for k in range(CONV_WIDTH):
                dw_ref[k:k + 1, :] = jnp.sum(dwacc_ref[k], axis=0, keepdims=True)
            dw_ref[CONV_WIDTH:, :] = jnp.zeros((wrows - CONV_WIDTH, D), F32)

    row = pl.BlockSpec((tc, D), lambda i: (i, 0))
    nxt = pl.BlockSpec((CONV_HALO, D), lambda i: (jnp.minimum((i + 1) * hb, n_halo - 1), 0))
    prv = pl.BlockSpec((CONV_HALO, D), lambda i: (jnp.maximum(i * hb - 1, 0), 0))
    return _call(
        name, body, (S // tc,),
        [row, nxt, row, prv, pl.BlockSpec((CONV_WIDTH, D), lambda i: (0, 0)), row, row],
        [pl.BlockSpec((tc, 2 * D), lambda i: (i, 0)), pl.BlockSpec((1, 2 * D), lambda i: (0, 0)), pl.BlockSpec((wrows, D), lambda i: (0, 0))],
        [_sds((S, 2 * D), BF16), _sds((1, 2 * D), F32), _sds((wrows, D), F32)],
        (du2, du2, u1, u1, w_dw, a_pre, g_pre),
        scratch=[
            pltpu.VMEM((tc + CONV_HALO + SUBLANES, D), F32), pltpu.VMEM((tc + CONV_HALO + SUBLANES, D), F32),
            pltpu.VMEM((tc, D), F32), pltpu.VMEM((CONV_WIDTH, 8, D), F32),
            pltpu.VMEM((SUBLANES, tc + CONV_HALO, D), F32), pltpu.VMEM((SUBLANES, tc + CONV_HALO, D), F32),
        ],
    )


def _pool_out_bwd_call(name, dy, ypre, scale, wp):
    S, D = dy.shape
    ng, gc, _ = wp.shape
    tm = _tile(S, FULL_ROW_TILE)

    def body(dy_ref, y_ref, s_ref, w_ref, dm_ref, dyp_ref, ds_ref):
        dy = dy_ref[...]
        _accumulate(ds_ref, jnp.sum(dy * y_ref[...].astype(F32), axis=0, keepdims=True), pl.program_id(0) == 0)
        dyp = (dy * s_ref[...]).astype(BF16)
        dyp_ref[...] = dyp
        dm_ref[...] = jnp.concatenate([_dot(dyp[:, gi * gc:(gi + 1) * gc], w_ref[gi], "nt") for gi in range(ng)], axis=-1)

    row = pl.BlockSpec((tm, D), lambda i: (i, 0))
    vec = pl.BlockSpec((1, D), lambda i: (0, 0))
    return _call(
        name, body, (S // tm,), [row, row, vec, pl.BlockSpec((ng, gc, gc), lambda i: (0, 0, 0))],
        [row, row, vec], [_sds((S, D), F32), _sds((S, D), BF16), _sds((1, D), F32)], (dy, ypre, scale, wp),
    )


def _pool_w_grad_call(name, mix, dyp, ng):
    S, D = mix.shape
    gc = D // ng
    tk = _tile(S, K_TILE)
    nk = S // tk

    def body(m_ref, d_ref, o_ref, acc_ref):
        def finish():
            o_ref[...] = acc_ref[...]

        _kloop(pl.program_id(1), nk, [acc_ref], [_dot(m_ref[...], d_ref[...], "tn")], finish)

    blk = pl.BlockSpec((tk, gc), lambda g, k: (k, g))
    return _call(
        name, body, (ng, nk), [blk, blk], pl.BlockSpec((None, gc, gc), lambda g, k: (g, 0, 0)),
        _sds((ng, gc, gc), F32), (mix, dyp), scratch=[pltpu.VMEM((gc, gc), F32)],
    )


def _pool_mix_bwd_call(name, dmix, x, g, dres):
    S, D = x.shape
    ts = _tile(S, ELT_TILE)
    hb = ts // POOL_HALO
    n_halo = S // POOL_HALO
    gc = D // len(POOL_WINDOWS)

    def body(cur_ref, nxt_ref, x_ref, g_ref, r_ref, dx_ref, dxb_ref, dg_ref, ext_ref, dh_ref):
        i = pl.program_id(0)
        last = S // ts - 1
        t = i * ts + lax.broadcasted_iota(jnp.int32, (ts + POOL_HALO, 1), 0)
        for gi, win in enumerate(POOL_WINDOWS):
            lanes = slice(gi * gc, (gi + 1) * gc)
            cnt = jnp.minimum(t + 1, win).astype(F32)
            ext_ref[0:ts, lanes] = cur_ref[:, lanes] / cnt[0:ts]
            ext_ref[ts:, lanes] = jnp.where(i < last, nxt_ref[:, lanes] / cnt[ts:], 0.0)
        for gi, win in enumerate(POOL_WINDOWS):
            lanes = pl.ds(gi * gc, gc)
            acc = ext_ref[pl.ds(0, ts), lanes]
            for j in range(1, win):
                acc = acc + ext_ref[pl.ds(j, ts), lanes]
            dh_ref[:, gi * gc:(gi + 1) * gc] = acc - cur_ref[:, gi * gc:(gi + 1) * gc]
        d, dg = _rms_bwd(dh_ref[...], x_ref[...], g_ref[...])
        dx = r_ref[...] + d
        dx_ref[...] = dx
        dxb_ref[...] = dx.astype(BF16)
        _accumulate(dg_ref, dg, i == 0)

    row = pl.BlockSpec((ts, D), lambda i: (i, 0))
    vec = pl.BlockSpec((1, D), lambda i: (0, 0))
    nxt = pl.BlockSpec((POOL_HALO, D), lambda i: (jnp.minimum((i + 1) * hb, n_halo - 1), 0))
    return _call(
        name, body, (S // ts,), [row, nxt, row, vec, row], [row, row, vec],
        [_sds((S, D), F32), _sds((S, D), BF16), _sds((1, D), F32)], (dmix, dmix, x, g, dres),
        scratch=[pltpu.VMEM((ts + POOL_HALO, D), F32), pltpu.VMEM((ts, D), F32)],
    )


def _attn_bwd_call(name, q, k, v, do, qg, kg, bias, sinks):
    H, S, _ = q.shape
    n_kv = k.shape[0]
    grp = H // n_kv
    nb = S // Q_BLOCK
    scale = HEAD_DIM ** -0.5

    def body(q_ref, kp_ref, kc_ref, vp_ref, vc_ref, do_ref, qg_ref, kg_ref, bias_ref, sink_ref,
             dq_ref, dk_ref, dv_ref, dqg_ref, dkg_ref, dbias_ref, dsink_ref, ck_ref, cv_ref):
        n = pl.program_id(1)

        @pl.when(n == 0)
        def _():
            dqg_ref[...] = jnp.zeros_like(dqg_ref)
            dkg_ref[...] = jnp.zeros_like(dkg_ref)
            dbias_ref[...] = jnp.zeros_like(dbias_ref)
            dsink_ref[...] = jnp.zeros_like(dsink_ref)
            ck_ref[...] = jnp.zeros_like(ck_ref)
            cv_ref[...] = jnp.zeros_like(cv_ref)

        def block_grads(hh):
            hs = slice(hh * grp, (hh + 1) * grp)
            q, qn, _, kn, p, ps = _attn_probs(q_ref[hs], kp_ref[hh], kc_ref[hh], qg_ref[...], kg_ref[...], bias_ref[hs], sink_ref[hs], n)
            rows = grp * Q_BLOCK
            dob = do_ref[hs].reshape(rows, HEAD_DIM).astype(BF16)
            vv = jnp.concatenate([vp_ref[hh], vc_ref[hh]], axis=0).astype(BF16)
            dp = _dot(dob, vv, "nt").reshape(grp, Q_BLOCK, 2 * Q_BLOCK)
            delta = jnp.sum(p * dp, axis=-1, keepdims=True)
            dl = p * (dp - delta)
            dbias_ref[hs] += dl
            dsink_ref[hs] += jnp.sum(-ps * delta, axis=1, keepdims=True)
            dlb = dl.reshape(rows, 2 * Q_BLOCK).astype(BF16)
            dqn = (_dot(dlb, kn.astype(BF16), "nn") * scale).reshape(grp, Q_BLOCK, HEAD_DIM)
            dkn = _dot(dlb, qn.reshape(rows, HEAD_DIM).astype(BF16), "tn") * scale
            dvv = _dot(p.reshape(rows, 2 * Q_BLOCK).astype(BF16), dob, "tn")
            qgain = qg_ref[...]
            r = lax.rsqrt(jnp.mean(q * q, axis=-1, keepdims=True) + EPS)
            qh = q * r
            dqg_ref[hh] += jnp.sum(jnp.sum(dqn * qh, axis=1), axis=0, keepdims=True)
            dqh = dqn * qgain
            dq_ref[hs] = r * (dqh - qh * jnp.mean(dqh * qh, axis=-1, keepdims=True))
            return dkn, dvv

        def finish_prev(hh, dkn_prev, dv_prev):
            dk, dkg = _rms_bwd(dkn_prev, kp_ref[hh], kg_ref[...])
            dk_ref[hh] = dk
            dv_ref[hh] = dv_prev
            dkg_ref[hh] += dkg

        @pl.when(n < nb)
        def _():
            grads = [block_grads(hh) for hh in range(KV_STEP)]

            @pl.when(n > 0)
            def _():
                for hh, (dkn, dvv) in enumerate(grads):
                    finish_prev(hh, ck_ref[hh] + dkn[0:Q_BLOCK], cv_ref[hh] + dvv[0:Q_BLOCK])

            for hh, (dkn, dvv) in enumerate(grads):
                ck_ref[hh] = dkn[Q_BLOCK:]
                cv_ref[hh] = dvv[Q_BLOCK:]

        @pl.when(n == nb)
        def _():
            for hh in range(KV_STEP):
                finish_prev(hh, ck_ref[hh], cv_ref[hh])

    qspec, prev, cur, gain, bspec, sspec = _attn_specs(grp, nb)
    kout = pl.BlockSpec((KV_STEP, Q_BLOCK, HEAD_DIM), lambda j, n: (j, jnp.maximum(n - 1, 0), 0))
    gout = pl.BlockSpec((KV_STEP, 1, HEAD_DIM), lambda j, n: (j, 0, 0))
    return _call(
        name, body, (n_kv // KV_STEP, nb + 1),
        [qspec, prev, cur, prev, cur, qspec, gain, gain, bspec, sspec],
        [qspec, kout, kout, gout, gout, bspec, sspec],
        [
            _sds((H, S, HEAD_DIM), F32), _sds((n_kv, S, HEAD_DIM), F32), _sds((n_kv, S, HEAD_DIM), F32),
            _sds((n_kv, 1, HEAD_DIM), F32), _sds((n_kv, 1, HEAD_DIM), F32),
            _sds((H, Q_BLOCK, 2 * Q_BLOCK), F32), _sds((H, 1, 1), F32),
        ],
        (q, k, k, v, v, do, qg, kg, bias, sinks),
        scratch=[pltpu.VMEM((KV_STEP, Q_BLOCK, HEAD_DIM), F32), pltpu.VMEM((KV_STEP, Q_BLOCK, HEAD_DIM), F32)],
    )


def _bucket_sum_call(name, onehot, dbias):
    nbk, n = onehot.shape
    H = dbias.shape[0]

    def body(o_ref, d_ref, out_ref):
        out_ref[...] = lax.dot_general(o_ref[...], d_ref[...], _DIMS["nt"], precision=lax.Precision.HIGHEST, preferred_element_type=F32)

    return _call(
        name, body, (1,), [pl.BlockSpec((nbk, n), lambda i: (0, 0)), pl.BlockSpec((H, n), lambda i: (0, 0))],
        pl.BlockSpec((nbk, H), lambda i: (0, 0)), _sds((nbk, H), F32), (onehot, dbias),
    )


def _bias_table_call(name, rel_bias, onehot):
    nbk, n = onehot.shape
    H = rel_bias.shape[1]

    def body(r_ref, o_ref, out_ref):
        out_ref[...] = lax.dot_general(r_ref[...], o_ref[...], _DIMS["tn"], precision=lax.Precision.HIGHEST, preferred_element_type=F32)

    return _call(
        name, body, (1,), [pl.BlockSpec((nbk, H), lambda i: (0, 0)), pl.BlockSpec((nbk, n), lambda i: (0, 0))],
        pl.BlockSpec((H, n), lambda i: (0, 0)), _sds((H, n), F32), (rel_bias, onehot),
    )


def _exchange_call(name, comm):
    n_src, n_dst = len(comm.srcs), len(comm.out_shape)

    def body(*refs):
        src_refs, dst_refs, sem_refs = refs[:n_src], refs[n_src:n_src + n_dst], refs[n_src + n_dst:]
        comm.start(src_refs, dst_refs, sem_refs)
        comm.wait(src_refs, dst_refs, sem_refs)

    hbm = pl.BlockSpec(memory_space=pltpu.HBM)
    return pl.pallas_call(
        body, name=name, in_specs=[hbm] * n_src, out_specs=[hbm] * n_dst, out_shape=list(comm.out_shape),
        scratch_shapes=list(comm.sems), compiler_params=pltpu.CompilerParams(has_side_effects=True),
    )(*comm.srcs)


def _gather_forward_call(name, bufs):
    n = len(bufs)

    def body(*refs):
        in_refs, out_refs = refs[:n], refs[n:2 * n]
        send_sems, recv_sems = refs[2 * n:]
        x, y, c = _mesh_pos()
        sibling = (x, y, 1 - c)
        chips = [(1 - x, y), (x, 1 - y), (1 - x, 1 - y)]
        sends, recvs = [], []
        for t in range(n):
            for j, chip in enumerate(chips):
                s = 3 * t + j
                mine, theirs = _slot(*chip, c), _slot(*chip, 1 - c)
                sends.append(_remote(in_refs[t].at[mine], out_refs[t].at[mine], send_sems.at[s], recv_sems.at[s], sibling))
                recvs.append(_remote(in_refs[t].at[mine], out_refs[t].at[theirs], send_sems.at[s], recv_sems.at[s], sibling))
        for cp in sends:
            cp.start()
        for cp in recvs:
            cp.wait_recv()
        for cp in sends:
            cp.wait_send()

    hbm = pl.BlockSpec(memory_space=pltpu.HBM)
    return pl.pallas_call(
        body, name=name, in_specs=[hbm] * n, out_specs=[hbm] * n, out_shape=[_sds(b.shape, b.dtype) for b in bufs],
        scratch_shapes=[pltpu.SemaphoreType.DMA((3 * n,)), pltpu.SemaphoreType.DMA((3 * n,))],
        input_output_aliases={t: t for t in range(n)},
        compiler_params=pltpu.CompilerParams(has_side_effects=True),
    )(*bufs)


def _all_gather_call(name, shards):
    n = len(shards)

    def body(*refs):
        in_refs, out_refs = refs[:n], refs[n:2 * n]
        send_sems, recv_sems, local_sems = refs[2 * n:]
        x, y, c = _mesh_pos()
        me, sibling = (x, y, c), (x, y, 1 - c)
        chips = [(1 - x, y), (x, 1 - y), (1 - x, 1 - y)]

        def copy(t, k, block, to, src=None):
            dst = out_refs[t].at[_slot(*block)]
            return pltpu.make_async_remote_copy(
                src_ref=dst if src is None else src, dst_ref=dst,
                send_sem=send_sems.at[7 * t + k], recv_sem=recv_sems.at[7 * t + k],
                device_id=to, device_id_type=pl.DeviceIdType.MESH,
            )

        mine = [pltpu.make_async_copy(in_refs[t], out_refs[t].at[_slot(*me)], local_sems.at[t]) for t in range(n)]
        for cp in mine:
            cp.start()
        first = []
        for t in range(n):
            first.append(copy(t, 0, me, sibling, src=in_refs[t]))
            first += [copy(t, 1 + j, me, (*chip, c), src=in_refs[t]) for j, chip in enumerate(chips)]
        for cp in first:
            cp.start()
        passed = []
        for j, chip in enumerate(chips):
            for t in range(n):
                copy(t, 1 + j, (*chip, c), me).wait_recv()
                fwd = copy(t, 4 + j, (*chip, c), sibling)
                fwd.start()
                passed.append(fwd)
        for t in range(n):
            copy(t, 0, sibling, me).wait_recv()
            for j, chip in enumerate(chips):
                copy(t, 4 + j, (*chip, 1 - c), me).wait_recv()
        for cp in first + passed:
            cp.wait_send()
        for cp in mine:
            cp.wait()

    hbm = pl.BlockSpec(memory_space=pltpu.HBM)
    return pl.pallas_call(
        body, name=name,
        in_specs=[hbm] * n, out_specs=[hbm] * n,
        out_shape=[_sds((N_DEV, *s.shape), s.dtype) for s in shards],
        scratch_shapes=[pltpu.SemaphoreType.DMA((7 * n,)), pltpu.SemaphoreType.DMA((7 * n,)), pltpu.SemaphoreType.DMA((n,))],
        compiler_params=pltpu.CompilerParams(has_side_effects=True),
    )(*shards)


def _all_reduce_small_call(name, pack):
    R, C = pack.shape

    def body(in_ref, out_ref, land_ref, send_sems, recv_sems):
        x, y, c = _mesh_pos()
        me = _slot(x, y, c)
        peers = _peers(x, y, c)
        land_ref[me] = in_ref[...]
        sends = [
            pltpu.make_async_remote_copy(
                src_ref=in_ref, dst_ref=land_ref.at[me], send_sem=send_sems.at[k], recv_sem=recv_sems.at[k],
                device_id=to, device_id_type=pl.DeviceIdType.MESH,
            )
            for k, to in enumerate(peers)
        ]
        for cp in sends:
            cp.start()
        for k, frm in enumerate(peers):
            pltpu.make_async_remote_copy(
                src_ref=in_ref, dst_ref=land_ref.at[_slot(*frm)], send_sem=send_sems.at[k], recv_sem=recv_sems.at[k],
                device_id=frm, device_id_type=pl.DeviceIdType.MESH,
            ).wait_recv()
        for cp in sends:
            cp.wait_send()
        total = land_ref[0]
        for d in range(1, N_DEV):
            total = total + land_ref[d]
        out_ref[...] = total

    vmem = pl.BlockSpec(memory_space=pltpu.VMEM)
    return pl.pallas_call(
        body, name=name, in_specs=[vmem], out_specs=vmem, out_shape=_sds((R, C), F32),
        scratch_shapes=[pltpu.VMEM((N_DEV, R, C), F32), pltpu.SemaphoreType.DMA((7,)), pltpu.SemaphoreType.DMA((7,))],
        compiler_params=pltpu.CompilerParams(has_side_effects=True, vmem_limit_bytes=VMEM_LIMIT_BYTES),
    )(pack)


def _adamw_call(name, grad, landed, w, m, v, layer, prev):
    L, R, C = w.shape
    tr = _div_tile(R, ADAM_BLOCK_BYTES // (C * 4))
    c1 = 1.0 / (1.0 - ADAM_B1 ** ADAM_STEP)
    c2 = 1.0 / (1.0 - ADAM_B2 ** ADAM_STEP)

    def body(g_ref, w_ref, m_ref, v_ref, *rest):
        go_ref, d_ref, mo_ref, vo_ref = rest[-4:]
        if landed:
            g = g_ref[0].astype(F32)
            for d in range(1, N_DEV):
                g = g + g_ref[d].astype(F32)
        else:
            g = g_ref[...]
        go_ref[...] = g
        mn = ADAM_B1 * m_ref[...] + (1.0 - ADAM_B1) * g
        vn = ADAM_B2 * v_ref[...] + (1.0 - ADAM_B2) * (g * g)
        mo_ref[...] = mn
        vo_ref[...] = vn
        d_ref[...] = -ADAM_LR * ((mn * c1) / (jnp.sqrt(vn * c2) + ADAM_EPS) + ADAM_WD * w_ref[...])

    blk = pl.BlockSpec((None, tr, C), lambda i: (layer, i, 0))
    gspec = pl.BlockSpec((N_DEV, tr, C), lambda i: (0, i, 0)) if landed else pl.BlockSpec((tr, C), lambda i: (i, 0))
    in_specs, ins, aliases = [gspec, blk, blk, blk], [grad, w, m, v], {}
    if prev is not None:
        in_specs += [pl.BlockSpec(memory_space=pl.ANY)] * 4
        ins += list(prev)
        aliases = {4 + q: q for q in range(4)}
    return _call(name, body, (R // tr,), in_specs, [blk] * 4, [_sds((L, R, C), F32)] * 4, ins, aliases=aliases)


def _t5_bucket(rel):
    nb = NUM_BUCKETS // 2
    n = -rel
    ret = jnp.where(n < 0, nb, 0)
    n = jnp.abs(n)
    max_exact = nb // 2
    nf = jnp.maximum(n, 1).astype(jnp.float32)
    large = max_exact + (jnp.log(nf / max_exact) / math.log(REL_MAX_DIST / max_exact) * (nb - max_exact)).astype(jnp.int32)
    large = jnp.minimum(large, nb - 1)
    return ret + jnp.where(n < max_exact, n, large)


def _band_buckets():
    i = jnp.arange(Q_BLOCK)[:, None]
    j = jnp.arange(2 * Q_BLOCK)[None, :]
    return _t5_bucket(j - Q_BLOCK - i)


def _to_heads(t, n_heads):
    S = t.shape[0]
    return t.reshape(S, n_heads, HEAD_DIM).transpose(1, 0, 2)


def _from_heads(t):
    H, S, _ = t.shape
    return t.transpose(1, 0, 2).reshape(S, H * HEAD_DIM)


def _gathered_vec(t):
    nd, L, n = t.shape
    return t.transpose(1, 0, 2).reshape(L, nd * n)


_MIXER_WEIGHTS = {0: ("conv_w_in", "conv_w_out"), 1: (), 2: ("attn_w_qkv", "attn_w_o")}
_FFN_UP_WEIGHTS = ("ffn_w_gate", "ffn_w_up")
_FFN_REST_WEIGHTS = ("ffn_w_down", "ple_w_proj", "ple_w_gate")
_SMALL_SHARDED = ("conv_b_in", "conv_w_dw", "conv_b_dw", "conv_ln_g", "conv_ln_b", "conv_b_out")


def _mixer_keys(i):
    return [(n, i // 3) for n in _MIXER_WEIGHTS[i % 3]]


def _step(x, p, target, wb, small_pack, small_shapes, V):
    S, D = x.shape
    depth = V["norm_mix"].shape[0]
    n_heads = D // HEAD_DIM
    n_kv = (wb["attn_w_qkv"].shape[-1] * N_DEV - D) // (2 * HEAD_DIM)
    ng = len(POOL_WINDOWS)
    vec = lambda t, i: t[i][None, :]
    shard = lambda key: wb[key[0]][key[1]:key[1] + 1]

    keys_a0 = [(n, 0) for n in _FFN_UP_WEIGHTS]
    keys_b0 = [(n, 0) for n in _FFN_REST_WEIGHTS]
    first = _mixer_keys(0) + [("pool_w", 0)]
    gathered = _all_gather_call("all_gather0", [shard(k) for k in first] + [small_pack])
    W = dict(zip(first, gathered[:-1]))
    pw = W.pop(("pool_w", 0))
    wp = pw[:, 0].transpose(1, 0, 2, 3).reshape(pw.shape[2], pw.shape[3] * N_DEV, pw.shape[4])
    V = dict(V)
    for n, t in zip(_SMALL_SHARDED, _unpack(gathered[-1], small_shapes, lead=(N_DEV,))):
        if n == "conv_w_dw":
            V[n] = t.transpose(1, 2, 0, 3).reshape(t.shape[1], t.shape[2], -1)
        else:
            V[n] = _gathered_vec(t)

    buckets = _band_buckets()
    onehot = (buckets.reshape(1, -1) == jnp.arange(NUM_BUCKETS)[:, None]).astype(F32)
    bias_tab = _bias_table_call("bias_table", V["rel_bias"], onehot).reshape(n_heads, Q_BLOCK, 2 * Q_BLOCK)
    sinks3 = V["attn_sinks"].reshape(n_heads, 1, 1)
    pb = p.astype(BF16)

    def gather_next(keys):
        return _GatherOwn([shard(k) for k in keys]) if keys else None

    def finish_gather(name, keys, bufs):
        W.update(zip(keys, _gather_forward_call(name, bufs)))

    saved = []
    hb = _rms_call("rms_in", x, vec(V["norm_mix"], 0))
    for i in range(depth):
        kind, j = i % 3, i // 3
        sv = {"x0": x, "h0": hb}
        g_ffn = vec(V["norm_ffn"], i)
        if kind == 0:
            ride = i == 0
            res = _conv_in_call(f"conv_in{i}", hb, W["conv_w_in", j], 0, vec(V["conv_b_in"], j), comm=gather_next(keys_b0) if ride else None)
            if ride:
                res, bufs_b = res
            a_pre, g_pre, u1 = res
            res = _dwconv_call(
                f"dwconv{i}", u1, V["conv_w_dw"][j], vec(V["conv_b_dw"], j), vec(V["conv_ln_g"], j), vec(V["conv_ln_b"], j),
                comm=gather_next(keys_a0) if ride else None)
            if ride:
                res, bufs_a = res
                finish_gather("gather_fwd0", keys_a0 + keys_b0, bufs_a + bufs_b)
            u2, u4 = res
            x1, h2 = _fullmm_res_call(f"conv_out{i}", u4, W["conv_w_out", j], 0, x, vec(V["conv_b_out"], j), g_ffn)
            sv.update(a_pre=a_pre, g_pre=g_pre, u1=u1, u2=u2, u4=u4)
        elif kind == 1:
            mix = _pool_mix_call(f"pool_mix{i}", x, vec(V["norm_mix"], i))
            x1, ypre, h2 = _pool_out_call(f"pool_out{i}", mix, wp, vec(V["pool_scale"], j), x, g_ffn)
            sv.update(mix=mix, ypre=ypre)
        else:
            qkv = _colmm_call(f"qkv{i}", hb, W["attn_w_qkv", j], 0, F32)
            q = _to_heads(qkv[:, :D], n_heads)
            k = _to_heads(qkv[:, D:D + n_kv * HEAD_DIM], n_kv)
            v = _to_heads(qkv[:, D + n_kv * HEAD_DIM:], n_kv)
            o = _attn_fwd_call(f"attn{i}", q, k, v, vec(V["attn_q_norm"], j), vec(V["attn_k_norm"], j), bias_tab, sinks3)
            ob = _from_heads(o)
            x1, h2 = _fullmm_res_call(f"attn_out{i}", ob, W["attn_w_o", j], 0, x, None, g_ffn)
            sv.update(q=q, k=k, v=v, ob=ob)
        more = i + 1 < depth
        keys_a = [(n, i + 1) for n in _FFN_UP_WEIGHTS] if more else []
        keys_b = [(n, i + 1) for n in _FFN_REST_WEIGHTS] + _mixer_keys(i + 1) if more else []
        res = _ffn_up_call(f"ffn_up{i}", h2, W["ffn_w_gate", i], W["ffn_w_up", i], 0, comm=gather_next(keys_a))
        if more:
            (gpre, upre, act), bufs_a = res
        else:
            gpre, upre, act = res
        res = _mm_rows_call(f"ffn_down{i}", act, W["ffn_w_down", i], 0, None, comm=gather_next(keys_b))
        if more:
            y, bufs_b = res
            finish_gather(f"gather_fwd{i + 1}", keys_a + keys_b, bufs_a + bufs_b)
        else:
            y = res
        x2, h3 = _res_rms_call(f"ffn_res{i}", y, x1, vec(V["norm_ple"], i))
        pp = _colmm_call(f"ple_proj{i}", pb[i], W["ple_w_proj", i], 0, BF16)
        g_next = vec(V["norm_mix"], i + 1) if more else None
        x3, gate, hb_next = _ple_gate_call(f"ple_gate{i}", h3, W["ple_w_gate", i], 0, x2, vec(V["ple_b_gate"], i), pp, g_next)
        sv.update(x1=x1, h2=h2, gpre=gpre, upre=upre, act=act, x2=x2, h3=h3, pp=pp, gate=gate)
        saved.append(sv)
        x, hb = x3, hb_next

    dx, dxb, loss_tile = _loss_call("loss", x, target)
    loss = loss_tile[0, 0]

    landed = {}
    GV = {n: [None] * V[n].shape[0] for n in ("norm_mix", "norm_ffn", "norm_ple", "ple_b_gate", "conv_b_in", "conv_w_dw", "conv_b_dw", "conv_ln_g", "conv_ln_b", "conv_b_out", "pool_scale")}

    def scatter_of(pending):
        return _ScatterSlots([g for _, g in pending]) if pending else None

    def record(pending, bufs):
        landed.update(zip([k for k, _ in pending], bufs))

    mixer_pending = []
    for i in reversed(range(depth)):
        kind, j = i % 3, i // 3
        sv = saved[i]
        dz, dpp, db_gate = _ple_bwd_elt_call(f"ple_bwd{i}", dx, sv["gate"], sv["pp"])
        GV["ple_b_gate"][i] = db_gate
        pending = [(("ple_w_gate", i), _grad_w_call(f"g_ple_gate{i}", sv["h3"], "nat", dz, "full"))]
        pending.append((("ple_w_proj", i), _grad_w_call(f"g_ple_proj{i}", pb[i], "full", dpp, "nat")))
        dx, dxb, dg = _dx_full_rms_call(f"d_x2_{i}", dz, W["ple_w_gate", i], 0, sv["x2"], vec(V["norm_ple"], i), dx)
        GV["norm_ple"][i] = dg
        res = _ffn_bwd_hidden_call(f"ffn_bwd{i}", dxb, W["ffn_w_down", i], 0, sv["gpre"], sv["upre"], comm=scatter_of(mixer_pending))
        if mixer_pending:
            (dgp, dup), bufs = res
            record(mixer_pending, bufs)
        else:
            dgp, dup = res
        g_down = _grad_rows_call(f"g_ffn_down{i}", sv["act"], dxb)
        g_gate, bufs = _grad_rows_call(f"g_ffn_gate{i}", dgp, sv["h2"], comm=scatter_of(pending))
        record(pending, bufs)
        pending = [(("ffn_w_down", i), g_down)]
        g_up, bufs = _grad_rows_call(f"g_ffn_up{i}", dup, sv["h2"], comm=scatter_of(pending))
        record(pending, bufs)
        pending = [(("ffn_w_gate", i), g_gate)]
        dh, bufs = _mm_rows_call(f"d_h2g_{i}", dgp, W["ffn_w_gate", i], 0, None, comm=scatter_of(pending))
        record(pending, bufs)
        pending = [(("ffn_w_up", i), g_up)]
        dh, bufs = _mm_rows_call(f"d_h2_{i}", dup, W["ffn_w_up", i], 0, dh, comm=scatter_of(pending))
        record(pending, bufs)
        dx, dxb, dg, colsum = _rms_bwd_res_call(f"d_x1_{i}", dh, sv["x1"], vec(V["norm_ffn"], i), dx, kind == 0)
        GV["norm_ffn"][i] = dg
        g_mix = vec(V["norm_mix"], i)
        if kind == 0:
            GV["conv_b_out"][j] = colsum
            g_out = _grad_w_call(f"g_conv_out{i}", sv["u4"], "nat", dxb, "full")
            du2, d_ln_g, d_ln_b, d_b_dw = _conv_out_bwd_call(f"conv_out_bwd{i}", dxb, W["conv_w_out", j], 0, sv["u2"], vec(V["conv_ln_g"], j), vec(V["conv_ln_b"], j))
            du, d_b_in, d_w_dw = _dwconv_bwd_call(f"dwconv_bwd{i}", du2, sv["u1"], V["conv_w_dw"][j], sv["a_pre"], sv["g_pre"])
            GV["conv_ln_g"][j], GV["conv_ln_b"][j], GV["conv_b_dw"][j] = d_ln_g, d_ln_b, d_b_dw
            GV["conv_b_in"][j], GV["conv_w_dw"][j] = d_b_in, d_w_dw[:CONV_WIDTH]
            g_in = _grad_w_call(f"g_conv_in{i}", sv["h0"], "full", du, "nat")
            mixer_pending = [(("conv_w_in", j), g_in), (("conv_w_out", j), g_out)]
            res = _dx_colsharded_rms_call(
                f"d_x0_{i}", [du], False, [W["conv_w_in", j]], 0, sv["x0"], g_mix, dx, False, comm=scatter_of(mixer_pending) if i == 0 else None)
            if i == 0:
                res, bufs = res
                record(mixer_pending, bufs)
                mixer_pending = []
            dx, dxb, dg, _ = res
        elif kind == 1:
            dmix, dyp, d_scale = _pool_out_bwd_call(f"pool_out_bwd{i}", dx, sv["ypre"], vec(V["pool_scale"], j), wp)
            GV["pool_scale"][j] = d_scale
            g_pool = _pool_w_grad_call(f"g_pool_w{i}", sv["mix"], dyp, ng)
            gc = g_pool.shape[1]
            g_pool = g_pool.reshape(ng, N_DEV, gc // N_DEV, gc).transpose(1, 0, 2, 3).reshape(N_DEV, ng * (gc // N_DEV), gc)
            mixer_pending = [(("pool_w", j), g_pool.astype(BF16))]
            dx, dxb, dg = _pool_mix_bwd_call(f"pool_mix_bwd{i}", dmix, sv["x0"], g_mix, dx)
        else:
            g_o = _grad_w_call(f"g_attn_o{i}", sv["ob"], "nat", dxb, "full")
            do = _to_heads(_dx_full_call(f"d_attn_o{i}", dxb, W["attn_w_o", j], 0, F32), n_heads)
            dq, dk, dv, dqg, dkg, dbias, dsink = _attn_bwd_call(
                f"attn_bwd{i}", sv["q"], sv["k"], sv["v"], do, vec(V["attn_q_norm"], j), vec(V["attn_k_norm"], j), bias_tab, sinks3)
            GV["attn_q_norm"] = jnp.sum(dqg, axis=0)
            GV["attn_k_norm"] = jnp.sum(dkg, axis=0)
            GV["attn_sinks"] = dsink.reshape(1, n_heads)
            GV["rel_bias"] = _bucket_sum_call(f"g_rel_bias{i}", onehot, dbias.reshape(n_heads, -1))
            dqkv = jnp.concatenate([_from_heads(dq), _from_heads(dk), _from_heads(dv)], axis=-1).astype(BF16)
            g_qkv = _grad_w_call(f"g_qkv{i}", sv["h0"], "full", dqkv, "nat")
            mixer_pending = [(("attn_w_qkv", j), g_qkv), (("attn_w_o", j), g_o)]
            dx, dxb, dg, _ = _dx_colsharded_rms_call(f"d_x0_{i}", [dqkv], False, [W["attn_w_qkv", j]], 0, sv["x0"], g_mix, dx, False)
        GV["norm_mix"][i] = dg
    if mixer_pending:
        record(mixer_pending, _exchange_call("grad_scatter_tail", scatter_of(mixer_pending)))
    return loss, dx, landed, GV, V


_BIG = ("conv_w_in", "conv_w_out", "pool_w", "attn_w_qkv", "attn_w_o", "ffn_w_gate", "ffn_w_up", "ffn_w_down", "ple_w_proj", "ple_w_gate")
_SMALL_REPLICATED = ("norm_mix", "norm_ffn", "norm_ple", "pool_scale", "attn_q_norm", "attn_k_norm", "attn_sinks", "rel_bias", "ple_b_gate")
_WEIGHTS = ("norm_mix", "norm_ffn", "norm_ple", "conv_w_in", "conv_b_in", "conv_w_dw", "conv_b_dw", "conv_ln_g", "conv_ln_b", "conv_w_out",
            "conv_b_out", "pool_w", "pool_scale", "attn_w_qkv", "attn_q_norm", "attn_k_norm", "attn_sinks", "attn_w_o", "rel_bias",
            "ffn_w_gate", "ffn_w_up", "ffn_w_down", "ple_w_proj", "ple_w_gate", "ple_b_gate")
PACK_LANES = 128


def _pack(parts):
    flat = jnp.concatenate([t.reshape(-1).astype(F32) for t in parts])
    rows = -(-flat.shape[0] // (8 * PACK_LANES)) * 8
    flat = jnp.pad(flat, (0, rows * PACK_LANES - flat.shape[0]))
    return flat.reshape(rows, PACK_LANES)


def _unpack(pack, shapes, lead=()):
    flat = pack.reshape(*lead, -1)
    out, pos = [], 0
    for s in shapes:
        n = math.prod(s)
        out.append(flat[..., pos:pos + n].reshape(*lead, *s))
        pos += n
    return out


def _as2d(t):
    return t.reshape(-1, t.shape[-1])


def kernel(x, p, norm_mix, norm_ffn, norm_ple, conv_w_in, conv_b_in, conv_w_dw, conv_b_dw, conv_ln_g, conv_ln_b, conv_w_out, conv_b_out, pool_w, pool_scale, attn_w_qkv, attn_q_norm, attn_k_norm, attn_sinks, attn_w_o, rel_bias, ffn_w_gate, ffn_w_up, ffn_w_down, ple_w_proj, ple_w_gate, ple_b_gate, loss_target, m_norm_mix, m_norm_ffn, m_norm_ple, m_conv_w_in, m_conv_b_in, m_conv_w_dw, m_conv_b_dw, m_conv_ln_g, m_conv_ln_b, m_conv_w_out, m_conv_b_out, m_pool_w, m_pool_scale, m_attn_w_qkv, m_attn_q_norm, m_attn_k_norm, m_attn_sinks, m_attn_w_o, m_rel_bias, m_ffn_w_gate, m_ffn_w_up, m_ffn_w_down, m_ple_w_proj, m_ple_w_gate, m_ple_b_gate, v_norm_mix, v_norm_ffn, v_norm_ple, v_conv_w_in, v_conv_b_in, v_conv_w_dw, v_conv_b_dw, v_conv_ln_g, v_conv_ln_b, v_conv_w_out, v_conv_b_out, v_pool_w, v_pool_scale, v_attn_w_qkv, v_attn_q_norm, v_attn_k_norm, v_attn_sinks, v_attn_w_o, v_rel_bias, v_ffn_w_gate, v_ffn_w_up, v_ffn_w_down, v_ple_w_proj, v_ple_w_gate, v_ple_b_gate):
    given = dict(locals())
    w = {n: given[n] for n in _WEIGHTS}
    m = {n: given["m_" + n] for n in _WEIGHTS}
    v = {n: given["v_" + n] for n in _WEIGHTS}
    me = _slot(*_mesh_pos())

    wb = {n: w[n].astype(BF16) for n in _BIG}
    for n in _FFN_UP_WEIGHTS:
        wb[n] = wb[n].transpose(0, 2, 1)
    small_pack = _pack([w[n] for n in _SMALL_SHARDED])
    small_shapes = [w[n].shape for n in _SMALL_SHARDED]
    loss, grad_x, landed, GV, V = _step(x[0], p[:, 0], loss_target[0], wb, small_pack, small_shapes, {n: w[n] for n in _SMALL_REPLICATED})
    loss = lax.psum(loss, ("x", "y", "c"))

    small_names = list(_SMALL_REPLICATED) + list(_SMALL_SHARDED)
    small_full = []
    for n in small_names:
        g = GV[n]
        g = jnp.stack([t.reshape(V[n].shape[1:]) for t in g]) if isinstance(g, list) else g.reshape(V[n].shape)
        small_full.append(g)
    reduced = _unpack(_all_reduce_small_call("all_reduce_small", _pack(small_full)), [t.shape for t in small_full])
    small_grad = {}
    for n, g in zip(small_names, reduced):
        if n in _SMALL_SHARDED:
            c = w[n].shape[-1]
            g = lax.dynamic_slice_in_dim(g, me * c, c, axis=g.ndim - 1)
        small_grad[n] = g

    out = {}
    for n in _BIG:
        layers = w[n].shape[0]
        turned = n in _FFN_UP_WEIGHTS
        as3d = lambda t: t.transpose(0, 2, 1) if turned else t.reshape(layers, -1, t.shape[-1])
        res = None
        for l in range(layers):
            res = _adamw_call(f"adamw_{n}{l}", landed[n, l], True, as3d(w[n]), as3d(m[n]), as3d(v[n]), l, res)
        out[n] = [t.transpose(0, 2, 1) if turned else t.reshape(w[n].shape) for t in res]
    for n in small_names:
        res = _adamw_call(f"adamw_{n}", _as2d(small_grad[n]), False, _as2d(w[n])[None], _as2d(m[n])[None], _as2d(v[n])[None], 0, None)
        out[n] = [t.reshape(w[n].shape) for t in res]

    grads = [out[n][0] for n in _WEIGHTS]
    deltas = [out[n][1] for n in _WEIGHTS]
    new_m = [out[n][2] for n in _WEIGHTS]
    new_v = [out[n][3] for n in _WEIGHTS]
    return (loss, grad_x[None], *grads, *deltas, *new_m, *new_v)
```

```python
import functools
import math

import jax
import jax.numpy as jnp
from jax import lax
from jax.experimental import pallas as pl
from jax.experimental.pallas import tpu as pltpu

F32, BF16 = jnp.float32, jnp.bfloat16
N_DEV = 8
EPS = 1e-6
NEG_INF = -1e30
HEAD_DIM = 64
Q_BLOCK = 128
CHUNK = 64
WINDOW_CHUNKS = 2
CONV_WIDTH = 31
CONV_HALO = 32
POOL_WINDOWS = (2, 4, 8, 16)
POOL_HALO = 16
NUM_BUCKETS = 32
REL_MAX_DIST = 128
ADAM_LR, ADAM_B1, ADAM_B2, ADAM_EPS, ADAM_WD, ADAM_STEP = 0.001, 0.9, 0.999, 1e-08, 0.01, 10
VMEM_LIMIT_BYTES = 44 * 1024 * 1024
BIG_VMEM_LIMIT_BYTES = 52 * 1024 * 1024
ROW_TILE = 512
FULL_ROW_TILE = 256
ELT_TILE = 256
CONV_TILE = 128
CONV_SUB = 32
CONV_LANES = 512
COL_TILE = 1024
GRAD_COL_TILE = 1024
K_TILE = 2048
ADAM_BLOCK_BYTES = 1 << 20

_DIMS = {
    "nn": (((1,), (0,)), ((), ())),
    "nt": (((1,), (1,)), ((), ())),
    "tn": (((0,), (0,)), ((), ())),
}


def _dot(a, b, mode):
    return lax.dot_general(a, b, _DIMS[mode], preferred_element_type=F32)


def _tile(n, t):
    t = min(n, t)
    assert n % t == 0, (n, t)
    return t


def _div_tile(n, t):
    if n <= t:
        return n
    for cand in range(t // 16 * 16, 15, -16):
        if n % cand == 0:
            return cand
    return n


def _sds(shape, dtype):
    return jax.ShapeDtypeStruct(tuple(shape), dtype)


def _call(name, body, grid, in_specs, out_specs, out_shape, ins, scratch=(), comm=None, vmem=VMEM_LIMIT_BYTES, aliases=None):
    params = pltpu.CompilerParams(dimension_semantics=("arbitrary",) * len(grid), vmem_limit_bytes=vmem)
    if comm is None:
        return pl.pallas_call(
            body, name=name, grid=grid, in_specs=list(in_specs), out_specs=out_specs, out_shape=out_shape,
            scratch_shapes=list(scratch), compiler_params=params, input_output_aliases=aliases or {},
        )(*ins)
    assert not aliases
    single = not isinstance(out_shape, (list, tuple))
    own_specs = [out_specs] if single else list(out_specs)
    own_shape = [out_shape] if single else list(out_shape)
    n_in, n_out, n_scr = len(ins), len(own_shape), len(scratch)
    n_src, n_dst = len(comm.srcs), len(comm.out_shape)
    hbm = pl.BlockSpec(memory_space=pltpu.HBM)

    def with_comm(*refs):
        a = n_in
        b = a + n_src
        c = b + n_out
        d = c + n_dst
        e = d + n_scr
        src_refs, dst_refs, sem_refs = refs[a:b], refs[c:d], refs[e:]
        first = functools.reduce(jnp.logical_and, [pl.program_id(i) == 0 for i in range(len(grid))])
        last = functools.reduce(jnp.logical_and, [pl.program_id(i) == grid[i] - 1 for i in range(len(grid))])

        @pl.when(first)
        def _():
            comm.start(src_refs, dst_refs, sem_refs)

        body(*refs[:a], *refs[b:c], *refs[d:e])

        @pl.when(last)
        def _():
            comm.wait(src_refs, dst_refs, sem_refs)

    res = pl.pallas_call(
        with_comm, name=name, grid=grid,
        in_specs=list(in_specs) + [hbm] * n_src,
        out_specs=own_specs + [hbm] * n_dst,
        out_shape=own_shape + list(comm.out_shape),
        scratch_shapes=list(scratch) + list(comm.sems),
        compiler_params=params,
    )(*ins, *comm.srcs)
    own = res[0] if single else list(res[:n_out])
    return own, list(res[n_out:])


def _mesh_pos():
    return lax.axis_index("x"), lax.axis_index("y"), lax.axis_index("c")


def _slot(px, py, pc):
    return 4 * px + 2 * py + pc


def _peers(x, y, c):
    flips = [(fx, fy, fc) for fx in (0, 1) for fy in (0, 1) for fc in (0, 1)][1:]
    return [(1 - x if fx else x, 1 - y if fy else y, 1 - c if fc else c) for fx, fy, fc in flips]


def _remote(src, dst, send_sem, recv_sem, to):
    return pltpu.make_async_remote_copy(
        src_ref=src, dst_ref=dst, send_sem=send_sem, recv_sem=recv_sem, device_id=to, device_id_type=pl.DeviceIdType.MESH
    )


class _GatherOwn:
    N_TO = 4

    def __init__(self, shards):
        n = len(shards)
        self.srcs = list(shards)
        self.out_shape = [_sds((N_DEV, *s.shape), s.dtype) for s in shards]
        self.sems = [pltpu.SemaphoreType.DMA((self.N_TO * n,)), pltpu.SemaphoreType.DMA((self.N_TO * n,)), pltpu.SemaphoreType.DMA((n,))]

    def _copies(self, src_refs, dst_refs, sem_refs):
        send_sems, recv_sems, local_sems = sem_refs
        x, y, c = _mesh_pos()
        me = (x, y, c)
        targets = [(x, y, 1 - c), (1 - x, y, c), (x, 1 - y, c), (1 - x, 1 - y, c)]
        sends, recvs, local = [], [], []
        for t, (src, dst) in enumerate(zip(src_refs, dst_refs)):
            local.append(pltpu.make_async_copy(src, dst.at[_slot(*me)], local_sems.at[t]))
            for k, to in enumerate(targets):
                s = self.N_TO * t + k
                sends.append(_remote(src, dst.at[_slot(*me)], send_sems.at[s], recv_sems.at[s], to))
                recvs.append(_remote(src, dst.at[_slot(*to)], send_sems.at[s], recv_sems.at[s], to))
        return sends, recvs, local

    def start(self, src_refs, dst_refs, sem_refs):
        sends, _, local = self._copies(src_refs, dst_refs, sem_refs)
        for cp in local + sends:
            cp.start()

    def wait(self, src_refs, dst_refs, sem_refs):
        sends, recvs, local = self._copies(src_refs, dst_refs, sem_refs)
        for cp in recvs:
            cp.wait_recv()
        for cp in sends:
            cp.wait_send()
        for cp in local:
            cp.wait()


class _ScatterSlots:
    def __init__(self, grads):
        n = len(grads)
        self.srcs = list(grads)
        self.out_shape = [_sds(g.shape, g.dtype) for g in grads]
        self.sems = [pltpu.SemaphoreType.DMA((7 * n,)), pltpu.SemaphoreType.DMA((7 * n,)), pltpu.SemaphoreType.DMA((n,))]

    def _copies(self, src_refs, dst_refs, sem_refs):
        send_sems, recv_sems, local_sems = sem_refs
        x, y, c = _mesh_pos()
        me = _slot(x, y, c)
        sends, recvs, local = [], [], []
        for t, (src, dst) in enumerate(zip(src_refs, dst_refs)):
            local.append(pltpu.make_async_copy(src.at[me], dst.at[me], local_sems.at[t]))
            for k, to in enumerate(_peers(x, y, c)):
                s = 7 * t + k
                sends.append(_remote(src.at[_slot(*to)], dst.at[me], send_sems.at[s], recv_sems.at[s], to))
                recvs.append(_remote(src.at[me], dst.at[_slot(*to)], send_sems.at[s], recv_sems.at[s], to))
        return sends, recvs, local

    start = _GatherOwn.start
    wait = _GatherOwn.wait


def _sig(x):
    return 1.0 / (1.0 + jnp.exp(-x))


def _rms(x, g):
    r = lax.rsqrt(jnp.mean(x * x, axis=-1, keepdims=True) + EPS)
    return x * r * g


def _rms_bwd(dy, x, g):
    r = lax.rsqrt(jnp.mean(x * x, axis=-1, keepdims=True) + EPS)
    xh = x * r
    dg = jnp.sum(dy * xh, axis=0, keepdims=True)
    dxh = dy * g
    dx = r * (dxh - xh * jnp.mean(dxh * xh, axis=-1, keepdims=True))
    return dx, dg


def _accumulate(ref, val, first):
    @pl.when(first)
    def _():
        ref[...] = val

    @pl.when(jnp.logical_not(first))
    def _():
        ref[...] += val


def _kloop(k, nk, acc_refs, contribs, finish):
    @pl.when(k == 0)
    def _():
        for r, c in zip(acc_refs, contribs):
            r[...] = c

    @pl.when(k > 0)
    def _():
        for r, c in zip(acc_refs, contribs):
            r[...] += c

    @pl.when(k == nk - 1)
    def _():
        finish()


def _rms_call(name, x, g):
    S, D = x.shape
    ts = _tile(S, ELT_TILE)

    def body(x_ref, g_ref, o_ref):
        o_ref[...] = _rms(x_ref[...], g_ref[...]).astype(BF16)

    return _call(
        name, body, (S // ts,),
        [pl.BlockSpec((ts, D), lambda i: (i, 0)), pl.BlockSpec((1, D), lambda i: (0, 0))],
        pl.BlockSpec((ts, D), lambda i: (i, 0)), _sds((S, D), BF16), (x, g),
    )


def _conv_in_call(name, hb, w, layer, b_in, comm=None):
    S, D = hb.shape
    nsh = w.shape[2]
    half = N_DEV // 2
    assert nsh * half == D
    tm = _tile(S, ROW_TILE)

    def body(h_ref, wa_ref, wg_ref, ba_ref, bg_ref, a_ref, g_ref, u_ref):
        h = h_ref[...]
        a = _dot(h, wa_ref[...], "nt") + ba_ref[...]
        g = _dot(h, wg_ref[...], "nt") + bg_ref[...]
        a_ref[...] = a.astype(BF16)
        g_ref[...] = g.astype(BF16)
        u_ref[...] = a * _sig(g)

    out_spec = pl.BlockSpec((tm, nsh), lambda d, i: (i, d))
    return _call(
        name, body, (half, S // tm),
        [
            pl.BlockSpec((tm, D), lambda d, i: (i, 0)),
            pl.BlockSpec((None, None, nsh, D), lambda d, i: (d, layer, 0, 0)),
            pl.BlockSpec((None, None, nsh, D), lambda d, i: (d + half, layer, 0, 0)),
            pl.BlockSpec((1, nsh), lambda d, i: (0, d)),
            pl.BlockSpec((1, nsh), lambda d, i: (0, d + half)),
        ],
        [out_spec, out_spec, out_spec],
        [_sds((S, D), BF16), _sds((S, D), BF16), _sds((S, D), F32)],
        (hb, w, w, b_in, b_in), comm=comm,
    )


SUBLANES = 8


def _fill_shifted(sh_ref, ext_ref, rows):
    ext_ref[rows:, :] = jnp.zeros((SUBLANES, ext_ref.shape[1]), F32)
    for s in range(SUBLANES):
        sh_ref[s] = ext_ref[pl.ds(s, rows), :]


def _shifted_window(sh_ref, row, lane0):
    s = row % SUBLANES
    return sh_ref[s, pl.ds(row - s, CONV_SUB), pl.ds(lane0, CONV_LANES)]


def _conv_taps(sh_ref, w_ref, row0, lane0, offset, reverse):
    acc = None
    for k in range(CONV_WIDTH):
        off = offset - k if reverse else offset + k
        term = w_ref[k:k + 1, lane0:lane0 + CONV_LANES] * _shifted_window(sh_ref, row0 + off, lane0)
        acc = term if acc is None else acc + term
    return acc


def _dwconv_call(name, u1, w_dw, b_dw, ln_g, ln_b, comm=None):
    S, D = u1.shape
    tc = _tile(S, CONV_TILE)
    hb = tc // CONV_HALO
    lanes = min(D, CONV_LANES)
    assert lanes == CONV_LANES and D % CONV_LANES == 0 and tc % CONV_SUB == 0

    def body(cur_ref, halo_ref, w_ref, b_ref, g_ref, bb_ref, u2_ref, u4_ref, ext_ref, sh_ref):
        i = pl.program_id(0)
        ext_ref[0:CONV_HALO, :] = jnp.where(i > 0, halo_ref[...], 0.0)
        ext_ref[CONV_HALO:tc + CONV_HALO, :] = cur_ref[...]
        _fill_shifted(sh_ref, ext_ref, tc + CONV_HALO)
        first_tap = CONV_HALO - (CONV_WIDTH - 1)
        for r in range(0, tc, CONV_SUB):
            for c in range(0, D, CONV_LANES):
                u2_ref[r:r + CONV_SUB, c:c + CONV_LANES] = (
                    _conv_taps(sh_ref, w_ref, r, c, first_tap, False) + b_ref[:, c:c + CONV_LANES]
                )
        u2 = u2_ref[...]
        mu = jnp.mean(u2, axis=-1, keepdims=True)
        xc = u2 - mu
        u3 = xc * lax.rsqrt(jnp.mean(xc * xc, axis=-1, keepdims=True) + EPS) * g_ref[...] + bb_ref[...]
        u4_ref[...] = (u3 * _sig(u3)).astype(BF16)

    vec = pl.BlockSpec((1, D), lambda i: (0, 0))
    row = pl.BlockSpec((tc, D), lambda i: (i, 0))
    return _call(
        name, body, (S // tc,),
        [
            row,
            pl.BlockSpec((CONV_HALO, D), lambda i: (jnp.maximum(i * hb - 1, 0), 0)),
            pl.BlockSpec((CONV_WIDTH, D), lambda i: (0, 0)),
            vec, vec, vec,
        ],
        [row, row],
        [_sds((S, D), F32), _sds((S, D), BF16)],
        (u1, u1, w_dw, b_dw, ln_g, ln_b),
        scratch=[pltpu.VMEM((tc + CONV_HALO + SUBLANES, D), F32), pltpu.VMEM((SUBLANES, tc + CONV_HALO, D), F32)], comm=comm,
    )


def _full_weight_spec(w, layer):
    _, _, ksh, D = w.shape
    return pl.BlockSpec((N_DEV, None, ksh, D), lambda i: (0, layer, 0, 0), pipeline_mode=pl.Buffered(1))


def _fullmm_res_call(name, a, w, layer, x_res, bias, g_next):
    S, D = x_res.shape
    K = a.shape[1]
    tm = _tile(S, FULL_ROW_TILE)
    has_b = bias is not None

    def body(*refs):
        a_ref, w_ref, x_ref = refs[:3]
        b_ref = refs[3] if has_b else None
        g_ref, xo_ref, ho_ref = refs[3 + int(has_b):]
        xn = x_ref[...] + _dot(a_ref[...], w_ref[...].reshape(K, D), "nn")
        if has_b:
            xn = xn + b_ref[...]
        xo_ref[...] = xn
        ho_ref[...] = _rms(xn, g_ref[...]).astype(BF16)

    row = pl.BlockSpec((tm, D), lambda i: (i, 0))
    vec = pl.BlockSpec((1, D), lambda i: (0, 0))
    in_specs = [pl.BlockSpec((tm, K), lambda i: (i, 0)), _full_weight_spec(w, layer), row] + ([vec] if has_b else []) + [vec]
    ins = [a, w, x_res] + ([bias] if has_b else []) + [g_next]
    return _call(name, body, (S // tm,), in_specs, [row, row], [_sds((S, D), F32), _sds((S, D), BF16)], ins)


SHARD_PAIR = 2


def _pair_spec(w, layer):
    _, _, fsh, D = w.shape
    return pl.BlockSpec((SHARD_PAIR, None, fsh, D), lambda d, i: (d, layer, 0, 0))


def _ffn_up_call(name, hb, wgt, wut, layer, comm=None):
    S, D = hb.shape
    fsh = wgt.shape[2]
    tn = SHARD_PAIR * fsh
    tm = _tile(S, ROW_TILE)

    def body(h_ref, wg_ref, wu_ref, g_ref, u_ref, a_ref):
        h = h_ref[...]
        g = _dot(h, wg_ref[...].reshape(tn, D), "nt")
        u = _dot(h, wu_ref[...].reshape(tn, D), "nt")
        g_ref[...] = g.astype(BF16)
        u_ref[...] = u.astype(BF16)
        a_ref[...] = (g * _sig(g) * u).astype(BF16)

    ospec = pl.BlockSpec((tm, tn), lambda d, i: (i, d))
    return _call(
        name, body, (N_DEV // SHARD_PAIR, S // tm),
        [pl.BlockSpec((tm, D), lambda d, i: (i, 0)), _pair_spec(wgt, layer), _pair_spec(wut, layer)],
        [ospec] * 3, [_sds((S, N_DEV * fsh), BF16)] * 3, (hb, wgt, wut), comm=comm,
    )


def _mm_rows_call(name, a, w, layer, add, comm=None):
    S, K = a.shape
    _, _, ksh, N = w.shape
    assert K == N_DEV * ksh
    tm = _tile(S, ROW_TILE)
    tn = _tile(N, COL_TILE)
    has_add = add is not None

    def body(*refs):
        a_ref, w_ref = refs[:2]
        o_ref = refs[-1]
        y = _dot(a_ref[...], w_ref[...].reshape(K, tn), "nn")
        o_ref[...] = y + refs[2][...] if has_add else y

    tile = pl.BlockSpec((tm, tn), lambda i, n: (i, n))
    in_specs = [pl.BlockSpec((tm, K), lambda i, n: (i, 0)), pl.BlockSpec((N_DEV, None, ksh, tn), lambda i, n: (0, layer, 0, n))]
    return _call(
        name, body, (S // tm, N // tn), in_specs + ([tile] if has_add else []), tile, _sds((S, N), F32),
        (a, w) + ((add,) if has_add else ()), comm=comm, vmem=BIG_VMEM_LIMIT_BYTES,
    )


def _res_rms_call(name, y, x_res, g_next):
    S, D = x_res.shape
    ts = _tile(S, ELT_TILE)

    def body(y_ref, x_ref, g_ref, xo_ref, ho_ref):
        xn = x_ref[...] + y_ref[...]
        xo_ref[...] = xn
        ho_ref[...] = _rms(xn, g_ref[...]).astype(BF16)

    row = pl.BlockSpec((ts, D), lambda i: (i, 0))
    return _call(
        name, body, (S // ts,), [row, row, pl.BlockSpec((1, D), lambda i: (0, 0))],
        [row, row], [_sds((S, D), F32), _sds((S, D), BF16)], (y, x_res, g_next),
    )


def _colmm_call(name, a, w, layer, out_dtype, turned=False):
    S, K = a.shape
    nsh = w.shape[2] if turned else w.shape[3]
    tm = _tile(S, ROW_TILE)
    mode = "nt" if turned else "nn"

    def body(a_ref, w_ref, o_ref):
        o_ref[...] = _dot(a_ref[...], w_ref[...], mode).astype(out_dtype)

    return _call(
        name, body, (N_DEV, S // tm),
        [pl.BlockSpec((tm, K), lambda d, i: (i, 0)), pl.BlockSpec((None, None, *w.shape[2:]), lambda d, i: (d, layer, 0, 0))],
        pl.BlockSpec((tm, nsh), lambda d, i: (i, d)), _sds((S, N_DEV * nsh), out_dtype), (a, w),
    )


def _ple_gate_call(name, hb, w, layer, x_res, bias, pp, g_next):
    S, D = x_res.shape
    K = hb.shape[1]
    tm = _tile(S, FULL_ROW_TILE)
    has_g = g_next is not None

    def body(*refs):
        a_ref, w_ref, x_ref, b_ref, p_ref = refs[:5]
        g_ref = refs[5] if has_g else None
        outs = refs[5 + int(has_g):]
        gate = _sig(_dot(a_ref[...], w_ref[...].reshape(K, D), "nn") + b_ref[...])
        xn = x_ref[...] + gate * p_ref[...].astype(F32)
        outs[0][...] = xn
        outs[1][...] = gate.astype(BF16)
        if has_g:
            outs[2][...] = _rms(xn, g_ref[...]).astype(BF16)

    row = pl.BlockSpec((tm, D), lambda i: (i, 0))
    vec = pl.BlockSpec((1, D), lambda i: (0, 0))
    in_specs = [pl.BlockSpec((tm, K), lambda i: (i, 0)), _full_weight_spec(w, layer), row, vec, row]
    ins = [hb, w, x_res, bias, pp]
    out_specs, out_shape = [row, row], [_sds((S, D), F32), _sds((S, D), BF16)]
    if has_g:
        in_specs.append(vec)
        ins.append(g_next)
        out_specs.append(row)
        out_shape.append(_sds((S, D), BF16))
    res = _call(name, body, (S // tm,), in_specs, out_specs, out_shape, ins)
    return res if has_g else (res[0], res[1], None)


def _pool_mix_call(name, x, g):
    S, D = x.shape
    ts = _tile(S, ELT_TILE)
    hb = ts // POOL_HALO
    gc = D // len(POOL_WINDOWS)

    def body(cur_ref, halo_ref, g_ref, o_ref, ext_ref):
        i = pl.program_id(0)
        gain = g_ref[...]
        ext_ref[0:POOL_HALO, :] = jnp.where(i > 0, _rms(halo_ref[...], gain), 0.0)
        ext_ref[POOL_HALO:, :] = _rms(cur_ref[...], gain)
        t = i * ts + lax.broadcasted_iota(jnp.int32, (ts, 1), 0)
        for gi, win in enumerate(POOL_WINDOWS):
            lanes = pl.ds(gi * gc, gc)
            h = ext_ref[pl.ds(POOL_HALO, ts), lanes]
            acc = h
            for j in range(1, win):
                acc = acc + ext_ref[pl.ds(POOL_HALO - j, ts), lanes]
            cnt = jnp.minimum(t + 1, win).astype(F32)
            o_ref[:, gi * gc:(gi + 1) * gc] = (acc / cnt - h).astype(BF16)

    row = pl.BlockSpec((ts, D), lambda i: (i, 0))
    return _call(
        name, body, (S // ts,),
        [row, pl.BlockSpec((POOL_HALO, D), lambda i: (jnp.maximum(i * hb - 1, 0), 0)), pl.BlockSpec((1, D), lambda i: (0, 0))],
        row, _sds((S, D), BF16), (x, x, g), scratch=[pltpu.VMEM((ts + POOL_HALO, D), F32)],
    )


def _pool_out_call(name, mix, wp, scale, x_res, g_next):
    S, D = x_res.shape
    ng, gc, _ = wp.shape
    tm = _tile(S, FULL_ROW_TILE)

    def body(m_ref, w_ref, s_ref, x_ref, g_ref, xo_ref, y_ref, h_ref):
        parts = [_dot(m_ref[:, gi * gc:(gi + 1) * gc], w_ref[gi], "nn") for gi in range(ng)]
        ypre = jnp.concatenate(parts, axis=-1)
        y_ref[...] = ypre.astype(BF16)
        xn = x_ref[...] + ypre * s_ref[...]
        xo_ref[...] = xn
        h_ref[...] = _rms(xn, g_ref[...]).astype(BF16)

    row = pl.BlockSpec((tm, D), lambda i: (i, 0))
    vec = pl.BlockSpec((1, D), lambda i: (0, 0))
    return _call(
        name, body, (S // tm,),
        [row, pl.BlockSpec((ng, gc, gc), lambda i: (0, 0, 0)), vec, row, vec],
        [row, row, row], [_sds((S, D), F32), _sds((S, D), BF16), _sds((S, D), BF16)],
        (mix, wp, scale, x_res, g_next),
    )


KV_STEP = 2


def _attn_probs(q, kprev, kcur, qg, kg, bias, sink, n):
    grp = q.shape[0]
    qn = _rms(q, qg)
    k = jnp.concatenate([kprev, kcur], axis=0)
    kn = _rms(k, kg)
    s = _dot(qn.reshape(grp * Q_BLOCK, HEAD_DIM).astype(BF16), kn.astype(BF16), "nt") * (HEAD_DIM ** -0.5)
    s = s.reshape(grp, Q_BLOCK, 2 * Q_BLOCK) + bias
    qi = lax.broadcasted_iota(jnp.int32, (Q_BLOCK, 2 * Q_BLOCK), 0)
    kj = lax.broadcasted_iota(jnp.int32, (Q_BLOCK, 2 * Q_BLOCK), 1)
    qc = qi // CHUNK
    kc = kj // CHUNK - Q_BLOCK // CHUNK
    ok = (kc <= qc) & (kc >= qc - WINDOW_CHUNKS) & ((n > 0) | (kj >= Q_BLOCK))
    s = jnp.where(ok[None], s, NEG_INF)
    m = jnp.maximum(jnp.max(s, axis=-1, keepdims=True), sink)
    e = jnp.exp(s - m)
    es = jnp.exp(sink - m)
    inv = 1.0 / (jnp.sum(e, axis=-1, keepdims=True) + es)
    return q, qn, k, kn, e * inv, es * inv


def _attn_specs(grp, nb):
    heads = KV_STEP * grp
    qspec = pl.BlockSpec((heads, Q_BLOCK, HEAD_DIM), lambda j, n: (j, jnp.minimum(n, nb - 1), 0))
    prev = pl.BlockSpec((KV_STEP, Q_BLOCK, HEAD_DIM), lambda j, n: (j, jnp.maximum(n - 1, 0), 0))
    cur = pl.BlockSpec((KV_STEP, Q_BLOCK, HEAD_DIM), lambda j, n: (j, jnp.minimum(n, nb - 1), 0))
    gain = pl.BlockSpec((1, HEAD_DIM), lambda j, n: (0, 0))
    bias = pl.BlockSpec((heads, Q_BLOCK, 2 * Q_BLOCK), lambda j, n: (j, 0, 0))
    sink = pl.BlockSpec((heads, 1, 1), lambda j, n: (j, 0, 0))
    return qspec, prev, cur, gain, bias, sink


def _attn_fwd_call(name, q, k, v, qg, kg, bias, sinks):
    H, S, _ = q.shape
    n_kv = k.shape[0]
    grp = H // n_kv
    nb = S // Q_BLOCK
    assert n_kv % KV_STEP == 0

    def body(q_ref, kp_ref, kc_ref, vp_ref, vc_ref, qg_ref, kg_ref, bias_ref, sink_ref, o_ref):
        n = pl.program_id(1)
        for hh in range(KV_STEP):
            hs = slice(hh * grp, (hh + 1) * grp)
            _, _, _, _, p, _ = _attn_probs(q_ref[hs], kp_ref[hh], kc_ref[hh], qg_ref[...], kg_ref[...], bias_ref[hs], sink_ref[hs], n)
            vv = jnp.concatenate([vp_ref[hh], vc_ref[hh]], axis=0).astype(BF16)
            o = _dot(p.reshape(grp * Q_BLOCK, 2 * Q_BLOCK).astype(BF16), vv, "nn")
            o_ref[hs] = o.reshape(grp, Q_BLOCK, HEAD_DIM).astype(BF16)

    qspec, prev, cur, gain, bspec, sspec = _attn_specs(grp, nb)
    return _call(
        name, body, (n_kv // KV_STEP, nb),
        [qspec, prev, cur, prev, cur, gain, gain, bspec, sspec],
        qspec, _sds((H, S, HEAD_DIM), BF16), (q, k, k, v, v, qg, kg, bias, sinks),
    )


def _loss_call(name, y, target):
    S, D = y.shape
    ts = _tile(S, ELT_TILE)

    def body(y_ref, t_ref, d_ref, db_ref, l_ref):
        err = y_ref[...] - t_ref[...]
        dy = err * (1.0 / D)
        d_ref[...] = dy
        db_ref[...] = dy.astype(BF16)
        part = 0.5 * jnp.sum(jnp.sum(err * err, axis=-1, keepdims=True), axis=0, keepdims=True) * (1.0 / D)
        _accumulate(l_ref, jnp.broadcast_to(part, l_ref.shape), pl.program_id(0) == 0)

    row = pl.BlockSpec((ts, D), lambda i: (i, 0))
    return _call(
        name, body, (S // ts,), [row, row],
        [row, row, pl.BlockSpec((8, 128), lambda i: (0, 0))],
        [_sds((S, D), F32), _sds((S, D), BF16), _sds((8, 128), F32)], (y, target),
    )


def _ple_bwd_elt_call(name, dx, gate, pp):
    S, D = dx.shape
    ts = _tile(S, ELT_TILE)

    def body(dx_ref, gt_ref, p_ref, dz_ref, dp_ref, db_ref):
        d = dx_ref[...]
        gt = gt_ref[...].astype(F32)
        dz = d * p_ref[...].astype(F32) * gt * (1.0 - gt)
        dz_ref[...] = dz.astype(BF16)
        dp_ref[...] = (d * gt).astype(BF16)
        _accumulate(db_ref, jnp.sum(dz, axis=0, keepdims=True), pl.program_id(0) == 0)

    row = pl.BlockSpec((ts, D), lambda i: (i, 0))
    return _call(
        name, body, (S // ts,), [row, row, row],
        [row, row, pl.BlockSpec((1, D), lambda i: (0, 0))],
        [_sds((S, D), BF16), _sds((S, D), BF16), _sds((1, D), F32)], (dx, gate, pp),
    )


def _grad_w_call(name, a, a_mode, b, b_mode, comm=None):
    bs = b if isinstance(b, (list, tuple)) else [b]
    S = a.shape[-2]
    tk = _tile(S, K_TILE)
    nk = S // tk

    def spec(arr, mode):
        if mode == "full":
            c = arr.shape[-1]
            return pl.BlockSpec((tk, c), lambda d, k: (k, 0)), c
        if mode == "nat":
            c = arr.shape[-1] // N_DEV
            return pl.BlockSpec((tk, c), lambda d, k: (k, d)), c
        c = arr.shape[-1]
        return pl.BlockSpec((None, tk, c), lambda d, k: (d, k, 0)), c

    a_spec, ca = spec(a, a_mode)
    b_specs, cbs = zip(*[spec(x, b_mode) for x in bs])
    nb = len(bs)

    def body(*refs):
        a_ref = refs[0]
        b_refs = refs[1:1 + nb]
        o_refs = refs[1 + nb:1 + 2 * nb]
        acc_refs = refs[1 + 2 * nb:]
        k = pl.program_id(1)
        av = a_ref[...]

        def finish():
            for o, acc in zip(o_refs, acc_refs):
                o[...] = acc[...].astype(BF16)

        _kloop(k, nk, acc_refs, [_dot(av, br[...], "tn") for br in b_refs], finish)

    res = _call(
        name, body, (N_DEV, nk), [a_spec, *b_specs],
        [pl.BlockSpec((None, ca, cb), lambda d, k: (d, 0, 0)) for cb in cbs],
        [_sds((N_DEV, ca, cb), BF16) for cb in cbs], (a, *bs),
        scratch=[pltpu.VMEM((ca, cb), F32) for cb in cbs], comm=comm,
    )
    own, landed = res if comm is not None else (res, None)
    own = own if isinstance(b, (list, tuple)) else own[0]
    return own if comm is None else (own, landed)


def _dx_full_call(name, dy, w, layer, out_dtype):
    S, D = dy.shape
    K = N_DEV * w.shape[2]
    tm = _tile(S, ROW_TILE)

    def body(dy_ref, w_ref, o_ref):
        o_ref[...] = _dot(dy_ref[...], w_ref[...].reshape(K, D), "nt").astype(out_dtype)

    return _call(
        name, body, (S // tm,), [pl.BlockSpec((tm, D), lambda i: (i, 0)), _full_weight_spec(w, layer)],
        pl.BlockSpec((tm, K), lambda i: (i, 0)), _sds((S, K), out_dtype), (dy, w),
    )


def _dx_full_rms_call(name, dy, w, layer, x, g, dres):
    S, D = x.shape
    K = N_DEV * w.shape[2]
    assert K == D
    tm = _tile(S, FULL_ROW_TILE)

    def body(dy_ref, w_ref, x_ref, g_ref, r_ref, dx_ref, dxb_ref, dg_ref):
        dh, dg = _rms_bwd(_dot(dy_ref[...], w_ref[...].reshape(K, dy_ref.shape[1]), "nt"), x_ref[...], g_ref[...])
        dx = r_ref[...] + dh
        dx_ref[...] = dx
        dxb_ref[...] = dx.astype(BF16)
        _accumulate(dg_ref, dg, pl.program_id(0) == 0)

    row = pl.BlockSpec((tm, D), lambda i: (i, 0))
    vec = pl.BlockSpec((1, D), lambda i: (0, 0))
    return _call(
        name, body, (S // tm,), [pl.BlockSpec((tm, dy.shape[1]), lambda i: (i, 0)), _full_weight_spec(w, layer), row, vec, row],
        [row, row, vec], [_sds((S, D), F32), _sds((S, D), BF16), _sds((1, D), F32)], (dy, w, x, g, dres),
    )


def _ffn_bwd_hidden_call(name, dyb, w, layer, gpre, upre, comm=None):
    S, D = dyb.shape
    fsh = w.shape[2]
    tn = SHARD_PAIR * fsh
    tm = _tile(S, ROW_TILE)

    def body(dy_ref, w_ref, g_ref, u_ref, dg_ref, du_ref):
        w = w_ref[...].reshape(tn, D)
        half = max(tm // 2, 8)
        for r in range(0, tm, half):
            rows = slice(r, r + half)
            da = _dot(dy_ref[rows, :], w, "nt")
            g = g_ref[rows, :].astype(F32)
            u = u_ref[rows, :].astype(F32)
            s = _sig(g)
            dg_ref[rows, :] = (da * u * s * (1.0 + g * (1.0 - s))).astype(BF16)
            du_ref[rows, :] = (da * g * s).astype(BF16)

    cspec = pl.BlockSpec((tm, tn), lambda d, i: (i, d))
    return _call(
        name, body, (N_DEV // SHARD_PAIR, S // tm),
        [pl.BlockSpec((tm, D), lambda d, i: (i, 0)), _pair_spec(w, layer), cspec, cspec],
        [cspec, cspec], [_sds((S, N_DEV * fsh), BF16)] * 2, (dyb, w, gpre, upre), comm=comm,
    )


def _grad_rows_call(name, a, b, comm=None):
    S, F = a.shape
    N = b.shape[1]
    fsh = F // N_DEV
    tr = SHARD_PAIR * fsh
    tn = _tile(N, GRAD_COL_TILE)
    tk = _tile(S, K_TILE)
    nk = S // tk

    def body(a_ref, b_ref, o_ref, acc_ref):
        def finish():
            o_ref[...] = acc_ref[...].astype(BF16).reshape(SHARD_PAIR, fsh, tn)

        _kloop(pl.program_id(2), nk, [acc_ref], [_dot(a_ref[...], b_ref[...], "tn")], finish)

    res = _call(
        name, body, (N_DEV // SHARD_PAIR, N // tn, nk),
        [pl.BlockSpec((tk, tr), lambda d, n, k: (k, d)), pl.BlockSpec((tk, tn), lambda d, n, k: (k, n))],
        pl.BlockSpec((SHARD_PAIR, fsh, tn), lambda d, n, k: (d, 0, n)), _sds((N_DEV, fsh, N), BF16), (a, b),
        scratch=[pltpu.VMEM((tr, tn), F32)], comm=comm,
    )
    return res


def _rms_bwd_res_call(name, dh, x, g, dres, want_colsum):
    S, D = x.shape
    ts = _tile(S, ELT_TILE)

    def body(dh_ref, x_ref, g_ref, r_ref, *outs):
        i = pl.program_id(0)
        d, dg = _rms_bwd(dh_ref[...], x_ref[...], g_ref[...])
        dx = r_ref[...] + d
        outs[0][...] = dx
        outs[1][...] = dx.astype(BF16)
        _accumulate(outs[2], dg, i == 0)
        if want_colsum:
            _accumulate(outs[3], jnp.sum(dx, axis=0, keepdims=True), i == 0)

    row = pl.BlockSpec((ts, D), lambda i: (i, 0))
    vec = pl.BlockSpec((1, D), lambda i: (0, 0))
    out_specs = [row, row, vec] + ([vec] if want_colsum else [])
    out_shape = [_sds((S, D), F32), _sds((S, D), BF16), _sds((1, D), F32)] + ([_sds((1, D), F32)] if want_colsum else [])
    res = _call(name, body, (S // ts,), [row, row, vec, row], out_specs, out_shape, (dh, x, g, dres))
    return tuple(res) if want_colsum else (*res, None)


def _conv_out_bwd_call(name, dyb, w, layer, u2, ln_g, ln_b):
    S, D = u2.shape
    ksh = w.shape[2]
    tm = _tile(S, ELT_TILE)

    def body(dy_ref, w_ref, u2_ref, g_ref, b_ref, du2_ref, dg_ref, db_ref, dbdw_ref):
        i = pl.program_id(0)
        dy = dy_ref[...]
        du4 = jnp.concatenate([_dot(dy, w_ref[d], "nt") for d in range(N_DEV)], axis=-1)
        u2 = u2_ref[...]
        mu = jnp.mean(u2, axis=-1, keepdims=True)
        xc = u2 - mu
        r = lax.rsqrt(jnp.mean(xc * xc, axis=-1, keepdims=True) + EPS)
        xh = xc * r
        gain = g_ref[...]
        u3 = xh * gain + b_ref[...]
        s = _sig(u3)
        du3 = du4 * s * (1.0 + u3 * (1.0 - s))
        dxh = du3 * gain
        du2 = r * (dxh - jnp.mean(dxh, axis=-1, keepdims=True) - xh * jnp.mean(dxh * xh, axis=-1, keepdims=True))
        du2_ref[...] = du2
        _accumulate(dg_ref, jnp.sum(du3 * xh, axis=0, keepdims=True), i == 0)
        _accumulate(db_ref, jnp.sum(du3, axis=0, keepdims=True), i == 0)
        _accumulate(dbdw_ref, jnp.sum(du2, axis=0, keepdims=True), i == 0)

    row = pl.BlockSpec((tm, D), lambda i: (i, 0))
    vec = pl.BlockSpec((1, D), lambda i: (0, 0))
    return _call(
        name, body, (S // tm,),
        [row, pl.BlockSpec((N_DEV, None, ksh, D), lambda i: (0, layer, 0, 0)), row, vec, vec],
        [row, vec, vec, vec], [_sds((S, D), F32)] + [_sds((1, D), F32)] * 3, (dyb, w, u2, ln_g, ln_b),
    )


def _dwconv_bwd_call(name, du2, u1, w_dw, a_pre, g_pre):
    S, D = u1.shape
    tc = _tile(S, CONV_TILE)
    hb = tc // CONV_HALO
    n_halo = S // CONV_HALO
    assert D % CONV_LANES == 0 and tc % CONV_SUB == 0
    wrows = CONV_HALO

    def body(d_cur, d_next, u_cur, u_prev, w_ref, a_ref, g_ref, du_ref, dbin_ref, dw_ref, dext_ref, uext_ref, du1_ref, dwacc_ref, dsh_ref, ush_ref):
        i = pl.program_id(0)
        last = S // tc - 1
        dext_ref[0:tc, :] = d_cur[...]
        dext_ref[tc:tc + CONV_HALO, :] = jnp.where(i < last, d_next[...], 0.0)
        uext_ref[0:CONV_HALO, :] = jnp.where(i > 0, u_prev[...], 0.0)
        uext_ref[CONV_HALO:tc + CONV_HALO, :] = u_cur[...]
        _fill_shifted(dsh_ref, dext_ref, tc + CONV_HALO)
        _fill_shifted(ush_ref, uext_ref, tc + CONV_HALO)

        @pl.when(i == 0)
        def _():
            dwacc_ref[...] = jnp.zeros_like(dwacc_ref)

        first_tap = CONV_HALO - (CONV_WIDTH - 1)
        for r in range(0, tc, CONV_SUB):
            for c in range(0, D, CONV_LANES):
                du1_ref[r:r + CONV_SUB, c:c + CONV_LANES] = _conv_taps(dsh_ref, w_ref, r, c, CONV_WIDTH - 1, True)
                dcur = dext_ref[r:r + CONV_SUB, c:c + CONV_LANES]
                for k in range(CONV_WIDTH):
                    prod = dcur * _shifted_window(ush_ref, r + first_tap + k, c)
                    part = prod[0:8]
                    for q in range(8, CONV_SUB, 8):
                        part = part + prod[q:q + 8]
                    dwacc_ref[k, :, c:c + CONV_LANES] += part

        du1 = du1_ref[...]
        a = a_ref[...].astype(F32)
        sg = _sig(g_ref[...].astype(F32))
        da = du1 * sg
        dgate = du1 * a * sg * (1.0 - sg)
        du_ref[:, 0:D] = da.astype(BF16)
        du_ref[:, D:2 * D] = dgate.astype(BF16)
        _accumulate(dbin_ref, jnp.concatenate([jnp.sum(da, axis=0, keepdims=True), jnp.sum(dgate, axis=0, keepdims=True)], axis=-1), i == 0)

        @pl.when(i == last)
        def _():
            for k in range(CONV_WIDTH):
                dw_ref[k:k + 1, :] = jnp.sum(dwacc_ref[k], axis=0, keepdims=True)
            dw_ref[CONV_WIDTH:, :] = jnp.zeros((wrows - CONV_WIDTH, D), F32)

    row = pl.BlockSpec((tc, D), lambda i: (i, 0))
    nxt = pl.BlockSpec((CONV_HALO, D), lambda i: (jnp.minimum((i + 1) * hb, n_halo - 1), 0))
    prv = pl.BlockSpec((CONV_HALO, D), lambda i: (jnp.maximum(i * hb - 1, 0), 0))
    return _call(
        name, body, (S // tc,),
        [row, nxt, row, prv, pl.BlockSpec((CONV_WIDTH, D), lambda i: (0, 0)), row, row],
        [pl.BlockSpec((tc, 2 * D), lambda i: (i, 0)), pl.BlockSpec((1, 2 * D), lambda i: (0, 0)), pl.BlockSpec((wrows, D), lambda i: (0, 0))],
        [_sds((S, 2 * D), BF16), _sds((1, 2 * D), F32), _sds((wrows, D), F32)],
        (du2, du2, u1, u1, w_dw, a_pre, g_pre),
        scratch=[
            pltpu.VMEM((tc + CONV_HALO + SUBLANES, D), F32), pltpu.VMEM((tc + CONV_HALO + SUBLANES, D), F32),
            pltpu.VMEM((tc, D), F32), pltpu.VMEM((CONV_WIDTH, 8, D), F32),
            pltpu.VMEM((SUBLANES, tc + CONV_HALO, D), F32), pltpu.VMEM((SUBLANES, tc + CONV_HALO, D), F32),
        ],
    )


def _pool_out_bwd_call(name, dy, ypre, scale, wp):
    S, D = dy.shape
    ng, gc, _ = wp.shape
    tm = _tile(S, FULL_ROW_TILE)

    def body(dy_ref, y_ref, s_ref, w_ref, dm_ref, dyp_ref, ds_ref):
        dy = dy_ref[...]
        _accumulate(ds_ref, jnp.sum(dy * y_ref[...].astype(F32), axis=0, keepdims=True), pl.program_id(0) == 0)
        dyp = (dy * s_ref[...]).astype(BF16)
        dyp_ref[...] = dyp
        dm_ref[...] = jnp.concatenate([_dot(dyp[:, gi * gc:(gi + 1) * gc], w_ref[gi], "nt") for gi in range(ng)], axis=-1)

    row = pl.BlockSpec((tm, D), lambda i: (i, 0))
    vec = pl.BlockSpec((1, D), lambda i: (0, 0))
    return _call(
        name, body, (S // tm,), [row, row, vec, pl.BlockSpec((ng, gc, gc), lambda i: (0, 0, 0))],
        [row, row, vec], [_sds((S, D), F32), _sds((S, D), BF16), _sds((1, D), F32)], (dy, ypre, scale, wp),
    )


def _pool_w_grad_call(name, mix, dyp, ng):
    S, D = mix.shape
    gc = D // ng
    tk = _tile(S, K_TILE)
    nk = S // tk

    def body(m_ref, d_ref, o_ref, acc_ref):
        def finish():
            o_ref[...] = acc_ref[...]

        _kloop(pl.program_id(1), nk, [acc_ref], [_dot(m_ref[...], d_ref[...], "tn")], finish)

    blk = pl.BlockSpec((tk, gc), lambda g, k: (k, g))
    return _call(
        name, body, (ng, nk), [blk, blk], pl.BlockSpec((None, gc, gc), lambda g, k: (g, 0, 0)),
        _sds((ng, gc, gc), F32), (mix, dyp), scratch=[pltpu.VMEM((gc, gc), F32)],
    )


def _pool_mix_bwd_call(name, dmix, x, g, dres):
    S, D = x.shape
    ts = _tile(S, ELT_TILE)
    hb = ts // POOL_HALO
    n_halo = S // POOL_HALO
    gc = D // len(POOL_WINDOWS)

    def body(cur_ref, nxt_ref, x_ref, g_ref, r_ref, dx_ref, dxb_ref, dg_ref, ext_ref, dh_ref):
        i = pl.program_id(0)
        last = S // ts - 1
        t = i * ts + lax.broadcasted_iota(jnp.int32, (ts + POOL_HALO, 1), 0)
        for gi, win in enumerate(POOL_WINDOWS):
            lanes = slice(gi * gc, (gi + 1) * gc)
            cnt = jnp.minimum(t + 1, win).astype(F32)
            ext_ref[0:ts, lanes] = cur_ref[:, lanes] / cnt[0:ts]
            ext_ref[ts:, lanes] = jnp.where(i < last, nxt_ref[:, lanes] / cnt[ts:], 0.0)
        for gi, win in enumerate(POOL_WINDOWS):
            lanes = pl.ds(gi * gc, gc)
            acc = ext_ref[pl.ds(0, ts), lanes]
            for j in range(1, win):
                acc = acc + ext_ref[pl.ds(j, ts), lanes]
            dh_ref[:, gi * gc:(gi + 1) * gc] = acc - cur_ref[:, gi * gc:(gi + 1) * gc]
        d, dg = _rms_bwd(dh_ref[...], x_ref[...], g_ref[...])
        dx = r_ref[...] + d
        dx_ref[...] = dx
        dxb_ref[...] = dx.astype(BF16)
        _accumulate(dg_ref, dg, i == 0)

    row = pl.BlockSpec((ts, D), lambda i: (i, 0))
    vec = pl.BlockSpec((1, D), lambda i: (0, 0))
    nxt = pl.BlockSpec((POOL_HALO, D), lambda i: (jnp.minimum((i + 1) * hb, n_halo - 1), 0))
    return _call(
        name, body, (S // ts,), [row, nxt, row, vec, row], [row, row, vec],
        [_sds((S, D), F32), _sds((S, D), BF16), _sds((1, D), F32)], (dmix, dmix, x, g, dres),
        scratch=[pltpu.VMEM((ts + POOL_HALO, D), F32), pltpu.VMEM((ts, D), F32)],
    )


def _attn_bwd_call(name, q, k, v, do, qg, kg, bias, sinks):
    H, S, _ = q.shape
    n_kv = k.shape[0]
    grp = H // n_kv
    nb = S // Q_BLOCK
    scale = HEAD_DIM ** -0.5

    def body(q_ref, kp_ref, kc_ref, vp_ref, vc_ref, do_ref, qg_ref, kg_ref, bias_ref, sink_ref,
             dq_ref, dk_ref, dv_ref, dqg_ref, dkg_ref, dbias_ref, dsink_ref, ck_ref, cv_ref):
        n = pl.program_id(1)

        @pl.when(n == 0)
        def _():
            dqg_ref[...] = jnp.zeros_like(dqg_ref)
            dkg_ref[...] = jnp.zeros_like(dkg_ref)
            dbias_ref[...] = jnp.zeros_like(dbias_ref)
            dsink_ref[...] = jnp.zeros_like(dsink_ref)
            ck_ref[...] = jnp.zeros_like(ck_ref)
            cv_ref[...] = jnp.zeros_like(cv_ref)

        def block_grads(hh):
            hs = slice(hh * grp, (hh + 1) * grp)
            q, qn, _, kn, p, ps = _attn_probs(q_ref[hs], kp_ref[hh], kc_ref[hh], qg_ref[...], kg_ref[...], bias_ref[hs], sink_ref[hs], n)
            rows = grp * Q_BLOCK
            dob = do_ref[hs].reshape(rows, HEAD_DIM).astype(BF16)
            vv = jnp.concatenate([vp_ref[hh], vc_ref[hh]], axis=0).astype(BF16)
            dp = _dot(dob, vv, "nt").reshape(grp, Q_BLOCK, 2 * Q_BLOCK)
            delta = jnp.sum(p * dp, axis=-1, keepdims=True)
            dl = p * (dp - delta)
            dbias_ref[hs] += dl
            dsink_ref[hs] += jnp.sum(-ps * delta, axis=1, keepdims=True)
            dlb = dl.reshape(rows, 2 * Q_BLOCK).astype(BF16)
            dqn = (_dot(dlb, kn.astype(BF16), "nn") * scale).reshape(grp, Q_BLOCK, HEAD_DIM)
            dkn = _dot(dlb, qn.reshape(rows, HEAD_DIM).astype(BF16), "tn") * scale
            dvv = _dot(p.reshape(rows, 2 * Q_BLOCK).astype(BF16), dob, "tn")
            qgain = qg_ref[...]
            r = lax.rsqrt(jnp.mean(q * q, axis=-1, keepdims=True) + EPS)
            qh = q * r
            dqg_ref[hh] += jnp.sum(jnp.sum(dqn * qh, axis=1), axis=0, keepdims=True)
            dqh = dqn * qgain
            dq_ref[hs] = r * (dqh - qh * jnp.mean(dqh * qh, axis=-1, keepdims=True))
            return dkn, dvv

        def finish_prev(hh, dkn_prev, dv_prev):
            dk, dkg = _rms_bwd(dkn_prev, kp_ref[hh], kg_ref[...])
            dk_ref[hh] = dk
            dv_ref[hh] = dv_prev
            dkg_ref[hh] += dkg

        @pl.when(n < nb)
        def _():
            grads = [block_grads(hh) for hh in range(KV_STEP)]

            @pl.when(n > 0)
            def _():
                for hh, (dkn, dvv) in enumerate(grads):
                    finish_prev(hh, ck_ref[hh] + dkn[0:Q_BLOCK], cv_ref[hh] + dvv[0:Q_BLOCK])

            for hh, (dkn, dvv) in enumerate(grads):
                ck_ref[hh] = dkn[Q_BLOCK:]
                cv_ref[hh] = dvv[Q_BLOCK:]

        @pl.when(n == nb)
        def _():
            for hh in range(KV_STEP):
                finish_prev(hh, ck_ref[hh], cv_ref[hh])

    qspec, prev, cur, gain, bspec, sspec = _attn_specs(grp, nb)
    kout = pl.BlockSpec((KV_STEP, Q_BLOCK, HEAD_DIM), lambda j, n: (j, jnp.maximum(n - 1, 0), 0))
    gout = pl.BlockSpec((KV_STEP, 1, HEAD_DIM), lambda j, n: (j, 0, 0))
    return _call(
        name, body, (n_kv // KV_STEP, nb + 1),
        [qspec, prev, cur, prev, cur, qspec, gain, gain, bspec, sspec],
        [qspec, kout, kout, gout, gout, bspec, sspec],
        [
            _sds((H, S, HEAD_DIM), F32), _sds((n_kv, S, HEAD_DIM), F32), _sds((n_kv, S, HEAD_DIM), F32),
            _sds((n_kv, 1, HEAD_DIM), F32), _sds((n_kv, 1, HEAD_DIM), F32),
            _sds((H, Q_BLOCK, 2 * Q_BLOCK), F32), _sds((H, 1, 1), F32),
        ],
        (q, k, k, v, v, do, qg, kg, bias, sinks),
        scratch=[pltpu.VMEM((KV_STEP, Q_BLOCK, HEAD_DIM), F32), pltpu.VMEM((KV_STEP, Q_BLOCK, HEAD_DIM), F32)],
    )


def _bucket_sum_call(name, onehot, dbias):
    nbk, n = onehot.shape
    H = dbias.shape[0]

    def body(o_ref, d_ref, out_ref):
        out_ref[...] = lax.dot_general(o_ref[...], d_ref[...], _DIMS["nt"], precision=lax.Precision.HIGHEST, preferred_element_type=F32)

    return _call(
        name, body, (1,), [pl.BlockSpec((nbk, n), lambda i: (0, 0)), pl.BlockSpec((H, n), lambda i: (0, 0))],
        pl.BlockSpec((nbk, H), lambda i: (0, 0)), _sds((nbk, H), F32), (onehot, dbias),
    )


def _bias_table_call(name, rel_bias, onehot):
    nbk, n = onehot.shape
    H = rel_bias.shape[1]

    def body(r_ref, o_ref, out_ref):
        out_ref[...] = lax.dot_general(r_ref[...], o_ref[...], _DIMS["tn"], precision=lax.Precision.HIGHEST, preferred_element_type=F32)

    return _call(
        name, body, (1,), [pl.BlockSpec((nbk, H), lambda i: (0, 0)), pl.BlockSpec((nbk, n), lambda i: (0, 0))],
        pl.BlockSpec((H, n), lambda i: (0, 0)), _sds((H, n), F32), (rel_bias, onehot),
    )


def _exchange_call(name, comm):
    n_src, n_dst = len(comm.srcs), len(comm.out_shape)

    def body(*refs):
        src_refs, dst_refs, sem_refs = refs[:n_src], refs[n_src:n_src + n_dst], refs[n_src + n_dst:]
        comm.start(src_refs, dst_refs, sem_refs)
        comm.wait(src_refs, dst_refs, sem_refs)

    hbm = pl.BlockSpec(memory_space=pltpu.HBM)
    return pl.pallas_call(
        body, name=name, in_specs=[hbm] * n_src, out_specs=[hbm] * n_dst, out_shape=list(comm.out_shape),
        scratch_shapes=list(comm.sems), compiler_params=pltpu.CompilerParams(has_side_effects=True),
    )(*comm.srcs)


def _gather_forward_call(name, bufs):
    n = len(bufs)

    def body(*refs):
        in_refs, out_refs = refs[:n], refs[n:2 * n]
        send_sems, recv_sems = refs[2 * n:]
        x, y, c = _mesh_pos()
        sibling = (x, y, 1 - c)
        chips = [(1 - x, y), (x, 1 - y), (1 - x, 1 - y)]
        sends, recvs = [], []
        for t in range(n):
            for j, chip in enumerate(chips):
                s = 3 * t + j
                mine, theirs = _slot(*chip, c), _slot(*chip, 1 - c)
                sends.append(_remote(in_refs[t].at[mine], out_refs[t].at[mine], send_sems.at[s], recv_sems.at[s], sibling))
                recvs.append(_remote(in_refs[t].at[mine], out_refs[t].at[theirs], send_sems.at[s], recv_sems.at[s], sibling))
        for cp in sends:
            cp.start()
        for cp in recvs:
            cp.wait_recv()
        for cp in sends:
            cp.wait_send()

    hbm = pl.BlockSpec(memory_space=pltpu.HBM)
    return pl.pallas_call(
        body, name=name, in_specs=[hbm] * n, out_specs=[hbm] * n, out_shape=[_sds(b.shape, b.dtype) for b in bufs],
        scratch_shapes=[pltpu.SemaphoreType.DMA((3 * n,)), pltpu.SemaphoreType.DMA((3 * n,))],
        input_output_aliases={t: t for t in range(n)},
        compiler_params=pltpu.CompilerParams(has_side_effects=True),
    )(*bufs)


def _all_gather_call(name, shards):
    n = len(shards)

    def body(*refs):
        in_refs, out_refs = refs[:n], refs[n:2 * n]
        send_sems, recv_sems, local_sems = refs[2 * n:]
        x, y, c = _mesh_pos()
        me, sibling = (x, y, c), (x, y, 1 - c)
        chips = [(1 - x, y), (x, 1 - y), (1 - x, 1 - y)]

        def copy(t, k, block, to, src=None):
            dst = out_refs[t].at[_slot(*block)]
            return pltpu.make_async_remote_copy(
                src_ref=dst if src is None else src, dst_ref=dst,
                send_sem=send_sems.at[7 * t + k], recv_sem=recv_sems.at[7 * t + k],
                device_id=to, device_id_type=pl.DeviceIdType.MESH,
            )

        mine = [pltpu.make_async_copy(in_refs[t], out_refs[t].at[_slot(*me)], local_sems.at[t]) for t in range(n)]
        for cp in mine:
            cp.start()
        first = []
        for t in range(n):
            first.append(copy(t, 0, me, sibling, src=in_refs[t]))
            first += [copy(t, 1 + j, me, (*chip, c), src=in_refs[t]) for j, chip in enumerate(chips)]
        for cp in first:
            cp.start()
        passed = []
        for j, chip in enumerate(chips):
            for t in range(n):
                copy(t, 1 + j, (*chip, c), me).wait_recv()
                fwd = copy(t, 4 + j, (*chip, c), sibling)
                fwd.start()
                passed.append(fwd)
        for t in range(n):
            copy(t, 0, sibling, me).wait_recv()
            for j, chip in enumerate(chips):
                copy(t, 4 + j, (*chip, 1 - c), me).wait_recv()
        for cp in first + passed:
            cp.wait_send()
        for cp in mine:
            cp.wait()

    hbm = pl.BlockSpec(memory_space=pltpu.HBM)
    return pl.pallas_call(
        body, name=name,
        in_specs=[hbm] * n, out_specs=[hbm] * n,
        out_shape=[_sds((N_DEV, *s.shape), s.dtype) for s in shards],
        scratch_shapes=[pltpu.SemaphoreType.DMA((7 * n,)), pltpu.SemaphoreType.DMA((7 * n,)), pltpu.SemaphoreType.DMA((n,))],
        compiler_params=pltpu.CompilerParams(has_side_effects=True),
    )(*shards)


def _all_reduce_small_call(name, pack):
    R, C = pack.shape

    def body(in_ref, out_ref, land_ref, send_sems, recv_sems):
        x, y, c = _mesh_pos()
        me = _slot(x, y, c)
        peers = _peers(x, y, c)
        land_ref[me] = in_ref[...]
        sends = [
            pltpu.make_async_remote_copy(
                src_ref=in_ref, dst_ref=land_ref.at[me], send_sem=send_sems.at[k], recv_sem=recv_sems.at[k],
                device_id=to, device_id_type=pl.DeviceIdType.MESH,
            )
            for k, to in enumerate(peers)
        ]
        for cp in sends:
            cp.start()
        for k, frm in enumerate(peers):
            pltpu.make_async_remote_copy(
                src_ref=in_ref, dst_ref=land_ref.at[_slot(*frm)], send_sem=send_sems.at[k], recv_sem=recv_sems.at[k],
                device_id=frm, device_id_type=pl.DeviceIdType.MESH,
            ).wait_recv()
        for cp in sends:
            cp.wait_send()
        total = land_ref[0]
        for d in range(1, N_DEV):
            total = total + land_ref[d]
        out_ref[...] = total

    vmem = pl.BlockSpec(memory_space=pltpu.VMEM)
    return pl.pallas_call(
        body, name=name, in_specs=[vmem], out_specs=vmem, out_shape=_sds((R, C), F32),
        scratch_shapes=[pltpu.VMEM((N_DEV, R, C), F32), pltpu.SemaphoreType.DMA((7,)), pltpu.SemaphoreType.DMA((7,))],
        compiler_params=pltpu.CompilerParams(has_side_effects=True, vmem_limit_bytes=VMEM_LIMIT_BYTES),
    )(pack)


def _adamw_call(name, grad, landed, w, m, v, layer, prev):
    L, R, C = w.shape
    tr = _div_tile(R, ADAM_BLOCK_BYTES // (C * 4))
    c1 = 1.0 / (1.0 - ADAM_B1 ** ADAM_STEP)
    c2 = 1.0 / (1.0 - ADAM_B2 ** ADAM_STEP)

    def body(g_ref, w_ref, m_ref, v_ref, *rest):
        go_ref, d_ref, mo_ref, vo_ref = rest[-4:]
        if landed:
            g = g_ref[0].astype(F32)
            for d in range(1, N_DEV):
                g = g + g_ref[d].astype(F32)
        else:
            g = g_ref[...]
        go_ref[...] = g
        mn = ADAM_B1 * m_ref[...] + (1.0 - ADAM_B1) * g
        vn = ADAM_B2 * v_ref[...] + (1.0 - ADAM_B2) * (g * g)
        mo_ref[...] = mn
        vo_ref[...] = vn
        d_ref[...] = -ADAM_LR * ((mn * c1) / (jnp.sqrt(vn * c2) + ADAM_EPS) + ADAM_WD * w_ref[...])

    blk = pl.BlockSpec((None, tr, C), lambda i: (layer, i, 0))
    gspec = pl.BlockSpec((N_DEV, tr, C), lambda i: (0, i, 0)) if landed else pl.BlockSpec((tr, C), lambda i: (i, 0))
    in_specs, ins, aliases = [gspec, blk, blk, blk], [grad, w, m, v], {}
    if prev is not None:
        in_specs += [pl.BlockSpec(memory_space=pl.ANY)] * 4
        ins += list(prev)
        aliases = {4 + q: q for q in range(4)}
    return _call(name, body, (R // tr,), in_specs, [blk] * 4, [_sds((L, R, C), F32)] * 4, ins, aliases=aliases)


def _t5_bucket(rel):
    nb = NUM_BUCKETS // 2
    n = -rel
    ret = jnp.where(n < 0, nb, 0)
    n = jnp.abs(n)
    max_exact = nb // 2
    nf = jnp.maximum(n, 1).astype(jnp.float32)
    large = max_exact + (jnp.log(nf / max_exact) / math.log(REL_MAX_DIST / max_exact) * (nb - max_exact)).astype(jnp.int32)
    large = jnp.minimum(large, nb - 1)
    return ret + jnp.where(n < max_exact, n, large)


def _band_buckets():
    i = jnp.arange(Q_BLOCK)[:, None]
    j = jnp.arange(2 * Q_BLOCK)[None, :]
    return _t5_bucket(j - Q_BLOCK - i)


def _to_heads(t, n_heads):
    S = t.shape[0]
    return t.reshape(S, n_heads, HEAD_DIM).transpose(1, 0, 2)


def _from_heads(t):
    H, S, _ = t.shape
    return t.transpose(1, 0, 2).reshape(S, H * HEAD_DIM)


def _gathered_vec(t):
    nd, L, n = t.shape
    return t.transpose(1, 0, 2).reshape(L, nd * n)


_MIXER_WEIGHTS = {0: ("conv_w_in", "conv_w_out"), 1: (), 2: ("attn_w_qkv", "attn_w_o")}
_FFN_UP_WEIGHTS = ("ffn_w_gate", "ffn_w_up")
_FFN_REST_WEIGHTS = ("ffn_w_down", "ple_w_proj", "ple_w_gate")
_TURNED = _FFN_UP_WEIGHTS + ("conv_w_in", "attn_w_qkv")
_SMALL_SHARDED = ("conv_b_in", "conv_w_dw", "conv_b_dw", "conv_ln_g", "conv_ln_b", "conv_b_out")


def _mixer_keys(i):
    return [(n, i // 3) for n in _MIXER_WEIGHTS[i % 3]]


def _step(x, p, target, wb, small_pack, small_shapes, V):
    S, D = x.shape
    depth = V["norm_mix"].shape[0]
    n_heads = D // HEAD_DIM
    n_kv = (wb["attn_w_qkv"].shape[1] * N_DEV - D) // (2 * HEAD_DIM)
    ng = len(POOL_WINDOWS)
    vec = lambda t, i: t[i][None, :]
    shard = lambda key: wb[key[0]][key[1]:key[1] + 1]

    keys_a0 = [(n, 0) for n in _FFN_UP_WEIGHTS]
    keys_b0 = [(n, 0) for n in _FFN_REST_WEIGHTS]
    first = _mixer_keys(0) + [("pool_w", 0)]
    gathered = _all_gather_call("all_gather0", [shard(k) for k in first] + [small_pack])
    W = dict(zip(first, gathered[:-1]))
    pw = W.pop(("pool_w", 0))
    wp = pw[:, 0].transpose(1, 0, 2, 3).reshape(pw.shape[2], pw.shape[3] * N_DEV, pw.shape[4])
    V = dict(V)
    for n, t in zip(_SMALL_SHARDED, _unpack(gathered[-1], small_shapes, lead=(N_DEV,))):
        if n == "conv_w_dw":
            V[n] = t.transpose(1, 2, 0, 3).reshape(t.shape[1], t.shape[2], -1)
        else:
            V[n] = _gathered_vec(t)

    buckets = _band_buckets()
    onehot = (buckets.reshape(1, -1) == jnp.arange(NUM_BUCKETS)[:, None]).astype(F32)
    bias_tab = _bias_table_call("bias_table", V["rel_bias"], onehot).reshape(n_heads, Q_BLOCK, 2 * Q_BLOCK)
    sinks3 = V["attn_sinks"].reshape(n_heads, 1, 1)
    pb = p.astype(BF16)

    def gather_next(keys):
        return _GatherOwn([shard(k) for k in keys]) if keys else None

    def finish_gather(name, keys, bufs):
        W.update(zip(keys, _gather_forward_call(name, bufs)))

    saved = []
    hb = _rms_call("rms_in", x, vec(V["norm_mix"], 0))
    for i in range(depth):
        kind, j = i % 3, i // 3
        sv = {"x0": x, "h0": hb}
        g_ffn = vec(V["norm_ffn"], i)
        if kind == 0:
            ride = i == 0
            res = _conv_in_call(f"conv_in{i}", hb, W["conv_w_in", j], 0, vec(V["conv_b_in"], j), comm=gather_next(keys_b0) if ride else None)
            if ride:
                res, bufs_b = res
            a_pre, g_pre, u1 = res
            res = _dwconv_call(
                f"dwconv{i}", u1, V["conv_w_dw"][j], vec(V["conv_b_dw"], j), vec(V["conv_ln_g"], j), vec(V["conv_ln_b"], j),
                comm=gather_next(keys_a0) if ride else None)
            if ride:
                res, bufs_a = res
                finish_gather("gather_fwd0", keys_a0 + keys_b0, bufs_a + bufs_b)
            u2, u4 = res
            x1, h2 = _fullmm_res_call(f"conv_out{i}", u4, W["conv_w_out", j], 0, x, vec(V["conv_b_out"], j), g_ffn)
            sv.update(a_pre=a_pre, g_pre=g_pre, u1=u1, u2=u2, u4=u4)
        elif kind == 1:
            mix = _pool_mix_call(f"pool_mix{i}", x, vec(V["norm_mix"], i))
            x1, ypre, h2 = _pool_out_call(f"pool_out{i}", mix, wp, vec(V["pool_scale"], j), x, g_ffn)
            sv.update(mix=mix, ypre=ypre)
        else:
            qkv = _colmm_call(f"qkv{i}", hb, W["attn_w_qkv", j], 0, F32, turned=True)
            q = _to_heads(qkv[:, :D], n_heads)
            k = _to_heads(qkv[:, D:D + n_kv * HEAD_DIM], n_kv)
            v = _to_heads(qkv[:, D + n_kv * HEAD_DIM:], n_kv)
            o = _attn_fwd_call(f"attn{i}", q, k, v, vec(V["attn_q_norm"], j), vec(V["attn_k_norm"], j), bias_tab, sinks3)
            ob = _from_heads(o)
            x1, h2 = _fullmm_res_call(f"attn_out{i}", ob, W["attn_w_o", j], 0, x, None, g_ffn)
            sv.update(q=q, k=k, v=v, ob=ob)
        more = i + 1 < depth
        keys_a = [(n, i + 1) for n in _FFN_UP_WEIGHTS] if more else []
        keys_b = [(n, i + 1) for n in _FFN_REST_WEIGHTS] + _mixer_keys(i + 1) if more else []
        res = _ffn_up_call(f"ffn_up{i}", h2, W["ffn_w_gate", i], W["ffn_w_up", i], 0, comm=gather_next(keys_a))
        if more:
            (gpre, upre, act), bufs_a = res
        else:
            gpre, upre, act = res
        res = _mm_rows_call(f"ffn_down{i}", act, W["ffn_w_down", i], 0, None, comm=gather_next(keys_b))
        if more:
            y, bufs_b = res
            finish_gather(f"gather_fwd{i + 1}", keys_a + keys_b, bufs_a + bufs_b)
        else:
            y = res
        x2, h3 = _res_rms_call(f"ffn_res{i}", y, x1, vec(V["norm_ple"], i))
        pp = _colmm_call(f"ple_proj{i}", pb[i], W["ple_w_proj", i], 0, BF16)
        g_next = vec(V["norm_mix"], i + 1) if more else None
        x3, gate, hb_next = _ple_gate_call(f"ple_gate{i}", h3, W["ple_w_gate", i], 0, x2, vec(V["ple_b_gate"], i), pp, g_next)
        sv.update(x1=x1, h2=h2, gpre=gpre, upre=upre, act=act, x2=x2, h3=h3, pp=pp, gate=gate)
        saved.append(sv)
        x, hb = x3, hb_next

    dx, dxb, loss_tile = _loss_call("loss", x, target)
    loss = loss_tile[0, 0]

    landed = {}
    GV = {n: [None] * V[n].shape[0] for n in ("norm_mix", "norm_ffn", "norm_ple", "ple_b_gate", "conv_b_in", "conv_w_dw", "conv_b_dw", "conv_ln_g", "conv_ln_b", "conv_b_out", "pool_scale")}

    def scatter_of(pending):
        return _ScatterSlots([g for _, g in pending]) if pending else None

    def record(pending, bufs):
        landed.update(zip([k for k, _ in pending], bufs))

    mixer_pending = []
    for i in reversed(range(depth)):
        kind, j = i % 3, i // 3
        sv = saved[i]
        dz, dpp, db_gate = _ple_bwd_elt_call(f"ple_bwd{i}", dx, sv["gate"], sv["pp"])
        GV["ple_b_gate"][i] = db_gate
        pending = [(("ple_w_gate", i), _grad_w_call(f"g_ple_gate{i}", sv["h3"], "nat", dz, "full"))]
        pending.append((("ple_w_proj", i), _grad_w_call(f"g_ple_proj{i}", pb[i], "full", dpp, "nat")))
        dx, dxb, dg = _dx_full_rms_call(f"d_x2_{i}", dz, W["ple_w_gate", i], 0, sv["x2"], vec(V["norm_ple"], i), dx)
        GV["norm_ple"][i] = dg
        res = _ffn_bwd_hidden_call(f"ffn_bwd{i}", dxb, W["ffn_w_down", i], 0, sv["gpre"], sv["upre"], comm=scatter_of(mixer_pending))
        if mixer_pending:
            (dgp, dup), bufs = res
            record(mixer_pending, bufs)
        else:
            dgp, dup = res
        g_down = _grad_rows_call(f"g_ffn_down{i}", sv["act"], dxb)
        g_gate, bufs = _grad_rows_call(f"g_ffn_gate{i}", dgp, sv["h2"], comm=scatter_of(pending))
        record(pending, bufs)
        pending = [(("ffn_w_down", i), g_down)]
        g_up, bufs = _grad_rows_call(f"g_ffn_up{i}", dup, sv["h2"], comm=scatter_of(pending))
        record(pending, bufs)
        pending = [(("ffn_w_gate", i), g_gate)]
        dh, bufs = _mm_rows_call(f"d_h2g_{i}", dgp, W["ffn_w_gate", i], 0, None, comm=scatter_of(pending))
        record(pending, bufs)
        pending = [(("ffn_w_up", i), g_up)]
        dh, bufs = _mm_rows_call(f"d_h2_{i}", dup, W["ffn_w_up", i], 0, dh, comm=scatter_of(pending))
        record(pending, bufs)
        dx, dxb, dg, colsum = _rms_bwd_res_call(f"d_x1_{i}", dh, sv["x1"], vec(V["norm_ffn"], i), dx, kind == 0)
        GV["norm_ffn"][i] = dg
        g_mix = vec(V["norm_mix"], i)
        if kind == 0:
            GV["conv_b_out"][j] = colsum
            g_out = _grad_w_call(f"g_conv_out{i}", sv["u4"], "nat", dxb, "full")
            du2, d_ln_g, d_ln_b, d_b_dw = _conv_out_bwd_call(f"conv_out_bwd{i}", dxb, W["conv_w_out", j], 0, sv["u2"], vec(V["conv_ln_g"], j), vec(V["conv_ln_b"], j))
            du, d_b_in, d_w_dw = _dwconv_bwd_call(f"dwconv_bwd{i}", du2, sv["u1"], V["conv_w_dw"][j], sv["a_pre"], sv["g_pre"])
            GV["conv_ln_g"][j], GV["conv_ln_b"][j], GV["conv_b_dw"][j] = d_ln_g, d_ln_b, d_b_dw
            GV["conv_b_in"][j], GV["conv_w_dw"][j] = d_b_in, d_w_dw[:CONV_WIDTH]
            g_in = _grad_rows_call(f"g_conv_in{i}", du, sv["h0"])
            mixer_pending = [(("conv_w_in", j), g_in), (("conv_w_out", j), g_out)]
            dh = _mm_rows_call(f"d_h0_{i}", du, W["conv_w_in", j], 0, None, comm=scatter_of(mixer_pending) if i == 0 else None)
            if i == 0:
                dh, bufs = dh
                record(mixer_pending, bufs)
                mixer_pending = []
            dx, dxb, dg, _ = _rms_bwd_res_call(f"d_x0_{i}", dh, sv["x0"], g_mix, dx, False)
        elif kind == 1:
            dmix, dyp, d_scale = _pool_out_bwd_call(f"pool_out_bwd{i}", dx, sv["ypre"], vec(V["pool_scale"], j), wp)
            GV["pool_scale"][j] = d_scale
            g_pool = _pool_w_grad_call(f"g_pool_w{i}", sv["mix"], dyp, ng)
            gc = g_pool.shape[1]
            g_pool = g_pool.reshape(ng, N_DEV, gc // N_DEV, gc).transpose(1, 0, 2, 3).reshape(N_DEV, ng * (gc // N_DEV), gc)
            mixer_pending = [(("pool_w", j), g_pool.astype(BF16))]
            dx, dxb, dg = _pool_mix_bwd_call(f"pool_mix_bwd{i}", dmix, sv["x0"], g_mix, dx)
        else:
            g_o = _grad_w_call(f"g_attn_o{i}", sv["ob"], "nat", dxb, "full")
            do = _to_heads(_dx_full_call(f"d_attn_o{i}", dxb, W["attn_w_o", j], 0, F32), n_heads)
            dq, dk, dv, dqg, dkg, dbias, dsink = _attn_bwd_call(
                f"attn_bwd{i}", sv["q"], sv["k"], sv["v"], do, vec(V["attn_q_norm"], j), vec(V["attn_k_norm"], j), bias_tab, sinks3)
            GV["attn_q_norm"] = jnp.sum(dqg, axis=0)
            GV["attn_k_norm"] = jnp.sum(dkg, axis=0)
            GV["attn_sinks"] = dsink.reshape(1, n_heads)
            GV["rel_bias"] = _bucket_sum_call(f"g_rel_bias{i}", onehot, dbias.reshape(n_heads, -1))
            dqkv = jnp.concatenate([_from_heads(dq), _from_heads(dk), _from_heads(dv)], axis=-1).astype(BF16)
            g_qkv = _grad_rows_call(f"g_qkv{i}", dqkv, sv["h0"])
            mixer_pending = [(("attn_w_qkv", j), g_qkv), (("attn_w_o", j), g_o)]
            dh = _mm_rows_call(f"d_h0_{i}", dqkv, W["attn_w_qkv", j], 0, None)
            dx, dxb, dg, _ = _rms_bwd_res_call(f"d_x0_{i}", dh, sv["x0"], g_mix, dx, False)
        GV["norm_mix"][i] = dg
    if mixer_pending:
        record(mixer_pending, _exchange_call("grad_scatter_tail", scatter_of(mixer_pending)))
    return loss, dx, landed, GV, V


_BIG = ("conv_w_in", "conv_w_out", "pool_w", "attn_w_qkv", "attn_w_o", "ffn_w_gate", "ffn_w_up", "ffn_w_down", "ple_w_proj", "ple_w_gate")
_SMALL_REPLICATED = ("norm_mix", "norm_ffn", "norm_ple", "pool_scale", "attn_q_norm", "attn_k_norm", "attn_sinks", "rel_bias", "ple_b_gate")
_WEIGHTS = ("norm_mix", "norm_ffn", "norm_ple", "conv_w_in", "conv_b_in", "conv_w_dw", "conv_b_dw", "conv_ln_g", "conv_ln_b", "conv_w_out",
            "conv_b_out", "pool_w", "pool_scale", "attn_w_qkv", "attn_q_norm", "attn_k_norm", "attn_sinks", "attn_w_o", "rel_bias",
            "ffn_w_gate", "ffn_w_up", "ffn_w_down", "ple_w_proj", "ple_w_gate", "ple_b_gate")
PACK_LANES = 128


def _pack(parts):
    flat = jnp.concatenate([t.reshape(-1).astype(F32) for t in parts])
    rows = -(-flat.shape[0] // (8 * PACK_LANES)) * 8
    flat = jnp.pad(flat, (0, rows * PACK_LANES - flat.shape[0]))
    return flat.reshape(rows, PACK_LANES)


def _unpack(pack, shapes, lead=()):
    flat = pack.reshape(*lead, -1)
    out, pos = [], 0
    for s in shapes:
        n = math.prod(s)
        out.append(flat[..., pos:pos + n].reshape(*lead, *s))
        pos += n
    return out


def _as2d(t):
    return t.reshape(-1, t.shape[-1])


def kernel(x, p, norm_mix, norm_ffn, norm_ple, conv_w_in, conv_b_in, conv_w_dw, conv_b_dw, conv_ln_g, conv_ln_b, conv_w_out, conv_b_out, pool_w, pool_scale, attn_w_qkv, attn_q_norm, attn_k_norm, attn_sinks, attn_w_o, rel_bias, ffn_w_gate, ffn_w_up, ffn_w_down, ple_w_proj, ple_w_gate, ple_b_gate, loss_target, m_norm_mix, m_norm_ffn, m_norm_ple, m_conv_w_in, m_conv_b_in, m_conv_w_dw, m_conv_b_dw, m_conv_ln_g, m_conv_ln_b, m_conv_w_out, m_conv_b_out, m_pool_w, m_pool_scale, m_attn_w_qkv, m_attn_q_norm, m_attn_k_norm, m_attn_sinks, m_attn_w_o, m_rel_bias, m_ffn_w_gate, m_ffn_w_up, m_ffn_w_down, m_ple_w_proj, m_ple_w_gate, m_ple_b_gate, v_norm_mix, v_norm_ffn, v_norm_ple, v_conv_w_in, v_conv_b_in, v_conv_w_dw, v_conv_b_dw, v_conv_ln_g, v_conv_ln_b, v_conv_w_out, v_conv_b_out, v_pool_w, v_pool_scale, v_attn_w_qkv, v_attn_q_norm, v_attn_k_norm, v_attn_sinks, v_attn_w_o, v_rel_bias, v_ffn_w_gate, v_ffn_w_up, v_ffn_w_down, v_ple_w_proj, v_ple_w_gate, v_ple_b_gate):
    given = dict(locals())
    w = {n: given[n] for n in _WEIGHTS}
    m = {n: given["m_" + n] for n in _WEIGHTS}
    v = {n: given["v_" + n] for n in _WEIGHTS}
    me = _slot(*_mesh_pos())

    wb = {n: w[n].astype(BF16) for n in _BIG}
    for n in _TURNED:
        wb[n] = wb[n].transpose(0, 2, 1)
    small_pack = _pack([w[n] for n in _SMALL_SHARDED])
    small_shapes = [w[n].shape for n in _SMALL_SHARDED]
    loss, grad_x, landed, GV, V = _step(x[0], p[:, 0], loss_target[0], wb, small_pack, small_shapes, {n: w[n] for n in _SMALL_REPLICATED})
    loss = lax.psum(loss, ("x", "y", "c"))

    small_names = list(_SMALL_REPLICATED) + list(_SMALL_SHARDED)
    small_full = []
    for n in small_names:
        g = GV[n]
        g = jnp.stack([t.reshape(V[n].shape[1:]) for t in g]) if isinstance(g, list) else g.reshape(V[n].shape)
        small_full.append(g)
    reduced = _unpack(_all_reduce_small_call("all_reduce_small", _pack(small_full)), [t.shape for t in small_full])
    small_grad = {}
    for n, g in zip(small_names, reduced):
        if n in _SMALL_SHARDED:
            c = w[n].shape[-1]
            g = lax.dynamic_slice_in_dim(g, me * c, c, axis=g.ndim - 1)
        small_grad[n] = g

    out = {}
    for n in _BIG:
        layers = w[n].shape[0]
        turned = n in _TURNED
        as3d = lambda t: t.transpose(0, 2, 1) if turned else t.reshape(layers, -1, t.shape[-1])
        res = None
        for l in range(layers):
            res = _adamw_call(f"adamw_{n}{l}", landed[n, l], True, as3d(w[n]), as3d(m[n]), as3d(v[n]), l, res)
        out[n] = [t.transpose(0, 2, 1) if turned else t.reshape(w[n].shape) for t in res]
    for n in small_names:
        res = _adamw_call(f"adamw_{n}", _as2d(small_grad[n]), False, _as2d(w[n])[None], _as2d(m[n])[None], _as2d(v[n])[None], 0, None)
        out[n] = [t.reshape(w[n].shape) for t in res]

    grads = [out[n][0] for n in _WEIGHTS]
    deltas = [out[n][1] for n in _WEIGHTS]
    new_m = [out[n][2] for n in _WEIGHTS]
    new_v = [out[n][3] for n in _WEIGHTS]
    return (loss, grad_x[None], *grads, *deltas, *new_m, *new_v)
```

```python
import functools
import math

import jax
import jax.numpy as jnp
from jax import lax
from jax.experimental import pallas as pl
from jax.experimental.pallas import tpu as pltpu

F32, BF16 = jnp.float32, jnp.bfloat16
N_DEV = 8
EPS = 1e-6
NEG_INF = -1e30
HEAD_DIM = 64
Q_BLOCK = 128
CHUNK = 64
WINDOW_CHUNKS = 2
CONV_WIDTH = 31
CONV_HALO = 32
POOL_WINDOWS = (2, 4, 8, 16)
POOL_HALO = 16
NUM_BUCKETS = 32
REL_MAX_DIST = 128
ADAM_LR, ADAM_B1, ADAM_B2, ADAM_EPS, ADAM_WD, ADAM_STEP = 0.001, 0.9, 0.999, 1e-08, 0.01, 10
VMEM_LIMIT_BYTES = 44 * 1024 * 1024
BIG_VMEM_LIMIT_BYTES = 52 * 1024 * 1024
ROW_TILE = 512
FULL_ROW_TILE = 256
ELT_TILE = 256
CONV_TILE = 128
CONV_SUB = 32
CONV_LANES = 512
COL_TILE = 1024
GRAD_COL_TILE = 1024
K_TILE = 2048
ADAM_BLOCK_BYTES = 1 << 20

_DIMS = {
    "nn": (((1,), (0,)), ((), ())),
    "nt": (((1,), (1,)), ((), ())),
    "tn": (((0,), (0,)), ((), ())),
}


def _dot(a, b, mode):
    return lax.dot_general(a, b, _DIMS[mode], preferred_element_type=F32)


def _tile(n, t):
    t = min(n, t)
    assert n % t == 0, (n, t)
    return t


def _div_tile(n, t):
    if n <= t:
        return n
    for cand in range(t // 16 * 16, 15, -16):
        if n % cand == 0:
            return cand
    return n


def _sds(shape, dtype):
    return jax.ShapeDtypeStruct(tuple(shape), dtype)


def _call(name, body, grid, in_specs, out_specs, out_shape, ins, scratch=(), comm=None, vmem=VMEM_LIMIT_BYTES, aliases=None):
    params = pltpu.CompilerParams(dimension_semantics=("arbitrary",) * len(grid), vmem_limit_bytes=vmem)
    if comm is None:
        return pl.pallas_call(
            body, name=name, grid=grid, in_specs=list(in_specs), out_specs=out_specs, out_shape=out_shape,
            scratch_shapes=list(scratch), compiler_params=params, input_output_aliases=aliases or {},
        )(*ins)
    assert not aliases
    single = not isinstance(out_shape, (list, tuple))
    own_specs = [out_specs] if single else list(out_specs)
    own_shape = [out_shape] if single else list(out_shape)
    n_in, n_out, n_scr = len(ins), len(own_shape), len(scratch)
    n_src, n_dst = len(comm.srcs), len(comm.out_shape)
    hbm = pl.BlockSpec(memory_space=pltpu.HBM)

    def with_comm(*refs):
        a = n_in
        b = a + n_src
        c = b + n_out
        d = c + n_dst
        e = d + n_scr
        src_refs, dst_refs, sem_refs = refs[a:b], refs[c:d], refs[e:]
        first = functools.reduce(jnp.logical_and, [pl.program_id(i) == 0 for i in range(len(grid))])
        last = functools.reduce(jnp.logical_and, [pl.program_id(i) == grid[i] - 1 for i in range(len(grid))])

        @pl.when(first)
        def _():
            comm.start(src_refs, dst_refs, sem_refs)

        body(*refs[:a], *refs[b:c], *refs[d:e])

        @pl.when(last)
        def _():
            comm.wait(src_refs, dst_refs, sem_refs)

    res = pl.pallas_call(
        with_comm, name=name, grid=grid,
        in_specs=list(in_specs) + [hbm] * n_src,
        out_specs=own_specs + [hbm] * n_dst,
        out_shape=own_shape + list(comm.out_shape),
        scratch_shapes=list(scratch) + list(comm.sems),
        compiler_params=params,
        input_output_aliases={n_in + t: n_out + t for t in range(n_src)} if getattr(comm, "aliased", False) else {},
    )(*ins, *comm.srcs)
    own = res[0] if single else list(res[:n_out])
    return own, list(res[n_out:])


def _mesh_pos():
    return lax.axis_index("x"), lax.axis_index("y"), lax.axis_index("c")


def _slot(px, py, pc):
    return 4 * px + 2 * py + pc


def _peers(x, y, c):
    flips = [(fx, fy, fc) for fx in (0, 1) for fy in (0, 1) for fc in (0, 1)][1:]
    return [(1 - x if fx else x, 1 - y if fy else y, 1 - c if fc else c) for fx, fy, fc in flips]


def _remote(src, dst, send_sem, recv_sem, to):
    return pltpu.make_async_remote_copy(
        src_ref=src, dst_ref=dst, send_sem=send_sem, recv_sem=recv_sem, device_id=to, device_id_type=pl.DeviceIdType.MESH
    )


class _GatherOwn:
    N_TO = 4

    def __init__(self, shards):
        n = len(shards)
        self.srcs = list(shards)
        self.out_shape = [_sds((N_DEV, *s.shape), s.dtype) for s in shards]
        self.sems = [pltpu.SemaphoreType.DMA((self.N_TO * n,)), pltpu.SemaphoreType.DMA((self.N_TO * n,)), pltpu.SemaphoreType.DMA((n,))]

    def _copies(self, src_refs, dst_refs, sem_refs):
        send_sems, recv_sems, local_sems = sem_refs
        x, y, c = _mesh_pos()
        me = (x, y, c)
        targets = [(x, y, 1 - c), (1 - x, y, c), (x, 1 - y, c), (1 - x, 1 - y, c)]
        sends, recvs, local = [], [], []
        for t, (src, dst) in enumerate(zip(src_refs, dst_refs)):
            local.append(pltpu.make_async_copy(src, dst.at[_slot(*me)], local_sems.at[t]))
            for k, to in enumerate(targets):
                s = self.N_TO * t + k
                sends.append(_remote(src, dst.at[_slot(*me)], send_sems.at[s], recv_sems.at[s], to))
                recvs.append(_remote(src, dst.at[_slot(*to)], send_sems.at[s], recv_sems.at[s], to))
        return sends, recvs, local

    def start(self, src_refs, dst_refs, sem_refs):
        sends, _, local = self._copies(src_refs, dst_refs, sem_refs)
        for cp in local + sends:
            cp.start()

    def wait(self, src_refs, dst_refs, sem_refs):
        sends, recvs, local = self._copies(src_refs, dst_refs, sem_refs)
        for cp in recvs:
            cp.wait_recv()
        for cp in sends:
            cp.wait_send()
        for cp in local:
            cp.wait()


class _GatherForward:
    aliased = True

    def __init__(self, bufs):
        n = len(bufs)
        self.srcs = list(bufs)
        self.out_shape = [_sds(b.shape, b.dtype) for b in bufs]
        self.sems = [pltpu.SemaphoreType.DMA((3 * n,)), pltpu.SemaphoreType.DMA((3 * n,))]

    def _copies(self, src_refs, dst_refs, sem_refs):
        send_sems, recv_sems = sem_refs
        x, y, c = _mesh_pos()
        sibling = (x, y, 1 - c)
        sends, recvs = [], []
        for t, (src, dst) in enumerate(zip(src_refs, dst_refs)):
            for j, chip in enumerate([(1 - x, y), (x, 1 - y), (1 - x, 1 - y)]):
                s = 3 * t + j
                mine, theirs = _slot(*chip, c), _slot(*chip, 1 - c)
                sends.append(_remote(src.at[mine], dst.at[mine], send_sems.at[s], recv_sems.at[s], sibling))
                recvs.append(_remote(src.at[mine], dst.at[theirs], send_sems.at[s], recv_sems.at[s], sibling))
        return sends, recvs, []

    start = _GatherOwn.start
    wait = _GatherOwn.wait


class _ScatterSlots:
    def __init__(self, grads):
        n = len(grads)
        self.srcs = list(grads)
        self.out_shape = [_sds(g.shape, g.dtype) for g in grads]
        self.sems = [pltpu.SemaphoreType.DMA((7 * n,)), pltpu.SemaphoreType.DMA((7 * n,)), pltpu.SemaphoreType.DMA((n,))]

    def _copies(self, src_refs, dst_refs, sem_refs):
        send_sems, recv_sems, local_sems = sem_refs
        x, y, c = _mesh_pos()
        me = _slot(x, y, c)
        sends, recvs, local = [], [], []
        for t, (src, dst) in enumerate(zip(src_refs, dst_refs)):
            local.append(pltpu.make_async_copy(src.at[me], dst.at[me], local_sems.at[t]))
            for k, to in enumerate(_peers(x, y, c)):
                s = 7 * t + k
                sends.append(_remote(src.at[_slot(*to)], dst.at[me], send_sems.at[s], recv_sems.at[s], to))
                recvs.append(_remote(src.at[me], dst.at[_slot(*to)], send_sems.at[s], recv_sems.at[s], to))
        return sends, recvs, local

    start = _GatherOwn.start
    wait = _GatherOwn.wait


def _sig(x):
    return 1.0 / (1.0 + jnp.exp(-x))


def _rms(x, g):
    r = lax.rsqrt(jnp.mean(x * x, axis=-1, keepdims=True) + EPS)
    return x * r * g


def _rms_bwd(dy, x, g):
    r = lax.rsqrt(jnp.mean(x * x, axis=-1, keepdims=True) + EPS)
    xh = x * r
    dg = jnp.sum(dy * xh, axis=0, keepdims=True)
    dxh = dy * g
    dx = r * (dxh - xh * jnp.mean(dxh * xh, axis=-1, keepdims=True))
    return dx, dg


def _accumulate(ref, val, first):
    @pl.when(first)
    def _():
        ref[...] = val

    @pl.when(jnp.logical_not(first))
    def _():
        ref[...] += val


def _kloop(k, nk, acc_refs, contribs, finish):
    @pl.when(k == 0)
    def _():
        for r, c in zip(acc_refs, contribs):
            r[...] = c

    @pl.when(k > 0)
    def _():
        for r, c in zip(acc_refs, contribs):
            r[...] += c

    @pl.when(k == nk - 1)
    def _():
        finish()


def _rms_call(name, x, g):
    S, D = x.shape
    ts = _tile(S, ELT_TILE)

    def body(x_ref, g_ref, o_ref):
        o_ref[...] = _rms(x_ref[...], g_ref[...]).astype(BF16)

    return _call(
        name, body, (S // ts,),
        [pl.BlockSpec((ts, D), lambda i: (i, 0)), pl.BlockSpec((1, D), lambda i: (0, 0))],
        pl.BlockSpec((ts, D), lambda i: (i, 0)), _sds((S, D), BF16), (x, g),
    )


def _conv_in_call(name, hb, w, layer, b_in, comm=None):
    S, D = hb.shape
    nsh = w.shape[2]
    half = N_DEV // 2
    assert nsh * half == D
    tm = _tile(S, ROW_TILE)

    def body(h_ref, wa_ref, wg_ref, ba_ref, bg_ref, a_ref, g_ref, u_ref):
        h = h_ref[...]
        a = _dot(h, wa_ref[...], "nt") + ba_ref[...]
        g = _dot(h, wg_ref[...], "nt") + bg_ref[...]
        a_ref[...] = a.astype(BF16)
        g_ref[...] = g.astype(BF16)
        u_ref[...] = a * _sig(g)

    out_spec = pl.BlockSpec((tm, nsh), lambda d, i: (i, d))
    return _call(
        name, body, (half, S // tm),
        [
            pl.BlockSpec((tm, D), lambda d, i: (i, 0)),
            pl.BlockSpec((None, None, nsh, D), lambda d, i: (d, layer, 0, 0)),
            pl.BlockSpec((None, None, nsh, D), lambda d, i: (d + half, layer, 0, 0)),
            pl.BlockSpec((1, nsh), lambda d, i: (0, d)),
            pl.BlockSpec((1, nsh), lambda d, i: (0, d + half)),
        ],
        [out_spec, out_spec, out_spec],
        [_sds((S, D), BF16), _sds((S, D), BF16), _sds((S, D), F32)],
        (hb, w, w, b_in, b_in), comm=comm,
    )


SUBLANES = 8


def _fill_shifted(sh_ref, ext_ref, rows):
    ext_ref[rows:, :] = jnp.zeros((SUBLANES, ext_ref.shape[1]), F32)
    for s in range(SUBLANES):
        sh_ref[s] = ext_ref[pl.ds(s, rows), :]


def _shifted_window(sh_ref, row, lane0):
    s = row % SUBLANES
    return sh_ref[s, pl.ds(row - s, CONV_SUB), pl.ds(lane0, CONV_LANES)]


def _conv_taps(sh_ref, w_ref, row0, lane0, offset, reverse):
    acc = None
    for k in range(CONV_WIDTH):
        off = offset - k if reverse else offset + k
        term = w_ref[k:k + 1, lane0:lane0 + CONV_LANES] * _shifted_window(sh_ref, row0 + off, lane0)
        acc = term if acc is None else acc + term
    return acc


def _dwconv_call(name, u1, w_dw, b_dw, ln_g, ln_b, comm=None):
    S, D = u1.shape
    tc = _tile(S, CONV_TILE)
    hb = tc // CONV_HALO
    lanes = min(D, CONV_LANES)
    assert lanes == CONV_LANES and D % CONV_LANES == 0 and tc % CONV_SUB == 0

    def body(cur_ref, halo_ref, w_ref, b_ref, g_ref, bb_ref, u2_ref, u4_ref, ext_ref, sh_ref):
        i = pl.program_id(0)
        ext_ref[0:CONV_HALO, :] = jnp.where(i > 0, halo_ref[...], 0.0)
        ext_ref[CONV_HALO:tc + CONV_HALO, :] = cur_ref[...]
        _fill_shifted(sh_ref, ext_ref, tc + CONV_HALO)
        first_tap = CONV_HALO - (CONV_WIDTH - 1)
        for r in range(0, tc, CONV_SUB):
            for c in range(0, D, CONV_LANES):
                u2_ref[r:r + CONV_SUB, c:c + CONV_LANES] = (
                    _conv_taps(sh_ref, w_ref, r, c, first_tap, False) + b_ref[:, c:c + CONV_LANES]
                )
        u2 = u2_ref[...]
        mu = jnp.mean(u2, axis=-1, keepdims=True)
        xc = u2 - mu
        u3 = xc * lax.rsqrt(jnp.mean(xc * xc, axis=-1, keepdims=True) + EPS) * g_ref[...] + bb_ref[...]
        u4_ref[...] = (u3 * _sig(u3)).astype(BF16)

    vec = pl.BlockSpec((1, D), lambda i: (0, 0))
    row = pl.BlockSpec((tc, D), lambda i: (i, 0))
    return _call(
        name, body, (S // tc,),
        [
            row,
            pl.BlockSpec((CONV_HALO, D), lambda i: (jnp.maximum(i * hb - 1, 0), 0)),
            pl.BlockSpec((CONV_WIDTH, D), lambda i: (0, 0)),
            vec, vec, vec,
        ],
        [row, row],
        [_sds((S, D), F32), _sds((S, D), BF16)],
        (u1, u1, w_dw, b_dw, ln_g, ln_b),
        scratch=[pltpu.VMEM((tc + CONV_HALO + SUBLANES, D), F32), pltpu.VMEM((SUBLANES, tc + CONV_HALO, D), F32)], comm=comm,
    )


def _full_weight_spec(w, layer):
    _, _, ksh, D = w.shape
    return pl.BlockSpec((N_DEV, None, ksh, D), lambda i: (0, layer, 0, 0), pipeline_mode=pl.Buffered(1))


def _fullmm_res_call(name, a, w, layer, x_res, bias, g_next, comm=None):
    S, D = x_res.shape
    K = a.shape[1]
    tm = _tile(S, FULL_ROW_TILE)
    has_b = bias is not None

    def body(*refs):
        a_ref, w_ref, x_ref = refs[:3]
        b_ref = refs[3] if has_b else None
        g_ref, xo_ref, ho_ref = refs[3 + int(has_b):]
        xn = x_ref[...] + _dot(a_ref[...], w_ref[...].reshape(K, D), "nn")
        if has_b:
            xn = xn + b_ref[...]
        xo_ref[...] = xn
        ho_ref[...] = _rms(xn, g_ref[...]).astype(BF16)

    row = pl.BlockSpec((tm, D), lambda i: (i, 0))
    vec = pl.BlockSpec((1, D), lambda i: (0, 0))
    in_specs = [pl.BlockSpec((tm, K), lambda i: (i, 0)), _full_weight_spec(w, layer), row] + ([vec] if has_b else []) + [vec]
    ins = [a, w, x_res] + ([bias] if has_b else []) + [g_next]
    return _call(name, body, (S // tm,), in_specs, [row, row], [_sds((S, D), F32), _sds((S, D), BF16)], ins, comm=comm)


SHARD_PAIR = 2


def _pair_spec(w, layer):
    _, _, fsh, D = w.shape
    return pl.BlockSpec((SHARD_PAIR, None, fsh, D), lambda d, i: (d, layer, 0, 0))


def _ffn_up_call(name, hb, wgt, wut, layer, comm=None):
    S, D = hb.shape
    fsh = wgt.shape[2]
    tn = SHARD_PAIR * fsh
    tm = _tile(S, ROW_TILE)

    def body(h_ref, wg_ref, wu_ref, g_ref, u_ref, a_ref):
        h = h_ref[...]
        g = _dot(h, wg_ref[...].reshape(tn, D), "nt")
        u = _dot(h, wu_ref[...].reshape(tn, D), "nt")
        g_ref[...] = g.astype(BF16)
        u_ref[...] = u.astype(BF16)
        a_ref[...] = (g * _sig(g) * u).astype(BF16)

    ospec = pl.BlockSpec((tm, tn), lambda d, i: (i, d))
    return _call(
        name, body, (N_DEV // SHARD_PAIR, S // tm),
        [pl.BlockSpec((tm, D), lambda d, i: (i, 0)), _pair_spec(wgt, layer), _pair_spec(wut, layer)],
        [ospec] * 3, [_sds((S, N_DEV * fsh), BF16)] * 3, (hb, wgt, wut), comm=comm,
    )


def _mm_rows_call(name, a, w, layer, add, comm=None):
    S, K = a.shape
    _, _, ksh, N = w.shape
    assert K == N_DEV * ksh
    tm = _tile(S, ROW_TILE)
    tn = _tile(N, COL_TILE)
    has_add = add is not None

    def body(*refs):
        a_ref, w_ref = refs[:2]
        o_ref = refs[-1]
        y = _dot(a_ref[...], w_ref[...].reshape(K, tn), "nn")
        o_ref[...] = y + refs[2][...] if has_add else y

    tile = pl.BlockSpec((tm, tn), lambda i, n: (i, n))
    in_specs = [pl.BlockSpec((tm, K), lambda i, n: (i, 0)), pl.BlockSpec((N_DEV, None, ksh, tn), lambda i, n: (0, layer, 0, n))]
    return _call(
        name, body, (S // tm, N // tn), in_specs + ([tile] if has_add else []), tile, _sds((S, N), F32),
        (a, w) + ((add,) if has_add else ()), comm=comm, vmem=BIG_VMEM_LIMIT_BYTES,
    )


def _res_rms_call(name, y, x_res, g_next, comm=None):
    S, D = x_res.shape
    ts = _tile(S, ELT_TILE)

    def body(y_ref, x_ref, g_ref, xo_ref, ho_ref):
        xn = x_ref[...] + y_ref[...]
        xo_ref[...] = xn
        ho_ref[...] = _rms(xn, g_ref[...]).astype(BF16)

    row = pl.BlockSpec((ts, D), lambda i: (i, 0))
    return _call(
        name, body, (S // ts,), [row, row, pl.BlockSpec((1, D), lambda i: (0, 0))],
        [row, row], [_sds((S, D), F32), _sds((S, D), BF16)], (y, x_res, g_next), comm=comm,
    )


def _colmm_call(name, a, w, layer, out_dtype, turned=False):
    S, K = a.shape
    nsh = w.shape[2] if turned else w.shape[3]
    tm = _tile(S, ROW_TILE)
    mode = "nt" if turned else "nn"

    def body(a_ref, w_ref, o_ref):
        o_ref[...] = _dot(a_ref[...], w_ref[...], mode).astype(out_dtype)

    return _call(
        name, body, (N_DEV, S // tm),
        [pl.BlockSpec((tm, K), lambda d, i: (i, 0)), pl.BlockSpec((None, None, *w.shape[2:]), lambda d, i: (d, layer, 0, 0))],
        pl.BlockSpec((tm, nsh), lambda d, i: (i, d)), _sds((S, N_DEV * nsh), out_dtype), (a, w),
    )


def _ple_gate_call(name, hb, w, layer, x_res, bias, pp, g_next):
    S, D = x_res.shape
    K = hb.shape[1]
    tm = _tile(S, FULL_ROW_TILE)
    has_g = g_next is not None

    def body(*refs):
        a_ref, w_ref, x_ref, b_ref, p_ref = refs[:5]
        g_ref = refs[5] if has_g else None
        outs = refs[5 + int(has_g):]
        gate = _sig(_dot(a_ref[...], w_ref[...].reshape(K, D), "nn") + b_ref[...])
        xn = x_ref[...] + gate * p_ref[...].astype(F32)
        outs[0][...] = xn
        outs[1][...] = gate.astype(BF16)
        if has_g:
            outs[2][...] = _rms(xn, g_ref[...]).astype(BF16)

    row = pl.BlockSpec((tm, D), lambda i: (i, 0))
    vec = pl.BlockSpec((1, D), lambda i: (0, 0))
    in_specs = [pl.BlockSpec((tm, K), lambda i: (i, 0)), _full_weight_spec(w, layer), row, vec, row]
    ins = [hb, w, x_res, bias, pp]
    out_specs, out_shape = [row, row], [_sds((S, D), F32), _sds((S, D), BF16)]
    if has_g:
        in_specs.append(vec)
        ins.append(g_next)
        out_specs.append(row)
        out_shape.append(_sds((S, D), BF16))
    res = _call(name, body, (S // tm,), in_specs, out_specs, out_shape, ins)
    return res if has_g else (res[0], res[1], None)


def _pool_mix_call(name, x, g):
    S, D = x.shape
    ts = _tile(S, ELT_TILE)
    hb = ts // POOL_HALO
    gc = D // len(POOL_WINDOWS)

    def body(cur_ref, halo_ref, g_ref, o_ref, ext_ref):
        i = pl.program_id(0)
        gain = g_ref[...]
        ext_ref[0:POOL_HALO, :] = jnp.where(i > 0, _rms(halo_ref[...], gain), 0.0)
        ext_ref[POOL_HALO:, :] = _rms(cur_ref[...], gain)
        t = i * ts + lax.broadcasted_iota(jnp.int32, (ts, 1), 0)
        for gi, win in enumerate(POOL_WINDOWS):
            lanes = pl.ds(gi * gc, gc)
            h = ext_ref[pl.ds(POOL_HALO, ts), lanes]
            acc = h
            for j in range(1, win):
                acc = acc + ext_ref[pl.ds(POOL_HALO - j, ts), lanes]
            cnt = jnp.minimum(t + 1, win).astype(F32)
            o_ref[:, gi * gc:(gi + 1) * gc] = (acc / cnt - h).astype(BF16)

    row = pl.BlockSpec((ts, D), lambda i: (i, 0))
    return _call(
        name, body, (S // ts,),
        [row, pl.BlockSpec((POOL_HALO, D), lambda i: (jnp.maximum(i * hb - 1, 0), 0)), pl.BlockSpec((1, D), lambda i: (0, 0))],
        row, _sds((S, D), BF16), (x, x, g), scratch=[pltpu.VMEM((ts + POOL_HALO, D), F32)],
    )


def _pool_out_call(name, mix, wp, scale, x_res, g_next):
    S, D = x_res.shape
    ng, gc, _ = wp.shape
    tm = _tile(S, FULL_ROW_TILE)

    def body(m_ref, w_ref, s_ref, x_ref, g_ref, xo_ref, y_ref, h_ref):
        parts = [_dot(m_ref[:, gi * gc:(gi + 1) * gc], w_ref[gi], "nn") for gi in range(ng)]
        ypre = jnp.concatenate(parts, axis=-1)
        y_ref[...] = ypre.astype(BF16)
        xn = x_ref[...] + ypre * s_ref[...]
        xo_ref[...] = xn
        h_ref[...] = _rms(xn, g_ref[...]).astype(BF16)

    row = pl.BlockSpec((tm, D), lambda i: (i, 0))
    vec = pl.BlockSpec((1, D), lambda i: (0, 0))
    return _call(
        name, body, (S // tm,),
        [row, pl.BlockSpec((ng, gc, gc), lambda i: (0, 0, 0)), vec, row, vec],
        [row, row, row], [_sds((S, D), F32), _sds((S, D), BF16), _sds((S, D), BF16)],
        (mix, wp, scale, x_res, g_next),
    )


KV_STEP = 2


def _attn_probs(q, kprev, kcur, qg, kg, bias, sink, n):
    grp = q.shape[0]
    qn = _rms(q, qg)
    k = jnp.concatenate([kprev, kcur], axis=0)
    kn = _rms(k, kg)
    s = _dot(qn.reshape(grp * Q_BLOCK, HEAD_DIM).astype(BF16), kn.astype(BF16), "nt") * (HEAD_DIM ** -0.5)
    s = s.reshape(grp, Q_BLOCK, 2 * Q_BLOCK) + bias
    qi = lax.broadcasted_iota(jnp.int32, (Q_BLOCK, 2 * Q_BLOCK), 0)
    kj = lax.broadcasted_iota(jnp.int32, (Q_BLOCK, 2 * Q_BLOCK), 1)
    qc = qi // CHUNK
    kc = kj // CHUNK - Q_BLOCK // CHUNK
    ok = (kc <= qc) & (kc >= qc - WINDOW_CHUNKS) & ((n > 0) | (kj >= Q_BLOCK))
    s = jnp.where(ok[None], s, NEG_INF)
    m = jnp.maximum(jnp.max(s, axis=-1, keepdims=True), sink)
    e = jnp.exp(s - m)
    es = jnp.exp(sink - m)
    inv = 1.0 / (jnp.sum(e, axis=-1, keepdims=True) + es)
    return q, qn, k, kn, e * inv, es * inv


def _attn_specs(grp, nb):
    heads = KV_STEP * grp
    qspec = pl.BlockSpec((heads, Q_BLOCK, HEAD_DIM), lambda j, n: (j, jnp.minimum(n, nb - 1), 0))
    prev = pl.BlockSpec((KV_STEP, Q_BLOCK, HEAD_DIM), lambda j, n: (j, jnp.maximum(n - 1, 0), 0))
    cur = pl.BlockSpec((KV_STEP, Q_BLOCK, HEAD_DIM), lambda j, n: (j, jnp.minimum(n, nb - 1), 0))
    gain = pl.BlockSpec((1, HEAD_DIM), lambda j, n: (0, 0))
    bias = pl.BlockSpec((heads, Q_BLOCK, 2 * Q_BLOCK), lambda j, n: (j, 0, 0))
    sink = pl.BlockSpec((heads, 1, 1), lambda j, n: (j, 0, 0))
    return qspec, prev, cur, gain, bias, sink


def _attn_fwd_call(name, q, k, v, qg, kg, bias, sinks):
    H, S, _ = q.shape
    n_kv = k.shape[0]
    grp = H // n_kv
    nb = S // Q_BLOCK
    assert n_kv % KV_STEP == 0

    def body(q_ref, kp_ref, kc_ref, vp_ref, vc_ref, qg_ref, kg_ref, bias_ref, sink_ref, o_ref):
        n = pl.program_id(1)
        for hh in range(KV_STEP):
            hs = slice(hh * grp, (hh + 1) * grp)
            _, _, _, _, p, _ = _attn_probs(q_ref[hs], kp_ref[hh], kc_ref[hh], qg_ref[...], kg_ref[...], bias_ref[hs], sink_ref[hs], n)
            vv = jnp.concatenate([vp_ref[hh], vc_ref[hh]], axis=0).astype(BF16)
            o = _dot(p.reshape(grp * Q_BLOCK, 2 * Q_BLOCK).astype(BF16), vv, "nn")
            o_ref[hs] = o.reshape(grp, Q_BLOCK, HEAD_DIM).astype(BF16)

    qspec, prev, cur, gain, bspec, sspec = _attn_specs(grp, nb)
    return _call(
        name, body, (n_kv // KV_STEP, nb),
        [qspec, prev, cur, prev, cur, gain, gain, bspec, sspec],
        qspec, _sds((H, S, HEAD_DIM), BF16), (q, k, k, v, v, qg, kg, bias, sinks),
    )


def _loss_call(name, y, target):
    S, D = y.shape
    ts = _tile(S, ELT_TILE)

    def body(y_ref, t_ref, d_ref, db_ref, l_ref):
        err = y_ref[...] - t_ref[...]
        dy = err * (1.0 / D)
        d_ref[...] = dy
        db_ref[...] = dy.astype(BF16)
        part = 0.5 * jnp.sum(jnp.sum(err * err, axis=-1, keepdims=True), axis=0, keepdims=True) * (1.0 / D)
        _accumulate(l_ref, jnp.broadcast_to(part, l_ref.shape), pl.program_id(0) == 0)

    row = pl.BlockSpec((ts, D), lambda i: (i, 0))
    return _call(
        name, body, (S // ts,), [row, row],
        [row, row, pl.BlockSpec((8, 128), lambda i: (0, 0))],
        [_sds((S, D), F32), _sds((S, D), BF16), _sds((8, 128), F32)], (y, target),
    )


def _ple_bwd_elt_call(name, dx, gate, pp):
    S, D = dx.shape
    ts = _tile(S, ELT_TILE)

    def body(dx_ref, gt_ref, p_ref, dz_ref, dp_ref, db_ref):
        d = dx_ref[...]
        gt = gt_ref[...].astype(F32)
        dz = d * p_ref[...].astype(F32) * gt * (1.0 - gt)
        dz_ref[...] = dz.astype(BF16)
        dp_ref[...] = (d * gt).astype(BF16)
        _accumulate(db_ref, jnp.sum(dz, axis=0, keepdims=True), pl.program_id(0) == 0)

    row = pl.BlockSpec((ts, D), lambda i: (i, 0))
    return _call(
        name, body, (S // ts,), [row, row, row],
        [row, row, pl.BlockSpec((1, D), lambda i: (0, 0))],
        [_sds((S, D), BF16), _sds((S, D), BF16), _sds((1, D), F32)], (dx, gate, pp),
    )


def _grad_w_call(name, a, a_mode, b, b_mode, comm=None):
    bs = b if isinstance(b, (list, tuple)) else [b]
    S = a.shape[-2]
    tk = _tile(S, K_TILE)
    nk = S // tk

    def spec(arr, mode):
        if mode == "full":
            c = arr.shape[-1]
            return pl.BlockSpec((tk, c), lambda d, k: (k, 0)), c
        if mode == "nat":
            c = arr.shape[-1] // N_DEV
            return pl.BlockSpec((tk, c), lambda d, k: (k, d)), c
        c = arr.shape[-1]
        return pl.BlockSpec((None, tk, c), lambda d, k: (d, k, 0)), c

    a_spec, ca = spec(a, a_mode)
    b_specs, cbs = zip(*[spec(x, b_mode) for x in bs])
    nb = len(bs)

    def body(*refs):
        a_ref = refs[0]
        b_refs = refs[1:1 + nb]
        o_refs = refs[1 + nb:1 + 2 * nb]
        acc_refs = refs[1 + 2 * nb:]
        k = pl.program_id(1)
        av = a_ref[...]

        def finish():
            for o, acc in zip(o_refs, acc_refs):
                o[...] = acc[...].astype(BF16)

        _kloop(k, nk, acc_refs, [_dot(av, br[...], "tn") for br in b_refs], finish)

    res = _call(
        name, body, (N_DEV, nk), [a_spec, *b_specs],
        [pl.BlockSpec((None, ca, cb), lambda d, k: (d, 0, 0)) for cb in cbs],
        [_sds((N_DEV, ca, cb), BF16) for cb in cbs], (a, *bs),
        scratch=[pltpu.VMEM((ca, cb), F32) for cb in cbs], comm=comm,
    )
    own, landed = res if comm is not None else (res, None)
    own = own if isinstance(b, (list, tuple)) else own[0]
    return own if comm is None else (own, landed)


def _dx_full_call(name, dy, w, layer, out_dtype):
    S, D = dy.shape
    K = N_DEV * w.shape[2]
    tm = _tile(S, ROW_TILE)

    def body(dy_ref, w_ref, o_ref):
        o_ref[...] = _dot(dy_ref[...], w_ref[...].reshape(K, D), "nt").astype(out_dtype)

    return _call(
        name, body, (S // tm,), [pl.BlockSpec((tm, D), lambda i: (i, 0)), _full_weight_spec(w, layer)],
        pl.BlockSpec((tm, K), lambda i: (i, 0)), _sds((S, K), out_dtype), (dy, w),
    )


def _dx_full_rms_call(name, dy, w, layer, x, g, dres):
    S, D = x.shape
    K = N_DEV * w.shape[2]
    assert K == D
    tm = _tile(S, FULL_ROW_TILE)

    def body(dy_ref, w_ref, x_ref, g_ref, r_ref, dx_ref, dxb_ref, dg_ref):
        dh, dg = _rms_bwd(_dot(dy_ref[...], w_ref[...].reshape(K, dy_ref.shape[1]), "nt"), x_ref[...], g_ref[...])
        dx = r_ref[...] + dh
        dx_ref[...] = dx
        dxb_ref[...] = dx.astype(BF16)
        _accumulate(dg_ref, dg, pl.program_id(0) == 0)

    row = pl.BlockSpec((tm, D), lambda i: (i, 0))
    vec = pl.BlockSpec((1, D), lambda i: (0, 0))
    return _call(
        name, body, (S // tm,), [pl.BlockSpec((tm, dy.shape[1]), lambda i: (i, 0)), _full_weight_spec(w, layer), row, vec, row],
        [row, row, vec], [_sds((S, D), F32), _sds((S, D), BF16), _sds((1, D), F32)], (dy, w, x, g, dres),
    )


def _ffn_bwd_hidden_call(name, dyb, w, layer, gpre, upre, comm=None):
    S, D = dyb.shape
    fsh = w.shape[2]
    tn = SHARD_PAIR * fsh
    tm = _tile(S, ROW_TILE)

    def body(dy_ref, w_ref, g_ref, u_ref, dg_ref, du_ref):
        w = w_ref[...].reshape(tn, D)
        half = max(tm // 2, 8)
        for r in range(0, tm, half):
            rows = slice(r, r + half)
            da = _dot(dy_ref[rows, :], w, "nt")
            g = g_ref[rows, :].astype(F32)
            u = u_ref[rows, :].astype(F32)
            s = _sig(g)
            dg_ref[rows, :] = (da * u * s * (1.0 + g * (1.0 - s))).astype(BF16)
            du_ref[rows, :] = (da * g * s).astype(BF16)

    cspec = pl.BlockSpec((tm, tn), lambda d, i: (i, d))
    return _call(
        name, body, (N_DEV // SHARD_PAIR, S // tm),
        [pl.BlockSpec((tm, D), lambda d, i: (i, 0)), _pair_spec(w, layer), cspec, cspec],
        [cspec, cspec], [_sds((S, N_DEV * fsh), BF16)] * 2, (dyb, w, gpre, upre), comm=comm,
    )


def _grad_rows_call(name, a, b, comm=None):
    S, F = a.shape
    N = b.shape[1]
    fsh = F // N_DEV
    tr = SHARD_PAIR * fsh
    tn = _tile(N, GRAD_COL_TILE)
    tk = _tile(S, K_TILE)
    nk = S // tk

    def body(a_ref, b_ref, o_ref, acc_ref):
        def finish():
            o_ref[...] = acc_ref[...].astype(BF16).reshape(SHARD_PAIR, fsh, tn)

        _kloop(pl.program_id(2), nk, [acc_ref], [_dot(a_ref[...], b_ref[...], "tn")], finish)

    res = _call(
        name, body, (N_DEV // SHARD_PAIR, N // tn, nk),
        [pl.BlockSpec((tk, tr), lambda d, n, k: (k, d)), pl.BlockSpec((tk, tn), lambda d, n, k: (k, n))],
        pl.BlockSpec((SHARD_PAIR, fsh, tn), lambda d, n, k: (d, 0, n)), _sds((N_DEV, fsh, N), BF16), (a, b),
        scratch=[pltpu.VMEM((tr, tn), F32)], comm=comm,
    )
    return res


def _rms_bwd_res_call(name, dh, x, g, dres, want_colsum):
    S, D = x.shape
    ts = _tile(S, ELT_TILE)

    def body(dh_ref, x_ref, g_ref, r_ref, *outs):
        i = pl.program_id(0)
        d, dg = _rms_bwd(dh_ref[...], x_ref[...], g_ref[...])
        dx = r_ref[...] + d
        outs[0][...] = dx
        outs[1][...] = dx.astype(BF16)
        _accumulate(outs[2], dg, i == 0)
        if want_colsum:
            _accumulate(outs[3], jnp.sum(dx, axis=0, keepdims=True), i == 0)

    row = pl.BlockSpec((ts, D), lambda i: (i, 0))
    vec = pl.BlockSpec((1, D), lambda i: (0, 0))
    out_specs = [row, row, vec] + ([vec] if want_colsum else [])
    out_shape = [_sds((S, D), F32), _sds((S, D), BF16), _sds((1, D), F32)] + ([_sds((1, D), F32)] if want_colsum else [])
    res = _call(name, body, (S // ts,), [row, row, vec, row], out_specs, out_shape, (dh, x, g, dres))
    return tuple(res) if want_colsum else (*res, None)


def _conv_out_bwd_call(name, dyb, w, layer, u2, ln_g, ln_b):
    S, D = u2.shape
    ksh = w.shape[2]
    tm = _tile(S, ELT_TILE)

    def body(dy_ref, w_ref, u2_ref, g_ref, b_ref, du2_ref, dg_ref, db_ref, dbdw_ref):
        i = pl.program_id(0)
        dy = dy_ref[...]
        du4 = jnp.concatenate([_dot(dy, w_ref[d], "nt") for d in range(N_DEV)], axis=-1)
        u2 = u2_ref[...]
        mu = jnp.mean(u2, axis=-1, keepdims=True)
        xc = u2 - mu
        r = lax.rsqrt(jnp.mean(xc * xc, axis=-1, keepdims=True) + EPS)
        xh = xc * r
        gain = g_ref[...]
        u3 = xh * gain + b_ref[...]
        s = _sig(u3)
        du3 = du4 * s * (1.0 + u3 * (1.0 - s))
        dxh = du3 * gain
        du2 = r * (dxh - jnp.mean(dxh, axis=-1, keepdims=True) - xh * jnp.mean(dxh * xh, axis=-1, keepdims=True))
        du2_ref[...] = du2
        _accumulate(dg_ref, jnp.sum(du3 * xh, axis=0, keepdims=True), i == 0)
        _accumulate(db_ref, jnp.sum(du3, axis=0, keepdims=True), i == 0)
        _accumulate(dbdw_ref, jnp.sum(du2, axis=0, keepdims=True), i == 0)

    row = pl.BlockSpec((tm, D), lambda i: (i, 0))
    vec = pl.BlockSpec((1, D), lambda i: (0, 0))
    return _call(
        name, body, (S // tm,),
        [row, pl.BlockSpec((N_DEV, None, ksh, D), lambda i: (0, layer, 0, 0)), row, vec, vec],
        [row, vec, vec, vec], [_sds((S, D), F32)] + [_sds((1, D), F32)] * 3, (dyb, w, u2, ln_g, ln_b),
    )


def _dwconv_bwd_call(name, du2, u1, w_dw, a_pre, g_pre):
    S, D = u1.shape
    tc = _tile(S, CONV_TILE)
    hb = tc // CONV_HALO
    n_halo = S // CONV_HALO
    assert D % CONV_LANES == 0 and tc % CONV_SUB == 0
    wrows = CONV_HALO

    def body(d_cur, d_next, u_cur, u_prev, w_ref, a_ref, g_ref, du_ref, dbin_ref, dw_ref, dext_ref, uext_ref, du1_ref, dwacc_ref, dsh_ref, ush_ref):
        i = pl.program_id(0)
        last = S // tc - 1
        dext_ref[0:tc, :] = d_cur[...]
        dext_ref[tc:tc + CONV_HALO, :] = jnp.where(i < last, d_next[...], 0.0)
        uext_ref[0:CONV_HALO, :] = jnp.where(i > 0, u_prev[...], 0.0)
        uext_ref[CONV_HALO:tc + CONV_HALO, :] = u_cur[...]
        _fill_shifted(dsh_ref, dext_ref, tc + CONV_HALO)
        _fill_shifted(ush_ref, uext_ref, tc + CONV_HALO)

        @pl.when(i == 0)
        def _():
            dwacc_ref[...] = jnp.zeros_like(dwacc_ref)

        first_tap = CONV_HALO - (CONV_WIDTH - 1)
        for r in range(0, tc, CONV_SUB):
            for c in range(0, D, CONV_LANES):
                du1_ref[r:r + CONV_SUB, c:c + CONV_LANES] = _conv_taps(dsh_ref, w_ref, r, c, CONV_WIDTH - 1, True)
                dcur = dext_ref[r:r + CONV_SUB, c:c + CONV_LANES]
                for k in range(CONV_WIDTH):
                    prod = dcur * _shifted_window(ush_ref, r + first_tap + k, c)
                    part = prod[0:8]
                    for q in range(8, CONV_SUB, 8):
                        part = part + prod[q:q + 8]
                    dwacc_ref[k, :, c:c + CONV_LANES] += part

        du1 = du1_ref[...]
        a = a_ref[...].astype(F32)
        sg = _sig(g_ref[...].astype(F32))
        da = du1 * sg
        dgate = du1 * a * sg * (1.0 - sg)
        du_ref[:, 0:D] = da.astype(BF16)
        du_ref[:, D:2 * D] = dgate.astype(BF16)
        _accumulate(dbin_ref, jnp.concatenate([jnp.sum(da, axis=0, keepdims=True), jnp.sum(dgate, axis=0, keepdims=True)], axis=-1), i == 0)

        @pl.when(i == last)
        def _():
            for k in range(CONV_WIDTH):
                dw_ref[k:k + 1, :] = jnp.sum(dwacc_ref[k], axis=0, keepdims=True)
            dw_ref[CONV_WIDTH:, :] = jnp.zeros((wrows - CONV_WIDTH, D), F32)

    row = pl.BlockSpec((tc, D), lambda i: (i, 0))
    nxt = pl.BlockSpec((CONV_HALO, D), lambda i: (jnp.minimum((i + 1) * hb, n_halo - 1), 0))
    prv = pl.BlockSpec((CONV_HALO, D), lambda i: (jnp.maximum(i * hb - 1, 0), 0))
    return _call(
        name, body, (S // tc,),
        [row, nxt, row, prv, pl.BlockSpec((CONV_WIDTH, D), lambda i: (0, 0)), row, row],
        [pl.BlockSpec((tc, 2 * D), lambda i: (i, 0)), pl.BlockSpec((1, 2 * D), lambda i: (0, 0)), pl.BlockSpec((wrows, D), lambda i: (0, 0))],
        [_sds((S, 2 * D), BF16), _sds((1, 2 * D), F32), _sds((wrows, D), F32)],
        (du2, du2, u1, u1, w_dw, a_pre, g_pre),
        scratch=[
            pltpu.VMEM((tc + CONV_HALO + SUBLANES, D), F32), pltpu.VMEM((tc + CONV_HALO + SUBLANES, D), F32),
            pltpu.VMEM((tc, D), F32), pltpu.VMEM((CONV_WIDTH, 8, D), F32),
            pltpu.VMEM((SUBLANES, tc + CONV_HALO, D), F32), pltpu.VMEM((SUBLANES, tc + CONV_HALO, D), F32),
        ],
    )


def _pool_out_bwd_call(name, dy, ypre, scale, wp):
    S, D = dy.shape
    ng, gc, _ = wp.shape
    tm = _tile(S, FULL_ROW_TILE)

    def body(dy_ref, y_ref, s_ref, w_ref, dm_ref, dyp_ref, ds_ref):
        dy = dy_ref[...]
        _accumulate(ds_ref, jnp.sum(dy * y_ref[...].astype(F32), axis=0, keepdims=True), pl.program_id(0) == 0)
        dyp = (dy * s_ref[...]).astype(BF16)
        dyp_ref[...] = dyp
        dm_ref[...] = jnp.concatenate([_dot(dyp[:, gi * gc:(gi + 1) * gc], w_ref[gi], "nt") for gi in range(ng)], axis=-1)

    row = pl.BlockSpec((tm, D), lambda i: (i, 0))
    vec = pl.BlockSpec((1, D), lambda i: (0, 0))
    return _call(
        name, body, (S // tm,), [row, row, vec, pl.BlockSpec((ng, gc, gc), lambda i: (0, 0, 0))],
        [row, row, vec], [_sds((S, D), F32), _sds((S, D), BF16), _sds((1, D), F32)], (dy, ypre, scale, wp),
    )


def _pool_w_grad_call(name, mix, dyp, ng):
    S, D = mix.shape
    gc = D // ng
    tk = _tile(S, K_TILE)
    nk = S // tk

    def body(m_ref, d_ref, o_ref, acc_ref):
        def finish():
            o_ref[...] = acc_ref[...]

        _kloop(pl.program_id(1), nk, [acc_ref], [_dot(m_ref[...], d_ref[...], "tn")], finish)

    blk = pl.BlockSpec((tk, gc), lambda g, k: (k, g))
    return _call(
        name, body, (ng, nk), [blk, blk], pl.BlockSpec((None, gc, gc), lambda g, k: (g, 0, 0)),
        _sds((ng, gc, gc), F32), (mix, dyp), scratch=[pltpu.VMEM((gc, gc), F32)],
    )


def _pool_mix_bwd_call(name, dmix, x, g, dres):
    S, D = x.shape
    ts = _tile(S, ELT_TILE)
    hb = ts // POOL_HALO
    n_halo = S // POOL_HALO
    gc = D // len(POOL_WINDOWS)

    def body(cur_ref, nxt_ref, x_ref, g_ref, r_ref, dx_ref, dxb_ref, dg_ref, ext_ref, dh_ref):
        i = pl.program_id(0)
        last = S // ts - 1
        t = i * ts + lax.broadcasted_iota(jnp.int32, (ts + POOL_HALO, 1), 0)
        for gi, win in enumerate(POOL_WINDOWS):
            lanes = slice(gi * gc, (gi + 1) * gc)
            cnt = jnp.minimum(t + 1, win).astype(F32)
            ext_ref[0:ts, lanes] = cur_ref[:, lanes] / cnt[0:ts]
            ext_ref[ts:, lanes] = jnp.where(i < last, nxt_ref[:, lanes] / cnt[ts:], 0.0)
        for gi, win in enumerate(POOL_WINDOWS):
            lanes = pl.ds(gi * gc, gc)
            acc = ext_ref[pl.ds(0, ts), lanes]
            for j in range(1, win):
                acc = acc + ext_ref[pl.ds(j, ts), lanes]
            dh_ref[:, gi * gc:(gi + 1) * gc] = acc - cur_ref[:, gi * gc:(gi + 1) * gc]
        d, dg = _rms_bwd(dh_ref[...], x_ref[...], g_ref[...])
        dx = r_ref[...] + d
        dx_ref[...] = dx
        dxb_ref[...] = dx.astype(BF16)
        _accumulate(dg_ref, dg, i == 0)

    row = pl.BlockSpec((ts, D), lambda i: (i, 0))
    vec = pl.BlockSpec((1, D), lambda i: (0, 0))
    nxt = pl.BlockSpec((POOL_HALO, D), lambda i: (jnp.minimum((i + 1) * hb, n_halo - 1), 0))
    return _call(
        name, body, (S // ts,), [row, nxt, row, vec, row], [row, row, vec],
        [_sds((S, D), F32), _sds((S, D), BF16), _sds((1, D), F32)], (dmix, dmix, x, g, dres),
        scratch=[pltpu.VMEM((ts + POOL_HALO, D), F32), pltpu.VMEM((ts, D), F32)],
    )


def _attn_bwd_call(name, q, k, v, do, qg, kg, bias, sinks):
    H, S, _ = q.shape
    n_kv = k.shape[0]
    grp = H // n_kv
    nb = S // Q_BLOCK
    scale = HEAD_DIM ** -0.5

    def body(q_ref, kp_ref, kc_ref, vp_ref, vc_ref, do_ref, qg_ref, kg_ref, bias_ref, sink_ref,
             dq_ref, dk_ref, dv_ref, dqg_ref, dkg_ref, dbias_ref, dsink_ref, ck_ref, cv_ref):
        n = pl.program_id(1)

        @pl.when(n == 0)
        def _():
            dqg_ref[...] = jnp.zeros_like(dqg_ref)
            dkg_ref[...] = jnp.zeros_like(dkg_ref)
            dbias_ref[...] = jnp.zeros_like(dbias_ref)
            dsink_ref[...] = jnp.zeros_like(dsink_ref)
            ck_ref[...] = jnp.zeros_like(ck_ref)
            cv_ref[...] = jnp.zeros_like(cv_ref)

        def block_grads(hh):
            hs = slice(hh * grp, (hh + 1) * grp)
            q, qn, _, kn, p, ps = _attn_probs(q_ref[hs], kp_ref[hh], kc_ref[hh], qg_ref[...], kg_ref[...], bias_ref[hs], sink_ref[hs], n)
            rows = grp * Q_BLOCK
            dob = do_ref[hs].reshape(rows, HEAD_DIM).astype(BF16)
            vv = jnp.concatenate([vp_ref[hh], vc_ref[hh]], axis=0).astype(BF16)
            dp = _dot(dob, vv, "nt").reshape(grp, Q_BLOCK, 2 * Q_BLOCK)
            delta = jnp.sum(p * dp, axis=-1, keepdims=True)
            dl = p * (dp - delta)
            dbias_ref[hs] += dl
            dsink_ref[hs] += jnp.sum(-ps * delta, axis=1, keepdims=True)
            dlb = dl.reshape(rows, 2 * Q_BLOCK).astype(BF16)
            dqn = (_dot(dlb, kn.astype(BF16), "nn") * scale).reshape(grp, Q_BLOCK, HEAD_DIM)
            dkn = _dot(dlb, qn.reshape(rows, HEAD_DIM).astype(BF16), "tn") * scale
            dvv = _dot(p.reshape(rows, 2 * Q_BLOCK).astype(BF16), dob, "tn")
            qgain = qg_ref[...]
            r = lax.rsqrt(jnp.mean(q * q, axis=-1, keepdims=True) + EPS)
            qh = q * r
            dqg_ref[hh] += jnp.sum(jnp.sum(dqn * qh, axis=1), axis=0, keepdims=True)
            dqh = dqn * qgain
            dq_ref[hs] = r * (dqh - qh * jnp.mean(dqh * qh, axis=-1, keepdims=True))
            return dkn, dvv

        def finish_prev(hh, dkn_prev, dv_prev):
            dk, dkg = _rms_bwd(dkn_prev, kp_ref[hh], kg_ref[...])
            dk_ref[hh] = dk
            dv_ref[hh] = dv_prev
            dkg_ref[hh] += dkg

        @pl.when(n < nb)
        def _():
            grads = [block_grads(hh) for hh in range(KV_STEP)]

            @pl.when(n > 0)
            def _():
                for hh, (dkn, dvv) in enumerate(grads):
                    finish_prev(hh, ck_ref[hh] + dkn[0:Q_BLOCK], cv_ref[hh] + dvv[0:Q_BLOCK])

            for hh, (dkn, dvv) in enumerate(grads):
                ck_ref[hh] = dkn[Q_BLOCK:]
                cv_ref[hh] = dvv[Q_BLOCK:]

        @pl.when(n == nb)
        def _():
            for hh in range(KV_STEP):
                finish_prev(hh, ck_ref[hh], cv_ref[hh])

    qspec, prev, cur, gain, bspec, sspec = _attn_specs(grp, nb)
    kout = pl.BlockSpec((KV_STEP, Q_BLOCK, HEAD_DIM), lambda j, n: (j, jnp.maximum(n - 1, 0), 0))
    gout = pl.BlockSpec((KV_STEP, 1, HEAD_DIM), lambda j, n: (j, 0, 0))
    return _call(
        name, body, (n_kv // KV_STEP, nb + 1),
        [qspec, prev, cur, prev, cur, qspec, gain, gain, bspec, sspec],
        [qspec, kout, kout, gout, gout, bspec, sspec],
        [
            _sds((H, S, HEAD_DIM), F32), _sds((n_kv, S, HEAD_DIM), F32), _sds((n_kv, S, HEAD_DIM), F32),
            _sds((n_kv, 1, HEAD_DIM), F32), _sds((n_kv, 1, HEAD_DIM), F32),
            _sds((H, Q_BLOCK, 2 * Q_BLOCK), F32), _sds((H, 1, 1), F32),
        ],
        (q, k, k, v, v, do, qg, kg, bias, sinks),
        scratch=[pltpu.VMEM((KV_STEP, Q_BLOCK, HEAD_DIM), F32), pltpu.VMEM((KV_STEP, Q_BLOCK, HEAD_DIM), F32)],
    )


def _bucket_sum_call(name, onehot, dbias):
    nbk, n = onehot.shape
    H = dbias.shape[0]

    def body(o_ref, d_ref, out_ref):
        out_ref[...] = lax.dot_general(o_ref[...], d_ref[...], _DIMS["nt"], precision=lax.Precision.HIGHEST, preferred_element_type=F32)

    return _call(
        name, body, (1,), [pl.BlockSpec((nbk, n), lambda i: (0, 0)), pl.BlockSpec((H, n), lambda i: (0, 0))],
        pl.BlockSpec((nbk, H), lambda i: (0, 0)), _sds((nbk, H), F32), (onehot, dbias),
    )


def _bias_table_call(name, rel_bias, onehot):
    nbk, n = onehot.shape
    H = rel_bias.shape[1]

    def body(r_ref, o_ref, out_ref):
        out_ref[...] = lax.dot_general(r_ref[...], o_ref[...], _DIMS["tn"], precision=lax.Precision.HIGHEST, preferred_element_type=F32)

    return _call(
        name, body, (1,), [pl.BlockSpec((nbk, H), lambda i: (0, 0)), pl.BlockSpec((nbk, n), lambda i: (0, 0))],
        pl.BlockSpec((H, n), lambda i: (0, 0)), _sds((H, n), F32), (rel_bias, onehot),
    )


def _exchange_call(name, comm):
    n_src, n_dst = len(comm.srcs), len(comm.out_shape)

    def body(*refs):
        src_refs, dst_refs, sem_refs = refs[:n_src], refs[n_src:n_src + n_dst], refs[n_src + n_dst:]
        comm.start(src_refs, dst_refs, sem_refs)
        comm.wait(src_refs, dst_refs, sem_refs)

    hbm = pl.BlockSpec(memory_space=pltpu.HBM)
    return pl.pallas_call(
        body, name=name, in_specs=[hbm] * n_src, out_specs=[hbm] * n_dst, out_shape=list(comm.out_shape),
        scratch_shapes=list(comm.sems), compiler_params=pltpu.CompilerParams(has_side_effects=True),
    )(*comm.srcs)


def _all_gather_call(name, shards):
    n = len(shards)

    def body(*refs):
        in_refs, out_refs = refs[:n], refs[n:2 * n]
        send_sems, recv_sems, local_sems = refs[2 * n:]
        x, y, c = _mesh_pos()
        me, sibling = (x, y, c), (x, y, 1 - c)
        chips = [(1 - x, y), (x, 1 - y), (1 - x, 1 - y)]

        def copy(t, k, block, to, src=None):
            dst = out_refs[t].at[_slot(*block)]
            return pltpu.make_async_remote_copy(
                src_ref=dst if src is None else src, dst_ref=dst,
                send_sem=send_sems.at[7 * t + k], recv_sem=recv_sems.at[7 * t + k],
                device_id=to, device_id_type=pl.DeviceIdType.MESH,
            )

        mine = [pltpu.make_async_copy(in_refs[t], out_refs[t].at[_slot(*me)], local_sems.at[t]) for t in range(n)]
        for cp in mine:
            cp.start()
        first = []
        for t in range(n):
            first.append(copy(t, 0, me, sibling, src=in_refs[t]))
            first += [copy(t, 1 + j, me, (*chip, c), src=in_refs[t]) for j, chip in enumerate(chips)]
        for cp in first:
            cp.start()
        passed = []
        for j, chip in enumerate(chips):
            for t in range(n):
                copy(t, 1 + j, (*chip, c), me).wait_recv()
                fwd = copy(t, 4 + j, (*chip, c), sibling)
                fwd.start()
                passed.append(fwd)
        for t in range(n):
            copy(t, 0, sibling, me).wait_recv()
            for j, chip in enumerate(chips):
                copy(t, 4 + j, (*chip, 1 - c), me).wait_recv()
        for cp in first + passed:
            cp.wait_send()
        for cp in mine:
            cp.wait()

    hbm = pl.BlockSpec(memory_space=pltpu.HBM)
    return pl.pallas_call(
        body, name=name,
        in_specs=[hbm] * n, out_specs=[hbm] * n,
        out_shape=[_sds((N_DEV, *s.shape), s.dtype) for s in shards],
        scratch_shapes=[pltpu.SemaphoreType.DMA((7 * n,)), pltpu.SemaphoreType.DMA((7 * n,)), pltpu.SemaphoreType.DMA((n,))],
        compiler_params=pltpu.CompilerParams(has_side_effects=True),
    )(*shards)


def _all_reduce_small_call(name, pack):
    R, C = pack.shape

    def body(in_ref, out_ref, land_ref, send_sems, recv_sems):
        x, y, c = _mesh_pos()
        me = _slot(x, y, c)
        peers = _peers(x, y, c)
        land_ref[me] = in_ref[...]
        sends = [
            pltpu.make_async_remote_copy(
                src_ref=in_ref, dst_ref=land_ref.at[me], send_sem=send_sems.at[k], recv_sem=recv_sems.at[k],
                device_id=to, device_id_type=pl.DeviceIdType.MESH,
            )
            for k, to in enumerate(peers)
        ]
        for cp in sends:
            cp.start()
        for k, frm in enumerate(peers):
            pltpu.make_async_remote_copy(
                src_ref=in_ref, dst_ref=land_ref.at[_slot(*frm)], send_sem=send_sems.at[k], recv_sem=recv_sems.at[k],
                device_id=frm, device_id_type=pl.DeviceIdType.MESH,
            ).wait_recv()
        for cp in sends:
            cp.wait_send()
        total = land_ref[0]
        for d in range(1, N_DEV):
            total = total + land_ref[d]
        out_ref[...] = total

    vmem = pl.BlockSpec(memory_space=pltpu.VMEM)
    return pl.pallas_call(
        body, name=name, in_specs=[vmem], out_specs=vmem, out_shape=_sds((R, C), F32),
        scratch_shapes=[pltpu.VMEM((N_DEV, R, C), F32), pltpu.SemaphoreType.DMA((7,)), pltpu.SemaphoreType.DMA((7,))],
        compiler_params=pltpu.CompilerParams(has_side_effects=True, vmem_limit_bytes=VMEM_LIMIT_BYTES),
    )(pack)


def _adamw_call(name, grad, landed, w, m, v, layer, prev):
    L, R, C = w.shape
    tr = _div_tile(R, ADAM_BLOCK_BYTES // (C * 4))
    c1 = 1.0 / (1.0 - ADAM_B1 ** ADAM_STEP)
    c2 = 1.0 / (1.0 - ADAM_B2 ** ADAM_STEP)

    def body(g_ref, w_ref, m_ref, v_ref, *rest):
        go_ref, d_ref, mo_ref, vo_ref = rest[-4:]
        if landed:
            g = g_ref[0].astype(F32)
            for d in range(1, N_DEV):
                g = g + g_ref[d].astype(F32)
        else:
            g = g_ref[...]
        go_ref[...] = g
        mn = ADAM_B1 * m_ref[...] + (1.0 - ADAM_B1) * g
        vn = ADAM_B2 * v_ref[...] + (1.0 - ADAM_B2) * (g * g)
        mo_ref[...] = mn
        vo_ref[...] = vn
        d_ref[...] = -ADAM_LR * ((mn * c1) / (jnp.sqrt(vn * c2) + ADAM_EPS) + ADAM_WD * w_ref[...])

    blk = pl.BlockSpec((None, tr, C), lambda i: (layer, i, 0))
    gspec = pl.BlockSpec((N_DEV, tr, C), lambda i: (0, i, 0)) if landed else pl.BlockSpec((tr, C), lambda i: (i, 0))
    in_specs, ins, aliases = [gspec, blk, blk, blk], [grad, w, m, v], {}
    if prev is not None:
        in_specs += [pl.BlockSpec(memory_space=pl.ANY)] * 4
        ins += list(prev)
        aliases = {4 + q: q for q in range(4)}
    return _call(name, body, (R // tr,), in_specs, [blk] * 4, [_sds((L, R, C), F32)] * 4, ins, aliases=aliases)


def _t5_bucket(rel):
    nb = NUM_BUCKETS // 2
    n = -rel
    ret = jnp.where(n < 0, nb, 0)
    n = jnp.abs(n)
    max_exact = nb // 2
    nf = jnp.maximum(n, 1).astype(jnp.float32)
    large = max_exact + (jnp.log(nf / max_exact) / math.log(REL_MAX_DIST / max_exact) * (nb - max_exact)).astype(jnp.int32)
    large = jnp.minimum(large, nb - 1)
    return ret + jnp.where(n < max_exact, n, large)


def _band_buckets():
    i = jnp.arange(Q_BLOCK)[:, None]
    j = jnp.arange(2 * Q_BLOCK)[None, :]
    return _t5_bucket(j - Q_BLOCK - i)


def _to_heads(t, n_heads):
    S = t.shape[0]
    return t.reshape(S, n_heads, HEAD_DIM).transpose(1, 0, 2)


def _from_heads(t):
    H, S, _ = t.shape
    return t.transpose(1, 0, 2).reshape(S, H * HEAD_DIM)


def _gathered_vec(t):
    nd, L, n = t.shape
    return t.transpose(1, 0, 2).reshape(L, nd * n)


_MIXER_WEIGHTS = {0: ("conv_w_in", "conv_w_out"), 1: (), 2: ("attn_w_qkv", "attn_w_o")}
_FFN_UP_WEIGHTS = ("ffn_w_gate", "ffn_w_up")
_FFN_REST_WEIGHTS = ("ffn_w_down", "ple_w_proj", "ple_w_gate")
_TURNED = _FFN_UP_WEIGHTS + ("conv_w_in", "attn_w_qkv")
_SMALL_SHARDED = ("conv_b_in", "conv_w_dw", "conv_b_dw", "conv_ln_g", "conv_ln_b", "conv_b_out")


def _mixer_keys(i):
    return [(n, i // 3) for n in _MIXER_WEIGHTS[i % 3]]


def _step(x, p, target, wb, small_pack, small_shapes, V):
    S, D = x.shape
    depth = V["norm_mix"].shape[0]
    n_heads = D // HEAD_DIM
    n_kv = (wb["attn_w_qkv"].shape[1] * N_DEV - D) // (2 * HEAD_DIM)
    ng = len(POOL_WINDOWS)
    vec = lambda t, i: t[i][None, :]
    shard = lambda key: wb[key[0]][key[1]:key[1] + 1]

    keys_a0 = [(n, 0) for n in _FFN_UP_WEIGHTS]
    keys_b0 = [(n, 0) for n in _FFN_REST_WEIGHTS]
    first = _mixer_keys(0) + [("pool_w", 0)]
    gathered = _all_gather_call("all_gather0", [shard(k) for k in first] + [small_pack])
    W = dict(zip(first, gathered[:-1]))
    pw = W.pop(("pool_w", 0))
    wp = pw[:, 0].transpose(1, 0, 2, 3).reshape(pw.shape[2], pw.shape[3] * N_DEV, pw.shape[4])
    V = dict(V)
    for n, t in zip(_SMALL_SHARDED, _unpack(gathered[-1], small_shapes, lead=(N_DEV,))):
        if n == "conv_w_dw":
            V[n] = t.transpose(1, 2, 0, 3).reshape(t.shape[1], t.shape[2], -1)
        else:
            V[n] = _gathered_vec(t)

    buckets = _band_buckets()
    onehot = (buckets.reshape(1, -1) == jnp.arange(NUM_BUCKETS)[:, None]).astype(F32)
    bias_tab = _bias_table_call("bias_table", V["rel_bias"], onehot).reshape(n_heads, Q_BLOCK, 2 * Q_BLOCK)
    sinks3 = V["attn_sinks"].reshape(n_heads, 1, 1)
    pb = p.astype(BF16)

    def gather_next(keys):
        return _GatherOwn([shard(k) for k in keys]) if keys else None

    saved = []
    hb = _rms_call("rms_in", x, vec(V["norm_mix"], 0))
    for i in range(depth):
        kind, j = i % 3, i // 3
        sv = {"x0": x, "h0": hb}
        g_ffn = vec(V["norm_ffn"], i)
        if kind == 0:
            ride = i == 0
            res = _conv_in_call(f"conv_in{i}", hb, W["conv_w_in", j], 0, vec(V["conv_b_in"], j), comm=gather_next(keys_b0) if ride else None)
            if ride:
                res, bufs_b = res
            a_pre, g_pre, u1 = res
            res = _dwconv_call(
                f"dwconv{i}", u1, V["conv_w_dw"][j], vec(V["conv_b_dw"], j), vec(V["conv_ln_g"], j), vec(V["conv_ln_b"], j),
                comm=gather_next(keys_a0) if ride else None)
            if ride:
                res, bufs_a = res
            u2, u4 = res
            res = _fullmm_res_call(
                f"conv_out{i}", u4, W["conv_w_out", j], 0, x, vec(V["conv_b_out"], j), g_ffn,
                comm=_GatherForward(bufs_a + bufs_b) if ride else None)
            if ride:
                res, bufs = res
                W.update(zip(keys_a0 + keys_b0, bufs))
            x1, h2 = res
            sv.update(a_pre=a_pre, g_pre=g_pre, u1=u1, u2=u2, u4=u4)
        elif kind == 1:
            mix = _pool_mix_call(f"pool_mix{i}", x, vec(V["norm_mix"], i))
            x1, ypre, h2 = _pool_out_call(f"pool_out{i}", mix, wp, vec(V["pool_scale"], j), x, g_ffn)
            sv.update(mix=mix, ypre=ypre)
        else:
            qkv = _colmm_call(f"qkv{i}", hb, W["attn_w_qkv", j], 0, F32, turned=True)
            q = _to_heads(qkv[:, :D], n_heads)
            k = _to_heads(qkv[:, D:D + n_kv * HEAD_DIM], n_kv)
            v = _to_heads(qkv[:, D + n_kv * HEAD_DIM:], n_kv)
            o = _attn_fwd_call(f"attn{i}", q, k, v, vec(V["attn_q_norm"], j), vec(V["attn_k_norm"], j), bias_tab, sinks3)
            ob = _from_heads(o)
            x1, h2 = _fullmm_res_call(f"attn_out{i}", ob, W["attn_w_o", j], 0, x, None, g_ffn)
            sv.update(q=q, k=k, v=v, ob=ob)
        more = i + 1 < depth
        keys_a = [(n, i + 1) for n in _FFN_UP_WEIGHTS] if more else []
        keys_b = [(n, i + 1) for n in _FFN_REST_WEIGHTS] + _mixer_keys(i + 1) if more else []
        res = _ffn_up_call(f"ffn_up{i}", h2, W["ffn_w_gate", i], W["ffn_w_up", i], 0, comm=gather_next(keys_a))
        if more:
            (gpre, upre, act), bufs_a = res
        else:
            gpre, upre, act = res
        res = _mm_rows_call(f"ffn_down{i}", act, W["ffn_w_down", i], 0, None, comm=gather_next(keys_b))
        if more:
            y, bufs_b = res
        else:
            y = res
        res = _res_rms_call(f"ffn_res{i}", y, x1, vec(V["norm_ple"], i), comm=_GatherForward(bufs_a + bufs_b) if more else None)
        if more:
            res, bufs = res
            W.update(zip(keys_a + keys_b, bufs))
        x2, h3 = res
        pp = _colmm_call(f"ple_proj{i}", pb[i], W["ple_w_proj", i], 0, BF16)
        g_next = vec(V["norm_mix"], i + 1) if more else None
        x3, gate, hb_next = _ple_gate_call(f"ple_gate{i}", h3, W["ple_w_gate", i], 0, x2, vec(V["ple_b_gate"], i), pp, g_next)
        sv.update(x1=x1, h2=h2, gpre=gpre, upre=upre, act=act, x2=x2, h3=h3, pp=pp, gate=gate)
        saved.append(sv)
        x, hb = x3, hb_next

    dx, dxb, loss_tile = _loss_call("loss", x, target)
    loss = loss_tile[0, 0]

    landed = {}
    GV = {n: [None] * V[n].shape[0] for n in ("norm_mix", "norm_ffn", "norm_ple", "ple_b_gate", "conv_b_in", "conv_w_dw", "conv_b_dw", "conv_ln_g", "conv_ln_b", "conv_b_out", "pool_scale")}

    def scatter_of(pending):
        return _ScatterSlots([g for _, g in pending]) if pending else None

    def record(pending, bufs):
        landed.update(zip([k for k, _ in pending], bufs))

    mixer_pending = []
    for i in reversed(range(depth)):
        kind, j = i % 3, i // 3
        sv = saved[i]
        dz, dpp, db_gate = _ple_bwd_elt_call(f"ple_bwd{i}", dx, sv["gate"], sv["pp"])
        GV["ple_b_gate"][i] = db_gate
        pending = [(("ple_w_gate", i), _grad_w_call(f"g_ple_gate{i}", sv["h3"], "nat", dz, "full"))]
        pending.append((("ple_w_proj", i), _grad_w_call(f"g_ple_proj{i}", pb[i], "full", dpp, "nat")))
        dx, dxb, dg = _dx_full_rms_call(f"d_x2_{i}", dz, W["ple_w_gate", i], 0, sv["x2"], vec(V["norm_ple"], i), dx)
        GV["norm_ple"][i] = dg
        res = _ffn_bwd_hidden_call(f"ffn_bwd{i}", dxb, W["ffn_w_down", i], 0, sv["gpre"], sv["upre"], comm=scatter_of(mixer_pending))
        if mixer_pending:
            (dgp, dup), bufs = res
            record(mixer_pending, bufs)
        else:
            dgp, dup = res
        g_down = _grad_rows_call(f"g_ffn_down{i}", sv["act"], dxb)
        g_gate, bufs = _grad_rows_call(f"g_ffn_gate{i}", dgp, sv["h2"], comm=scatter_of(pending))
        record(pending, bufs)
        pending = [(("ffn_w_down", i), g_down)]
        g_up, bufs = _grad_rows_call(f"g_ffn_up{i}", dup, sv["h2"], comm=scatter_of(pending))
        record(pending, bufs)
        pending = [(("ffn_w_gate", i), g_gate)]
        dh, bufs = _mm_rows_call(f"d_h2g_{i}", dgp, W["ffn_w_gate", i], 0, None, comm=scatter_of(pending))
        record(pending, bufs)
        pending = [(("ffn_w_up", i), g_up)]
        dh, bufs = _mm_rows_call(f"d_h2_{i}", dup, W["ffn_w_up", i], 0, dh, comm=scatter_of(pending))
        record(pending, bufs)
        dx, dxb, dg, colsum = _rms_bwd_res_call(f"d_x1_{i}", dh, sv["x1"], vec(V["norm_ffn"], i), dx, kind == 0)
        GV["norm_ffn"][i] = dg
        g_mix = vec(V["norm_mix"], i)
        if kind == 0:
            GV["conv_b_out"][j] = colsum
            g_out = _grad_w_call(f"g_conv_out{i}", sv["u4"], "nat", dxb, "full")
            du2, d_ln_g, d_ln_b, d_b_dw = _conv_out_bwd_call(f"conv_out_bwd{i}", dxb, W["conv_w_out", j], 0, sv["u2"], vec(V["conv_ln_g"], j), vec(V["conv_ln_b"], j))
            du, d_b_in, d_w_dw = _dwconv_bwd_call(f"dwconv_bwd{i}", du2, sv["u1"], V["conv_w_dw"][j], sv["a_pre"], sv["g_pre"])
            GV["conv_ln_g"][j], GV["conv_ln_b"][j], GV["conv_b_dw"][j] = d_ln_g, d_ln_b, d_b_dw
            GV["conv_b_in"][j], GV["conv_w_dw"][j] = d_b_in, d_w_dw[:CONV_WIDTH]
            g_in = _grad_rows_call(f"g_conv_in{i}", du, sv["h0"])
            mixer_pending = [(("conv_w_in", j), g_in), (("conv_w_out", j), g_out)]
            dh = _mm_rows_call(f"d_h0_{i}", du, W["conv_w_in", j], 0, None, comm=scatter_of(mixer_pending) if i == 0 else None)
            if i == 0:
                dh, bufs = dh
                record(mixer_pending, bufs)
                mixer_pending = []
            dx, dxb, dg, _ = _rms_bwd_res_call(f"d_x0_{i}", dh, sv["x0"], g_mix, dx, False)
        elif kind == 1:
            dmix, dyp, d_scale = _pool_out_bwd_call(f"pool_out_bwd{i}", dx, sv["ypre"], vec(V["pool_scale"], j), wp)
            GV["pool_scale"][j] = d_scale
            g_pool = _pool_w_grad_call(f"g_pool_w{i}", sv["mix"], dyp, ng)
            gc = g_pool.shape[1]
            g_pool = g_pool.reshape(ng, N_DEV, gc // N_DEV, gc).transpose(1, 0, 2, 3).reshape(N_DEV, ng * (gc // N_DEV), gc)
            mixer_pending = [(("pool_w", j), g_pool.astype(BF16))]
            dx, dxb, dg = _pool_mix_bwd_call(f"pool_mix_bwd{i}", dmix, sv["x0"], g_mix, dx)
        else:
            g_o = _grad_w_call(f"g_attn_o{i}", sv["ob"], "nat", dxb, "full")
            do = _to_heads(_dx_full_call(f"d_attn_o{i}", dxb, W["attn_w_o", j], 0, F32), n_heads)
            dq, dk, dv, dqg, dkg, dbias, dsink = _attn_bwd_call(
                f"attn_bwd{i}", sv["q"], sv["k"], sv["v"], do, vec(V["attn_q_norm"], j), vec(V["attn_k_norm"], j), bias_tab, sinks3)
            GV["attn_q_norm"] = jnp.sum(dqg, axis=0)
            GV["attn_k_norm"] = jnp.sum(dkg, axis=0)
            GV["attn_sinks"] = dsink.reshape(1, n_heads)
            GV["rel_bias"] = _bucket_sum_call(f"g_rel_bias{i}", onehot, dbias.reshape(n_heads, -1))
            dqkv = jnp.concatenate([_from_heads(dq), _from_heads(dk), _from_heads(dv)], axis=-1).astype(BF16)
            g_qkv = _grad_rows_call(f"g_qkv{i}", dqkv, sv["h0"])
            mixer_pending = [(("attn_w_qkv", j), g_qkv), (("attn_w_o", j), g_o)]
            dh = _mm_rows_call(f"d_h0_{i}", dqkv, W["attn_w_qkv", j], 0, None)
            dx, dxb, dg, _ = _rms_bwd_res_call(f"d_x0_{i}", dh, sv["x0"], g_mix, dx, False)
        GV["norm_mix"][i] = dg
    if mixer_pending:
        record(mixer_pending, _exchange_call("grad_scatter_tail", scatter_of(mixer_pending)))
    return loss, dx, landed, GV, V


_BIG = ("conv_w_in", "conv_w_out", "pool_w", "attn_w_qkv", "attn_w_o", "ffn_w_gate", "ffn_w_up", "ffn_w_down", "ple_w_proj", "ple_w_gate")
_SMALL_REPLICATED = ("norm_mix", "norm_ffn", "norm_ple", "pool_scale", "attn_q_norm", "attn_k_norm", "attn_sinks", "rel_bias", "ple_b_gate")
_WEIGHTS = ("norm_mix", "norm_ffn", "norm_ple", "conv_w_in", "conv_b_in", "conv_w_dw", "conv_b_dw", "conv_ln_g", "conv_ln_b", "conv_w_out",
            "conv_b_out", "pool_w", "pool_scale", "attn_w_qkv", "attn_q_norm", "attn_k_norm", "attn_sinks", "attn_w_o", "rel_bias",
            "ffn_w_gate", "ffn_w_up", "ffn_w_down", "ple_w_proj", "ple_w_gate", "ple_b_gate")
PACK_LANES = 128


def _pack(parts):
    flat = jnp.concatenate([t.reshape(-1).astype(F32) for t in parts])
    rows = -(-flat.shape[0] // (8 * PACK_LANES)) * 8
    flat = jnp.pad(flat, (0, rows * PACK_LANES - flat.shape[0]))
    return flat.reshape(rows, PACK_LANES)


def _unpack(pack, shapes, lead=()):
    flat = pack.reshape(*lead, -1)
    out, pos = [], 0
    for s in shapes:
        n = math.prod(s)
        out.append(flat[..., pos:pos + n].reshape(*lead, *s))
        pos += n
    return out


def _as2d(t):
    return t.reshape(-1, t.shape[-1])


def kernel(x, p, norm_mix, norm_ffn, norm_ple, conv_w_in, conv_b_in, conv_w_dw, conv_b_dw, conv_ln_g, conv_ln_b, conv_w_out, conv_b_out, pool_w, pool_scale, attn_w_qkv, attn_q_norm, attn_k_norm, attn_sinks, attn_w_o, rel_bias, ffn_w_gate, ffn_w_up, ffn_w_down, ple_w_proj, ple_w_gate, ple_b_gate, loss_target, m_norm_mix, m_norm_ffn, m_norm_ple, m_conv_w_in, m_conv_b_in, m_conv_w_dw, m_conv_b_dw, m_conv_ln_g, m_conv_ln_b, m_conv_w_out, m_conv_b_out, m_pool_w, m_pool_scale, m_attn_w_qkv, m_attn_q_norm, m_attn_k_norm, m_attn_sinks, m_attn_w_o, m_rel_bias, m_ffn_w_gate, m_ffn_w_up, m_ffn_w_down, m_ple_w_proj, m_ple_w_gate, m_ple_b_gate, v_norm_mix, v_norm_ffn, v_norm_ple, v_conv_w_in, v_conv_b_in, v_conv_w_dw, v_conv_b_dw, v_conv_ln_g, v_conv_ln_b, v_conv_w_out, v_conv_b_out, v_pool_w, v_pool_scale, v_attn_w_qkv, v_attn_q_norm, v_attn_k_norm, v_attn_sinks, v_attn_w_o, v_rel_bias, v_ffn_w_gate, v_ffn_w_up, v_ffn_w_down, v_ple_w_proj, v_ple_w_gate, v_ple_b_gate):
    given = dict(locals())
    w = {n: given[n] for n in _WEIGHTS}
    m = {n: given["m_" + n] for n in _WEIGHTS}
    v = {n: given["v_" + n] for n in _WEIGHTS}
    me = _slot(*_mesh_pos())

    wb = {n: w[n].astype(BF16) for n in _BIG}
    for n in _TURNED:
        wb[n] = wb[n].transpose(0, 2, 1)
    small_pack = _pack([w[n] for n in _SMALL_SHARDED])
    small_shapes = [w[n].shape for n in _SMALL_SHARDED]
    loss, grad_x, landed, GV, V = _step(x[0], p[:, 0], loss_target[0], wb, small_pack, small_shapes, {n: w[n] for n in _SMALL_REPLICATED})
    loss = lax.psum(loss, ("x", "y", "c"))

    small_names = list(_SMALL_REPLICATED) + list(_SMALL_SHARDED)
    small_full = []
    for n in small_names:
        g = GV[n]
        g = jnp.stack([t.reshape(V[n].shape[1:]) for t in g]) if isinstance(g, list) else g.reshape(V[n].shape)
        small_full.append(g)
    reduced = _unpack(_all_reduce_small_call("all_reduce_small", _pack(small_full)), [t.shape for t in small_full])
    small_grad = {}
    for n, g in zip(small_names, reduced):
        if n in _SMALL_SHARDED:
            c = w[n].shape[-1]
            g = lax.dynamic_slice_in_dim(g, me * c, c, axis=g.ndim - 1)
        small_grad[n] = g

    out = {}
    for n in _BIG:
        layers = w[n].shape[0]
        turned = n in _TURNED
        as3d = lambda t: t.transpose(0, 2, 1) if turned else t.reshape(layers, -1, t.shape[-1])
        res = None
        for l in range(layers):
            res = _adamw_call(f"adamw_{n}{l}", landed[n, l], True, as3d(w[n]), as3d(m[n]), as3d(v[n]), l, res)
        out[n] = [t.transpose(0, 2, 1) if turned else t.reshape(w[n].shape) for t in res]
    for n in small_names:
        res = _adamw_call(f"adamw_{n}", _as2d(small_grad[n]), False, _as2d(w[n])[None], _as2d(m[n])[None], _as2d(v[n])[None], 0, None)
        out[n] = [t.reshape(w[n].shape) for t in res]

    grads = [out[n][0] for n in _WEIGHTS]
    deltas = [out[n][1] for n in _WEIGHTS]
    new_m = [out[n][2] for n in _WEIGHTS]
    new_v = [out[n][3] for n in _WEIGHTS]
    return (loss, grad_x[None], *grads, *deltas, *new_m, *new_v)
```

```python
import functools
import math

import jax
import jax.numpy as jnp
from jax import lax
from jax.experimental import pallas as pl
from jax.experimental.pallas import tpu as pltpu

F32, BF16 = jnp.float32, jnp.bfloat16
N_DEV = 8
EPS = 1e-6
NEG_INF = -1e30
HEAD_DIM = 64
Q_BLOCK = 128
CHUNK = 64
WINDOW_CHUNKS = 2
CONV_WIDTH = 31
CONV_HALO = 32
POOL_WINDOWS = (2, 4, 8, 16)
POOL_HALO = 16
NUM_BUCKETS = 32
REL_MAX_DIST = 128
ADAM_LR, ADAM_B1, ADAM_B2, ADAM_EPS, ADAM_WD, ADAM_STEP = 0.001, 0.9, 0.999, 1e-08, 0.01, 10
VMEM_LIMIT_BYTES = 44 * 1024 * 1024
BIG_VMEM_LIMIT_BYTES = 52 * 1024 * 1024
ROW_TILE = 512
FULL_ROW_TILE = 256
ELT_TILE = 256
CONV_TILE = 128
CONV_SUB = 32
CONV_LANES = 512
COL_TILE = 1024
GRAD_COL_TILE = 1024
K_TILE = 2048
ADAM_BLOCK_BYTES = 1 << 20

_DIMS = {
    "nn": (((1,), (0,)), ((), ())),
    "nt": (((1,), (1,)), ((), ())),
    "tn": (((0,), (0,)), ((), ())),
}


def _dot(a, b, mode):
    return lax.dot_general(a, b, _DIMS[mode], preferred_element_type=F32)


def _tile(n, t):
    t = min(n, t)
    assert n % t == 0, (n, t)
    return t


def _div_tile(n, t):
    if n <= t:
        return n
    for cand in range(t // 16 * 16, 15, -16):
        if n % cand == 0:
            return cand
    return n


def _sds(shape, dtype):
    return jax.ShapeDtypeStruct(tuple(shape), dtype)


def _call(name, body, grid, in_specs, out_specs, out_shape, ins, scratch=(), comm=None, vmem=VMEM_LIMIT_BYTES, aliases=None):
    params = pltpu.CompilerParams(dimension_semantics=("arbitrary",) * len(grid), vmem_limit_bytes=vmem)
    if comm is None:
        return pl.pallas_call(
            body, name=name, grid=grid, in_specs=list(in_specs), out_specs=out_specs, out_shape=out_shape,
            scratch_shapes=list(scratch), compiler_params=params, input_output_aliases=aliases or {},
        )(*ins)
    assert not aliases
    single = not isinstance(out_shape, (list, tuple))
    own_specs = [out_specs] if single else list(out_specs)
    own_shape = [out_shape] if single else list(out_shape)
    n_in, n_out, n_scr = len(ins), len(own_shape), len(scratch)
    n_src, n_dst = len(comm.srcs), len(comm.out_shape)
    hbm = pl.BlockSpec(memory_space=pltpu.HBM)

    def with_comm(*refs):
        a = n_in
        b = a + n_src
        c = b + n_out
        d = c + n_dst
        e = d + n_scr
        src_refs, dst_refs, sem_refs = refs[a:b], refs[c:d], refs[e:]
        first = functools.reduce(jnp.logical_and, [pl.program_id(i) == 0 for i in range(len(grid))])
        last = functools.reduce(jnp.logical_and, [pl.program_id(i) == grid[i] - 1 for i in range(len(grid))])

        @pl.when(first)
        def _():
            comm.start(src_refs, dst_refs, sem_refs)

        body(*refs[:a], *refs[b:c], *refs[d:e])

        @pl.when(last)
        def _():
            comm.wait(src_refs, dst_refs, sem_refs)

    res = pl.pallas_call(
        with_comm, name=name, grid=grid,
        in_specs=list(in_specs) + [hbm] * n_src,
        out_specs=own_specs + [hbm] * n_dst,
        out_shape=own_shape + list(comm.out_shape),
        scratch_shapes=list(scratch) + list(comm.sems),
        compiler_params=params,
        input_output_aliases={n_in + t: n_out + t for t in range(n_src)} if getattr(comm, "aliased", False) else {},
    )(*ins, *comm.srcs)
    own = res[0] if single else list(res[:n_out])
    return own, list(res[n_out:])


def _mesh_pos():
    return lax.axis_index("x"), lax.axis_index("y"), lax.axis_index("c")


def _slot(px, py, pc):
    return 4 * px + 2 * py + pc


def _peers(x, y, c):
    flips = [(fx, fy, fc) for fx in (0, 1) for fy in (0, 1) for fc in (0, 1)][1:]
    return [(1 - x if fx else x, 1 - y if fy else y, 1 - c if fc else c) for fx, fy, fc in flips]


def _remote(src, dst, send_sem, recv_sem, to):
    return pltpu.make_async_remote_copy(
        src_ref=src, dst_ref=dst, send_sem=send_sem, recv_sem=recv_sem, device_id=to, device_id_type=pl.DeviceIdType.MESH
    )


class _GatherOwn:
    N_TO = 4

    def __init__(self, shards):
        n = len(shards)
        self.srcs = list(shards)
        self.out_shape = [_sds((N_DEV, *s.shape), s.dtype) for s in shards]
        self.sems = [pltpu.SemaphoreType.DMA((self.N_TO * n,)), pltpu.SemaphoreType.DMA((self.N_TO * n,)), pltpu.SemaphoreType.DMA((n,))]

    def _copies(self, src_refs, dst_refs, sem_refs):
        send_sems, recv_sems, local_sems = sem_refs
        x, y, c = _mesh_pos()
        me = (x, y, c)
        targets = [(x, y, 1 - c), (1 - x, y, c), (x, 1 - y, c), (1 - x, 1 - y, c)]
        sends, recvs, local = [], [], []
        for t, (src, dst) in enumerate(zip(src_refs, dst_refs)):
            local.append(pltpu.make_async_copy(src, dst.at[_slot(*me)], local_sems.at[t]))
            for k, to in enumerate(targets):
                s = self.N_TO * t + k
                sends.append(_remote(src, dst.at[_slot(*me)], send_sems.at[s], recv_sems.at[s], to))
                recvs.append(_remote(src, dst.at[_slot(*to)], send_sems.at[s], recv_sems.at[s], to))
        return sends, recvs, local

    def start(self, src_refs, dst_refs, sem_refs):
        sends, _, local = self._copies(src_refs, dst_refs, sem_refs)
        for cp in local + sends:
            cp.start()

    def wait(self, src_refs, dst_refs, sem_refs):
        sends, recvs, local = self._copies(src_refs, dst_refs, sem_refs)
        for cp in recvs:
            cp.wait_recv()
        for cp in sends:
            cp.wait_send()
        for cp in local:
            cp.wait()


class _GatherForward:
    aliased = True

    def __init__(self, bufs):
        n = len(bufs)
        self.srcs = list(bufs)
        self.out_shape = [_sds(b.shape, b.dtype) for b in bufs]
        self.sems = [pltpu.SemaphoreType.DMA((3 * n,)), pltpu.SemaphoreType.DMA((3 * n,))]

    def _copies(self, src_refs, dst_refs, sem_refs):
        send_sems, recv_sems = sem_refs
        x, y, c = _mesh_pos()
        sibling = (x, y, 1 - c)
        sends, recvs = [], []
        for t, (src, dst) in enumerate(zip(src_refs, dst_refs)):
            for j, chip in enumerate([(1 - x, y), (x, 1 - y), (1 - x, 1 - y)]):
                s = 3 * t + j
                mine, theirs = _slot(*chip, c), _slot(*chip, 1 - c)
                sends.append(_remote(src.at[mine], dst.at[mine], send_sems.at[s], recv_sems.at[s], sibling))
                recvs.append(_remote(src.at[mine], dst.at[theirs], send_sems.at[s], recv_sems.at[s], sibling))
        return sends, recvs, []

    start = _GatherOwn.start
    wait = _GatherOwn.wait


class _ScatterSlots:
    def __init__(self, grads):
        n = len(grads)
        self.srcs = list(grads)
        self.out_shape = [_sds(g.shape, g.dtype) for g in grads]
        self.sems = [pltpu.SemaphoreType.DMA((7 * n,)), pltpu.SemaphoreType.DMA((7 * n,)), pltpu.SemaphoreType.DMA((n,))]

    def _copies(self, src_refs, dst_refs, sem_refs):
        send_sems, recv_sems, local_sems = sem_refs
        x, y, c = _mesh_pos()
        me = _slot(x, y, c)
        sends, recvs, local = [], [], []
        for t, (src, dst) in enumerate(zip(src_refs, dst_refs)):
            local.append(pltpu.make_async_copy(src.at[me], dst.at[me], local_sems.at[t]))
            for k, to in enumerate(_peers(x, y, c)):
                s = 7 * t + k
                sends.append(_remote(src.at[_slot(*to)], dst.at[me], send_sems.at[s], recv_sems.at[s], to))
                recvs.append(_remote(src.at[me], dst.at[_slot(*to)], send_sems.at[s], recv_sems.at[s], to))
        return sends, recvs, local

    start = _GatherOwn.start
    wait = _GatherOwn.wait


def _sig(x):
    return 1.0 / (1.0 + jnp.exp(-x))


def _rms(x, g):
    r = lax.rsqrt(jnp.mean(x * x, axis=-1, keepdims=True) + EPS)
    return x * r * g


def _rms_bwd(dy, x, g):
    r = lax.rsqrt(jnp.mean(x * x, axis=-1, keepdims=True) + EPS)
    xh = x * r
    dg = jnp.sum(dy * xh, axis=0, keepdims=True)
    dxh = dy * g
    dx = r * (dxh - xh * jnp.mean(dxh * xh, axis=-1, keepdims=True))
    return dx, dg


def _accumulate(ref, val, first):
    @pl.when(first)
    def _():
        ref[...] = val

    @pl.when(jnp.logical_not(first))
    def _():
        ref[...] += val


def _kloop(k, nk, acc_refs, contribs, finish):
    @pl.when(k == 0)
    def _():
        for r, c in zip(acc_refs, contribs):
            r[...] = c

    @pl.when(k > 0)
    def _():
        for r, c in zip(acc_refs, contribs):
            r[...] += c

    @pl.when(k == nk - 1)
    def _():
        finish()


def _rms_call(name, x, g):
    S, D = x.shape
    ts = _tile(S, ELT_TILE)

    def body(x_ref, g_ref, o_ref):
        o_ref[...] = _rms(x_ref[...], g_ref[...]).astype(BF16)

    return _call(
        name, body, (S // ts,),
        [pl.BlockSpec((ts, D), lambda i: (i, 0)), pl.BlockSpec((1, D), lambda i: (0, 0))],
        pl.BlockSpec((ts, D), lambda i: (i, 0)), _sds((S, D), BF16), (x, g),
    )


def _conv_in_call(name, hb, w, layer, b_in, comm=None):
    S, D = hb.shape
    nsh = w.shape[2]
    half = N_DEV // 2
    assert nsh * half == D
    tm = _tile(S, ROW_TILE)

    def body(h_ref, wa_ref, wg_ref, ba_ref, bg_ref, a_ref, g_ref, u_ref):
        h = h_ref[...]
        a = _dot(h, wa_ref[...], "nt") + ba_ref[...]
        g = _dot(h, wg_ref[...], "nt") + bg_ref[...]
        a_ref[...] = a.astype(BF16)
        g_ref[...] = g.astype(BF16)
        u_ref[...] = a * _sig(g)

    out_spec = pl.BlockSpec((tm, nsh), lambda d, i: (i, d))
    return _call(
        name, body, (half, S // tm),
        [
            pl.BlockSpec((tm, D), lambda d, i: (i, 0)),
            pl.BlockSpec((None, None, nsh, D), lambda d, i: (d, layer, 0, 0)),
            pl.BlockSpec((None, None, nsh, D), lambda d, i: (d + half, layer, 0, 0)),
            pl.BlockSpec((1, nsh), lambda d, i: (0, d)),
            pl.BlockSpec((1, nsh), lambda d, i: (0, d + half)),
        ],
        [out_spec, out_spec, out_spec],
        [_sds((S, D), BF16), _sds((S, D), BF16), _sds((S, D), F32)],
        (hb, w, w, b_in, b_in), comm=comm,
    )


SUBLANES = 8


def _fill_shifted(sh_ref, ext_ref, rows):
    ext_ref[rows:, :] = jnp.zeros((SUBLANES, ext_ref.shape[1]), F32)
    for s in range(SUBLANES):
        sh_ref[s] = ext_ref[pl.ds(s, rows), :]


def _shifted_window(sh_ref, row, lane0):
    s = row % SUBLANES
    return sh_ref[s, pl.ds(row - s, CONV_SUB), pl.ds(lane0, CONV_LANES)]


def _conv_taps(sh_ref, w_ref, row0, lane0, offset, reverse):
    acc = None
    for k in range(CONV_WIDTH):
        off = offset - k if reverse else offset + k
        term = w_ref[k:k + 1, lane0:lane0 + CONV_LANES] * _shifted_window(sh_ref, row0 + off, lane0)
        acc = term if acc is None else acc + term
    return acc


def _dwconv_call(name, u1, w_dw, b_dw, ln_g, ln_b, comm=None):
    S, D = u1.shape
    tc = _tile(S, CONV_TILE)
    hb = tc // CONV_HALO
    lanes = min(D, CONV_LANES)
    assert lanes == CONV_LANES and D % CONV_LANES == 0 and tc % CONV_SUB == 0

    def body(cur_ref, halo_ref, w_ref, b_ref, g_ref, bb_ref, u2_ref, u4_ref, ext_ref, sh_ref):
        i = pl.program_id(0)
        ext_ref[0:CONV_HALO, :] = jnp.where(i > 0, halo_ref[...], 0.0)
        ext_ref[CONV_HALO:tc + CONV_HALO, :] = cur_ref[...]
        _fill_shifted(sh_ref, ext_ref, tc + CONV_HALO)
        first_tap = CONV_HALO - (CONV_WIDTH - 1)
        for r in range(0, tc, CONV_SUB):
            for c in range(0, D, CONV_LANES):
                u2_ref[r:r + CONV_SUB, c:c + CONV_LANES] = (
                    _conv_taps(sh_ref, w_ref, r, c, first_tap, False) + b_ref[:, c:c + CONV_LANES]
                )
        u2 = u2_ref[...]
        mu = jnp.mean(u2, axis=-1, keepdims=True)
        xc = u2 - mu
        u3 = xc * lax.rsqrt(jnp.mean(xc * xc, axis=-1, keepdims=True) + EPS) * g_ref[...] + bb_ref[...]
        u4_ref[...] = (u3 * _sig(u3)).astype(BF16)

    vec = pl.BlockSpec((1, D), lambda i: (0, 0))
    row = pl.BlockSpec((tc, D), lambda i: (i, 0))
    return _call(
        name, body, (S // tc,),
        [
            row,
            pl.BlockSpec((CONV_HALO, D), lambda i: (jnp.maximum(i * hb - 1, 0), 0)),
            pl.BlockSpec((CONV_WIDTH, D), lambda i: (0, 0)),
            vec, vec, vec,
        ],
        [row, row],
        [_sds((S, D), F32), _sds((S, D), BF16)],
        (u1, u1, w_dw, b_dw, ln_g, ln_b),
        scratch=[pltpu.VMEM((tc + CONV_HALO + SUBLANES, D), F32), pltpu.VMEM((SUBLANES, tc + CONV_HALO, D), F32)], comm=comm,
    )


def _full_weight_spec(w, layer):
    _, _, ksh, D = w.shape
    return pl.BlockSpec((N_DEV, None, ksh, D), lambda i: (0, layer, 0, 0), pipeline_mode=pl.Buffered(1))


def _fullmm_res_call(name, a, w, layer, x_res, bias, g_next, comm=None):
    S, D = x_res.shape
    K = a.shape[1]
    tm = _tile(S, FULL_ROW_TILE)
    has_b = bias is not None

    def body(*refs):
        a_ref, w_ref, x_ref = refs[:3]
        b_ref = refs[3] if has_b else None
        g_ref, xo_ref, ho_ref = refs[3 + int(has_b):]
        xn = x_ref[...] + _dot(a_ref[...], w_ref[...].reshape(K, D), "nn")
        if has_b:
            xn = xn + b_ref[...]
        xo_ref[...] = xn
        ho_ref[...] = _rms(xn, g_ref[...]).astype(BF16)

    row = pl.BlockSpec((tm, D), lambda i: (i, 0))
    vec = pl.BlockSpec((1, D), lambda i: (0, 0))
    in_specs = [pl.BlockSpec((tm, K), lambda i: (i, 0)), _full_weight_spec(w, layer), row] + ([vec] if has_b else []) + [vec]
    ins = [a, w, x_res] + ([bias] if has_b else []) + [g_next]
    return _call(name, body, (S // tm,), in_specs, [row, row], [_sds((S, D), F32), _sds((S, D), BF16)], ins, comm=comm)


SHARD_PAIR = 2


def _pair_spec(w, layer):
    _, _, fsh, D = w.shape
    return pl.BlockSpec((SHARD_PAIR, None, fsh, D), lambda d, i: (d, layer, 0, 0))


def _ffn_up_call(name, hb, wgt, wut, layer, comm=None):
    S, D = hb.shape
    fsh = wgt.shape[2]
    tn = SHARD_PAIR * fsh
    tm = _tile(S, ROW_TILE)

    def body(h_ref, wg_ref, wu_ref, g_ref, u_ref, a_ref):
        h = h_ref[...]
        g = _dot(h, wg_ref[...].reshape(tn, D), "nt")
        u = _dot(h, wu_ref[...].reshape(tn, D), "nt")
        g_ref[...] = g.astype(BF16)
        u_ref[...] = u.astype(BF16)
        a_ref[...] = (g * _sig(g) * u).astype(BF16)

    ospec = pl.BlockSpec((tm, tn), lambda d, i: (i, d))
    return _call(
        name, body, (N_DEV // SHARD_PAIR, S // tm),
        [pl.BlockSpec((tm, D), lambda d, i: (i, 0)), _pair_spec(wgt, layer), _pair_spec(wut, layer)],
        [ospec] * 3, [_sds((S, N_DEV * fsh), BF16)] * 3, (hb, wgt, wut), comm=comm,
    )


def _mm_rows_call(name, a, w, layer, add, comm=None):
    S, K = a.shape
    _, _, ksh, N = w.shape
    assert K == N_DEV * ksh
    tm = _tile(S, ROW_TILE)
    tn = _tile(N, COL_TILE)
    has_add = add is not None

    def body(*refs):
        a_ref, w_ref = refs[:2]
        o_ref = refs[-1]
        y = _dot(a_ref[...], w_ref[...].reshape(K, tn), "nn")
        o_ref[...] = y + refs[2][...] if has_add else y

    tile = pl.BlockSpec((tm, tn), lambda i, n: (i, n))
    in_specs = [pl.BlockSpec((tm, K), lambda i, n: (i, 0)), pl.BlockSpec((N_DEV, None, ksh, tn), lambda i, n: (0, layer, 0, n))]
    return _call(
        name, body, (S // tm, N // tn), in_specs + ([tile] if has_add else []), tile, _sds((S, N), F32),
        (a, w) + ((add,) if has_add else ()), comm=comm, vmem=BIG_VMEM_LIMIT_BYTES,
    )


def _res_rms_call(name, y, x_res, g_next, comm=None):
    S, D = x_res.shape
    ts = _tile(S, ELT_TILE)

    def body(y_ref, x_ref, g_ref, xo_ref, ho_ref):
        xn = x_ref[...] + y_ref[...]
        xo_ref[...] = xn
        ho_ref[...] = _rms(xn, g_ref[...]).astype(BF16)

    row = pl.BlockSpec((ts, D), lambda i: (i, 0))
    return _call(
        name, body, (S // ts,), [row, row, pl.BlockSpec((1, D), lambda i: (0, 0))],
        [row, row], [_sds((S, D), F32), _sds((S, D), BF16)], (y, x_res, g_next), comm=comm,
    )


def _colmm_call(name, a, w, layer, out_dtype, turned=False):
    S, K = a.shape
    nsh = w.shape[2] if turned else w.shape[3]
    tm = _tile(S, ROW_TILE)
    mode = "nt" if turned else "nn"

    def body(a_ref, w_ref, o_ref):
        o_ref[...] = _dot(a_ref[...], w_ref[...], mode).astype(out_dtype)

    return _call(
        name, body, (N_DEV, S // tm),
        [pl.BlockSpec((tm, K), lambda d, i: (i, 0)), pl.BlockSpec((None, None, *w.shape[2:]), lambda d, i: (d, layer, 0, 0))],
        pl.BlockSpec((tm, nsh), lambda d, i: (i, d)), _sds((S, N_DEV * nsh), out_dtype), (a, w),
    )


def _ple_gate_call(name, hb, w, layer, x_res, bias, pp, g_next):
    S, D = x_res.shape
    K = hb.shape[1]
    tm = _tile(S, FULL_ROW_TILE)
    has_g = g_next is not None

    def body(*refs):
        a_ref, w_ref, x_ref, b_ref, p_ref = refs[:5]
        g_ref = refs[5] if has_g else None
        outs = refs[5 + int(has_g):]
        gate = _sig(_dot(a_ref[...], w_ref[...].reshape(K, D), "nn") + b_ref[...])
        xn = x_ref[...] + gate * p_ref[...].astype(F32)
        outs[0][...] = xn
        outs[1][...] = gate.astype(BF16)
        if has_g:
            outs[2][...] = _rms(xn, g_ref[...]).astype(BF16)

    row = pl.BlockSpec((tm, D), lambda i: (i, 0))
    vec = pl.BlockSpec((1, D), lambda i: (0, 0))
    in_specs = [pl.BlockSpec((tm, K), lambda i: (i, 0)), _full_weight_spec(w, layer), row, vec, row]
    ins = [hb, w, x_res, bias, pp]
    out_specs, out_shape = [row, row], [_sds((S, D), F32), _sds((S, D), BF16)]
    if has_g:
        in_specs.append(vec)
        ins.append(g_next)
        out_specs.append(row)
        out_shape.append(_sds((S, D), BF16))
    res = _call(name, body, (S // tm,), in_specs, out_specs, out_shape, ins)
    return res if has_g else (res[0], res[1], None)


def _pool_mix_call(name, x, g):
    S, D = x.shape
    ts = _tile(S, ELT_TILE)
    hb = ts // POOL_HALO
    gc = D // len(POOL_WINDOWS)

    def body(cur_ref, halo_ref, g_ref, o_ref, ext_ref):
        i = pl.program_id(0)
        gain = g_ref[...]
        ext_ref[0:POOL_HALO, :] = jnp.where(i > 0, _rms(halo_ref[...], gain), 0.0)
        ext_ref[POOL_HALO:, :] = _rms(cur_ref[...], gain)
        t = i * ts + lax.broadcasted_iota(jnp.int32, (ts, 1), 0)
        for gi, win in enumerate(POOL_WINDOWS):
            lanes = pl.ds(gi * gc, gc)
            h = ext_ref[pl.ds(POOL_HALO, ts), lanes]
            acc = h
            for j in range(1, win):
                acc = acc + ext_ref[pl.ds(POOL_HALO - j, ts), lanes]
            cnt = jnp.minimum(t + 1, win).astype(F32)
            o_ref[:, gi * gc:(gi + 1) * gc] = (acc / cnt - h).astype(BF16)

    row = pl.BlockSpec((ts, D), lambda i: (i, 0))
    return _call(
        name, body, (S // ts,),
        [row, pl.BlockSpec((POOL_HALO, D), lambda i: (jnp.maximum(i * hb - 1, 0), 0)), pl.BlockSpec((1, D), lambda i: (0, 0))],
        row, _sds((S, D), BF16), (x, x, g), scratch=[pltpu.VMEM((ts + POOL_HALO, D), F32)],
    )


def _pool_out_call(name, mix, wp, scale, x_res, g_next):
    S, D = x_res.shape
    ng, gc, _ = wp.shape
    tm = _tile(S, FULL_ROW_TILE)

    def body(m_ref, w_ref, s_ref, x_ref, g_ref, xo_ref, y_ref, h_ref):
        parts = [_dot(m_ref[:, gi * gc:(gi + 1) * gc], w_ref[gi], "nn") for gi in range(ng)]
        ypre = jnp.concatenate(parts, axis=-1)
        y_ref[...] = ypre.astype(BF16)
        xn = x_ref[...] + ypre * s_ref[...]
        xo_ref[...] = xn
        h_ref[...] = _rms(xn, g_ref[...]).astype(BF16)

    row = pl.BlockSpec((tm, D), lambda i: (i, 0))
    vec = pl.BlockSpec((1, D), lambda i: (0, 0))
    return _call(
        name, body, (S // tm,),
        [row, pl.BlockSpec((ng, gc, gc), lambda i: (0, 0, 0)), vec, row, vec],
        [row, row, row], [_sds((S, D), F32), _sds((S, D), BF16), _sds((S, D), BF16)],
        (mix, wp, scale, x_res, g_next),
    )


KV_STEP = 4


def _attn_probs(q, kprev, kcur, qg, kg, bias, sink, n):
    grp = q.shape[0]
    qn = _rms(q, qg)
    k = jnp.concatenate([kprev, kcur], axis=0)
    kn = _rms(k, kg)
    s = _dot(qn.reshape(grp * Q_BLOCK, HEAD_DIM).astype(BF16), kn.astype(BF16), "nt") * (HEAD_DIM ** -0.5)
    s = s.reshape(grp, Q_BLOCK, 2 * Q_BLOCK) + bias
    qi = lax.broadcasted_iota(jnp.int32, (Q_BLOCK, 2 * Q_BLOCK), 0)
    kj = lax.broadcasted_iota(jnp.int32, (Q_BLOCK, 2 * Q_BLOCK), 1)
    qc = qi // CHUNK
    kc = kj // CHUNK - Q_BLOCK // CHUNK
    ok = (kc <= qc) & (kc >= qc - WINDOW_CHUNKS) & ((n > 0) | (kj >= Q_BLOCK))
    s = jnp.where(ok[None], s, NEG_INF)
    m = jnp.maximum(jnp.max(s, axis=-1, keepdims=True), sink)
    e = jnp.exp(s - m)
    es = jnp.exp(sink - m)
    inv = 1.0 / (jnp.sum(e, axis=-1, keepdims=True) + es)
    return q, qn, k, kn, e * inv, es * inv


def _attn_specs(grp, nb):
    heads = KV_STEP * grp
    qspec = pl.BlockSpec((heads, Q_BLOCK, HEAD_DIM), lambda j, n: (j, jnp.minimum(n, nb - 1), 0))
    prev = pl.BlockSpec((KV_STEP, Q_BLOCK, HEAD_DIM), lambda j, n: (j, jnp.maximum(n - 1, 0), 0))
    cur = pl.BlockSpec((KV_STEP, Q_BLOCK, HEAD_DIM), lambda j, n: (j, jnp.minimum(n, nb - 1), 0))
    gain = pl.BlockSpec((1, HEAD_DIM), lambda j, n: (0, 0))
    bias = pl.BlockSpec((heads, Q_BLOCK, 2 * Q_BLOCK), lambda j, n: (j, 0, 0))
    sink = pl.BlockSpec((heads, 1, 1), lambda j, n: (j, 0, 0))
    return qspec, prev, cur, gain, bias, sink


def _attn_fwd_call(name, q, k, v, qg, kg, bias, sinks):
    H, S, _ = q.shape
    n_kv = k.shape[0]
    grp = H // n_kv
    nb = S // Q_BLOCK
    assert n_kv % KV_STEP == 0

    def body(q_ref, kp_ref, kc_ref, vp_ref, vc_ref, qg_ref, kg_ref, bias_ref, sink_ref, o_ref):
        n = pl.program_id(1)
        for hh in range(KV_STEP):
            hs = slice(hh * grp, (hh + 1) * grp)
            _, _, _, _, p, _ = _attn_probs(q_ref[hs], kp_ref[hh], kc_ref[hh], qg_ref[...], kg_ref[...], bias_ref[hs], sink_ref[hs], n)
            vv = jnp.concatenate([vp_ref[hh], vc_ref[hh]], axis=0).astype(BF16)
            o = _dot(p.reshape(grp * Q_BLOCK, 2 * Q_BLOCK).astype(BF16), vv, "nn")
            o_ref[hs] = o.reshape(grp, Q_BLOCK, HEAD_DIM).astype(BF16)

    qspec, prev, cur, gain, bspec, sspec = _attn_specs(grp, nb)
    return _call(
        name, body, (n_kv // KV_STEP, nb),
        [qspec, prev, cur, prev, cur, gain, gain, bspec, sspec],
        qspec, _sds((H, S, HEAD_DIM), BF16), (q, k, k, v, v, qg, kg, bias, sinks),
    )


def _loss_call(name, y, target):
    S, D = y.shape
    ts = _tile(S, ELT_TILE)

    def body(y_ref, t_ref, d_ref, db_ref, l_ref):
        err = y_ref[...] - t_ref[...]
        dy = err * (1.0 / D)
        d_ref[...] = dy
        db_ref[...] = dy.astype(BF16)
        part = 0.5 * jnp.sum(jnp.sum(err * err, axis=-1, keepdims=True), axis=0, keepdims=True) * (1.0 / D)
        _accumulate(l_ref, jnp.broadcast_to(part, l_ref.shape), pl.program_id(0) == 0)

    row = pl.BlockSpec((ts, D), lambda i: (i, 0))
    return _call(
        name, body, (S // ts,), [row, row],
        [row, row, pl.BlockSpec((8, 128), lambda i: (0, 0))],
        [_sds((S, D), F32), _sds((S, D), BF16), _sds((8, 128), F32)], (y, target),
    )


def _ple_bwd_elt_call(name, dx, gate, pp):
    S, D = dx.shape
    ts = _tile(S, ELT_TILE)

    def body(dx_ref, gt_ref, p_ref, dz_ref, dp_ref, db_ref):
        d = dx_ref[...]
        gt = gt_ref[...].astype(F32)
        dz = d * p_ref[...].astype(F32) * gt * (1.0 - gt)
        dz_ref[...] = dz.astype(BF16)
        dp_ref[...] = (d * gt).astype(BF16)
        _accumulate(db_ref, jnp.sum(dz, axis=0, keepdims=True), pl.program_id(0) == 0)

    row = pl.BlockSpec((ts, D), lambda i: (i, 0))
    return _call(
        name, body, (S // ts,), [row, row, row],
        [row, row, pl.BlockSpec((1, D), lambda i: (0, 0))],
        [_sds((S, D), BF16), _sds((S, D), BF16), _sds((1, D), F32)], (dx, gate, pp),
    )


def _grad_w_call(name, a, a_mode, b, b_mode, comm=None):
    bs = b if isinstance(b, (list, tuple)) else [b]
    S = a.shape[-2]
    tk = _tile(S, K_TILE)
    nk = S // tk

    def spec(arr, mode):
        if mode == "full":
            c = arr.shape[-1]
            return pl.BlockSpec((tk, c), lambda d, k: (k, 0)), c
        if mode == "nat":
            c = arr.shape[-1] // N_DEV
            return pl.BlockSpec((tk, c), lambda d, k: (k, d)), c
        c = arr.shape[-1]
        return pl.BlockSpec((None, tk, c), lambda d, k: (d, k, 0)), c

    a_spec, ca = spec(a, a_mode)
    b_specs, cbs = zip(*[spec(x, b_mode) for x in bs])
    nb = len(bs)

    def body(*refs):
        a_ref = refs[0]
        b_refs = refs[1:1 + nb]
        o_refs = refs[1 + nb:1 + 2 * nb]
        acc_refs = refs[1 + 2 * nb:]
        k = pl.program_id(1)
        av = a_ref[...]

        def finish():
            for o, acc in zip(o_refs, acc_refs):
                o[...] = acc[...].astype(BF16)

        _kloop(k, nk, acc_refs, [_dot(av, br[...], "tn") for br in b_refs], finish)

    res = _call(
        name, body, (N_DEV, nk), [a_spec, *b_specs],
        [pl.BlockSpec((None, ca, cb), lambda d, k: (d, 0, 0)) for cb in cbs],
        [_sds((N_DEV, ca, cb), BF16) for cb in cbs], (a, *bs),
        scratch=[pltpu.VMEM((ca, cb), F32) for cb in cbs], comm=comm,
    )
    own, landed = res if comm is not None else (res, None)
    own = own if isinstance(b, (list, tuple)) else own[0]
    return own if comm is None else (own, landed)


def _dx_full_call(name, dy, w, layer, out_dtype):
    S, D = dy.shape
    K = N_DEV * w.shape[2]
    tm = _tile(S, ROW_TILE)

    def body(dy_ref, w_ref, o_ref):
        o_ref[...] = _dot(dy_ref[...], w_ref[...].reshape(K, D), "nt").astype(out_dtype)

    return _call(
        name, body, (S // tm,), [pl.BlockSpec((tm, D), lambda i: (i, 0)), _full_weight_spec(w, layer)],
        pl.BlockSpec((tm, K), lambda i: (i, 0)), _sds((S, K), out_dtype), (dy, w),
    )


def _dx_full_rms_call(name, dy, w, layer, x, g, dres):
    S, D = x.shape
    K = N_DEV * w.shape[2]
    assert K == D
    tm = _tile(S, FULL_ROW_TILE)

    def body(dy_ref, w_ref, x_ref, g_ref, r_ref, dx_ref, dxb_ref, dg_ref):
        dh, dg = _rms_bwd(_dot(dy_ref[...], w_ref[...].reshape(K, dy_ref.shape[1]), "nt"), x_ref[...], g_ref[...])
        dx = r_ref[...] + dh
        dx_ref[...] = dx
        dxb_ref[...] = dx.astype(BF16)
        _accumulate(dg_ref, dg, pl.program_id(0) == 0)

    row = pl.BlockSpec((tm, D), lambda i: (i, 0))
    vec = pl.BlockSpec((1, D), lambda i: (0, 0))
    return _call(
        name, body, (S // tm,), [pl.BlockSpec((tm, dy.shape[1]), lambda i: (i, 0)), _full_weight_spec(w, layer), row, vec, row],
        [row, row, vec], [_sds((S, D), F32), _sds((S, D), BF16), _sds((1, D), F32)], (dy, w, x, g, dres),
    )


def _ffn_bwd_hidden_call(name, dyb, w, layer, gpre, upre, comm=None):
    S, D = dyb.shape
    fsh = w.shape[2]
    tn = SHARD_PAIR * fsh
    tm = _tile(S, ROW_TILE)

    def body(dy_ref, w_ref, g_ref, u_ref, dg_ref, du_ref):
        w = w_ref[...].reshape(tn, D)
        half = max(tm // 2, 8)
        for r in range(0, tm, half):
            rows = slice(r, r + half)
            da = _dot(dy_ref[rows, :], w, "nt")
            g = g_ref[rows, :].astype(F32)
            u = u_ref[rows, :].astype(F32)
            s = _sig(g)
            dg_ref[rows, :] = (da * u * s * (1.0 + g * (1.0 - s))).astype(BF16)
            du_ref[rows, :] = (da * g * s).astype(BF16)

    cspec = pl.BlockSpec((tm, tn), lambda d, i: (i, d))
    return _call(
        name, body, (N_DEV // SHARD_PAIR, S // tm),
        [pl.BlockSpec((tm, D), lambda d, i: (i, 0)), _pair_spec(w, layer), cspec, cspec],
        [cspec, cspec], [_sds((S, N_DEV * fsh), BF16)] * 2, (dyb, w, gpre, upre), comm=comm,
    )


def _grad_rows_call(name, a, b, comm=None):
    S, F = a.shape
    N = b.shape[1]
    fsh = F // N_DEV
    tr = SHARD_PAIR * fsh
    tn = _tile(N, GRAD_COL_TILE)
    tk = _tile(S, K_TILE)
    nk = S // tk

    def body(a_ref, b_ref, o_ref, acc_ref):
        def finish():
            o_ref[...] = acc_ref[...].astype(BF16).reshape(SHARD_PAIR, fsh, tn)

        _kloop(pl.program_id(2), nk, [acc_ref], [_dot(a_ref[...], b_ref[...], "tn")], finish)

    res = _call(
        name, body, (N_DEV // SHARD_PAIR, N // tn, nk),
        [pl.BlockSpec((tk, tr), lambda d, n, k: (k, d)), pl.BlockSpec((tk, tn), lambda d, n, k: (k, n))],
        pl.BlockSpec((SHARD_PAIR, fsh, tn), lambda d, n, k: (d, 0, n)), _sds((N_DEV, fsh, N), BF16), (a, b),
        scratch=[pltpu.VMEM((tr, tn), F32)], comm=comm,
    )
    return res


def _rms_bwd_res_call(name, dh, x, g, dres, want_colsum):
    S, D = x.shape
    ts = _tile(S, ELT_TILE)

    def body(dh_ref, x_ref, g_ref, r_ref, *outs):
        i = pl.program_id(0)
        d, dg = _rms_bwd(dh_ref[...], x_ref[...], g_ref[...])
        dx = r_ref[...] + d
        outs[0][...] = dx
        outs[1][...] = dx.astype(BF16)
        _accumulate(outs[2], dg, i == 0)
        if want_colsum:
            _accumulate(outs[3], jnp.sum(dx, axis=0, keepdims=True), i == 0)

    row = pl.BlockSpec((ts, D), lambda i: (i, 0))
    vec = pl.BlockSpec((1, D), lambda i: (0, 0))
    out_specs = [row, row, vec] + ([vec] if want_colsum else [])
    out_shape = [_sds((S, D), F32), _sds((S, D), BF16), _sds((1, D), F32)] + ([_sds((1, D), F32)] if want_colsum else [])
    res = _call(name, body, (S // ts,), [row, row, vec, row], out_specs, out_shape, (dh, x, g, dres))
    return tuple(res) if want_colsum else (*res, None)


def _conv_out_bwd_call(name, dyb, w, layer, u2, ln_g, ln_b):
    S, D = u2.shape
    ksh = w.shape[2]
    tm = _tile(S, ELT_TILE)

    def body(dy_ref, w_ref, u2_ref, g_ref, b_ref, du2_ref, dg_ref, db_ref, dbdw_ref):
        i = pl.program_id(0)
        dy = dy_ref[...]
        du4 = jnp.concatenate([_dot(dy, w_ref[d], "nt") for d in range(N_DEV)], axis=-1)
        u2 = u2_ref[...]
        mu = jnp.mean(u2, axis=-1, keepdims=True)
        xc = u2 - mu
        r = lax.rsqrt(jnp.mean(xc * xc, axis=-1, keepdims=True) + EPS)
        xh = xc * r
        gain = g_ref[...]
        u3 = xh * gain + b_ref[...]
        s = _sig(u3)
        du3 = du4 * s * (1.0 + u3 * (1.0 - s))
        dxh = du3 * gain
        du2 = r * (dxh - jnp.mean(dxh, axis=-1, keepdims=True) - xh * jnp.mean(dxh * xh, axis=-1, keepdims=True))
        du2_ref[...] = du2
        _accumulate(dg_ref, jnp.sum(du3 * xh, axis=0, keepdims=True), i == 0)
        _accumulate(db_ref, jnp.sum(du3, axis=0, keepdims=True), i == 0)
        _accumulate(dbdw_ref, jnp.sum(du2, axis=0, keepdims=True), i == 0)

    row = pl.BlockSpec((tm, D), lambda i: (i, 0))
    vec = pl.BlockSpec((1, D), lambda i: (0, 0))
    return _call(
        name, body, (S // tm,),
        [row, pl.BlockSpec((N_DEV, None, ksh, D), lambda i: (0, layer, 0, 0)), row, vec, vec],
        [row, vec, vec, vec], [_sds((S, D), F32)] + [_sds((1, D), F32)] * 3, (dyb, w, u2, ln_g, ln_b),
    )


def _dwconv_bwd_call(name, du2, u1, w_dw, a_pre, g_pre):
    S, D = u1.shape
    tc = _tile(S, CONV_TILE)
    hb = tc // CONV_HALO
    n_halo = S // CONV_HALO
    assert D % CONV_LANES == 0 and tc % CONV_SUB == 0
    wrows = CONV_HALO

    def body(d_cur, d_next, u_cur, u_prev, w_ref, a_ref, g_ref, du_ref, dbin_ref, dw_ref, dext_ref, uext_ref, du1_ref, dwacc_ref, dsh_ref, ush_ref):
        i = pl.program_id(0)
        last = S // tc - 1
        dext_ref[0:tc, :] = d_cur[...]
        dext_ref[tc:tc + CONV_HALO, :] = jnp.where(i < last, d_next[...], 0.0)
        uext_ref[0:CONV_HALO, :] = jnp.where(i > 0, u_prev[...], 0.0)
        uext_ref[CONV_HALO:tc + CONV_HALO, :] = u_cur[...]
        _fill_shifted(dsh_ref, dext_ref, tc + CONV_HALO)
        _fill_shifted(ush_ref, uext_ref, tc + CONV_HALO)

        @pl.when(i == 0)
        def _():
            dwacc_ref[...] = jnp.zeros_like(dwacc_ref)

        first_tap = CONV_HALO - (CONV_WIDTH - 1)
        for r in range(0, tc, CONV_SUB):
            for c in range(0, D, CONV_LANES):
                du1_ref[r:r + CONV_SUB, c:c + CONV_LANES] = _conv_taps(dsh_ref, w_ref, r, c, CONV_WIDTH - 1, True)
                dcur = dext_ref[r:r + CONV_SUB, c:c + CONV_LANES]
                for k in range(CONV_WIDTH):
                    prod = dcur * _shifted_window(ush_ref, r + first_tap + k, c)
                    part = prod[0:8]
                    for q in range(8, CONV_SUB, 8):
                        part = part + prod[q:q + 8]
                    dwacc_ref[k, :, c:c + CONV_LANES] += part

        du1 = du1_ref[...]
        a = a_ref[...].astype(F32)
        sg = _sig(g_ref[...].astype(F32))
        da = du1 * sg
        dgate = du1 * a * sg * (1.0 - sg)
        du_ref[:, 0:D] = da.astype(BF16)
        du_ref[:, D:2 * D] = dgate.astype(BF16)
        _accumulate(dbin_ref, jnp.concatenate([jnp.sum(da, axis=0, keepdims=True), jnp.sum(dgate, axis=0, keepdims=True)], axis=-1), i == 0)

        @pl.when(i == last)
        def _():
            for k in range(CONV_WIDTH):
                dw_ref[k:k + 1, :] = jnp.sum(dwacc_ref[k], axis=0, keepdims=True)
            dw_ref[CONV_WIDTH:, :] = jnp.zeros((wrows - CONV_WIDTH, D), F32)

    row = pl.BlockSpec((tc, D), lambda i: (i, 0))
    nxt = pl.BlockSpec((CONV_HALO, D), lambda i: (jnp.minimum((i + 1) * hb, n_halo - 1), 0))
    prv = pl.BlockSpec((CONV_HALO, D), lambda i: (jnp.maximum(i * hb - 1, 0), 0))
    return _call(
        name, body, (S // tc,),
        [row, nxt, row, prv, pl.BlockSpec((CONV_WIDTH, D), lambda i: (0, 0)), row, row],
        [pl.BlockSpec((tc, 2 * D), lambda i: (i, 0)), pl.BlockSpec((1, 2 * D), lambda i: (0, 0)), pl.BlockSpec((wrows, D), lambda i: (0, 0))],
        [_sds((S, 2 * D), BF16), _sds((1, 2 * D), F32), _sds((wrows, D), F32)],
        (du2, du2, u1, u1, w_dw, a_pre, g_pre),
        scratch=[
            pltpu.VMEM((tc + CONV_HALO + SUBLANES, D), F32), pltpu.VMEM((tc + CONV_HALO + SUBLANES, D), F32),
            pltpu.VMEM((tc, D), F32), pltpu.VMEM((CONV_WIDTH, 8, D), F32),
            pltpu.VMEM((SUBLANES, tc + CONV_HALO, D), F32), pltpu.VMEM((SUBLANES, tc + CONV_HALO, D), F32),
        ],
    )


def _pool_out_bwd_call(name, dy, ypre, scale, wp):
    S, D = dy.shape
    ng, gc, _ = wp.shape
    tm = _tile(S, FULL_ROW_TILE)

    def body(dy_ref, y_ref, s_ref, w_ref, dm_ref, dyp_ref, ds_ref):
        dy = dy_ref[...]
        _accumulate(ds_ref, jnp.sum(dy * y_ref[...].astype(F32), axis=0, keepdims=True), pl.program_id(0) == 0)
        dyp = (dy * s_ref[...]).astype(BF16)
        dyp_ref[...] = dyp
        dm_ref[...] = jnp.concatenate([_dot(dyp[:, gi * gc:(gi + 1) * gc], w_ref[gi], "nt") for gi in range(ng)], axis=-1)

    row = pl.BlockSpec((tm, D), lambda i: (i, 0))
    vec = pl.BlockSpec((1, D), lambda i: (0, 0))
    return _call(
        name, body, (S // tm,), [row, row, vec, pl.BlockSpec((ng, gc, gc), lambda i: (0, 0, 0))],
        [row, row, vec], [_sds((S, D), F32), _sds((S, D), BF16), _sds((1, D), F32)], (dy, ypre, scale, wp),
    )


def _pool_w_grad_call(name, mix, dyp, ng):
    S, D = mix.shape
    gc = D // ng
    tk = _tile(S, K_TILE)
    nk = S // tk

    def body(m_ref, d_ref, o_ref, acc_ref):
        def finish():
            o_ref[...] = acc_ref[...]

        _kloop(pl.program_id(1), nk, [acc_ref], [_dot(m_ref[...], d_ref[...], "tn")], finish)

    blk = pl.BlockSpec((tk, gc), lambda g, k: (k, g))
    return _call(
        name, body, (ng, nk), [blk, blk], pl.BlockSpec((None, gc, gc), lambda g, k: (g, 0, 0)),
        _sds((ng, gc, gc), F32), (mix, dyp), scratch=[pltpu.VMEM((gc, gc), F32)],
    )


def _pool_mix_bwd_call(name, dmix, x, g, dres):
    S, D = x.shape
    ts = _tile(S, ELT_TILE)
    hb = ts // POOL_HALO
    n_halo = S // POOL_HALO
    gc = D // len(POOL_WINDOWS)

    def body(cur_ref, nxt_ref, x_ref, g_ref, r_ref, dx_ref, dxb_ref, dg_ref, ext_ref, dh_ref):
        i = pl.program_id(0)
        last = S // ts - 1
        t = i * ts + lax.broadcasted_iota(jnp.int32, (ts + POOL_HALO, 1), 0)
        for gi, win in enumerate(POOL_WINDOWS):
            lanes = slice(gi * gc, (gi + 1) * gc)
            cnt = jnp.minimum(t + 1, win).astype(F32)
            ext_ref[0:ts, lanes] = cur_ref[:, lanes] / cnt[0:ts]
            ext_ref[ts:, lanes] = jnp.where(i < last, nxt_ref[:, lanes] / cnt[ts:], 0.0)
        for gi, win in enumerate(POOL_WINDOWS):
            lanes = pl.ds(gi * gc, gc)
            acc = ext_ref[pl.ds(0, ts), lanes]
            for j in range(1, win):
                acc = acc + ext_ref[pl.ds(j, ts), lanes]
            dh_ref[:, gi * gc:(gi + 1) * gc] = acc - cur_ref[:, gi * gc:(gi + 1) * gc]
        d, dg = _rms_bwd(dh_ref[...], x_ref[...], g_ref[...])
        dx = r_ref[...] + d
        dx_ref[...] = dx
        dxb_ref[...] = dx.astype(BF16)
        _accumulate(dg_ref, dg, i == 0)

    row = pl.BlockSpec((ts, D), lambda i: (i, 0))
    vec = pl.BlockSpec((1, D), lambda i: (0, 0))
    nxt = pl.BlockSpec((POOL_HALO, D), lambda i: (jnp.minimum((i + 1) * hb, n_halo - 1), 0))
    return _call(
        name, body, (S // ts,), [row, nxt, row, vec, row], [row, row, vec],
        [_sds((S, D), F32), _sds((S, D), BF16), _sds((1, D), F32)], (dmix, dmix, x, g, dres),
        scratch=[pltpu.VMEM((ts + POOL_HALO, D), F32), pltpu.VMEM((ts, D), F32)],
    )


def _attn_bwd_call(name, q, k, v, do, qg, kg, bias, sinks):
    H, S, _ = q.shape
    n_kv = k.shape[0]
    grp = H // n_kv
    nb = S // Q_BLOCK
    scale = HEAD_DIM ** -0.5

    def body(q_ref, kp_ref, kc_ref, vp_ref, vc_ref, do_ref, qg_ref, kg_ref, bias_ref, sink_ref,
             dq_ref, dk_ref, dv_ref, dqg_ref, dkg_ref, dbias_ref, dsink_ref, ck_ref, cv_ref):
        n = pl.program_id(1)

        @pl.when(n == 0)
        def _():
            dqg_ref[...] = jnp.zeros_like(dqg_ref)
            dkg_ref[...] = jnp.zeros_like(dkg_ref)
            dbias_ref[...] = jnp.zeros_like(dbias_ref)
            dsink_ref[...] = jnp.zeros_like(dsink_ref)
            ck_ref[...] = jnp.zeros_like(ck_ref)
            cv_ref[...] = jnp.zeros_like(cv_ref)

        def block_grads(hh):
            hs = slice(hh * grp, (hh + 1) * grp)
            q, qn, _, kn, p, ps = _attn_probs(q_ref[hs], kp_ref[hh], kc_ref[hh], qg_ref[...], kg_ref[...], bias_ref[hs], sink_ref[hs], n)
            rows = grp * Q_BLOCK
            dob = do_ref[hs].reshape(rows, HEAD_DIM).astype(BF16)
            vv = jnp.concatenate([vp_ref[hh], vc_ref[hh]], axis=0).astype(BF16)
            dp = _dot(dob, vv, "nt").reshape(grp, Q_BLOCK, 2 * Q_BLOCK)
            delta = jnp.sum(p * dp, axis=-1, keepdims=True)
            dl = p * (dp - delta)
            dbias_ref[hs] += dl
            dsink_ref[hs] += jnp.sum(-ps * delta, axis=1, keepdims=True)
            dlb = dl.reshape(rows, 2 * Q_BLOCK).astype(BF16)
            dqn = (_dot(dlb, kn.astype(BF16), "nn") * scale).reshape(grp, Q_BLOCK, HEAD_DIM)
            dkn = _dot(dlb, qn.reshape(rows, HEAD_DIM).astype(BF16), "tn") * scale
            dvv = _dot(p.reshape(rows, 2 * Q_BLOCK).astype(BF16), dob, "tn")
            qgain = qg_ref[...]
            r = lax.rsqrt(jnp.mean(q * q, axis=-1, keepdims=True) + EPS)
            qh = q * r
            dqg_ref[hh] += jnp.sum(jnp.sum(dqn * qh, axis=1), axis=0, keepdims=True)
            dqh = dqn * qgain
            dq_ref[hs] = r * (dqh - qh * jnp.mean(dqh * qh, axis=-1, keepdims=True))
            return dkn, dvv

        def finish_prev(hh, dkn_prev, dv_prev):
            dk, dkg = _rms_bwd(dkn_prev, kp_ref[hh], kg_ref[...])
            dk_ref[hh] = dk
            dv_ref[hh] = dv_prev
            dkg_ref[hh] += dkg

        @pl.when(n < nb)
        def _():
            grads = [block_grads(hh) for hh in range(KV_STEP)]

            @pl.when(n > 0)
            def _():
                for hh, (dkn, dvv) in enumerate(grads):
                    finish_prev(hh, ck_ref[hh] + dkn[0:Q_BLOCK], cv_ref[hh] + dvv[0:Q_BLOCK])

            for hh, (dkn, dvv) in enumerate(grads):
                ck_ref[hh] = dkn[Q_BLOCK:]
                cv_ref[hh] = dvv[Q_BLOCK:]

        @pl.when(n == nb)
        def _():
            for hh in range(KV_STEP):
                finish_prev(hh, ck_ref[hh], cv_ref[hh])

    qspec, prev, cur, gain, bspec, sspec = _attn_specs(grp, nb)
    kout = pl.BlockSpec((KV_STEP, Q_BLOCK, HEAD_DIM), lambda j, n: (j, jnp.maximum(n - 1, 0), 0))
    gout = pl.BlockSpec((KV_STEP, 1, HEAD_DIM), lambda j, n: (j, 0, 0))
    return _call(
        name, body, (n_kv // KV_STEP, nb + 1),
        [qspec, prev, cur, prev, cur, qspec, gain, gain, bspec, sspec],
        [qspec, kout, kout, gout, gout, bspec, sspec],
        [
            _sds((H, S, HEAD_DIM), F32), _sds((n_kv, S, HEAD_DIM), F32), _sds((n_kv, S, HEAD_DIM), F32),
            _sds((n_kv, 1, HEAD_DIM), F32), _sds((n_kv, 1, HEAD_DIM), F32),
            _sds((H, Q_BLOCK, 2 * Q_BLOCK), F32), _sds((H, 1, 1), F32),
        ],
        (q, k, k, v, v, do, qg, kg, bias, sinks),
        scratch=[pltpu.VMEM((KV_STEP, Q_BLOCK, HEAD_DIM), F32), pltpu.VMEM((KV_STEP, Q_BLOCK, HEAD_DIM), F32)],
    )


def _bucket_sum_call(name, onehot, dbias):
    nbk, n = onehot.shape
    H = dbias.shape[0]

    def body(o_ref, d_ref, out_ref):
        out_ref[...] = lax.dot_general(o_ref[...], d_ref[...], _DIMS["nt"], precision=lax.Precision.HIGHEST, preferred_element_type=F32)

    return _call(
        name, body, (1,), [pl.BlockSpec((nbk, n), lambda i: (0, 0)), pl.BlockSpec((H, n), lambda i: (0, 0))],
        pl.BlockSpec((nbk, H), lambda i: (0, 0)), _sds((nbk, H), F32), (onehot, dbias),
    )


def _bias_table_call(name, rel_bias, onehot):
    nbk, n = onehot.shape
    H = rel_bias.shape[1]

    def body(r_ref, o_ref, out_ref):
        out_ref[...] = lax.dot_general(r_ref[...], o_ref[...], _DIMS["tn"], precision=lax.Precision.HIGHEST, preferred_element_type=F32)

    return _call(
        name, body, (1,), [pl.BlockSpec((nbk, H), lambda i: (0, 0)), pl.BlockSpec((nbk, n), lambda i: (0, 0))],
        pl.BlockSpec((H, n), lambda i: (0, 0)), _sds((H, n), F32), (rel_bias, onehot),
    )


def _exchange_call(name, comm):
    n_src, n_dst = len(comm.srcs), len(comm.out_shape)

    def body(*refs):
        src_refs, dst_refs, sem_refs = refs[:n_src], refs[n_src:n_src + n_dst], refs[n_src + n_dst:]
        comm.start(src_refs, dst_refs, sem_refs)
        comm.wait(src_refs, dst_refs, sem_refs)

    hbm = pl.BlockSpec(memory_space=pltpu.HBM)
    return pl.pallas_call(
        body, name=name, in_specs=[hbm] * n_src, out_specs=[hbm] * n_dst, out_shape=list(comm.out_shape),
        scratch_shapes=list(comm.sems), compiler_params=pltpu.CompilerParams(has_side_effects=True),
    )(*comm.srcs)


def _all_gather_call(name, shards):
    n = len(shards)

    def body(*refs):
        in_refs, out_refs = refs[:n], refs[n:2 * n]
        send_sems, recv_sems, local_sems = refs[2 * n:]
        x, y, c = _mesh_pos()
        me, sibling = (x, y, c), (x, y, 1 - c)
        chips = [(1 - x, y), (x, 1 - y), (1 - x, 1 - y)]

        def copy(t, k, block, to, src=None):
            dst = out_refs[t].at[_slot(*block)]
            return pltpu.make_async_remote_copy(
                src_ref=dst if src is None else src, dst_ref=dst,
                send_sem=send_sems.at[7 * t + k], recv_sem=recv_sems.at[7 * t + k],
                device_id=to, device_id_type=pl.DeviceIdType.MESH,
            )

        mine = [pltpu.make_async_copy(in_refs[t], out_refs[t].at[_slot(*me)], local_sems.at[t]) for t in range(n)]
        for cp in mine:
            cp.start()
        first = []
        for t in range(n):
            first.append(copy(t, 0, me, sibling, src=in_refs[t]))
            first += [copy(t, 1 + j, me, (*chip, c), src=in_refs[t]) for j, chip in enumerate(chips)]
        for cp in first:
            cp.start()
        passed = []
        for j, chip in enumerate(chips):
            for t in range(n):
                copy(t, 1 + j, (*chip, c), me).wait_recv()
                fwd = copy(t, 4 + j, (*chip, c), sibling)
                fwd.start()
                passed.append(fwd)
        for t in range(n):
            copy(t, 0, sibling, me).wait_recv()
            for j, chip in enumerate(chips):
                copy(t, 4 + j, (*chip, 1 - c), me).wait_recv()
        for cp in first + passed:
            cp.wait_send()
        for cp in mine:
            cp.wait()

    hbm = pl.BlockSpec(memory_space=pltpu.HBM)
    return pl.pallas_call(
        body, name=name,
        in_specs=[hbm] * n, out_specs=[hbm] * n,
        out_shape=[_sds((N_DEV, *s.shape), s.dtype) for s in shards],
        scratch_shapes=[pltpu.SemaphoreType.DMA((7 * n,)), pltpu.SemaphoreType.DMA((7 * n,)), pltpu.SemaphoreType.DMA((n,))],
        compiler_params=pltpu.CompilerParams(has_side_effects=True),
    )(*shards)


def _all_reduce_small_call(name, pack):
    R, C = pack.shape

    def body(in_ref, out_ref, land_ref, send_sems, recv_sems):
        x, y, c = _mesh_pos()
        me = _slot(x, y, c)
        peers = _peers(x, y, c)
        land_ref[me] = in_ref[...]
        sends = [
            pltpu.make_async_remote_copy(
                src_ref=in_ref, dst_ref=land_ref.at[me], send_sem=send_sems.at[k], recv_sem=recv_sems.at[k],
                device_id=to, device_id_type=pl.DeviceIdType.MESH,
            )
            for k, to in enumerate(peers)
        ]
        for cp in sends:
            cp.start()
        for k, frm in enumerate(peers):
            pltpu.make_async_remote_copy(
                src_ref=in_ref, dst_ref=land_ref.at[_slot(*frm)], send_sem=send_sems.at[k], recv_sem=recv_sems.at[k],
                device_id=frm, device_id_type=pl.DeviceIdType.MESH,
            ).wait_recv()
        for cp in sends:
            cp.wait_send()
        total = land_ref[0]
        for d in range(1, N_DEV):
            total = total + land_ref[d]
        out_ref[...] = total

    vmem = pl.BlockSpec(memory_space=pltpu.VMEM)
    return pl.pallas_call(
        body, name=name, in_specs=[vmem], out_specs=vmem, out_shape=_sds((R, C), F32),
        scratch_shapes=[pltpu.VMEM((N_DEV, R, C), F32), pltpu.SemaphoreType.DMA((7,)), pltpu.SemaphoreType.DMA((7,))],
        compiler_params=pltpu.CompilerParams(has_side_effects=True, vmem_limit_bytes=VMEM_LIMIT_BYTES),
    )(pack)


def _adamw_call(name, grad, landed, w, m, v, layer, prev):
    L, R, C = w.shape
    tr = _div_tile(R, ADAM_BLOCK_BYTES // (C * 4))
    c1 = 1.0 / (1.0 - ADAM_B1 ** ADAM_STEP)
    c2 = 1.0 / (1.0 - ADAM_B2 ** ADAM_STEP)

    def body(g_ref, w_ref, m_ref, v_ref, *rest):
        go_ref, d_ref, mo_ref, vo_ref = rest[-4:]
        if landed:
            g = g_ref[0].astype(F32)
            for d in range(1, N_DEV):
                g = g + g_ref[d].astype(F32)
        else:
            g = g_ref[...]
        go_ref[...] = g
        mn = ADAM_B1 * m_ref[...] + (1.0 - ADAM_B1) * g
        vn = ADAM_B2 * v_ref[...] + (1.0 - ADAM_B2) * (g * g)
        mo_ref[...] = mn
        vo_ref[...] = vn
        d_ref[...] = -ADAM_LR * ((mn * c1) / (jnp.sqrt(vn * c2) + ADAM_EPS) + ADAM_WD * w_ref[...])

    blk = pl.BlockSpec((None, tr, C), lambda i: (layer, i, 0))
    gspec = pl.BlockSpec((N_DEV, tr, C), lambda i: (0, i, 0)) if landed else pl.BlockSpec((tr, C), lambda i: (i, 0))
    in_specs, ins, aliases = [gspec, blk, blk, blk], [grad, w, m, v], {}
    if prev is not None:
        in_specs += [pl.BlockSpec(memory_space=pl.ANY)] * 4
        ins += list(prev)
        aliases = {4 + q: q for q in range(4)}
    return _call(name, body, (R // tr,), in_specs, [blk] * 4, [_sds((L, R, C), F32)] * 4, ins, aliases=aliases)


def _t5_bucket(rel):
    nb = NUM_BUCKETS // 2
    n = -rel
    ret = jnp.where(n < 0, nb, 0)
    n = jnp.abs(n)
    max_exact = nb // 2
    nf = jnp.maximum(n, 1).astype(jnp.float32)
    large = max_exact + (jnp.log(nf / max_exact) / math.log(REL_MAX_DIST / max_exact) * (nb - max_exact)).astype(jnp.int32)
    large = jnp.minimum(large, nb - 1)
    return ret + jnp.where(n < max_exact, n, large)


def _band_buckets():
    i = jnp.arange(Q_BLOCK)[:, None]
    j = jnp.arange(2 * Q_BLOCK)[None, :]
    return _t5_bucket(j - Q_BLOCK - i)


def _to_heads(t, n_heads):
    S = t.shape[0]
    return t.reshape(S, n_heads, HEAD_DIM).transpose(1, 0, 2)


def _from_heads(t):
    H, S, _ = t.shape
    return t.transpose(1, 0, 2).reshape(S, H * HEAD_DIM)


def _gathered_vec(t):
    nd, L, n = t.shape
    return t.transpose(1, 0, 2).reshape(L, nd * n)


_MIXER_WEIGHTS = {0: ("conv_w_in", "conv_w_out"), 1: (), 2: ("attn_w_qkv", "attn_w_o")}
_FFN_UP_WEIGHTS = ("ffn_w_gate", "ffn_w_up")
_FFN_REST_WEIGHTS = ("ffn_w_down", "ple_w_proj", "ple_w_gate")
_TURNED = _FFN_UP_WEIGHTS + ("conv_w_in", "attn_w_qkv")
_SMALL_SHARDED = ("conv_b_in", "conv_w_dw", "conv_b_dw", "conv_ln_g", "conv_ln_b", "conv_b_out")


def _mixer_keys(i):
    return [(n, i // 3) for n in _MIXER_WEIGHTS[i % 3]]


def _step(x, p, target, wb, small_pack, small_shapes, V):
    S, D = x.shape
    depth = V["norm_mix"].shape[0]
    n_heads = D // HEAD_DIM
    n_kv = (wb["attn_w_qkv"].shape[1] * N_DEV - D) // (2 * HEAD_DIM)
    ng = len(POOL_WINDOWS)
    vec = lambda t, i: t[i][None, :]
    shard = lambda key: wb[key[0]][key[1]:key[1] + 1]

    keys_a0 = [(n, 0) for n in _FFN_UP_WEIGHTS]
    keys_b0 = [(n, 0) for n in _FFN_REST_WEIGHTS]
    first = _mixer_keys(0) + [("pool_w", 0)]
    gathered = _all_gather_call("all_gather0", [shard(k) for k in first] + [small_pack])
    W = dict(zip(first, gathered[:-1]))
    pw = W.pop(("pool_w", 0))
    wp = pw[:, 0].transpose(1, 0, 2, 3).reshape(pw.shape[2], pw.shape[3] * N_DEV, pw.shape[4])
    V = dict(V)
    for n, t in zip(_SMALL_SHARDED, _unpack(gathered[-1], small_shapes, lead=(N_DEV,))):
        if n == "conv_w_dw":
            V[n] = t.transpose(1, 2, 0, 3).reshape(t.shape[1], t.shape[2], -1)
        else:
            V[n] = _gathered_vec(t)

    buckets = _band_buckets()
    onehot = (buckets.reshape(1, -1) == jnp.arange(NUM_BUCKETS)[:, None]).astype(F32)
    bias_tab = _bias_table_call("bias_table", V["rel_bias"], onehot).reshape(n_heads, Q_BLOCK, 2 * Q_BLOCK)
    sinks3 = V["attn_sinks"].reshape(n_heads, 1, 1)
    pb = p.astype(BF16)

    def gather_next(keys):
        return _GatherOwn([shard(k) for k in keys]) if keys else None

    saved = []
    hb = _rms_call("rms_in", x, vec(V["norm_mix"], 0))
    for i in range(depth):
        kind, j = i % 3, i // 3
        sv = {"x0": x, "h0": hb}
        g_ffn = vec(V["norm_ffn"], i)
        if kind == 0:
            ride = i == 0
            res = _conv_in_call(f"conv_in{i}", hb, W["conv_w_in", j], 0, vec(V["conv_b_in"], j), comm=gather_next(keys_b0) if ride else None)
            if ride:
                res, bufs_b = res
            a_pre, g_pre, u1 = res
            res = _dwconv_call(
                f"dwconv{i}", u1, V["conv_w_dw"][j], vec(V["conv_b_dw"], j), vec(V["conv_ln_g"], j), vec(V["conv_ln_b"], j),
                comm=gather_next(keys_a0) if ride else None)
            if ride:
                res, bufs_a = res
            u2, u4 = res
            res = _fullmm_res_call(
                f"conv_out{i}", u4, W["conv_w_out", j], 0, x, vec(V["conv_b_out"], j), g_ffn,
                comm=_GatherForward(bufs_a + bufs_b) if ride else None)
            if ride:
                res, bufs = res
                W.update(zip(keys_a0 + keys_b0, bufs))
            x1, h2 = res
            sv.update(a_pre=a_pre, g_pre=g_pre, u1=u1, u2=u2, u4=u4)
        elif kind == 1:
            mix = _pool_mix_call(f"pool_mix{i}", x, vec(V["norm_mix"], i))
            x1, ypre, h2 = _pool_out_call(f"pool_out{i}", mix, wp, vec(V["pool_scale"], j), x, g_ffn)
            sv.update(mix=mix, ypre=ypre)
        else:
            qkv = _colmm_call(f"qkv{i}", hb, W["attn_w_qkv", j], 0, F32, turned=True)
            q = _to_heads(qkv[:, :D], n_heads)
            k = _to_heads(qkv[:, D:D + n_kv * HEAD_DIM], n_kv)
            v = _to_heads(qkv[:, D + n_kv * HEAD_DIM:], n_kv)
            o = _attn_fwd_call(f"attn{i}", q, k, v, vec(V["attn_q_norm"], j), vec(V["attn_k_norm"], j), bias_tab, sinks3)
            ob = _from_heads(o)
            x1, h2 = _fullmm_res_call(f"attn_out{i}", ob, W["attn_w_o", j], 0, x, None, g_ffn)
            sv.update(q=q, k=k, v=v, ob=ob)
        more = i + 1 < depth
        keys_a = [(n, i + 1) for n in _FFN_UP_WEIGHTS] if more else []
        keys_b = [(n, i + 1) for n in _FFN_REST_WEIGHTS] + _mixer_keys(i + 1) if more else []
        res = _ffn_up_call(f"ffn_up{i}", h2, W["ffn_w_gate", i], W["ffn_w_up", i], 0, comm=gather_next(keys_a))
        if more:
            (gpre, upre, act), bufs_a = res
        else:
            gpre, upre, act = res
        res = _mm_rows_call(f"ffn_down{i}", act, W["ffn_w_down", i], 0, None, comm=gather_next(keys_b))
        if more:
            y, bufs_b = res
        else:
            y = res
        res = _res_rms_call(f"ffn_res{i}", y, x1, vec(V["norm_ple"], i), comm=_GatherForward(bufs_a + bufs_b) if more else None)
        if more:
            res, bufs = res
            W.update(zip(keys_a + keys_b, bufs))
        x2, h3 = res
        pp = _colmm_call(f"ple_proj{i}", pb[i], W["ple_w_proj", i], 0, BF16)
        g_next = vec(V["norm_mix"], i + 1) if more else None
        x3, gate, hb_next = _ple_gate_call(f"ple_gate{i}", h3, W["ple_w_gate", i], 0, x2, vec(V["ple_b_gate"], i), pp, g_next)
        sv.update(x1=x1, h2=h2, gpre=gpre, upre=upre, act=act, x2=x2, h3=h3, pp=pp, gate=gate)
        saved.append(sv)
        x, hb = x3, hb_next

    dx, dxb, loss_tile = _loss_call("loss", x, target)
    loss = loss_tile[0, 0]

    landed = {}
    GV = {n: [None] * V[n].shape[0] for n in ("norm_mix", "norm_ffn", "norm_ple", "ple_b_gate", "conv_b_in", "conv_w_dw", "conv_b_dw", "conv_ln_g", "conv_ln_b", "conv_b_out", "pool_scale")}

    def scatter_of(pending):
        return _ScatterSlots([g for _, g in pending]) if pending else None

    def record(pending, bufs):
        landed.update(zip([k for k, _ in pending], bufs))

    mixer_pending = []
    for i in reversed(range(depth)):
        kind, j = i % 3, i // 3
        sv = saved[i]
        dz, dpp, db_gate = _ple_bwd_elt_call(f"ple_bwd{i}", dx, sv["gate"], sv["pp"])
        GV["ple_b_gate"][i] = db_gate
        pending = [(("ple_w_gate", i), _grad_w_call(f"g_ple_gate{i}", sv["h3"], "nat", dz, "full"))]
        pending.append((("ple_w_proj", i), _grad_w_call(f"g_ple_proj{i}", pb[i], "full", dpp, "nat")))
        dx, dxb, dg = _dx_full_rms_call(f"d_x2_{i}", dz, W["ple_w_gate", i], 0, sv["x2"], vec(V["norm_ple"], i), dx)
        GV["norm_ple"][i] = dg
        res = _ffn_bwd_hidden_call(f"ffn_bwd{i}", dxb, W["ffn_w_down", i], 0, sv["gpre"], sv["upre"], comm=scatter_of(mixer_pending))
        if mixer_pending:
            (dgp, dup), bufs = res
            record(mixer_pending, bufs)
        else:
            dgp, dup = res
        g_down = _grad_rows_call(f"g_ffn_down{i}", sv["act"], dxb)
        g_gate, bufs = _grad_rows_call(f"g_ffn_gate{i}", dgp, sv["h2"], comm=scatter_of(pending))
        record(pending, bufs)
        pending = [(("ffn_w_down", i), g_down)]
        g_up, bufs = _grad_rows_call(f"g_ffn_up{i}", dup, sv["h2"], comm=scatter_of(pending))
        record(pending, bufs)
        pending = [(("ffn_w_gate", i), g_gate)]
        dh, bufs = _mm_rows_call(f"d_h2g_{i}", dgp, W["ffn_w_gate", i], 0, None, comm=scatter_of(pending))
        record(pending, bufs)
        pending = [(("ffn_w_up", i), g_up)]
        dh, bufs = _mm_rows_call(f"d_h2_{i}", dup, W["ffn_w_up", i], 0, dh, comm=scatter_of(pending))
        record(pending, bufs)
        dx, dxb, dg, colsum = _rms_bwd_res_call(f"d_x1_{i}", dh, sv["x1"], vec(V["norm_ffn"], i), dx, kind == 0)
        GV["norm_ffn"][i] = dg
        g_mix = vec(V["norm_mix"], i)
        if kind == 0:
            GV["conv_b_out"][j] = colsum
            g_out = _grad_w_call(f"g_conv_out{i}", sv["u4"], "nat", dxb, "full")
            du2, d_ln_g, d_ln_b, d_b_dw = _conv_out_bwd_call(f"conv_out_bwd{i}", dxb, W["conv_w_out", j], 0, sv["u2"], vec(V["conv_ln_g"], j), vec(V["conv_ln_b"], j))
            du, d_b_in, d_w_dw = _dwconv_bwd_call(f"dwconv_bwd{i}", du2, sv["u1"], V["conv_w_dw"][j], sv["a_pre"], sv["g_pre"])
            GV["conv_ln_g"][j], GV["conv_ln_b"][j], GV["conv_b_dw"][j] = d_ln_g, d_ln_b, d_b_dw
            GV["conv_b_in"][j], GV["conv_w_dw"][j] = d_b_in, d_w_dw[:CONV_WIDTH]
            g_in = _grad_rows_call(f"g_conv_in{i}", du, sv["h0"])
            mixer_pending = [(("conv_w_in", j), g_in), (("conv_w_out", j), g_out)]
            dh = _mm_rows_call(f"d_h0_{i}", du, W["conv_w_in", j], 0, None, comm=scatter_of(mixer_pending) if i == 0 else None)
            if i == 0:
                dh, bufs = dh
                record(mixer_pending, bufs)
                mixer_pending = []
            dx, dxb, dg, _ = _rms_bwd_res_call(f"d_x0_{i}", dh, sv["x0"], g_mix, dx, False)
        elif kind == 1:
            dmix, dyp, d_scale = _pool_out_bwd_call(f"pool_out_bwd{i}", dx, sv["ypre"], vec(V["pool_scale"], j), wp)
            GV["pool_scale"][j] = d_scale
            g_pool = _pool_w_grad_call(f"g_pool_w{i}", sv["mix"], dyp, ng)
            gc = g_pool.shape[1]
            g_pool = g_pool.reshape(ng, N_DEV, gc // N_DEV, gc).transpose(1, 0, 2, 3).reshape(N_DEV, ng * (gc // N_DEV), gc)
            mixer_pending = [(("pool_w", j), g_pool.astype(BF16))]
            dx, dxb, dg = _pool_mix_bwd_call(f"pool_mix_bwd{i}", dmix, sv["x0"], g_mix, dx)
        else:
            g_o = _grad_w_call(f"g_attn_o{i}", sv["ob"], "nat", dxb, "full")
            do = _to_heads(_dx_full_call(f"d_attn_o{i}", dxb, W["attn_w_o", j], 0, F32), n_heads)
            dq, dk, dv, dqg, dkg, dbias, dsink = _attn_bwd_call(
                f"attn_bwd{i}", sv["q"], sv["k"], sv["v"], do, vec(V["attn_q_norm"], j), vec(V["attn_k_norm"], j), bias_tab, sinks3)
            GV["attn_q_norm"] = jnp.sum(dqg, axis=0)
            GV["attn_k_norm"] = jnp.sum(dkg, axis=0)
            GV["attn_sinks"] = dsink.reshape(1, n_heads)
            GV["rel_bias"] = _bucket_sum_call(f"g_rel_bias{i}", onehot, dbias.reshape(n_heads, -1))
            dqkv = jnp.concatenate([_from_heads(dq), _from_heads(dk), _from_heads(dv)], axis=-1).astype(BF16)
            g_qkv = _grad_rows_call(f"g_qkv{i}", dqkv, sv["h0"])
            mixer_pending = [(("attn_w_qkv", j), g_qkv), (("attn_w_o", j), g_o)]
            dh = _mm_rows_call(f"d_h0_{i}", dqkv, W["attn_w_qkv", j], 0, None)
            dx, dxb, dg, _ = _rms_bwd_res_call(f"d_x0_{i}", dh, sv["x0"], g_mix, dx, False)
        GV["norm_mix"][i] = dg
    if mixer_pending:
        record(mixer_pending, _exchange_call("grad_scatter_tail", scatter_of(mixer_pending)))
    return loss, dx, landed, GV, V


_BIG = ("conv_w_in", "conv_w_out", "pool_w", "attn_w_qkv", "attn_w_o", "ffn_w_gate", "ffn_w_up", "ffn_w_down", "ple_w_proj", "ple_w_gate")
_SMALL_REPLICATED = ("norm_mix", "norm_ffn", "norm_ple", "pool_scale", "attn_q_norm", "attn_k_norm", "attn_sinks", "rel_bias", "ple_b_gate")
_WEIGHTS = ("norm_mix", "norm_ffn", "norm_ple", "conv_w_in", "conv_b_in", "conv_w_dw", "conv_b_dw", "conv_ln_g", "conv_ln_b", "conv_w_out",
            "conv_b_out", "pool_w", "pool_scale", "attn_w_qkv", "attn_q_norm", "attn_k_norm", "attn_sinks", "attn_w_o", "rel_bias",
            "ffn_w_gate", "ffn_w_up", "ffn_w_down", "ple_w_proj", "ple_w_gate", "ple_b_gate")
PACK_LANES = 128


def _pack(parts):
    flat = jnp.concatenate([t.reshape(-1).astype(F32) for t in parts])
    rows = -(-flat.shape[0] // (8 * PACK_LANES)) * 8
    flat = jnp.pad(flat, (0, rows * PACK_LANES - flat.shape[0]))
    return flat.reshape(rows, PACK_LANES)


def _unpack(pack, shapes, lead=()):
    flat = pack.reshape(*lead, -1)
    out, pos = [], 0
    for s in shapes:
        n = math.prod(s)
        out.append(flat[..., pos:pos + n].reshape(*lead, *s))
        pos += n
    return out


def _as2d(t):
    return t.reshape(-1, t.shape[-1])


def kernel(x, p, norm_mix, norm_ffn, norm_ple, conv_w_in, conv_b_in, conv_w_dw, conv_b_dw, conv_ln_g, conv_ln_b, conv_w_out, conv_b_out, pool_w, pool_scale, attn_w_qkv, attn_q_norm, attn_k_norm, attn_sinks, attn_w_o, rel_bias, ffn_w_gate, ffn_w_up, ffn_w_down, ple_w_proj, ple_w_gate, ple_b_gate, loss_target, m_norm_mix, m_norm_ffn, m_norm_ple, m_conv_w_in, m_conv_b_in, m_conv_w_dw, m_conv_b_dw, m_conv_ln_g, m_conv_ln_b, m_conv_w_out, m_conv_b_out, m_pool_w, m_pool_scale, m_attn_w_qkv, m_attn_q_norm, m_attn_k_norm, m_attn_sinks, m_attn_w_o, m_rel_bias, m_ffn_w_gate, m_ffn_w_up, m_ffn_w_down, m_ple_w_proj, m_ple_w_gate, m_ple_b_gate, v_norm_mix, v_norm_ffn, v_norm_ple, v_conv_w_in, v_conv_b_in, v_conv_w_dw, v_conv_b_dw, v_conv_ln_g, v_conv_ln_b, v_conv_w_out, v_conv_b_out, v_pool_w, v_pool_scale, v_attn_w_qkv, v_attn_q_norm, v_attn_k_norm, v_attn_sinks, v_attn_w_o, v_rel_bias, v_ffn_w_gate, v_ffn_w_up, v_ffn_w_down, v_ple_w_proj, v_ple_w_gate, v_ple_b_gate):
    given = dict(locals())
    w = {n: given[n] for n in _WEIGHTS}
    m = {n: given["m_" + n] for n in _WEIGHTS}
    v = {n: given["v_" + n] for n in _WEIGHTS}
    me = _slot(*_mesh_pos())

    wb = {n: w[n].astype(BF16) for n in _BIG}
    for n in _TURNED:
        wb[n] = wb[n].transpose(0, 2, 1)
    small_pack = _pack([w[n] for n in _SMALL_SHARDED])
    small_shapes = [w[n].shape for n in _SMALL_SHARDED]
    loss, grad_x, landed, GV, V = _step(x[0], p[:, 0], loss_target[0], wb, small_pack, small_shapes, {n: w[n] for n in _SMALL_REPLICATED})
    loss = lax.psum(loss, ("x", "y", "c"))

    small_names = list(_SMALL_REPLICATED) + list(_SMALL_SHARDED)
    small_full = []
    for n in small_names:
        g = GV[n]
        g = jnp.stack([t.reshape(V[n].shape[1:]) for t in g]) if isinstance(g, list) else g.reshape(V[n].shape)
        small_full.append(g)
    reduced = _unpack(_all_reduce_small_call("all_reduce_small", _pack(small_full)), [t.shape for t in small_full])
    small_grad = {}
    for n, g in zip(small_names, reduced):
        if n in _SMALL_SHARDED:
            c = w[n].shape[-1]
            g = lax.dynamic_slice_in_dim(g, me * c, c, axis=g.ndim - 1)
        small_grad[n] = g

    out = {}
    for n in _BIG:
        layers = w[n].shape[0]
        turned = n in _TURNED
        as3d = lambda t: t.transpose(0, 2, 1) if turned else t.reshape(layers, -1, t.shape[-1])
        res = None
        for l in range(layers):
            res = _adamw_call(f"adamw_{n}{l}", landed[n, l], True, as3d(w[n]), as3d(m[n]), as3d(v[n]), l, res)
        out[n] = [t.transpose(0, 2, 1) if turned else t.reshape(w[n].shape) for t in res]
    for n in small_names:
        res = _adamw_call(f"adamw_{n}", _as2d(small_grad[n]), False, _as2d(w[n])[None], _as2d(m[n])[None], _as2d(v[n])[None], 0, None)
        out[n] = [t.reshape(w[n].shape) for t in res]

    grads = [out[n][0] for n in _WEIGHTS]
    deltas = [out[n][1] for n in _WEIGHTS]
    new_m = [out[n][2] for n in _WEIGHTS]
    new_v = [out[n][3] for n in _WEIGHTS]
    return (loss, grad_x[None], *grads, *deltas, *new_m, *new_v)
```
